```python
import jax
import jax.numpy as jnp
from jax import lax

D_MODEL = 2048
BATCH = 2
SEQ = 4096
DEPTH = 4
DEC_BATCH = 32
DEC_SEQ = 4
PAST_LEN = 16384
PAGE_SIZE = 128

N_META = 16
N_A_LAYERS = DEPTH // 2
N_B_LAYERS = DEPTH - N_A_LAYERS
GLA_HEADS = 4
GLA_KEY_DIM = D_MODEL // 2
GLA_VAL_DIM = D_MODEL
GLA_DK = GLA_KEY_DIM // GLA_HEADS
GLA_DV = GLA_VAL_DIM // GLA_HEADS
GATE_RANK = 16
GATE_LOGIT_NORM = 16.0
GLA_CHUNK = 64
GLA_IN_WIDTH = 2 * GLA_KEY_DIM + 2 * GLA_VAL_DIM + GATE_RANK
HEAD_DIM = 64
N_Q_HEADS = D_MODEL // HEAD_DIM
N_KV_HEADS = 8
GROUP = N_Q_HEADS // N_KV_HEADS
Q_WIDTH = N_Q_HEADS * HEAD_DIM
KV_WIDTH = N_KV_HEADS * HEAD_DIM
WINDOW = 128
BLOCK = 128
RMS_EPS = 1e-6
NEG_INF = -1e30

kernel_name = 'yoco_gla_swa_sink_step'


def rmsnorm(x, g):
    xf = x.astype(jnp.float32)
    y = xf * lax.rsqrt(jnp.mean(xf * xf, axis=-1, keepdims=True) + RMS_EPS)
    return (y * g.astype(jnp.float32)).astype(x.dtype)


def gla_recurrence(q, k, v, g, s0, chunk):
    B, L, H, _ = q.shape
    DV = v.shape[-1]
    pad = (-L) % chunk
    if pad:
        widths = ((0, 0), (pad, 0), (0, 0), (0, 0))
        q, k, v, g = (jnp.pad(t, widths) for t in (q, k, v, g))
    n = (L + pad) // chunk

    def to_chunks(t):
        return t.reshape(B, n, chunk, H, t.shape[-1]).transpose(1, 0, 3, 2, 4)

    tri = jnp.tril(jnp.ones((chunk, chunk), dtype=bool))

    def step(S, inp):
        qc, kc, vc, gc = inp
        b = jnp.cumsum(gc, axis=2)
        diff = b[:, :, :, None, :] - b[:, :, None, :, :]
        decay = jnp.where(tri[:, :, None], jnp.exp(jnp.minimum(diff, 0.0)), 0.0)
        att = jnp.einsum('bhid,bhjd,bhijd->bhij', qc, kc, decay)
        o = (jnp.einsum('bhij,bhjv->bhiv', att, vc)
             + jnp.einsum('bhid,bhdv->bhiv', qc * jnp.exp(b), S))
        b_last = b[:, :, -1:, :]
        S_new = (S * jnp.exp(b_last[:, :, 0, :])[..., None]
                 + jnp.einsum('bhjd,bhjv->bhdv', kc * jnp.exp(b_last - b), vc))
        return S_new.astype(S.dtype), o.astype(vc.dtype)

    S_fin, o = lax.scan(step, s0, (to_chunks(q), to_chunks(k), to_chunks(v), to_chunks(g)))
    o = o.transpose(1, 0, 3, 2, 4).reshape(B, L + pad, H, DV)[:, pad:]
    return o, S_fin


def gla_layer(x, s0, norm_w, w_in, w_gk2, b_gk2, onorm, w_out, chunk):
    B, L, _ = x.shape
    h = rmsnorm(x, norm_w)
    proj = h @ w_in
    q, k, v, gate, glr = jnp.split(
        proj, [GLA_KEY_DIM, 2 * GLA_KEY_DIM, 2 * GLA_KEY_DIM + GLA_VAL_DIM,
               2 * GLA_KEY_DIM + 2 * GLA_VAL_DIM], axis=-1)
    g = (jax.nn.log_sigmoid((glr @ w_gk2 + b_gk2).astype(jnp.float32))
         / GATE_LOGIT_NORM).astype(x.dtype)
    q = q.reshape(B, L, GLA_HEADS, GLA_DK) * (GLA_DK ** -0.5)
    k = k.reshape(B, L, GLA_HEADS, GLA_DK)
    v = v.reshape(B, L, GLA_HEADS, GLA_DV)
    g = g.reshape(B, L, GLA_HEADS, GLA_DK)
    o, S = gla_recurrence(q, k, v, g, s0, chunk)
    o = rmsnorm(o, onorm).reshape(B, L, GLA_VAL_DIM) * jax.nn.silu(gate)
    return x + o @ w_out, S


def shared_kv(x, norm_kv, w_kv):
    B, L, _ = x.shape
    kv = rmsnorm(x, norm_kv) @ w_kv
    k, v = jnp.split(kv, [KV_WIDTH], axis=-1)
    return (k.reshape(B, L, N_KV_HEADS, HEAD_DIM), v.reshape(B, L, N_KV_HEADS, HEAD_DIM))


def window_mask(qpos, kpos):
    return (kpos >= 0) & (kpos <= qpos) & (qpos - kpos < WINDOW)


def sink_softmax(s, sinks):
    col = jnp.broadcast_to(
        sinks.reshape(N_KV_HEADS, GROUP, 1).astype(jnp.float32), s.shape[:-1])[..., None]
    return jax.nn.softmax(jnp.concatenate([s, col], axis=-1), axis=-1)[..., :-1]


def swa_banded(q, k, v, sinks):
    B, L = q.shape[:2]
    pad = (-L) % BLOCK
    widths = ((0, 0), (pad, 0), (0, 0), (0, 0))
    q, k, v = (jnp.pad(t, widths) for t in (q, k, v))
    Lp = L + pad
    nb = Lp // BLOCK
    qb = q.reshape(B, nb, BLOCK, N_KV_HEADS, GROUP, HEAD_DIM)
    kb = k.reshape(B, nb, BLOCK, N_KV_HEADS, HEAD_DIM)
    vb = v.reshape(B, nb, BLOCK, N_KV_HEADS, HEAD_DIM)

    def with_prev(t):
        prev = jnp.concatenate([jnp.zeros_like(t[:, :1]), t[:, :-1]], axis=1)
        return jnp.concatenate([prev, t], axis=2)

    kc, vc = with_prev(kb), with_prev(vb)
    pos = jnp.arange(Lp, dtype=jnp.int32).reshape(nb, BLOCK) - pad
    kpos = jnp.concatenate([pos - BLOCK, pos], axis=1)
    mask = window_mask(pos[:, :, None], kpos[:, None, :])
    s = jnp.einsum('bnqhgd,bnkhd->bnhgqk', qb, kc).astype(jnp.float32) * (HEAD_DIM ** -0.5)
    s = jnp.where(mask[None, :, None, None], s, NEG_INF)
    p = sink_softmax(s, sinks).astype(v.dtype)
    o = jnp.einsum('bnhgqk,bnkhd->bnqhgd', p, vc)
    return o.reshape(B, Lp, N_Q_HEADS, HEAD_DIM)[:, pad:]


def swa_decode(q, k_all, v_all, sinks):
    Bd, T = q.shape[:2]
    Wc = k_all.shape[1] - T
    qg = q.reshape(Bd, T, N_KV_HEADS, GROUP, HEAD_DIM)
    qpos = PAST_LEN + jnp.arange(T, dtype=jnp.int32)
    kpos = jnp.concatenate([PAST_LEN - Wc + jnp.arange(Wc, dtype=jnp.int32), qpos])
    mask = window_mask(qpos[:, None], kpos[None, :])
    s = jnp.einsum('bqhgd,bkhd->bhgqk', qg, k_all).astype(jnp.float32) * (HEAD_DIM ** -0.5)
    s = jnp.where(mask, s, NEG_INF)
    p = sink_softmax(s, sinks).astype(v_all.dtype)
    o = jnp.einsum('bhgqk,bkhd->bqhgd', p, v_all)
    return o.reshape(Bd, T, N_Q_HEADS, HEAD_DIM)


def swa_layer(x, k, v, norm_w, w_in, sinks, w_out, attend):
    B, L, _ = x.shape
    h = rmsnorm(x, norm_w)
    q, gate = jnp.split(h @ w_in, [Q_WIDTH], axis=-1)
    o = attend(q.reshape(B, L, N_Q_HEADS, HEAD_DIM), k, v, sinks).reshape(B, L, Q_WIDTH)
    return x + (o * jax.nn.silu(gate)) @ w_out


def trunk(x, gla_s0, gla_chunk, attend, norm_a, w_in_a, w_gk2_a, b_gk2_a, onorm_a,
          w_out_a, norm_kv, w_kv, norm_b, w_in_b, sinks_b, w_out_b, norm_f):
    finals = []
    k_sh = None
    v_sh = None
    for layer in range(DEPTH):
        if layer < N_A_LAYERS:
            i = layer
            x, s = gla_layer(x, gla_s0[i], norm_a[i], w_in_a[i], w_gk2_a[i], b_gk2_a[i],
                             onorm_a[i], w_out_a[i], gla_chunk)
            finals.append(s)
            if layer == N_A_LAYERS - 1:
                k_sh, v_sh = shared_kv(x, norm_kv, w_kv)
        else:
            j = layer - N_A_LAYERS
            x = swa_layer(x, k_sh, v_sh, norm_b[j], w_in_b[j], sinks_b[j], w_out_b[j], attend)
    return rmsnorm(x, norm_f), jnp.stack(finals), k_sh, v_sh


def setup_inputs(seed: int = 0) -> dict:
    key = jax.random.key(seed)
    ks = jax.random.split(key, 20)

    def nrm(k, shape, scale):
        return jax.random.normal(k, shape, jnp.float32) * scale

    win = min(WINDOW, PAST_LEN)
    return {
        'x_prompt': nrm(ks[0], (BATCH, SEQ, D_MODEL), 1.0),
        'x_sample': nrm(ks[1], (DEC_BATCH, DEC_SEQ, D_MODEL), 1.0),
        'state_gla': nrm(ks[2], (N_A_LAYERS, DEC_BATCH, GLA_HEADS, GLA_DK, GLA_DV), 1.0),
        'cache_k_win': nrm(ks[3], (DEC_BATCH, win, N_KV_HEADS, HEAD_DIM), 1.0),
        'cache_v_win': nrm(ks[4], (DEC_BATCH, win, N_KV_HEADS, HEAD_DIM), 1.0),
        'meta_tokens': nrm(ks[5], (N_META, D_MODEL), 1.0),
        'norm_a': 1.0 + nrm(ks[6], (N_A_LAYERS, D_MODEL), 0.02),
        'w_in_a': nrm(ks[7], (N_A_LAYERS, D_MODEL, GLA_IN_WIDTH), D_MODEL ** -0.5),
        'w_gk2_a': nrm(ks[8], (N_A_LAYERS, GATE_RANK, GLA_KEY_DIM), GATE_RANK ** -0.5),
        'b_gk2_a': nrm(ks[9], (N_A_LAYERS, GLA_KEY_DIM), 0.02),
        'onorm_a': 1.0 + nrm(ks[10], (N_A_LAYERS, GLA_DV), 0.02),
        'w_out_a': nrm(ks[11], (N_A_LAYERS, GLA_VAL_DIM, D_MODEL), GLA_VAL_DIM ** -0.5),
        'norm_kv': 1.0 + nrm(ks[12], (D_MODEL,), 0.02),
        'w_kv': nrm(ks[13], (D_MODEL, 2 * KV_WIDTH), D_MODEL ** -0.5),
        'norm_b': 1.0 + nrm(ks[14], (N_B_LAYERS, D_MODEL), 0.02),
        'w_in_b': nrm(ks[15], (N_B_LAYERS, D_MODEL, 2 * Q_WIDTH), D_MODEL ** -0.5),
        'sinks_b': nrm(ks[16], (N_B_LAYERS, N_Q_HEADS), 1.0),
        'w_out_b': nrm(ks[17], (N_B_LAYERS, Q_WIDTH, D_MODEL), Q_WIDTH ** -0.5),
        'norm_f': 1.0 + nrm(ks[18], (D_MODEL,), 0.02),
    }


def reference(x_prompt, x_sample, state_gla, cache_k_win, cache_v_win, meta_tokens,
              norm_a, w_in_a, w_gk2_a, b_gk2_a, onorm_a, w_out_a, norm_kv, w_kv,
              norm_b, w_in_b, sinks_b, w_out_b, norm_f):
    win = cache_k_win.shape[1]
    B = x_prompt.shape[0]
    meta = jnp.broadcast_to(meta_tokens[None].astype(x_prompt.dtype), (B, N_META, D_MODEL))
    xp = jnp.concatenate([meta, x_prompt], axis=1)
    s0_p = jnp.zeros((N_A_LAYERS, B, GLA_HEADS, GLA_DK, GLA_DV), x_prompt.dtype)
    yp, sgla_p, k_p, v_p = trunk(
        xp, s0_p, GLA_CHUNK, swa_banded, norm_a, w_in_a, w_gk2_a, b_gk2_a, onorm_a,
        w_out_a, norm_kv, w_kv, norm_b, w_in_b, sinks_b, w_out_b, norm_f)
    y_prompt = yp[:, N_META:]
    k_win_p = k_p[:, -win:]
    v_win_p = v_p[:, -win:]

    def attend_sample(q, k, v, sinks):
        return swa_decode(q, jnp.concatenate([cache_k_win, k], axis=1),
                          jnp.concatenate([cache_v_win, v], axis=1), sinks)

    y_sample, sgla_s, k_s, v_s = trunk(
        x_sample, state_gla, x_sample.shape[1], attend_sample, norm_a, w_in_a, w_gk2_a,
        b_gk2_a, onorm_a, w_out_a, norm_kv, w_kv, norm_b, w_in_b, sinks_b, w_out_b, norm_f)
    k_win_s = jnp.concatenate([cache_k_win, k_s], axis=1)[:, -win:]
    v_win_s = jnp.concatenate([cache_v_win, v_s], axis=1)[:, -win:]
    return (y_prompt, y_sample, sgla_p, sgla_s, k_win_p, v_win_p, k_win_s, v_win_s)
```

```python
import functools

import jax
import jax.numpy as jnp
from jax import lax
from jax.experimental import pallas as pl
from jax.experimental.pallas import tpu as pltpu

F32 = jnp.float32
BF16 = jnp.bfloat16

D_MODEL = 2048
N_META = 16
GLA_HEADS = 4
GLA_DK = 256
GLA_DV = 512
GLA_KEY_DIM = GLA_HEADS * GLA_DK
GLA_VAL_DIM = GLA_HEADS * GLA_DV
GLA_MAIN_WIDTH = 2 * GLA_KEY_DIM + 2 * GLA_VAL_DIM
GATE_RANK = 16
GATE_LOGIT_NORM = 16.0
HEAD_DIM = 64
N_Q_HEADS = 32
N_KV_HEADS = 8
Q_WIDTH = N_Q_HEADS * HEAD_DIM
KV_WIDTH = N_KV_HEADS * HEAD_DIM
WINDOW = 128
RMS_EPS = 1e-6
NEG_INF = -1e30

LANES = 128
CHUNK = 128
GLA_LEAF = 16
ROW_TILE = 512
COL_TILE = 512
SEQ_GROUP = 4
VMEM_LIMIT = 48 * 1024 * 1024

NT_DIMS = (((1,), (1,)), ((), ()))


def _silu(x):
    return x * (1.0 / (1.0 + jnp.exp(-x)))


def _params(sem):
    return pltpu.CompilerParams(dimension_semantics=sem, vmem_limit_bytes=VMEM_LIMIT)


def _normed(x_ref, nw_ref):
    x = x_ref[...]
    ms = jnp.mean(x * x, axis=-1, keepdims=True)
    return (x * lax.rsqrt(ms + RMS_EPS) * nw_ref[...]).astype(BF16)


def _norm_matmul_kernel(x_ref, nw_ref, w_ref, o_ref, hn_ref):
    @pl.when(pl.program_id(1) == 0)
    def _():
        hn_ref[...] = _normed(x_ref, nw_ref)

    o_ref[...] = jnp.dot(hn_ref[...], w_ref[...],
                         preferred_element_type=F32).astype(o_ref.dtype)


def _norm_matmul_gla_kernel(x_ref, nw_ref, w_ref, wg_ref, wgk_ref, bgk_ref,
                            o_ref, g_ref, hn_ref):
    @pl.when(pl.program_id(1) == 0)
    def _():
        hn = _normed(x_ref, nw_ref)
        hn_ref[...] = hn
        glr = jnp.dot(hn, wg_ref[...], preferred_element_type=F32)
        z = jnp.dot(glr.astype(BF16), wgk_ref[...], preferred_element_type=F32) + bgk_ref[...]
        log_sig = jnp.minimum(z, 0.0) - jnp.log1p(jnp.exp(-jnp.abs(z)))
        g_ref[...] = log_sig * (1.0 / GATE_LOGIT_NORM)

    o_ref[...] = jnp.dot(hn_ref[...], w_ref[...],
                         preferred_element_type=F32).astype(o_ref.dtype)


def _norm_matmul(x, nw, w, out_dtype, name):
    rows, n = x.shape[0], w.shape[1]
    return pl.pallas_call(
        _norm_matmul_kernel,
        grid=(rows // ROW_TILE, n // COL_TILE),
        in_specs=[
            pl.BlockSpec((ROW_TILE, D_MODEL), lambda i, j: (i, 0)),
            pl.BlockSpec((1, D_MODEL), lambda i, j: (0, 0)),
            pl.BlockSpec((D_MODEL, COL_TILE), lambda i, j: (0, j)),
        ],
        out_specs=pl.BlockSpec((ROW_TILE, COL_TILE), lambda i, j: (i, j)),
        out_shape=jax.ShapeDtypeStruct((rows, n), out_dtype),
        scratch_shapes=[pltpu.VMEM((ROW_TILE, D_MODEL), BF16)],
        compiler_params=_params(("parallel", "arbitrary")),
        name=name,
    )(x, nw, w)


def _norm_matmul_gla(x, nw, w, wg, wgk, bgk, name):
    rows, n = x.shape[0], w.shape[1]
    return pl.pallas_call(
        _norm_matmul_gla_kernel,
        grid=(rows // ROW_TILE, n // COL_TILE),
        in_specs=[
            pl.BlockSpec((ROW_TILE, D_MODEL), lambda i, j: (i, 0)),
            pl.BlockSpec((1, D_MODEL), lambda i, j: (0, 0)),
            pl.BlockSpec((D_MODEL, COL_TILE), lambda i, j: (0, j)),
            pl.BlockSpec((D_MODEL, LANES), lambda i, j: (0, 0)),
            pl.BlockSpec((LANES, GLA_KEY_DIM), lambda i, j: (0, 0)),
            pl.BlockSpec((1, GLA_KEY_DIM), lambda i, j: (0, 0)),
        ],
        out_specs=[
            pl.BlockSpec((ROW_TILE, COL_TILE), lambda i, j: (i, j)),
            pl.BlockSpec((ROW_TILE, GLA_KEY_DIM), lambda i, j: (i, 0)),
        ],
        out_shape=[
            jax.ShapeDtypeStruct((rows, n), BF16),
            jax.ShapeDtypeStruct((rows, GLA_KEY_DIM), F32),
        ],
        scratch_shapes=[pltpu.VMEM((ROW_TILE, D_MODEL), BF16)],
        compiler_params=_params(("parallel", "arbitrary")),
        name=name,
    )(x, nw, w, wg, wgk, bgk)


def _out_matmul_kernel(og_ref, w_ref, x_ref, o_ref):
    o_ref[...] = x_ref[...] + jnp.dot(og_ref[...], w_ref[...], preferred_element_type=F32)


def _out_matmul_norm_kernel(og_ref, w_ref, x_ref, nf_ref, o_ref):
    y = x_ref[...] + jnp.dot(og_ref[...], w_ref[...], preferred_element_type=F32)
    ms = jnp.mean(y * y, axis=-1, keepdims=True)
    o_ref[...] = y * lax.rsqrt(ms + RMS_EPS) * nf_ref[...]


def _out_matmul(og, w, x, norm_f, name):
    rows = x.shape[0]
    in_specs = [
        pl.BlockSpec((ROW_TILE, D_MODEL), lambda i: (i, 0)),
        pl.BlockSpec((D_MODEL, D_MODEL), lambda i: (0, 0)),
        pl.BlockSpec((ROW_TILE, D_MODEL), lambda i: (i, 0)),
    ]
    args = [og, w, x]
    body = _out_matmul_kernel
    if norm_f is not None:
        in_specs.append(pl.BlockSpec((1, D_MODEL), lambda i: (0, 0)))
        args.append(norm_f)
        body = _out_matmul_norm_kernel
    return pl.pallas_call(
        body,
        grid=(rows // ROW_TILE,),
        in_specs=in_specs,
        out_specs=pl.BlockSpec((ROW_TILE, D_MODEL), lambda i: (i, 0)),
        out_shape=jax.ShapeDtypeStruct((rows, D_MODEL), F32),
        compiler_params=_params(("parallel",)),
        name=name,
    )(*args)


def _exact_block(qs, k, b, n, width, col0):
    lane = lax.broadcasted_iota(jnp.int32, (n, width), 1)
    acc = jnp.zeros((n, width), F32)
    for j in range(n):
        t = qs * k[j:j + 1] * jnp.exp(jnp.minimum(b - b[j:j + 1], 0.0))
        acc = jnp.where(lane == col0 + j, jnp.sum(t, axis=-1, keepdims=True), acc)
    return acc


def _column_of(row_vec, n):
    return jnp.broadcast_to(row_vec, (LANES, n)).T[:, 0:1]


def _head_norm_gate(o, onorm, gate):
    ms = jnp.mean(o * o, axis=-1, keepdims=True)
    return (o * lax.rsqrt(ms + RMS_EPS) * onorm * _silu(gate)).astype(BF16)


def _gla_prompt_kernel(q_ref, k_ref, v_ref, gate_ref, g_ref, onorm_ref, og_in_ref,
                       og_ref, s_ref, b_scr, att_scr):
    del og_in_ref
    c_len = CHUNK

    @pl.when(pl.program_id(1) == 0)
    def _():
        s_ref[...] = jnp.zeros_like(s_ref)

    row = lax.broadcasted_iota(jnp.int32, (c_len, c_len), 0)
    col = lax.broadcasted_iota(jnp.int32, (c_len, c_len), 1)
    tri = (row >= col).astype(F32)
    row_xor_col = row ^ col
    leaf_row = lax.broadcasted_iota(jnp.int32, (GLA_LEAF, c_len), 0)
    leaf_col = lax.broadcasted_iota(jnp.int32, (GLA_LEAF, c_len), 1)
    onorm = onorm_ref[...]

    for h in range(GLA_HEADS):
        ks = slice(h * GLA_DK, (h + 1) * GLA_DK)
        vs = slice(h * GLA_DV, (h + 1) * GLA_DV)
        qs = q_ref[:, ks].astype(F32) * (GLA_DK ** -0.5)
        k = k_ref[:, ks].astype(F32)
        v = v_ref[:, vs]
        b = jnp.dot(tri, g_ref[:, ks], preferred_element_type=F32,
                    precision=lax.Precision.HIGHEST)
        b_scr[...] = b
        state = s_ref[0, h]

        def leaf_body(blk, carry):
            base = pl.multiple_of(blk * GLA_LEAF, GLA_LEAF)
            rows = pl.ds(base, GLA_LEAF)
            q_blk = q_ref[rows, ks].astype(F32) * (GLA_DK ** -0.5)
            k_blk = k_ref[rows, ks].astype(F32)
            b_blk = b_scr[rows, :]
            acc = jnp.zeros((GLA_LEAF, c_len), F32)
            for j in range(GLA_LEAF):
                t = q_blk * k_blk[j:j + 1] * jnp.exp(
                    jnp.minimum(b_blk - b_blk[j:j + 1], 0.0))
                acc = jnp.where(leaf_col == base + j,
                                jnp.sum(t, axis=-1, keepdims=True), acc)
            att_scr[rows, :] = jnp.where(leaf_col - base <= leaf_row, acc, 0.0)
            return carry

        lax.fori_loop(0, c_len // GLA_LEAF, leaf_body, 0)
        att = att_scr[...]

        m = c_len // 2
        while m >= GLA_LEAF:
            q_parts, k_parts = [], []
            zeros = jnp.zeros((m, GLA_DK), F32)
            for base in range(0, c_len, 2 * m):
                ref = b[base + m - 1:base + m]
                lo = slice(base, base + m)
                hi = slice(base + m, base + 2 * m)
                k_parts += [k[lo] * jnp.exp(ref - b[lo]), zeros]
                q_parts += [zeros, qs[hi] * jnp.exp(b[hi] - ref)]
            qt = jnp.concatenate(q_parts, axis=0).astype(BF16)
            kt = jnp.concatenate(k_parts, axis=0).astype(BF16)
            a = lax.dot_general(qt, kt, NT_DIMS, preferred_element_type=F32)
            att = att + jnp.where(row_xor_col < 2 * m, a, 0.0)
            m //= 2

        o = jnp.dot((qs * jnp.exp(b)).astype(BF16), state.astype(BF16),
                    preferred_element_type=F32)
        o = o + jnp.dot(att.astype(BF16), v, preferred_element_type=F32)

        b_last = b[c_len - 1:c_len]
        kd_t = (k * jnp.exp(b_last - b)).T.astype(BF16)
        s_ref[0, h] = (state * _column_of(jnp.exp(b_last), GLA_DK)
                       + jnp.dot(kd_t, v, preferred_element_type=F32))

        og_ref[:, vs] = _head_norm_gate(o, onorm, gate_ref[:, vs].astype(F32))


def _gla_prompt(proj, g, onorm, og, n_batch, chunks):
    rows = proj.shape[0]
    blk = lambda col: (lambda b, c: (b * chunks + c, col))
    return pl.pallas_call(
        _gla_prompt_kernel,
        grid=(n_batch, chunks),
        in_specs=[
            pl.BlockSpec((CHUNK, GLA_KEY_DIM), blk(0)),
            pl.BlockSpec((CHUNK, GLA_KEY_DIM), blk(1)),
            pl.BlockSpec((CHUNK, GLA_VAL_DIM), blk(1)),
            pl.BlockSpec((CHUNK, GLA_VAL_DIM), blk(2)),
            pl.BlockSpec((CHUNK, GLA_KEY_DIM), blk(0)),
            pl.BlockSpec((1, GLA_DV), lambda b, c: (0, 0)),
            pl.BlockSpec(memory_space=pl.ANY),
        ],
        out_specs=[
            pl.BlockSpec((CHUNK, GLA_VAL_DIM), blk(0)),
            pl.BlockSpec((1, GLA_HEADS, GLA_DK, GLA_DV), lambda b, c: (b, 0, 0, 0)),
        ],
        out_shape=[
            jax.ShapeDtypeStruct((rows, GLA_VAL_DIM), BF16),
            jax.ShapeDtypeStruct((n_batch, GLA_HEADS, GLA_DK, GLA_DV), F32),
        ],
        scratch_shapes=[pltpu.VMEM((CHUNK, GLA_DK), F32), pltpu.VMEM((CHUNK, CHUNK), F32)],
        input_output_aliases={6: 0},
        compiler_params=_params(("parallel", "arbitrary")),
        name="gla_prompt",
    )(proj, proj, proj, proj, g, onorm, og)


def _gla_sample_kernel(t_len, q_ref, k_ref, v_ref, gate_ref, g_ref, onorm_ref, s0_ref,
                       og_in_ref, *rest):
    og_ref, s_ref = rest[-2], rest[-1]
    n = SEQ_GROUP * t_len
    row = lax.broadcasted_iota(jnp.int32, (n, LANES), 0)
    col = lax.broadcasted_iota(jnp.int32, (n, LANES), 1)
    causal = ((row // t_len) == (col // t_len)) & (row >= col)
    row_seq = lax.broadcasted_iota(jnp.int32, (n, 1), 0) // t_len

    qs = q_ref[...].astype(F32) * (GLA_DK ** -0.5)
    k = k_ref[...].astype(F32)
    pad_k = jnp.zeros((LANES - n, GLA_DK), F32)
    v_pad = jnp.concatenate([v_ref[...], jnp.zeros((LANES - n, GLA_DV), BF16)], axis=0)
    b = jnp.dot(causal.astype(F32), jnp.concatenate([g_ref[...], pad_k], axis=0),
                preferred_element_type=F32, precision=lax.Precision.HIGHEST)
    att = jnp.where(causal, _exact_block(qs, k, b, n, LANES, 0), 0.0)
    o = jnp.dot(att.astype(BF16), v_pad, preferred_element_type=F32)

    q_dec = (qs * jnp.exp(b)).astype(BF16)
    for u in range(SEQ_GROUP):
        state = s0_ref[0, u, 0]
        o_u = jnp.dot(q_dec, state.astype(BF16), preferred_element_type=F32)
        o = o + jnp.where(row_seq == u, o_u, 0.0)
        b_last = b[(u + 1) * t_len - 1:(u + 1) * t_len]
        kd = jnp.where(row_seq == u, k * jnp.exp(jnp.minimum(b_last - b, 0.0)), 0.0)
        kd_t = jnp.concatenate([kd, pad_k], axis=0).T.astype(BF16)
        s_ref[0, u, 0] = (state * _column_of(jnp.exp(b_last), GLA_DK)
                          + jnp.dot(kd_t, v_pad, preferred_element_type=F32))

    og_ref[...] = _head_norm_gate(o, onorm_ref[...], gate_ref[...].astype(F32))


def _gla_sample(proj, g, onorm, state_in, og, s_out, layer, n_seq, t_len, row0):
    n = SEQ_GROUP * t_len
    rb0 = row0 // n
    n_layers = state_in.shape[0]
    key_blocks = GLA_KEY_DIM // GLA_DK
    val_blocks = GLA_VAL_DIM // GLA_DV
    st_spec = pl.BlockSpec((1, SEQ_GROUP, 1, GLA_DK, GLA_DV),
                           lambda s, h: (layer, s, h, 0, 0))
    in_specs = [
        pl.BlockSpec((n, GLA_DK), lambda s, h: (rb0 + s, h)),
        pl.BlockSpec((n, GLA_DK), lambda s, h: (rb0 + s, key_blocks + h)),
        pl.BlockSpec((n, GLA_DV), lambda s, h: (rb0 + s, val_blocks + h)),
        pl.BlockSpec((n, GLA_DV), lambda s, h: (rb0 + s, 2 * val_blocks + h)),
        pl.BlockSpec((n, GLA_DK), lambda s, h: (rb0 + s, h)),
        pl.BlockSpec((1, GLA_DV), lambda s, h: (0, 0)),
        st_spec,
        pl.BlockSpec(memory_space=pl.ANY),
    ]
    args = [proj, proj, proj, proj, g, onorm, state_in, og]
    aliases = {7: 0}
    if s_out is not None:
        in_specs.append(pl.BlockSpec(memory_space=pl.ANY))
        args.append(s_out)
        aliases[8] = 1
    return pl.pallas_call(
        functools.partial(_gla_sample_kernel, t_len),
        grid=(n_seq // SEQ_GROUP, GLA_HEADS),
        in_specs=in_specs,
        out_specs=[
            pl.BlockSpec((n, GLA_DV), lambda s, h: (rb0 + s, h)),
            st_spec,
        ],
        out_shape=[
            jax.ShapeDtypeStruct(og.shape, BF16),
            jax.ShapeDtypeStruct((n_layers, n_seq, GLA_HEADS, GLA_DK, GLA_DV), F32),
        ],
        input_output_aliases=aliases,
        compiler_params=_params(("parallel", "parallel")),
        name="gla_sample",
    )(*args)


def _head_pair_attention(q2, ka, kb, va, vb, mask, sink_a, sink_b):
    out = None
    for k_pad, v_pad, sink in ((ka, va, sink_a), (kb, vb, sink_b)):
        s = lax.dot_general(q2, k_pad, NT_DIMS, preferred_element_type=F32)
        s = jnp.where(mask, s, NEG_INF)
        m = jnp.maximum(jnp.max(s, axis=-1, keepdims=True), sink)
        p = jnp.exp(s - m)
        denom = jnp.sum(p, axis=-1, keepdims=True) + jnp.exp(sink - m)
        o = jnp.dot((p / denom).astype(BF16), v_pad, preferred_element_type=F32)
        out = o if out is None else out + o
    return out


def _padded_heads(kc, parity, low):
    kr = pltpu.roll(kc, HEAD_DIM, axis=1)
    lo_src, hi_src = (kc, kr) if parity == 0 else (kr, kc)
    return (jnp.where(low, lo_src, 0.0).astype(BF16),
            jnp.where(low, 0.0, hi_src).astype(BF16))


def _attend_rows(q_ref, gate_ref, og_ref, sink_ref, kv_pairs, mask_for):
    n_rows = q_ref.shape[0]
    group = N_Q_HEADS // N_KV_HEADS
    out_cols = [None] * (Q_WIDTH // LANES)
    for keys, vals, mask in kv_pairs:
        low = lax.broadcasted_iota(jnp.int32, keys[0].shape, 1) < HEAD_DIM
        for cpair in range(KV_WIDTH // LANES):
            for parity in range(2):
                kvh = 2 * cpair + parity
                ka, kb = _padded_heads(keys[cpair], parity, low)
                va, vb = _padded_heads(vals[cpair], parity, low)
                for pr in range(group // 2):
                    qc = kvh * (group // 2) + pr
                    q2 = (q_ref[:, qc * LANES:(qc + 1) * LANES].astype(F32)
                          * (HEAD_DIM ** -0.5)).astype(BF16)
                    o = _head_pair_attention(q2, ka, kb, va, vb, mask,
                                             sink_ref[2 * qc], sink_ref[2 * qc + 1])
                    out_cols[qc] = o if out_cols[qc] is None else out_cols[qc] + o
    del n_rows, mask_for
    for qc, o in enumerate(out_cols):
        cols = slice(qc * LANES, (qc + 1) * LANES)
        og_ref[:, cols] = (o * _silu(gate_ref[:, cols].astype(F32))).astype(BF16)


def _swa_prompt_kernel(front, sink_ref, q_ref, gate_ref, kc_ref, kp_ref, vc_ref, vp_ref,
                       og_in_ref, og_ref):
    del og_in_ref
    blk = pl.program_id(1)
    row = lax.broadcasted_iota(jnp.int32, (CHUNK, 2 * CHUNK), 0)
    col = lax.broadcasted_iota(jnp.int32, (CHUNK, 2 * CHUNK), 1)
    mask = (col > row) & (col <= row + WINDOW) & ((blk - 1) * CHUNK + col >= front)
    keys = [jnp.concatenate([kp_ref[:, c * LANES:(c + 1) * LANES],
                             kc_ref[:, c * LANES:(c + 1) * LANES]], axis=0)
            for c in range(KV_WIDTH // LANES)]
    vals = [jnp.concatenate([vp_ref[:, c * LANES:(c + 1) * LANES],
                             vc_ref[:, c * LANES:(c + 1) * LANES]], axis=0)
            for c in range(KV_WIDTH // LANES)]
    _attend_rows(q_ref, gate_ref, og_ref, sink_ref, [(keys, vals, mask)], None)


def _swa_prompt(pq, kv, sinks, og, n_batch, chunks, front):
    cur = lambda col: (lambda b, i: (b * chunks + i, col))
    prev = lambda col: (lambda b, i: (b * chunks + jnp.maximum(i - 1, 0), col))
    return pl.pallas_call(
        functools.partial(_swa_prompt_kernel, front),
        grid=(n_batch, chunks),
        in_specs=[
            pl.BlockSpec(memory_space=pltpu.SMEM),
            pl.BlockSpec((CHUNK, Q_WIDTH), cur(0)),
            pl.BlockSpec((CHUNK, Q_WIDTH), cur(1)),
            pl.BlockSpec((CHUNK, KV_WIDTH), cur(0)),
            pl.BlockSpec((CHUNK, KV_WIDTH), prev(0)),
            pl.BlockSpec((CHUNK, KV_WIDTH), cur(1)),
            pl.BlockSpec((CHUNK, KV_WIDTH), prev(1)),
            pl.BlockSpec(memory_space=pl.ANY),
        ],
        out_specs=pl.BlockSpec((CHUNK, Q_WIDTH), cur(0)),
        out_shape=jax.ShapeDtypeStruct(og.shape, BF16),
        input_output_aliases={7: 0},
        compiler_params=_params(("parallel", "parallel")),
        name="swa_prompt",
    )(sinks, pq, pq, kv, kv, kv, kv, og)


def _swa_sample_kernel(t_len, sink_ref, q_ref, gate_ref, kn_ref, vn_ref, ck_ref, cv_ref,
                       og_in_ref, og_ref):
    del og_in_ref
    n = SEQ_GROUP * t_len
    win = ck_ref.shape[1]
    row = lax.broadcasted_iota(jnp.int32, (n, 2 * win), 0)
    col = lax.broadcasted_iota(jnp.int32, (n, 2 * win), 1)
    t = row % t_len
    new = col - win
    in_window = ((col < win) & (col > t)) | (
        (new >= 0) & (new < n) & ((new // t_len) == (row // t_len)) & ((new % t_len) <= t))
    pad = jnp.zeros((win - n, LANES), F32)
    kv_pairs = []
    for u in range(SEQ_GROUP):
        keys = [jnp.concatenate([ck_ref[u, :, c * LANES:(c + 1) * LANES],
                                 kn_ref[:, c * LANES:(c + 1) * LANES], pad], axis=0)
                for c in range(KV_WIDTH // LANES)]
        vals = [jnp.concatenate([cv_ref[u, :, c * LANES:(c + 1) * LANES],
                                 vn_ref[:, c * LANES:(c + 1) * LANES], pad], axis=0)
                for c in range(KV_WIDTH // LANES)]
        kv_pairs.append((keys, vals, in_window & ((row // t_len) == u)))
    _attend_rows(q_ref, gate_ref, og_ref, sink_ref, kv_pairs, None)


def _swa_sample(pq, kv, cache_k, cache_v, sinks, og, n_seq, t_len, row0):
    n = SEQ_GROUP * t_len
    rb0 = row0 // n
    win = cache_k.shape[1]
    return pl.pallas_call(
        functools.partial(_swa_sample_kernel, t_len),
        grid=(n_seq // SEQ_GROUP,),
        in_specs=[
            pl.BlockSpec(memory_space=pltpu.SMEM),
            pl.BlockSpec((n, Q_WIDTH), lambda s: (rb0 + s, 0)),
            pl.BlockSpec((n, Q_WIDTH), lambda s: (rb0 + s, 1)),
            pl.BlockSpec((n, KV_WIDTH), lambda s: (rb0 + s, 0)),
            pl.BlockSpec((n, KV_WIDTH), lambda s: (rb0 + s, 1)),
            pl.BlockSpec((SEQ_GROUP, win, KV_WIDTH), lambda s: (s, 0, 0)),
            pl.BlockSpec((SEQ_GROUP, win, KV_WIDTH), lambda s: (s, 0, 0)),
            pl.BlockSpec(memory_space=pl.ANY),
        ],
        out_specs=pl.BlockSpec((n, Q_WIDTH), lambda s: (rb0 + s, 0)),
        out_shape=jax.ShapeDtypeStruct(og.shape, BF16),
        input_output_aliases={7: 0},
        compiler_params=_params(("parallel",)),
        name="swa_sample",
    )(sinks, pq, pq, kv, kv, cache_k, cache_v, og)


def kernel(x_prompt, x_sample, state_gla, cache_k_win, cache_v_win, meta_tokens, norm_a, w_in_a, w_gk2_a, b_gk2_a, onorm_a, w_out_a, norm_kv, w_kv, norm_b, w_in_b, sinks_b, w_out_b, norm_f):
    n_batch, seq, _ = x_prompt.shape
    n_seq, t_len, _ = x_sample.shape
    win = cache_k_win.shape[1]
    n_a = w_in_a.shape[0]
    n_b = w_in_b.shape[0]
    assert win == WINDOW == CHUNK and (n_seq * t_len) % (SEQ_GROUP * t_len) == 0

    length = seq + N_META
    front = (-length) % CHUNK
    lp = length + front
    chunks = lp // CHUNK
    srow = n_batch * lp
    ns = n_seq * t_len
    rows = -(-(srow + ns) // ROW_TILE) * ROW_TILE

    lead = jnp.concatenate([jnp.zeros((front, D_MODEL), F32), meta_tokens.astype(F32)], axis=0)
    pieces = []
    for b in range(n_batch):
        pieces += [lead, x_prompt[b]]
    pieces += [x_sample.reshape(ns, D_MODEL), jnp.zeros((rows - srow - ns, D_MODEL), F32)]
    x = jnp.concatenate(pieces, axis=0)
    og = jnp.zeros((rows, D_MODEL), BF16)

    states_p = []
    states_s = None
    for l in range(n_a):
        w_main = w_in_a[l][:, :GLA_MAIN_WIDTH].astype(BF16)
        w_gate = jnp.pad(w_in_a[l][:, GLA_MAIN_WIDTH:],
                         ((0, 0), (0, LANES - GATE_RANK))).astype(BF16)
        w_gk = jnp.pad(w_gk2_a[l], ((0, LANES - GATE_RANK), (0, 0))).astype(BF16)
        proj, g = _norm_matmul_gla(x, norm_a[l][None], w_main, w_gate, w_gk,
                                   b_gk2_a[l][None], "gla_in_proj")
        onorm = onorm_a[l][None]
        og, s_p = _gla_prompt(proj, g, onorm, og, n_batch, chunks)
        og, states_s = _gla_sample(proj, g, onorm, state_gla, og, states_s, l,
                                   n_seq, t_len, srow)
        states_p.append(s_p)
        x = _out_matmul(og, w_out_a[l].astype(BF16), x, None, "gla_out_proj")

    kv = _norm_matmul(x, norm_kv[None], w_kv.astype(BF16), F32, "kv_proj")
    cache_k = cache_k_win.reshape(n_seq, win, KV_WIDTH)
    cache_v = cache_v_win.reshape(n_seq, win, KV_WIDTH)
    for j in range(n_b):
        pq = _norm_matmul(x, norm_b[j][None], w_in_b[j].astype(BF16), BF16, "swa_in_proj")
        og = _swa_prompt(pq, kv, sinks_b[j], og, n_batch, chunks, front)
        og = _swa_sample(pq, kv, cache_k, cache_v, sinks_b[j], og, n_seq, t_len, srow)
        x = _out_matmul(og, w_out_b[j].astype(BF16), x,
                        norm_f[None] if j == n_b - 1 else None, "swa_out_proj")

    y_prompt = x[:srow].reshape(n_batch, lp, D_MODEL)[:, lp - seq:]
    y_sample = x[srow:srow + ns].reshape(n_seq, t_len, D_MODEL)
    kv_p = kv[:srow].reshape(n_batch, lp, 2, N_KV_HEADS, HEAD_DIM)[:, lp - win:]
    kv_s = kv[srow:srow + ns].reshape(n_seq, t_len, 2, N_KV_HEADS, HEAD_DIM)
    k_win_s = jnp.concatenate([cache_k_win, kv_s[:, :, 0]], axis=1)[:, -win:]
    v_win_s = jnp.concatenate([cache_v_win, kv_s[:, :, 1]], axis=1)[:, -win:]
    return (y_prompt, y_sample, jnp.stack(states_p), states_s,
            kv_p[:, :, 0], kv_p[:, :, 1], k_win_s, v_win_s)
```

```python
import functools

import jax
import jax.numpy as jnp
from jax import lax
from jax.experimental import pallas as pl
from jax.experimental.pallas import tpu as pltpu

F32 = jnp.float32
BF16 = jnp.bfloat16

D_MODEL = 2048
N_META = 16
GLA_HEADS = 4
GLA_DK = 256
GLA_DV = 512
GLA_KEY_DIM = GLA_HEADS * GLA_DK
GLA_VAL_DIM = GLA_HEADS * GLA_DV
GLA_MAIN_WIDTH = 2 * GLA_KEY_DIM + 2 * GLA_VAL_DIM
GATE_RANK = 16
GATE_LOGIT_NORM = 16.0
HEAD_DIM = 64
N_Q_HEADS = 32
N_KV_HEADS = 8
Q_WIDTH = N_Q_HEADS * HEAD_DIM
KV_WIDTH = N_KV_HEADS * HEAD_DIM
WINDOW = 128
RMS_EPS = 1e-6
NEG_INF = -1e30
LOG2_E = 1.4426950408889634

LANES = 128
CHUNK = 128
GLA_LEAF = 8
LEAF_BATCH = 4
ROW_TILE = 512
COL_TILE = 512
SEQ_GROUP = 4
VMEM_LIMIT = 48 * 1024 * 1024

NT_DIMS = (((1,), (1,)), ((), ()))


def _silu(x):
    return x * (1.0 / (1.0 + jnp.exp(-x)))


def _params(sem):
    return pltpu.CompilerParams(dimension_semantics=sem, vmem_limit_bytes=VMEM_LIMIT)


def _normed(x_ref, nw_ref):
    x = x_ref[...]
    ms = jnp.mean(x * x, axis=-1, keepdims=True)
    return (x * lax.rsqrt(ms + RMS_EPS) * nw_ref[...]).astype(BF16)


def _norm_matmul_kernel(x_ref, nw_ref, w_ref, o_ref, hn_ref):
    @pl.when(pl.program_id(1) == 0)
    def _():
        hn_ref[...] = _normed(x_ref, nw_ref)

    o_ref[...] = jnp.dot(hn_ref[...], w_ref[...],
                         preferred_element_type=F32).astype(o_ref.dtype)


def _norm_matmul_gla_kernel(x_ref, nw_ref, w_ref, wg_ref, wgk_ref, bgk_ref,
                            o_ref, g_ref, hn_ref):
    @pl.when(pl.program_id(1) == 0)
    def _():
        hn = _normed(x_ref, nw_ref)
        hn_ref[...] = hn
        glr = jnp.dot(hn, wg_ref[...], preferred_element_type=F32)
        z = jnp.dot(glr.astype(BF16), wgk_ref[...], preferred_element_type=F32) + bgk_ref[...]
        log_sig = jnp.minimum(z, 0.0) - jnp.log1p(jnp.exp(-jnp.abs(z)))
        g_ref[...] = log_sig * (1.0 / GATE_LOGIT_NORM)

    o_ref[...] = jnp.dot(hn_ref[...], w_ref[...],
                         preferred_element_type=F32).astype(o_ref.dtype)


def _norm_matmul(x, nw, w, out_dtype, name):
    rows, n = x.shape[0], w.shape[1]
    return pl.pallas_call(
        _norm_matmul_kernel,
        grid=(rows // ROW_TILE, n // COL_TILE),
        in_specs=[
            pl.BlockSpec((ROW_TILE, D_MODEL), lambda i, j: (i, 0)),
            pl.BlockSpec((1, D_MODEL), lambda i, j: (0, 0)),
            pl.BlockSpec((D_MODEL, COL_TILE), lambda i, j: (0, j)),
        ],
        out_specs=pl.BlockSpec((ROW_TILE, COL_TILE), lambda i, j: (i, j)),
        out_shape=jax.ShapeDtypeStruct((rows, n), out_dtype),
        scratch_shapes=[pltpu.VMEM((ROW_TILE, D_MODEL), BF16)],
        compiler_params=_params(("parallel", "arbitrary")),
        name=name,
    )(x, nw, w)


def _norm_matmul_gla(x, nw, w, wg, wgk, bgk, name):
    rows, n = x.shape[0], w.shape[1]
    return pl.pallas_call(
        _norm_matmul_gla_kernel,
        grid=(rows // ROW_TILE, n // COL_TILE),
        in_specs=[
            pl.BlockSpec((ROW_TILE, D_MODEL), lambda i, j: (i, 0)),
            pl.BlockSpec((1, D_MODEL), lambda i, j: (0, 0)),
            pl.BlockSpec((D_MODEL, COL_TILE), lambda i, j: (0, j)),
            pl.BlockSpec((D_MODEL, LANES), lambda i, j: (0, 0)),
            pl.BlockSpec((LANES, GLA_KEY_DIM), lambda i, j: (0, 0)),
            pl.BlockSpec((1, GLA_KEY_DIM), lambda i, j: (0, 0)),
        ],
        out_specs=[
            pl.BlockSpec((ROW_TILE, COL_TILE), lambda i, j: (i, j)),
            pl.BlockSpec((ROW_TILE, GLA_KEY_DIM), lambda i, j: (i, 0)),
        ],
        out_shape=[
            jax.ShapeDtypeStruct((rows, n), BF16),
            jax.ShapeDtypeStruct((rows, GLA_KEY_DIM), F32),
        ],
        scratch_shapes=[pltpu.VMEM((ROW_TILE, D_MODEL), BF16)],
        compiler_params=_params(("parallel", "arbitrary")),
        name=name,
    )(x, nw, w, wg, wgk, bgk)


def _out_matmul_kernel(og_ref, w_ref, x_ref, o_ref):
    o_ref[...] = x_ref[...] + jnp.dot(og_ref[...], w_ref[...], preferred_element_type=F32)


def _out_matmul_norm_kernel(og_ref, w_ref, x_ref, nf_ref, o_ref):
    y = x_ref[...] + jnp.dot(og_ref[...], w_ref[...], preferred_element_type=F32)
    ms = jnp.mean(y * y, axis=-1, keepdims=True)
    o_ref[...] = y * lax.rsqrt(ms + RMS_EPS) * nf_ref[...]


def _out_matmul(og, w, x, norm_f, name):
    rows = x.shape[0]
    in_specs = [
        pl.BlockSpec((ROW_TILE, D_MODEL), lambda i: (i, 0)),
        pl.BlockSpec((D_MODEL, D_MODEL), lambda i: (0, 0)),
        pl.BlockSpec((ROW_TILE, D_MODEL), lambda i: (i, 0)),
    ]
    args = [og, w, x]
    body = _out_matmul_kernel
    if norm_f is not None:
        in_specs.append(pl.BlockSpec((1, D_MODEL), lambda i: (0, 0)))
        args.append(norm_f)
        body = _out_matmul_norm_kernel
    return pl.pallas_call(
        body,
        grid=(rows // ROW_TILE,),
        in_specs=in_specs,
        out_specs=pl.BlockSpec((ROW_TILE, D_MODEL), lambda i: (i, 0)),
        out_shape=jax.ShapeDtypeStruct((rows, D_MODEL), F32),
        compiler_params=_params(("parallel",)),
        name=name,
    )(*args)


def _exact_block(qs, k, b, n, width, col0):
    lane = lax.broadcasted_iota(jnp.int32, (n, width), 1)
    acc = jnp.zeros((n, width), F32)
    for j in range(n):
        t = qs * k[j:j + 1] * jnp.exp(jnp.minimum(b - b[j:j + 1], 0.0))
        acc = jnp.where(lane == col0 + j, jnp.sum(t, axis=-1, keepdims=True), acc)
    return acc


def _masked_row_sums(mask_bf16, g):
    g_hi = g.astype(BF16)
    r1 = g - g_hi.astype(F32)
    g_mid = r1.astype(BF16)
    g_lo = (r1 - g_mid.astype(F32)).astype(BF16)
    return (jnp.dot(mask_bf16, g_hi, preferred_element_type=F32)
            + jnp.dot(mask_bf16, g_mid, preferred_element_type=F32)
            + jnp.dot(mask_bf16, g_lo, preferred_element_type=F32))


def _column_of(row_vec, n):
    return jnp.broadcast_to(row_vec, (LANES, n)).T[:, 0:1]


def _head_norm_gate(o, onorm, gate):
    ms = jnp.mean(o * o, axis=-1, keepdims=True)
    return (o * lax.rsqrt(ms + RMS_EPS) * onorm * _silu(gate)).astype(BF16)


def _gla_prompt_kernel(q_ref, k_ref, v_ref, gate_ref, g_ref, onorm_ref, og_in_ref,
                       og_ref, s_ref):
    del og_in_ref
    c_len = CHUNK
    leaf = GLA_LEAF

    @pl.when(pl.program_id(1) == 0)
    def _():
        s_ref[...] = jnp.zeros_like(s_ref)

    row = lax.broadcasted_iota(jnp.int32, (c_len, c_len), 0)
    col = lax.broadcasted_iota(jnp.int32, (c_len, c_len), 1)
    tri = jnp.where(row >= col, 1.0, 0.0).astype(BF16)
    row_xor_col = row ^ col
    leaf_row = lax.broadcasted_iota(jnp.int32, (leaf, c_len), 0)
    leaf_col = lax.broadcasted_iota(jnp.int32, (leaf, c_len), 1)
    lane_in_leaf = [(leaf_col & (leaf - 1)) == j for j in range(leaf)]
    onorm = onorm_ref[...]

    for h in range(GLA_HEADS):
        ks = slice(h * GLA_DK, (h + 1) * GLA_DK)
        vs = slice(h * GLA_DV, (h + 1) * GLA_DV)
        qs = q_ref[:, ks].astype(F32) * (GLA_DK ** -0.5)
        k_bf = k_ref[:, ks]
        k = k_bf.astype(F32)
        v = v_ref[:, vs]
        b = _masked_row_sums(tri, g_ref[:, ks]) * LOG2_E
        state = s_ref[0, h]

        att_rows = []
        for blk0 in range(0, c_len // leaf, LEAF_BATCH):
            lhs = []
            for blk in range(blk0, blk0 + LEAF_BATCH):
                q_blk = qs[blk * leaf:(blk + 1) * leaf]
                b_blk = b[blk * leaf:(blk + 1) * leaf]
                lhs += [q_blk * jnp.exp2(jnp.minimum(b_blk - b_blk[j:j + 1], 0.0))
                        for j in range(leaf)]
            r = lax.dot_general(jnp.concatenate(lhs, axis=0).astype(BF16), k_bf, NT_DIMS,
                                preferred_element_type=F32)
            for i, blk in enumerate(range(blk0, blk0 + LEAF_BATCH)):
                acc = jnp.zeros((leaf, c_len), F32)
                for j in range(leaf):
                    r0 = (i * leaf + j) * leaf
                    acc = jnp.where(lane_in_leaf[j], r[r0:r0 + leaf], acc)
                rel = leaf_col - blk * leaf
                att_rows.append(jnp.where((rel >= 0) & (rel <= leaf_row), acc, 0.0))
        att = jnp.concatenate(att_rows, axis=0)

        m = c_len // 2
        while m >= leaf:
            q_parts, k_parts = [], []
            zeros = jnp.zeros((m, GLA_DK), F32)
            for base in range(0, c_len, 2 * m):
                ref = b[base + m - 1:base + m]
                lo = slice(base, base + m)
                hi = slice(base + m, base + 2 * m)
                k_parts += [k[lo] * jnp.exp2(ref - b[lo]), zeros]
                q_parts += [zeros, qs[hi] * jnp.exp2(b[hi] - ref)]
            qt = jnp.concatenate(q_parts, axis=0).astype(BF16)
            kt = jnp.concatenate(k_parts, axis=0).astype(BF16)
            a = lax.dot_general(qt, kt, NT_DIMS, preferred_element_type=F32)
            att = att + jnp.where(row_xor_col < 2 * m, a, 0.0)
            m //= 2

        o = jnp.dot((qs * jnp.exp2(b)).astype(BF16), state.astype(BF16),
                    preferred_element_type=F32)
        o = o + jnp.dot(att.astype(BF16), v, preferred_element_type=F32)

        b_last = b[c_len - 1:c_len]
        kd_t = (k * jnp.exp2(b_last - b)).T.astype(BF16)
        s_ref[0, h] = (state * _column_of(jnp.exp2(b_last), GLA_DK)
                       + jnp.dot(kd_t, v, preferred_element_type=F32))

        og_ref[:, vs] = _head_norm_gate(o, onorm, gate_ref[:, vs].astype(F32))


def _gla_prompt(proj, g, onorm, og, n_batch, chunks):
    rows = proj.shape[0]
    blk = lambda col: (lambda b, c: (b * chunks + c, col))
    return pl.pallas_call(
        _gla_prompt_kernel,
        grid=(n_batch, chunks),
        in_specs=[
            pl.BlockSpec((CHUNK, GLA_KEY_DIM), blk(0)),
            pl.BlockSpec((CHUNK, GLA_KEY_DIM), blk(1)),
            pl.BlockSpec((CHUNK, GLA_VAL_DIM), blk(1)),
            pl.BlockSpec((CHUNK, GLA_VAL_DIM), blk(2)),
            pl.BlockSpec((CHUNK, GLA_KEY_DIM), blk(0)),
            pl.BlockSpec((1, GLA_DV), lambda b, c: (0, 0)),
            pl.BlockSpec(memory_space=pl.ANY),
        ],
        out_specs=[
            pl.BlockSpec((CHUNK, GLA_VAL_DIM), blk(0)),
            pl.BlockSpec((1, GLA_HEADS, GLA_DK, GLA_DV), lambda b, c: (b, 0, 0, 0)),
        ],
        out_shape=[
            jax.ShapeDtypeStruct((rows, GLA_VAL_DIM), BF16),
            jax.ShapeDtypeStruct((n_batch, GLA_HEADS, GLA_DK, GLA_DV), F32),
        ],
        input_output_aliases={6: 0},
        compiler_params=_params(("parallel", "arbitrary")),
        name="gla_prompt",
    )(proj, proj, proj, proj, g, onorm, og)


def _gla_sample_kernel(t_len, q_ref, k_ref, v_ref, gate_ref, g_ref, onorm_ref, s0_ref,
                       og_in_ref, *rest):
    og_ref, s_ref = rest[-2], rest[-1]
    n = SEQ_GROUP * t_len
    row = lax.broadcasted_iota(jnp.int32, (n, LANES), 0)
    col = lax.broadcasted_iota(jnp.int32, (n, LANES), 1)
    causal = ((row // t_len) == (col // t_len)) & (row >= col)
    row_seq = lax.broadcasted_iota(jnp.int32, (n, 1), 0) // t_len

    qs = q_ref[...].astype(F32) * (GLA_DK ** -0.5)
    k = k_ref[...].astype(F32)
    pad_k = jnp.zeros((LANES - n, GLA_DK), F32)
    v_pad = jnp.concatenate([v_ref[...], jnp.zeros((LANES - n, GLA_DV), BF16)], axis=0)
    b = _masked_row_sums(jnp.where(causal, 1.0, 0.0).astype(BF16),
                         jnp.concatenate([g_ref[...], pad_k], axis=0))
    att = jnp.where(causal, _exact_block(qs, k, b, n, LANES, 0), 0.0)
    o = jnp.dot(att.astype(BF16), v_pad, preferred_element_type=F32)

    q_dec = (qs * jnp.exp(b)).astype(BF16)
    for u in range(SEQ_GROUP):
        state = s0_ref[0, u, 0]
        o_u = jnp.dot(q_dec, state.astype(BF16), preferred_element_type=F32)
        o = o + jnp.where(row_seq == u, o_u, 0.0)
        b_last = b[(u + 1) * t_len - 1:(u + 1) * t_len]
        kd = jnp.where(row_seq == u, k * jnp.exp(jnp.minimum(b_last - b, 0.0)), 0.0)
        kd_t = jnp.concatenate([kd, pad_k], axis=0).T.astype(BF16)
        s_ref[0, u, 0] = (state * _column_of(jnp.exp(b_last), GLA_DK)
                          + jnp.dot(kd_t, v_pad, preferred_element_type=F32))

    og_ref[...] = _head_norm_gate(o, onorm_ref[...], gate_ref[...].astype(F32))


def _gla_sample(proj, g, onorm, state_in, og, s_out, layer, n_seq, t_len, row0):
    n = SEQ_GROUP * t_len
    rb0 = row0 // n
    n_layers = state_in.shape[0]
    key_blocks = GLA_KEY_DIM // GLA_DK
    val_blocks = GLA_VAL_DIM // GLA_DV
    st_spec = pl.BlockSpec((1, SEQ_GROUP, 1, GLA_DK, GLA_DV),
                           lambda s, h: (layer, s, h, 0, 0))
    in_specs = [
        pl.BlockSpec((n, GLA_DK), lambda s, h: (rb0 + s, h)),
        pl.BlockSpec((n, GLA_DK), lambda s, h: (rb0 + s, key_blocks + h)),
        pl.BlockSpec((n, GLA_DV), lambda s, h: (rb0 + s, val_blocks + h)),
        pl.BlockSpec((n, GLA_DV), lambda s, h: (rb0 + s, 2 * val_blocks + h)),
        pl.BlockSpec((n, GLA_DK), lambda s, h: (rb0 + s, h)),
        pl.BlockSpec((1, GLA_DV), lambda s, h: (0, 0)),
        st_spec,
        pl.BlockSpec(memory_space=pl.ANY),
    ]
    args = [proj, proj, proj, proj, g, onorm, state_in, og]
    aliases = {7: 0}
    if s_out is not None:
        in_specs.append(pl.BlockSpec(memory_space=pl.ANY))
        args.append(s_out)
        aliases[8] = 1
    return pl.pallas_call(
        functools.partial(_gla_sample_kernel, t_len),
        grid=(n_seq // SEQ_GROUP, GLA_HEADS),
        in_specs=in_specs,
        out_specs=[
            pl.BlockSpec((n, GLA_DV), lambda s, h: (rb0 + s, h)),
            st_spec,
        ],
        out_shape=[
            jax.ShapeDtypeStruct(og.shape, BF16),
            jax.ShapeDtypeStruct((n_layers, n_seq, GLA_HEADS, GLA_DK, GLA_DV), F32),
        ],
        input_output_aliases=aliases,
        compiler_params=_params(("parallel", "parallel")),
        name="gla_sample",
    )(*args)


def _half_lane_copies(kc, parity):
    kr = pltpu.roll(kc, HEAD_DIM, axis=1)
    return (kc, kr) if parity == 0 else (kr, kc)


def _attend_rows(q_ref, gate_ref, og_ref, sink_ref, kv_sets, s_scr, p_scr):
    n_rows = q_ref.shape[0]
    n_keys = s_scr.shape[2]
    group = N_Q_HEADS // N_KV_HEADS
    low_out = lax.broadcasted_iota(jnp.int32, (n_rows, LANES), 1) < HEAD_DIM
    lane0 = lax.broadcasted_iota(jnp.int32, (n_rows, LANES), 1) == 0
    key_lane = lax.broadcasted_iota(jnp.int32, (n_keys, LANES), 1)
    key_row = lax.broadcasted_iota(jnp.int32, (n_keys, LANES), 0)
    low = key_lane < HEAD_DIM
    slot0 = key_row == 0

    def head_order():
        for cpair in range(KV_WIDTH // LANES):
            for parity in range(2):
                for pr in range(group // 2):
                    yield cpair, parity, (2 * cpair + parity) * (group // 2) + pr

    idx = 0
    for keys, _, _ in kv_sets:
        k_pads = None
        for cpair, parity, qc in head_order():
            if qc % (group // 2) == 0:
                lo_src, hi_src = _half_lane_copies(keys[cpair], parity)
                k_pads = (jnp.where(low, lo_src, 0.0).astype(BF16),
                          jnp.where(low, 0.0, hi_src).astype(BF16))
            q2 = (q_ref[:, qc * LANES:(qc + 1) * LANES].astype(F32)
                  * (HEAD_DIM ** -0.5 * LOG2_E)).astype(BF16)
            for k_pad in k_pads:
                s_scr[idx] = lax.dot_general(q2, k_pad, NT_DIMS,
                                             preferred_element_type=F32)
                idx += 1

    idx = 0
    for _, _, mask in kv_sets:
        mask_first, mask_rest = mask[:, :LANES], mask[:, LANES:]
        for _, _, qc in head_order():
            for half in range(2):
                sink = sink_ref[2 * qc + half] * LOG2_E
                s = s_scr[idx]
                first = jnp.where(mask_first, s[:, :LANES],
                                  jnp.where(lane0, sink, NEG_INF))
                rest = jnp.where(mask_rest, s[:, LANES:], NEG_INF)
                m = jnp.max(jnp.maximum(first, rest), axis=-1, keepdims=True)
                p_scr[idx] = jnp.exp2(
                    jnp.concatenate([first, rest], axis=1) - m).astype(BF16)
                idx += 1

    idx = 0
    out_cols = [None] * (Q_WIDTH // LANES)
    for _, vals, _ in kv_sets:
        v_pads = None
        for cpair, parity, qc in head_order():
            if qc % (group // 2) == 0:
                lo_src, hi_src = _half_lane_copies(vals[cpair], parity)
                v_pads = (jnp.where(low, jnp.where(slot0, 0.0, lo_src), 1.0).astype(BF16),
                          jnp.where(low, 1.0, jnp.where(slot0, 0.0, hi_src)).astype(BF16))
            halves = []
            for v_pad in v_pads:
                o = jnp.dot(p_scr[idx], v_pad, preferred_element_type=F32)
                halves.append(o * (1.0 / pltpu.roll(o, HEAD_DIM, axis=1)))
                idx += 1
            o2 = jnp.where(low_out, halves[0], halves[1])
            out_cols[qc] = o2 if out_cols[qc] is None else out_cols[qc] + o2

    for qc, o in enumerate(out_cols):
        cols = slice(qc * LANES, (qc + 1) * LANES)
        og_ref[:, cols] = (o * _silu(gate_ref[:, cols].astype(F32))).astype(BF16)


def _attention_scratch(n_sets, n_rows, n_keys):
    n = n_sets * N_Q_HEADS
    return [pltpu.VMEM((n, n_rows, n_keys), F32), pltpu.VMEM((n, n_rows, n_keys), BF16)]


def _swa_prompt_kernel(front, sink_ref, q_ref, gate_ref, kc_ref, kp_ref, vc_ref, vp_ref,
                       og_in_ref, og_ref, s_scr, p_scr):
    del og_in_ref
    blk = pl.program_id(1)
    row = lax.broadcasted_iota(jnp.int32, (CHUNK, 2 * CHUNK), 0)
    col = lax.broadcasted_iota(jnp.int32, (CHUNK, 2 * CHUNK), 1)
    mask = (col > row) & (col <= row + WINDOW) & ((blk - 1) * CHUNK + col >= front)
    keys = [jnp.concatenate([kp_ref[:, c * LANES:(c + 1) * LANES],
                             kc_ref[:, c * LANES:(c + 1) * LANES]], axis=0)
            for c in range(KV_WIDTH // LANES)]
    vals = [jnp.concatenate([vp_ref[:, c * LANES:(c + 1) * LANES],
                             vc_ref[:, c * LANES:(c + 1) * LANES]], axis=0)
            for c in range(KV_WIDTH // LANES)]
    _attend_rows(q_ref, gate_ref, og_ref, sink_ref, [(keys, vals, mask)], s_scr, p_scr)


def _swa_prompt(pq, kv, sinks, og, n_batch, chunks, front):
    cur = lambda col: (lambda b, i: (b * chunks + i, col))
    prev = lambda col: (lambda b, i: (b * chunks + jnp.maximum(i - 1, 0), col))
    return pl.pallas_call(
        functools.partial(_swa_prompt_kernel, front),
        grid=(n_batch, chunks),
        in_specs=[
            pl.BlockSpec(memory_space=pltpu.SMEM),
            pl.BlockSpec((CHUNK, Q_WIDTH), cur(0)),
            pl.BlockSpec((CHUNK, Q_WIDTH), cur(1)),
            pl.BlockSpec((CHUNK, KV_WIDTH), cur(0)),
            pl.BlockSpec((CHUNK, KV_WIDTH), prev(0)),
            pl.BlockSpec((CHUNK, KV_WIDTH), cur(1)),
            pl.BlockSpec((CHUNK, KV_WIDTH), prev(1)),
            pl.BlockSpec(memory_space=pl.ANY),
        ],
        out_specs=pl.BlockSpec((CHUNK, Q_WIDTH), cur(0)),
        out_shape=jax.ShapeDtypeStruct(og.shape, BF16),
        scratch_shapes=_attention_scratch(1, CHUNK, 2 * CHUNK),
        input_output_aliases={7: 0},
        compiler_params=_params(("parallel", "parallel")),
        name="swa_prompt",
    )(sinks, pq, pq, kv, kv, kv, kv, og)


def _swa_sample_kernel(t_len, sink_ref, q_ref, gate_ref, kn_ref, vn_ref, ck_ref, cv_ref,
                       og_in_ref, og_ref, s_scr, p_scr):
    del og_in_ref
    n = SEQ_GROUP * t_len
    win = ck_ref.shape[1]
    row = lax.broadcasted_iota(jnp.int32, (n, 2 * win), 0)
    col = lax.broadcasted_iota(jnp.int32, (n, 2 * win), 1)
    t = row % t_len
    new = col - win
    in_window = ((col < win) & (col > t)) | (
        (new >= 0) & (new < n) & ((new // t_len) == (row // t_len)) & ((new % t_len) <= t))
    pad = jnp.zeros((win - n, LANES), F32)
    kv_pairs = []
    for u in range(SEQ_GROUP):
        keys = [jnp.concatenate([ck_ref[u, :, c * LANES:(c + 1) * LANES],
                                 kn_ref[:, c * LANES:(c + 1) * LANES], pad], axis=0)
                for c in range(KV_WIDTH // LANES)]
        vals = [jnp.concatenate([cv_ref[u, :, c * LANES:(c + 1) * LANES],
                                 vn_ref[:, c * LANES:(c + 1) * LANES], pad], axis=0)
                for c in range(KV_WIDTH // LANES)]
        kv_pairs.append((keys, vals, in_window & ((row // t_len) == u)))
    _attend_rows(q_ref, gate_ref, og_ref, sink_ref, kv_pairs, s_scr, p_scr)


def _swa_sample(pq, kv, cache_k, cache_v, sinks, og, n_seq, t_len, row0):
    n = SEQ_GROUP * t_len
    rb0 = row0 // n
    win = cache_k.shape[1]
    return pl.pallas_call(
        functools.partial(_swa_sample_kernel, t_len),
        grid=(n_seq // SEQ_GROUP,),
        in_specs=[
            pl.BlockSpec(memory_space=pltpu.SMEM),
            pl.BlockSpec((n, Q_WIDTH), lambda s: (rb0 + s, 0)),
            pl.BlockSpec((n, Q_WIDTH), lambda s: (rb0 + s, 1)),
            pl.BlockSpec((n, KV_WIDTH), lambda s: (rb0 + s, 0)),
            pl.BlockSpec((n, KV_WIDTH), lambda s: (rb0 + s, 1)),
            pl.BlockSpec((SEQ_GROUP, win, KV_WIDTH), lambda s: (s, 0, 0)),
            pl.BlockSpec((SEQ_GROUP, win, KV_WIDTH), lambda s: (s, 0, 0)),
            pl.BlockSpec(memory_space=pl.ANY),
        ],
        out_specs=pl.BlockSpec((n, Q_WIDTH), lambda s: (rb0 + s, 0)),
        out_shape=jax.ShapeDtypeStruct(og.shape, BF16),
        scratch_shapes=_attention_scratch(SEQ_GROUP, n, 2 * win),
        input_output_aliases={7: 0},
        compiler_params=_params(("parallel",)),
        name="swa_sample",
    )(sinks, pq, pq, kv, kv, cache_k, cache_v, og)


def kernel(x_prompt, x_sample, state_gla, cache_k_win, cache_v_win, meta_tokens, norm_a, w_in_a, w_gk2_a, b_gk2_a, onorm_a, w_out_a, norm_kv, w_kv, norm_b, w_in_b, sinks_b, w_out_b, norm_f):
    n_batch, seq, _ = x_prompt.shape
    n_seq, t_len, _ = x_sample.shape
    win = cache_k_win.shape[1]
    n_a = w_in_a.shape[0]
    n_b = w_in_b.shape[0]
    assert win == WINDOW == CHUNK and (n_seq * t_len) % (SEQ_GROUP * t_len) == 0

    length = seq + N_META
    front = (-length) % CHUNK
    lp = length + front
    chunks = lp // CHUNK
    srow = n_batch * lp
    ns = n_seq * t_len
    rows = -(-(srow + ns) // ROW_TILE) * ROW_TILE

    lead = jnp.concatenate([jnp.zeros((front, D_MODEL), F32), meta_tokens.astype(F32)], axis=0)
    pieces = []
    for b in range(n_batch):
        pieces += [lead, x_prompt[b]]
    pieces += [x_sample.reshape(ns, D_MODEL), jnp.zeros((rows - srow - ns, D_MODEL), F32)]
    x = jnp.concatenate(pieces, axis=0)
    og = jnp.zeros((rows, D_MODEL), BF16)

    states_p = []
    states_s = None
    for l in range(n_a):
        w_main = w_in_a[l][:, :GLA_MAIN_WIDTH].astype(BF16)
        w_gate = jnp.pad(w_in_a[l][:, GLA_MAIN_WIDTH:],
                         ((0, 0), (0, LANES - GATE_RANK))).astype(BF16)
        w_gk = jnp.pad(w_gk2_a[l], ((0, LANES - GATE_RANK), (0, 0))).astype(BF16)
        proj, g = _norm_matmul_gla(x, norm_a[l][None], w_main, w_gate, w_gk,
                                   b_gk2_a[l][None], "gla_in_proj")
        onorm = onorm_a[l][None]
        og, s_p = _gla_prompt(proj, g, onorm, og, n_batch, chunks)
        og, states_s = _gla_sample(proj, g, onorm, state_gla, og, states_s, l,
                                   n_seq, t_len, srow)
        states_p.append(s_p)
        x = _out_matmul(og, w_out_a[l].astype(BF16), x, None, "gla_out_proj")

    kv = _norm_matmul(x, norm_kv[None], w_kv.astype(BF16), F32, "kv_proj")
    cache_k = cache_k_win.reshape(n_seq, win, KV_WIDTH)
    cache_v = cache_v_win.reshape(n_seq, win, KV_WIDTH)
    for j in range(n_b):
        pq = _norm_matmul(x, norm_b[j][None], w_in_b[j].astype(BF16), BF16, "swa_in_proj")
        og = _swa_prompt(pq, kv, sinks_b[j], og, n_batch, chunks, front)
        og = _swa_sample(pq, kv, cache_k, cache_v, sinks_b[j], og, n_seq, t_len, srow)
        x = _out_matmul(og, w_out_b[j].astype(BF16), x,
                        norm_f[None] if j == n_b - 1 else None, "swa_out_proj")

    y_prompt = x[:srow].reshape(n_batch, lp, D_MODEL)[:, lp - seq:]
    y_sample = x[srow:srow + ns].reshape(n_seq, t_len, D_MODEL)
    kv_p = kv[:srow].reshape(n_batch, lp, 2, N_KV_HEADS, HEAD_DIM)[:, lp - win:]
    kv_s = kv[srow:srow + ns].reshape(n_seq, t_len, 2, N_KV_HEADS, HEAD_DIM)
    k_win_s = jnp.concatenate([cache_k_win, kv_s[:, :, 0]], axis=1)[:, -win:]
    v_win_s = jnp.concatenate([cache_v_win, kv_s[:, :, 1]], axis=1)[:, -win:]
    return (y_prompt, y_sample, jnp.stack(states_p), states_s,
            kv_p[:, :, 0], kv_p[:, :, 1], k_win_s, v_win_s)
```

```python
import functools

import jax
import jax.numpy as jnp
from jax import lax
from jax.experimental import pallas as pl
from jax.experimental.pallas import tpu as pltpu

F32 = jnp.float32
BF16 = jnp.bfloat16

D_MODEL = 2048
N_META = 16
GLA_HEADS = 4
GLA_DK = 256
GLA_DV = 512
GLA_KEY_DIM = GLA_HEADS * GLA_DK
GLA_VAL_DIM = GLA_HEADS * GLA_DV
GLA_MAIN_WIDTH = 2 * GLA_KEY_DIM + 2 * GLA_VAL_DIM
GATE_RANK = 16
GATE_LOGIT_NORM = 16.0
HEAD_DIM = 64
N_Q_HEADS = 32
N_KV_HEADS = 8
Q_WIDTH = N_Q_HEADS * HEAD_DIM
KV_WIDTH = N_KV_HEADS * HEAD_DIM
WINDOW = 128
RMS_EPS = 1e-6
NEG_INF = -1e30
LOG2_E = 1.4426950408889634

LANES = 128
CHUNK = 128
GLA_LEAF = 8
LEAF_BATCH = 4
ROW_TILE = 512
IN_PROJ_COLS = 1024
IN_PROJ_ROW_STEPS = 8
SEQ_GROUP = 4
VMEM_LIMIT = 48 * 1024 * 1024

NT_DIMS = (((1,), (1,)), ((), ()))


def _silu(x):
    return x * (1.0 / (1.0 + jnp.exp(-x)))


def _params(sem):
    return pltpu.CompilerParams(dimension_semantics=sem, vmem_limit_bytes=VMEM_LIMIT)


def _emit_normed(y, nw_refs, gate_refs, hn_refs, g_ref):
    ms = jnp.mean(y * y, axis=-1, keepdims=True)
    yn = y * lax.rsqrt(ms + RMS_EPS)
    for nw_ref, hn_ref in zip(nw_refs, hn_refs):
        hn_ref[...] = (yn * nw_ref[...]).astype(hn_ref.dtype)
    if gate_refs:
        wg_ref, wgk_ref, bgk_ref = gate_refs
        glr = jnp.dot(hn_refs[0][...], wg_ref[...], preferred_element_type=F32)
        z = jnp.dot(glr.astype(BF16), wgk_ref[...], preferred_element_type=F32) + bgk_ref[...]
        log_sig = jnp.minimum(z, 0.0) - jnp.log1p(jnp.exp(-jnp.abs(z)))
        g_ref[...] = log_sig * (1.0 / GATE_LOGIT_NORM)


def _split_refs(refs, n_norms, with_gate):
    nw_refs = refs[:n_norms]
    gate_refs = refs[n_norms:n_norms + 3] if with_gate else None
    return nw_refs, gate_refs, refs[n_norms + 3 * with_gate:]


def _norm_specs(norms, gate, index):
    specs = [pl.BlockSpec((1, D_MODEL), index) for _ in norms]
    args = [nw[None] for nw in norms]
    if gate is not None:
        specs += [pl.BlockSpec((D_MODEL, LANES), index),
                  pl.BlockSpec((LANES, GLA_KEY_DIM), index),
                  pl.BlockSpec((1, GLA_KEY_DIM), index)]
        args += list(gate)
    return specs, args


def _embed_kernel(lead_blocks, sample_block, front, with_gate, xp_ref, xs_ref, meta_ref,
                  *refs):
    nw_refs, gate_refs, outs = _split_refs(refs, 1, with_gate)
    x_ref = outs[0]
    i = pl.program_id(0)
    is_lead = functools.reduce(jnp.logical_or, [i == b for b in lead_blocks])

    @pl.when(is_lead)
    def _():
        x_ref[...] = jnp.concatenate(
            [jnp.zeros((front, D_MODEL), F32), meta_ref[...]], axis=0)

    @pl.when(jnp.logical_and(jnp.logical_not(is_lead), i < sample_block))
    def _():
        x_ref[...] = xp_ref[...]

    @pl.when(i == sample_block)
    def _():
        x_ref[...] = xs_ref[...]

    @pl.when(i > sample_block)
    def _():
        x_ref[...] = jnp.zeros_like(x_ref)

    _emit_normed(x_ref[...], nw_refs, gate_refs, outs[1:2], outs[2] if with_gate else None)


def _embed(x_prompt2d, x_sample2d, meta, norm, gate, rows, chunks, n_batch, front):
    prompt_blocks = x_prompt2d.shape[0] // CHUNK
    per_batch = prompt_blocks // n_batch

    def prompt_index(i):
        blk = (i // chunks) * per_batch + (i % chunks) - 1
        return (jnp.clip(blk, 0, prompt_blocks - 1), 0)

    norm_specs, norm_args = _norm_specs([norm], gate, lambda i: (0, 0))
    out_specs = [pl.BlockSpec((CHUNK, D_MODEL), lambda i: (i, 0)),
                 pl.BlockSpec((CHUNK, D_MODEL), lambda i: (i, 0))]
    out_shape = [jax.ShapeDtypeStruct((rows, D_MODEL), F32),
                 jax.ShapeDtypeStruct((rows, D_MODEL), BF16)]
    if gate is not None:
        out_specs.append(pl.BlockSpec((CHUNK, GLA_KEY_DIM), lambda i: (i, 0)))
        out_shape.append(jax.ShapeDtypeStruct((rows, GLA_KEY_DIM), F32))
    return pl.pallas_call(
        functools.partial(_embed_kernel, tuple(b * chunks for b in range(n_batch)),
                          n_batch * chunks, front, gate is not None),
        grid=(rows // CHUNK,),
        in_specs=[
            pl.BlockSpec((CHUNK, D_MODEL), prompt_index),
            pl.BlockSpec((CHUNK, D_MODEL), lambda i: (0, 0)),
            pl.BlockSpec((N_META, D_MODEL), lambda i: (0, 0)),
        ] + norm_specs,
        out_specs=out_specs,
        out_shape=out_shape,
        compiler_params=_params(("parallel",)),
        name="embed",
    )(x_prompt2d, x_sample2d, meta, *norm_args)


def _in_proj_kernel(hn_ref, w_ref, o_ref, w_bf16_ref):
    @pl.when(pl.program_id(1) == 0)
    def _():
        w_bf16_ref[...] = w_ref[...].astype(BF16)

    o_ref[...] = jnp.dot(hn_ref[...], w_bf16_ref[...],
                         preferred_element_type=F32).astype(o_ref.dtype)


def _in_proj(hn, w_stack, layer, n_cols, out_dtype, name):
    rows = hn.shape[0]
    row_tile = rows // IN_PROJ_ROW_STEPS
    return pl.pallas_call(
        _in_proj_kernel,
        grid=(n_cols // IN_PROJ_COLS, IN_PROJ_ROW_STEPS),
        in_specs=[
            pl.BlockSpec((row_tile, D_MODEL), lambda j, i: (i, 0)),
            pl.BlockSpec((None, D_MODEL, IN_PROJ_COLS), lambda j, i: (layer, 0, j)),
        ],
        out_specs=pl.BlockSpec((row_tile, IN_PROJ_COLS), lambda j, i: (i, j)),
        out_shape=jax.ShapeDtypeStruct((rows, n_cols), out_dtype),
        scratch_shapes=[pltpu.VMEM((D_MODEL, IN_PROJ_COLS), BF16)],
        compiler_params=_params(("parallel", "arbitrary")),
        name=name,
    )(hn, w_stack)


def _out_proj_kernel(n_norms, with_gate, keep_x, og_ref, w_ref, x_ref, *refs):
    nw_refs, gate_refs, outs = _split_refs(refs, n_norms, with_gate)
    y = x_ref[...] + jnp.dot(og_ref[...], w_ref[...], preferred_element_type=F32)
    if keep_x:
        outs[0][...] = y
        outs = outs[1:]
    _emit_normed(y, nw_refs, gate_refs, outs[:n_norms], outs[n_norms] if with_gate else None)


def _out_proj(og, w, x, norms, gate, keep_x, normed_dtype, name):
    rows = x.shape[0]
    row_spec = pl.BlockSpec((ROW_TILE, D_MODEL), lambda i: (i, 0))
    norm_specs, norm_args = _norm_specs(norms, gate, lambda i: (0, 0))
    out_specs, out_shape = [], []
    if keep_x:
        out_specs.append(row_spec)
        out_shape.append(jax.ShapeDtypeStruct((rows, D_MODEL), F32))
    for _ in norms:
        out_specs.append(row_spec)
        out_shape.append(jax.ShapeDtypeStruct((rows, D_MODEL), normed_dtype))
    if gate is not None:
        out_specs.append(pl.BlockSpec((ROW_TILE, GLA_KEY_DIM), lambda i: (i, 0)))
        out_shape.append(jax.ShapeDtypeStruct((rows, GLA_KEY_DIM), F32))
    return pl.pallas_call(
        functools.partial(_out_proj_kernel, len(norms), gate is not None, keep_x),
        grid=(rows // ROW_TILE,),
        in_specs=[
            row_spec,
            pl.BlockSpec((D_MODEL, D_MODEL), lambda i: (0, 0), pipeline_mode=pl.Buffered(1)),
            row_spec,
        ] + norm_specs,
        out_specs=out_specs,
        out_shape=out_shape,
        compiler_params=_params(("parallel",)),
        name=name,
    )(og, w, x, *norm_args)


def _exact_block(qs, k, b, n, width, col0):
    lane = lax.broadcasted_iota(jnp.int32, (n, width), 1)
    acc = jnp.zeros((n, width), F32)
    for j in range(n):
        t = qs * k[j:j + 1] * jnp.exp(jnp.minimum(b - b[j:j + 1], 0.0))
        acc = jnp.where(lane == col0 + j, jnp.sum(t, axis=-1, keepdims=True), acc)
    return acc


def _masked_row_sums(mask_bf16, g):
    g_hi = g.astype(BF16)
    r1 = g - g_hi.astype(F32)
    g_mid = r1.astype(BF16)
    g_lo = (r1 - g_mid.astype(F32)).astype(BF16)
    return (jnp.dot(mask_bf16, g_hi, preferred_element_type=F32)
            + jnp.dot(mask_bf16, g_mid, preferred_element_type=F32)
            + jnp.dot(mask_bf16, g_lo, preferred_element_type=F32))


def _column_of(row_vec, n):
    return jnp.broadcast_to(row_vec, (LANES, n)).T[:, 0:1]


def _head_norm_gate(o, onorm, gate):
    ms = jnp.mean(o * o, axis=-1, keepdims=True)
    return (o * lax.rsqrt(ms + RMS_EPS) * onorm * _silu(gate)).astype(BF16)


def _gla_prompt_kernel(q_ref, k_ref, v_ref, gate_ref, g_ref, onorm_ref, og_in_ref,
                       og_ref, s_ref):
    del og_in_ref
    c_len = CHUNK
    leaf = GLA_LEAF

    @pl.when(pl.program_id(1) == 0)
    def _():
        s_ref[...] = jnp.zeros_like(s_ref)

    row = lax.broadcasted_iota(jnp.int32, (c_len, c_len), 0)
    col = lax.broadcasted_iota(jnp.int32, (c_len, c_len), 1)
    tri = jnp.where(row >= col, 1.0, 0.0).astype(BF16)
    row_xor_col = row ^ col
    leaf_row = lax.broadcasted_iota(jnp.int32, (leaf, c_len), 0)
    leaf_col = lax.broadcasted_iota(jnp.int32, (leaf, c_len), 1)
    lane_in_leaf = [(leaf_col & (leaf - 1)) == j for j in range(leaf)]
    onorm = onorm_ref[...]

    for h in range(GLA_HEADS):
        ks = slice(h * GLA_DK, (h + 1) * GLA_DK)
        vs = slice(h * GLA_DV, (h + 1) * GLA_DV)
        qs = q_ref[:, ks].astype(F32) * (GLA_DK ** -0.5)
        k_bf = k_ref[:, ks]
        k = k_bf.astype(F32)
        v = v_ref[:, vs]
        b = _masked_row_sums(tri, g_ref[:, ks]) * LOG2_E
        state = s_ref[0, h]

        att_rows = []
        for blk0 in range(0, c_len // leaf, LEAF_BATCH):
            lhs = []
            for blk in range(blk0, blk0 + LEAF_BATCH):
                q_blk = qs[blk * leaf:(blk + 1) * leaf]
                b_blk = b[blk * leaf:(blk + 1) * leaf]
                lhs += [q_blk * jnp.exp2(jnp.minimum(b_blk - b_blk[j:j + 1], 0.0))
                        for j in range(leaf)]
            r = lax.dot_general(jnp.concatenate(lhs, axis=0).astype(BF16), k_bf, NT_DIMS,
                                preferred_element_type=F32)
            for i, blk in enumerate(range(blk0, blk0 + LEAF_BATCH)):
                acc = jnp.zeros((leaf, c_len), F32)
                for j in range(leaf):
                    r0 = (i * leaf + j) * leaf
                    acc = jnp.where(lane_in_leaf[j], r[r0:r0 + leaf], acc)
                rel = leaf_col - blk * leaf
                att_rows.append(jnp.where((rel >= 0) & (rel <= leaf_row), acc, 0.0))
        att = jnp.concatenate(att_rows, axis=0)

        m = c_len // 2
        while m >= leaf:
            q_parts, k_parts = [], []
            zeros = jnp.zeros((m, GLA_DK), F32)
            for base in range(0, c_len, 2 * m):
                ref = b[base + m - 1:base + m]
                lo = slice(base, base + m)
                hi = slice(base + m, base + 2 * m)
                k_parts += [k[lo] * jnp.exp2(ref - b[lo]), zeros]
                q_parts += [zeros, qs[hi] * jnp.exp2(b[hi] - ref)]
            qt = jnp.concatenate(q_parts, axis=0).astype(BF16)
            kt = jnp.concatenate(k_parts, axis=0).astype(BF16)
            a = lax.dot_general(qt, kt, NT_DIMS, preferred_element_type=F32)
            att = att + jnp.where(row_xor_col < 2 * m, a, 0.0)
            m //= 2

        o = jnp.dot((qs * jnp.exp2(b)).astype(BF16), state.astype(BF16),
                    preferred_element_type=F32)
        o = o + jnp.dot(att.astype(BF16), v, preferred_element_type=F32)

        b_last = b[c_len - 1:c_len]
        kd_t = (k * jnp.exp2(b_last - b)).T.astype(BF16)
        s_ref[0, h] = (state * _column_of(jnp.exp2(b_last), GLA_DK)
                       + jnp.dot(kd_t, v, preferred_element_type=F32))

        og_ref[:, vs] = _head_norm_gate(o, onorm, gate_ref[:, vs].astype(F32))


def _gla_prompt(proj, g, onorm, og, n_batch, chunks):
    rows = proj.shape[0]
    blk = lambda col: (lambda b, c: (b * chunks + c, col))
    return pl.pallas_call(
        _gla_prompt_kernel,
        grid=(n_batch, chunks),
        in_specs=[
            pl.BlockSpec((CHUNK, GLA_KEY_DIM), blk(0)),
            pl.BlockSpec((CHUNK, GLA_KEY_DIM), blk(1)),
            pl.BlockSpec((CHUNK, GLA_VAL_DIM), blk(1)),
            pl.BlockSpec((CHUNK, GLA_VAL_DIM), blk(2)),
            pl.BlockSpec((CHUNK, GLA_KEY_DIM), blk(0)),
            pl.BlockSpec((1, GLA_DV), lambda b, c: (0, 0)),
            pl.BlockSpec(memory_space=pl.ANY),
        ],
        out_specs=[
            pl.BlockSpec((CHUNK, GLA_VAL_DIM), blk(0)),
            pl.BlockSpec((1, GLA_HEADS, GLA_DK, GLA_DV), lambda b, c: (b, 0, 0, 0)),
        ],
        out_shape=[
            jax.ShapeDtypeStruct((rows, GLA_VAL_DIM), BF16),
            jax.ShapeDtypeStruct((n_batch, GLA_HEADS, GLA_DK, GLA_DV), F32),
        ],
        input_output_aliases={6: 0},
        compiler_params=_params(("parallel", "arbitrary")),
        name="gla_prompt",
    )(proj, proj, proj, proj, g, onorm, og)


def _gla_sample_kernel(t_len, q_ref, k_ref, v_ref, gate_ref, g_ref, onorm_ref, s0_ref,
                       og_in_ref, *rest):
    og_ref, s_ref = rest[-2], rest[-1]
    n = SEQ_GROUP * t_len
    row = lax.broadcasted_iota(jnp.int32, (n, LANES), 0)
    col = lax.broadcasted_iota(jnp.int32, (n, LANES), 1)
    causal = ((row // t_len) == (col // t_len)) & (row >= col)
    row_seq = lax.broadcasted_iota(jnp.int32, (n, 1), 0) // t_len

    qs = q_ref[...].astype(F32) * (GLA_DK ** -0.5)
    k = k_ref[...].astype(F32)
    pad_k = jnp.zeros((LANES - n, GLA_DK), F32)
    v_pad = jnp.concatenate([v_ref[...], jnp.zeros((LANES - n, GLA_DV), BF16)], axis=0)
    b = _masked_row_sums(jnp.where(causal, 1.0, 0.0).astype(BF16),
                         jnp.concatenate([g_ref[...], pad_k], axis=0))
    att = jnp.where(causal, _exact_block(qs, k, b, n, LANES, 0), 0.0)
    o = jnp.dot(att.astype(BF16), v_pad, preferred_element_type=F32)

    q_dec = (qs * jnp.exp(b)).astype(BF16)
    for u in range(SEQ_GROUP):
        state = s0_ref[0, u, 0]
        o_u = jnp.dot(q_dec, state.astype(BF16), preferred_element_type=F32)
        o = o + jnp.where(row_seq == u, o_u, 0.0)
        b_last = b[(u + 1) * t_len - 1:(u + 1) * t_len]
        kd = jnp.where(row_seq == u, k * jnp.exp(jnp.minimum(b_last - b, 0.0)), 0.0)
        kd_t = jnp.concatenate([kd, pad_k], axis=0).T.astype(BF16)
        s_ref[0, u, 0] = (state * _column_of(jnp.exp(b_last), GLA_DK)
                          + jnp.dot(kd_t, v_pad, preferred_element_type=F32))

    og_ref[...] = _head_norm_gate(o, onorm_ref[...], gate_ref[...].astype(F32))


def _gla_sample(proj, g, onorm, state_in, og, s_out, layer, n_seq, t_len, row0):
    n = SEQ_GROUP * t_len
    rb0 = row0 // n
    n_layers = state_in.shape[0]
    key_blocks = GLA_KEY_DIM // GLA_DK
    val_blocks = GLA_VAL_DIM // GLA_DV
    st_spec = pl.BlockSpec((1, SEQ_GROUP, 1, GLA_DK, GLA_DV),
                           lambda s, h: (layer, s, h, 0, 0))
    in_specs = [
        pl.BlockSpec((n, GLA_DK), lambda s, h: (rb0 + s, h)),
        pl.BlockSpec((n, GLA_DK), lambda s, h: (rb0 + s, key_blocks + h)),
        pl.BlockSpec((n, GLA_DV), lambda s, h: (rb0 + s, val_blocks + h)),
        pl.BlockSpec((n, GLA_DV), lambda s, h: (rb0 + s, 2 * val_blocks + h)),
        pl.BlockSpec((n, GLA_DK), lambda s, h: (rb0 + s, h)),
        pl.BlockSpec((1, GLA_DV), lambda s, h: (0, 0)),
        st_spec,
        pl.BlockSpec(memory_space=pl.ANY),
    ]
    args = [proj, proj, proj, proj, g, onorm, state_in, og]
    aliases = {7: 0}
    if s_out is not None:
        in_specs.append(pl.BlockSpec(memory_space=pl.ANY))
        args.append(s_out)
        aliases[8] = 1
    return pl.pallas_call(
        functools.partial(_gla_sample_kernel, t_len),
        grid=(n_seq // SEQ_GROUP, GLA_HEADS),
        in_specs=in_specs,
        out_specs=[
            pl.BlockSpec((n, GLA_DV), lambda s, h: (rb0 + s, h)),
            st_spec,
        ],
        out_shape=[
            jax.ShapeDtypeStruct(og.shape, BF16),
            jax.ShapeDtypeStruct((n_layers, n_seq, GLA_HEADS, GLA_DK, GLA_DV), F32),
        ],
        input_output_aliases=aliases,
        compiler_params=_params(("parallel", "parallel")),
        name="gla_sample",
    )(*args)


def _half_lane_copies(kc, parity):
    kr = pltpu.roll(kc, HEAD_DIM, axis=1)
    return (kc, kr) if parity == 0 else (kr, kc)


def _attend_rows(q_ref, gate_ref, og_ref, sink_ref, kv_sets, s_scr, p_scr):
    n_rows = q_ref.shape[0]
    n_keys = s_scr.shape[2]
    group = N_Q_HEADS // N_KV_HEADS
    low_out = lax.broadcasted_iota(jnp.int32, (n_rows, LANES), 1) < HEAD_DIM
    lane0 = lax.broadcasted_iota(jnp.int32, (n_rows, LANES), 1) == 0
    key_lane = lax.broadcasted_iota(jnp.int32, (n_keys, LANES), 1)
    key_row = lax.broadcasted_iota(jnp.int32, (n_keys, LANES), 0)
    low = key_lane < HEAD_DIM
    slot0 = key_row == 0

    def head_order():
        for cpair in range(KV_WIDTH // LANES):
            for parity in range(2):
                for pr in range(group // 2):
                    yield cpair, parity, (2 * cpair + parity) * (group // 2) + pr

    idx = 0
    for keys, _, _ in kv_sets:
        k_pads = None
        for cpair, parity, qc in head_order():
            if qc % (group // 2) == 0:
                lo_src, hi_src = _half_lane_copies(keys[cpair], parity)
                k_pads = (jnp.where(low, lo_src, 0.0).astype(BF16),
                          jnp.where(low, 0.0, hi_src).astype(BF16))
            q2 = (q_ref[:, qc * LANES:(qc + 1) * LANES].astype(F32)
                  * (HEAD_DIM ** -0.5 * LOG2_E)).astype(BF16)
            for k_pad in k_pads:
                s_scr[idx] = lax.dot_general(q2, k_pad, NT_DIMS,
                                             preferred_element_type=F32)
                idx += 1

    idx = 0
    for _, _, mask in kv_sets:
        mask_first, mask_rest = mask[:, :LANES], mask[:, LANES:]
        for _, _, qc in head_order():
            for half in range(2):
                sink = sink_ref[2 * qc + half] * LOG2_E
                s = s_scr[idx]
                first = jnp.where(mask_first, s[:, :LANES],
                                  jnp.where(lane0, sink, NEG_INF))
                rest = jnp.where(mask_rest, s[:, LANES:], NEG_INF)
                m = jnp.max(jnp.maximum(first, rest), axis=-1, keepdims=True)
                p_scr[idx] = jnp.exp2(
                    jnp.concatenate([first, rest], axis=1) - m).astype(BF16)
                idx += 1

    idx = 0
    out_cols = [None] * (Q_WIDTH // LANES)
    for _, vals, _ in kv_sets:
        v_pads = None
        for cpair, parity, qc in head_order():
            if qc % (group // 2) == 0:
                lo_src, hi_src = _half_lane_copies(vals[cpair], parity)
                v_pads = (jnp.where(low, jnp.where(slot0, 0.0, lo_src), 1.0).astype(BF16),
                          jnp.where(low, 1.0, jnp.where(slot0, 0.0, hi_src)).astype(BF16))
            halves = []
            for v_pad in v_pads:
                o = jnp.dot(p_scr[idx], v_pad, preferred_element_type=F32)
                halves.append(o * (1.0 / pltpu.roll(o, HEAD_DIM, axis=1)))
                idx += 1
            o2 = jnp.where(low_out, halves[0], halves[1])
            out_cols[qc] = o2 if out_cols[qc] is None else out_cols[qc] + o2

    for qc, o in enumerate(out_cols):
        cols = slice(qc * LANES, (qc + 1) * LANES)
        og_ref[:, cols] = (o * _silu(gate_ref[:, cols].astype(F32))).astype(BF16)


def _attention_scratch(n_sets, n_rows, n_keys):
    n = n_sets * N_Q_HEADS
    return [pltpu.VMEM((n, n_rows, n_keys), F32), pltpu.VMEM((n, n_rows, n_keys), BF16)]


def _swa_prompt_kernel(front, sink_ref, q_ref, gate_ref, kc_ref, kp_ref, vc_ref, vp_ref,
                       og_in_ref, og_ref, s_scr, p_scr):
    del og_in_ref
    blk = pl.program_id(1)
    row = lax.broadcasted_iota(jnp.int32, (CHUNK, 2 * CHUNK), 0)
    col = lax.broadcasted_iota(jnp.int32, (CHUNK, 2 * CHUNK), 1)
    mask = (col > row) & (col <= row + WINDOW) & ((blk - 1) * CHUNK + col >= front)
    keys = [jnp.concatenate([kp_ref[:, c * LANES:(c + 1) * LANES],
                             kc_ref[:, c * LANES:(c + 1) * LANES]], axis=0)
            for c in range(KV_WIDTH // LANES)]
    vals = [jnp.concatenate([vp_ref[:, c * LANES:(c + 1) * LANES],
                             vc_ref[:, c * LANES:(c + 1) * LANES]], axis=0)
            for c in range(KV_WIDTH // LANES)]
    _attend_rows(q_ref, gate_ref, og_ref, sink_ref, [(keys, vals, mask)], s_scr, p_scr)


def _swa_prompt(pq, kv, sinks, og, n_batch, chunks, front):
    cur = lambda col: (lambda b, i: (b * chunks + i, col))
    prev = lambda col: (lambda b, i: (b * chunks + jnp.maximum(i - 1, 0), col))
    return pl.pallas_call(
        functools.partial(_swa_prompt_kernel, front),
        grid=(n_batch, chunks),
        in_specs=[
            pl.BlockSpec(memory_space=pltpu.SMEM),
            pl.BlockSpec((CHUNK, Q_WIDTH), cur(0)),
            pl.BlockSpec((CHUNK, Q_WIDTH), cur(1)),
            pl.BlockSpec((CHUNK, KV_WIDTH), cur(0)),
            pl.BlockSpec((CHUNK, KV_WIDTH), prev(0)),
            pl.BlockSpec((CHUNK, KV_WIDTH), cur(1)),
            pl.BlockSpec((CHUNK, KV_WIDTH), prev(1)),
            pl.BlockSpec(memory_space=pl.ANY),
        ],
        out_specs=pl.BlockSpec((CHUNK, Q_WIDTH), cur(0)),
        out_shape=jax.ShapeDtypeStruct(og.shape, BF16),
        scratch_shapes=_attention_scratch(1, CHUNK, 2 * CHUNK),
        input_output_aliases={7: 0},
        compiler_params=_params(("parallel", "parallel")),
        name="swa_prompt",
    )(sinks, pq, pq, kv, kv, kv, kv, og)


def _swa_sample_kernel(t_len, sink_ref, q_ref, gate_ref, kn_ref, vn_ref, ck_ref, cv_ref,
                       og_in_ref, og_ref, s_scr, p_scr):
    del og_in_ref
    n = SEQ_GROUP * t_len
    win = ck_ref.shape[1]
    row = lax.broadcasted_iota(jnp.int32, (n, 2 * win), 0)
    col = lax.broadcasted_iota(jnp.int32, (n, 2 * win), 1)
    t = row % t_len
    new = col - win
    in_window = ((col < win) & (col > t)) | (
        (new >= 0) & (new < n) & ((new // t_len) == (row // t_len)) & ((new % t_len) <= t))
    pad = jnp.zeros((win - n, LANES), F32)
    kv_pairs = []
    for u in range(SEQ_GROUP):
        keys = [jnp.concatenate([ck_ref[u, :, c * LANES:(c + 1) * LANES],
                                 kn_ref[:, c * LANES:(c + 1) * LANES], pad], axis=0)
                for c in range(KV_WIDTH // LANES)]
        vals = [jnp.concatenate([cv_ref[u, :, c * LANES:(c + 1) * LANES],
                                 vn_ref[:, c * LANES:(c + 1) * LANES], pad], axis=0)
                for c in range(KV_WIDTH // LANES)]
        kv_pairs.append((keys, vals, in_window & ((row // t_len) == u)))
    _attend_rows(q_ref, gate_ref, og_ref, sink_ref, kv_pairs, s_scr, p_scr)


def _swa_sample(pq, kv, cache_k, cache_v, sinks, og, n_seq, t_len, row0):
    n = SEQ_GROUP * t_len
    rb0 = row0 // n
    win = cache_k.shape[1]
    return pl.pallas_call(
        functools.partial(_swa_sample_kernel, t_len),
        grid=(n_seq // SEQ_GROUP,),
        in_specs=[
            pl.BlockSpec(memory_space=pltpu.SMEM),
            pl.BlockSpec((n, Q_WIDTH), lambda s: (rb0 + s, 0)),
            pl.BlockSpec((n, Q_WIDTH), lambda s: (rb0 + s, 1)),
            pl.BlockSpec((n, KV_WIDTH), lambda s: (rb0 + s, 0)),
            pl.BlockSpec((n, KV_WIDTH), lambda s: (rb0 + s, 1)),
            pl.BlockSpec((SEQ_GROUP, win, KV_WIDTH), lambda s: (s, 0, 0)),
            pl.BlockSpec((SEQ_GROUP, win, KV_WIDTH), lambda s: (s, 0, 0)),
            pl.BlockSpec(memory_space=pl.ANY),
        ],
        out_specs=pl.BlockSpec((n, Q_WIDTH), lambda s: (rb0 + s, 0)),
        out_shape=jax.ShapeDtypeStruct(og.shape, BF16),
        scratch_shapes=_attention_scratch(SEQ_GROUP, n, 2 * win),
        input_output_aliases={7: 0},
        compiler_params=_params(("parallel",)),
        name="swa_sample",
    )(sinks, pq, pq, kv, kv, cache_k, cache_v, og)


def kernel(x_prompt, x_sample, state_gla, cache_k_win, cache_v_win, meta_tokens, norm_a, w_in_a, w_gk2_a, b_gk2_a, onorm_a, w_out_a, norm_kv, w_kv, norm_b, w_in_b, sinks_b, w_out_b, norm_f):
    n_batch, seq, _ = x_prompt.shape
    n_seq, t_len, _ = x_sample.shape
    win = cache_k_win.shape[1]
    n_a = w_in_a.shape[0]
    n_b = w_in_b.shape[0]
    assert win == WINDOW == CHUNK and (n_seq * t_len) % (SEQ_GROUP * t_len) == 0

    length = seq + N_META
    front = (-length) % CHUNK
    lp = length + front
    chunks = lp // CHUNK
    srow = n_batch * lp
    ns = n_seq * t_len
    rows = -(-(srow + ns) // ROW_TILE) * ROW_TILE

    assert front + N_META == CHUNK and ns == CHUNK and seq % CHUNK == 0
    assert rows % CHUNK == 0 and rows % (16 * IN_PROJ_ROW_STEPS) == 0

    def gate_weights(l):
        w_gate = jnp.pad(w_in_a[l][:, GLA_MAIN_WIDTH:],
                         ((0, 0), (0, LANES - GATE_RANK))).astype(BF16)
        w_gk = jnp.pad(w_gk2_a[l], ((0, LANES - GATE_RANK), (0, 0))).astype(BF16)
        return w_gate, w_gk, b_gk2_a[l][None]

    x, hn, g = _embed(x_prompt.reshape(n_batch * seq, D_MODEL), x_sample.reshape(ns, D_MODEL),
                      meta_tokens, norm_a[0], gate_weights(0), rows, chunks, n_batch, front)
    og = jnp.zeros((rows, D_MODEL), BF16)

    states_p = []
    states_s = None
    for l in range(n_a):
        proj = _in_proj(hn, w_in_a, l, GLA_MAIN_WIDTH, BF16, "gla_in_proj")
        onorm = onorm_a[l][None]
        og, s_p = _gla_prompt(proj, g, onorm, og, n_batch, chunks)
        og, states_s = _gla_sample(proj, g, onorm, state_gla, og, states_s, l,
                                   n_seq, t_len, srow)
        states_p.append(s_p)
        w_out = w_out_a[l].astype(BF16)
        if l + 1 < n_a:
            x, hn, g = _out_proj(og, w_out, x, [norm_a[l + 1]], gate_weights(l + 1), True,
                                 BF16, "gla_out_proj")
        else:
            x, hn_kv, hn = _out_proj(og, w_out, x, [norm_kv, norm_b[0]], None, True,
                                     BF16, "gla_out_proj")

    kv = _in_proj(hn_kv, w_kv[None], 0, 2 * KV_WIDTH, F32, "kv_proj")
    cache_k = cache_k_win.reshape(n_seq, win, KV_WIDTH)
    cache_v = cache_v_win.reshape(n_seq, win, KV_WIDTH)
    for j in range(n_b):
        pq = _in_proj(hn, w_in_b, j, 2 * Q_WIDTH, BF16, "swa_in_proj")
        og = _swa_prompt(pq, kv, sinks_b[j], og, n_batch, chunks, front)
        og = _swa_sample(pq, kv, cache_k, cache_v, sinks_b[j], og, n_seq, t_len, srow)
        w_out = w_out_b[j].astype(BF16)
        if j + 1 < n_b:
            x, hn = _out_proj(og, w_out, x, [norm_b[j + 1]], None, True, BF16, "swa_out_proj")
        else:
            (x,) = _out_proj(og, w_out, x, [norm_f], None, False, F32, "swa_out_proj")

    y_prompt = x[:srow].reshape(n_batch, lp, D_MODEL)[:, lp - seq:]
    y_sample = x[srow:srow + ns].reshape(n_seq, t_len, D_MODEL)
    kv_p = kv[:srow].reshape(n_batch, lp, 2, N_KV_HEADS, HEAD_DIM)[:, lp - win:]
    kv_s = kv[srow:srow + ns].reshape(n_seq, t_len, 2, N_KV_HEADS, HEAD_DIM)
    k_win_s = jnp.concatenate([cache_k_win, kv_s[:, :, 0]], axis=1)[:, -win:]
    v_win_s = jnp.concatenate([cache_v_win, kv_s[:, :, 1]], axis=1)[:, -win:]
    return (y_prompt, y_sample, jnp.stack(states_p), states_s,
            kv_p[:, :, 0], kv_p[:, :, 1], k_win_s, v_win_s)
```

```python
import functools

import jax
import jax.numpy as jnp
from jax import lax
from jax.experimental import pallas as pl
from jax.experimental.pallas import tpu as pltpu

F32 = jnp.float32
BF16 = jnp.bfloat16

D_MODEL = 2048
N_META = 16
GLA_HEADS = 4
GLA_DK = 256
GLA_DV = 512
GLA_KEY_DIM = GLA_HEADS * GLA_DK
GLA_VAL_DIM = GLA_HEADS * GLA_DV
GLA_MAIN_WIDTH = 2 * GLA_KEY_DIM + 2 * GLA_VAL_DIM
GATE_RANK = 16
GATE_LOGIT_NORM = 16.0
HEAD_DIM = 64
N_Q_HEADS = 32
N_KV_HEADS = 8
Q_WIDTH = N_Q_HEADS * HEAD_DIM
KV_WIDTH = N_KV_HEADS * HEAD_DIM
WINDOW = 128
RMS_EPS = 1e-6
NEG_INF = -1e30
LOG2_E = 1.4426950408889634

LANES = 128
CHUNK = 128
GLA_LEAF = 8
LEAF_BATCH = 4
ROW_TILE = 512
IN_PROJ_COLS = 1024
IN_PROJ_ROW_STEPS = 8
SEQ_GROUP = 4
VMEM_LIMIT = 48 * 1024 * 1024

NT_DIMS = (((1,), (1,)), ((), ()))


def _silu(x):
    return x * (1.0 / (1.0 + jnp.exp(-x)))


def _params(sem):
    return pltpu.CompilerParams(dimension_semantics=sem, vmem_limit_bytes=VMEM_LIMIT)


def _emit_normed(y, nw_refs, gate_refs, hn_refs, g_ref):
    ms = jnp.mean(y * y, axis=-1, keepdims=True)
    yn = y * lax.rsqrt(ms + RMS_EPS)
    for nw_ref, hn_ref in zip(nw_refs, hn_refs):
        hn_ref[...] = (yn * nw_ref[...]).astype(hn_ref.dtype)
    if gate_refs:
        wg_ref, wgk_ref, bgk_ref = gate_refs
        glr = lax.dot_general(hn_refs[0][...], wg_ref[...], NT_DIMS,
                              preferred_element_type=F32)
        z = jnp.dot(glr.astype(BF16), wgk_ref[...], preferred_element_type=F32) + bgk_ref[...]
        log_sig = jnp.minimum(z, 0.0) - jnp.log1p(jnp.exp(-jnp.abs(z)))
        g_ref[...] = log_sig * (1.0 / GATE_LOGIT_NORM)


def _split_refs(refs, n_norms, with_gate):
    nw_refs = refs[:n_norms]
    gate_refs = refs[n_norms:n_norms + 3] if with_gate else None
    return nw_refs, gate_refs, refs[n_norms + 3 * with_gate:]


def _norm_specs(norms, gate, index):
    specs = [pl.BlockSpec((1, D_MODEL), index) for _ in norms]
    args = [nw[None] for nw in norms]
    if gate is not None:
        specs += [pl.BlockSpec((LANES, D_MODEL), index),
                  pl.BlockSpec((LANES, GLA_KEY_DIM), index),
                  pl.BlockSpec((1, GLA_KEY_DIM), index)]
        args += list(gate)
    return specs, args


def _embed_kernel(sub, lead_blocks, sample_block, front, with_gate, *refs):
    xp_refs, (xs_ref, meta_ref), refs = refs[:sub], refs[sub:sub + 2], refs[sub + 2:]
    nw_refs, gate_refs, outs = _split_refs(refs, 1, with_gate)
    x_ref = outs[0]
    for k in range(sub):
        blk = pl.program_id(0) * sub + k
        rows = slice(k * CHUNK, (k + 1) * CHUNK)
        is_lead = functools.reduce(jnp.logical_or, [blk == b for b in lead_blocks])

        @pl.when(is_lead)
        def _():
            x_ref[rows, :] = jnp.concatenate(
                [jnp.zeros((front, D_MODEL), F32), meta_ref[...]], axis=0)

        @pl.when(jnp.logical_and(jnp.logical_not(is_lead), blk < sample_block))
        def _():
            x_ref[rows, :] = xp_refs[k][...]

        @pl.when(blk == sample_block)
        def _():
            x_ref[rows, :] = xs_ref[...]

        @pl.when(blk > sample_block)
        def _():
            x_ref[rows, :] = jnp.zeros((CHUNK, D_MODEL), F32)

    _emit_normed(x_ref[...], nw_refs, gate_refs, outs[1:2], outs[2] if with_gate else None)


def _embed(x_prompt2d, x_sample2d, meta, norm, gate, rows, chunks, n_batch, front):
    sub = ROW_TILE // CHUNK
    prompt_blocks = x_prompt2d.shape[0] // CHUNK
    per_batch = prompt_blocks // n_batch

    def prompt_index(k):
        def index(i):
            blk = i * sub + k
            src = (blk // chunks) * per_batch + (blk % chunks) - 1
            return (jnp.clip(src, 0, prompt_blocks - 1), 0)
        return index

    norm_specs, norm_args = _norm_specs([norm], gate, lambda i: (0, 0))
    row_spec = pl.BlockSpec((ROW_TILE, D_MODEL), lambda i: (i, 0))
    out_specs = [row_spec, row_spec]
    out_shape = [jax.ShapeDtypeStruct((rows, D_MODEL), F32),
                 jax.ShapeDtypeStruct((rows, D_MODEL), BF16)]
    if gate is not None:
        out_specs.append(pl.BlockSpec((ROW_TILE, GLA_KEY_DIM), lambda i: (i, 0)))
        out_shape.append(jax.ShapeDtypeStruct((rows, GLA_KEY_DIM), F32))
    return pl.pallas_call(
        functools.partial(_embed_kernel, sub, tuple(b * chunks for b in range(n_batch)),
                          n_batch * chunks, front, gate is not None),
        grid=(rows // ROW_TILE,),
        in_specs=[pl.BlockSpec((CHUNK, D_MODEL), prompt_index(k)) for k in range(sub)] + [
            pl.BlockSpec((CHUNK, D_MODEL), lambda i: (0, 0)),
            pl.BlockSpec((N_META, D_MODEL), lambda i: (0, 0)),
        ] + norm_specs,
        out_specs=out_specs,
        out_shape=out_shape,
        compiler_params=_params(("parallel",)),
        name="embed",
    )(*([x_prompt2d] * sub), x_sample2d, meta, *norm_args)


def _in_proj_kernel(transposed, hn_ref, w_ref, o_ref, w_bf16_ref):
    @pl.when(pl.program_id(1) == 0)
    def _():
        w_bf16_ref[...] = w_ref[...].astype(BF16)

    dims = NT_DIMS if transposed else (((1,), (0,)), ((), ()))
    o_ref[...] = lax.dot_general(hn_ref[...], w_bf16_ref[...], dims,
                                 preferred_element_type=F32).astype(o_ref.dtype)


def _in_proj(hn, w_stack, layer, n_cols, out_dtype, transposed, name):
    rows = hn.shape[0]
    row_tile = rows // IN_PROJ_ROW_STEPS
    if transposed:
        slab = (IN_PROJ_COLS, D_MODEL)
        w_spec = pl.BlockSpec((None,) + slab, lambda j, i: (layer, j, 0))
    else:
        slab = (D_MODEL, IN_PROJ_COLS)
        w_spec = pl.BlockSpec((None,) + slab, lambda j, i: (layer, 0, j))
    return pl.pallas_call(
        functools.partial(_in_proj_kernel, transposed),
        grid=(n_cols // IN_PROJ_COLS, IN_PROJ_ROW_STEPS),
        in_specs=[pl.BlockSpec((row_tile, D_MODEL), lambda j, i: (i, 0)), w_spec],
        out_specs=pl.BlockSpec((row_tile, IN_PROJ_COLS), lambda j, i: (i, j)),
        out_shape=jax.ShapeDtypeStruct((rows, n_cols), out_dtype),
        scratch_shapes=[pltpu.VMEM(slab, BF16)],
        compiler_params=_params(("parallel", "arbitrary")),
        name=name,
    )(hn, w_stack)


def _out_proj_kernel(sub, n_norms, with_gate, keep_x, *refs):
    og_refs, w_ref, x_refs, refs = refs[:sub], refs[sub], refs[sub + 1:2 * sub + 1], refs[2 * sub + 1:]
    nw_refs, gate_refs, outs = _split_refs(refs, n_norms, with_gate)
    if sub == 1:
        og, x = og_refs[0][...], x_refs[0][...]
    else:
        og = jnp.concatenate([r[...] for r in og_refs], axis=0)
        x = jnp.concatenate([r[...] for r in x_refs], axis=0)
    y = x + jnp.dot(og, w_ref[...], preferred_element_type=F32)
    if keep_x:
        outs[0][...] = y
        outs = outs[1:]
    _emit_normed(y, nw_refs, gate_refs, outs[:n_norms], outs[n_norms] if with_gate else None)


def _out_proj(og, w, x, norms, gate, keep_x, normed_dtype, name, gather=None):
    if gather is None:
        tile_rows, n_tiles, sub = ROW_TILE, x.shape[0] // ROW_TILE, 1
        in_row_specs = [pl.BlockSpec((ROW_TILE, D_MODEL), lambda i: (i, 0))]
    else:
        tile_rows, n_tiles, first_block = gather
        sub = tile_rows // CHUNK
        in_row_specs = [pl.BlockSpec((CHUNK, D_MODEL),
                                     functools.partial(lambda k, i: (first_block(i) + k, 0), k))
                        for k in range(sub)]
    rows = tile_rows * n_tiles
    row_spec = pl.BlockSpec((tile_rows, D_MODEL), lambda i: (i, 0))
    norm_specs, norm_args = _norm_specs(norms, gate, lambda i: (0, 0))
    out_specs, out_shape = [], []
    if keep_x:
        out_specs.append(row_spec)
        out_shape.append(jax.ShapeDtypeStruct((rows, D_MODEL), F32))
    for _ in norms:
        out_specs.append(row_spec)
        out_shape.append(jax.ShapeDtypeStruct((rows, D_MODEL), normed_dtype))
    if gate is not None:
        out_specs.append(pl.BlockSpec((tile_rows, GLA_KEY_DIM), lambda i: (i, 0)))
        out_shape.append(jax.ShapeDtypeStruct((rows, GLA_KEY_DIM), F32))
    w_spec = pl.BlockSpec((D_MODEL, D_MODEL), lambda i: (0, 0), pipeline_mode=pl.Buffered(1))
    return pl.pallas_call(
        functools.partial(_out_proj_kernel, sub, len(norms), gate is not None, keep_x),
        grid=(n_tiles,),
        in_specs=in_row_specs + [w_spec] + in_row_specs + norm_specs,
        out_specs=out_specs,
        out_shape=out_shape,
        compiler_params=_params(("parallel",)),
        name=name,
    )(*([og] * sub), w, *([x] * sub), *norm_args)


def _exact_block(qs, k, b, n, width, col0):
    lane = lax.broadcasted_iota(jnp.int32, (n, width), 1)
    acc = jnp.zeros((n, width), F32)
    for j in range(n):
        t = qs * k[j:j + 1] * jnp.exp(jnp.minimum(b - b[j:j + 1], 0.0))
        acc = jnp.where(lane == col0 + j, jnp.sum(t, axis=-1, keepdims=True), acc)
    return acc


def _masked_row_sums(mask_bf16, g):
    g_hi = g.astype(BF16)
    r1 = g - g_hi.astype(F32)
    g_mid = r1.astype(BF16)
    g_lo = (r1 - g_mid.astype(F32)).astype(BF16)
    return (jnp.dot(mask_bf16, g_hi, preferred_element_type=F32)
            + jnp.dot(mask_bf16, g_mid, preferred_element_type=F32)
            + jnp.dot(mask_bf16, g_lo, preferred_element_type=F32))


def _column_of(row_vec, n):
    return jnp.broadcast_to(row_vec, (LANES, n)).T[:, 0:1]


def _head_norm_gate(o, onorm, gate):
    ms = jnp.mean(o * o, axis=-1, keepdims=True)
    return (o * lax.rsqrt(ms + RMS_EPS) * onorm * _silu(gate)).astype(BF16)


def _gla_prompt_kernel(q_ref, k_ref, v_ref, gate_ref, g_ref, onorm_ref, og_in_ref,
                       og_ref, s_ref):
    del og_in_ref
    c_len = CHUNK
    leaf = GLA_LEAF

    @pl.when(pl.program_id(1) == 0)
    def _():
        s_ref[...] = jnp.zeros_like(s_ref)

    row = lax.broadcasted_iota(jnp.int32, (c_len, c_len), 0)
    col = lax.broadcasted_iota(jnp.int32, (c_len, c_len), 1)
    tri = jnp.where(row >= col, 1.0, 0.0).astype(BF16)
    row_xor_col = row ^ col
    leaf_row = lax.broadcasted_iota(jnp.int32, (leaf, c_len), 0)
    leaf_col = lax.broadcasted_iota(jnp.int32, (leaf, c_len), 1)
    lane_in_leaf = [(leaf_col & (leaf - 1)) == j for j in range(leaf)]
    onorm = onorm_ref[...]

    for h in range(GLA_HEADS):
        ks = slice(h * GLA_DK, (h + 1) * GLA_DK)
        vs = slice(h * GLA_DV, (h + 1) * GLA_DV)
        qs = q_ref[:, ks].astype(F32) * (GLA_DK ** -0.5)
        k_bf = k_ref[:, ks]
        k = k_bf.astype(F32)
        v = v_ref[:, vs]
        b = _masked_row_sums(tri, g_ref[:, ks]) * LOG2_E
        state = s_ref[0, h]

        att_rows = []
        for blk0 in range(0, c_len // leaf, LEAF_BATCH):
            lhs = []
            for blk in range(blk0, blk0 + LEAF_BATCH):
                q_blk = qs[blk * leaf:(blk + 1) * leaf]
                b_blk = b[blk * leaf:(blk + 1) * leaf]
                lhs += [q_blk * jnp.exp2(jnp.minimum(b_blk - b_blk[j:j + 1], 0.0))
                        for j in range(leaf)]
            r = lax.dot_general(jnp.concatenate(lhs, axis=0).astype(BF16), k_bf, NT_DIMS,
                                preferred_element_type=F32)
            for i, blk in enumerate(range(blk0, blk0 + LEAF_BATCH)):
                acc = jnp.zeros((leaf, c_len), F32)
                for j in range(leaf):
                    r0 = (i * leaf + j) * leaf
                    acc = jnp.where(lane_in_leaf[j], r[r0:r0 + leaf], acc)
                rel = leaf_col - blk * leaf
                att_rows.append(jnp.where((rel >= 0) & (rel <= leaf_row), acc, 0.0))
        att = jnp.concatenate(att_rows, axis=0)

        m = c_len // 2
        while m >= leaf:
            q_parts, k_parts = [], []
            zeros = jnp.zeros((m, GLA_DK), F32)
            for base in range(0, c_len, 2 * m):
                ref = b[base + m - 1:base + m]
                lo = slice(base, base + m)
                hi = slice(base + m, base + 2 * m)
                k_parts += [k[lo] * jnp.exp2(ref - b[lo]), zeros]
                q_parts += [zeros, qs[hi] * jnp.exp2(b[hi] - ref)]
            qt = jnp.concatenate(q_parts, axis=0).astype(BF16)
            kt = jnp.concatenate(k_parts, axis=0).astype(BF16)
            a = lax.dot_general(qt, kt, NT_DIMS, preferred_element_type=F32)
            att = att + jnp.where(row_xor_col < 2 * m, a, 0.0)
            m //= 2

        o = jnp.dot((qs * jnp.exp2(b)).astype(BF16), state.astype(BF16),
                    preferred_element_type=F32)
        o = o + jnp.dot(att.astype(BF16), v, preferred_element_type=F32)

        b_last = b[c_len - 1:c_len]
        kd_t = (k * jnp.exp2(b_last - b)).T.astype(BF16)
        s_ref[0, h] = (state * _column_of(jnp.exp2(b_last), GLA_DK)
                       + jnp.dot(kd_t, v, preferred_element_type=F32))

        og_ref[:, vs] = _head_norm_gate(o, onorm, gate_ref[:, vs].astype(F32))


def _gla_prompt(proj, g, onorm, og, n_batch, chunks):
    rows = proj.shape[0]
    blk = lambda col: (lambda b, c: (b * chunks + c, col))
    return pl.pallas_call(
        _gla_prompt_kernel,
        grid=(n_batch, chunks),
        in_specs=[
            pl.BlockSpec((CHUNK, GLA_KEY_DIM), blk(0)),
            pl.BlockSpec((CHUNK, GLA_KEY_DIM), blk(1)),
            pl.BlockSpec((CHUNK, GLA_VAL_DIM), blk(1)),
            pl.BlockSpec((CHUNK, GLA_VAL_DIM), blk(2)),
            pl.BlockSpec((CHUNK, GLA_KEY_DIM), blk(0)),
            pl.BlockSpec((1, GLA_DV), lambda b, c: (0, 0)),
            pl.BlockSpec(memory_space=pl.ANY),
        ],
        out_specs=[
            pl.BlockSpec((CHUNK, GLA_VAL_DIM), blk(0)),
            pl.BlockSpec((1, GLA_HEADS, GLA_DK, GLA_DV), lambda b, c: (b, 0, 0, 0)),
        ],
        out_shape=[
            jax.ShapeDtypeStruct((rows, GLA_VAL_DIM), BF16),
            jax.ShapeDtypeStruct((n_batch, GLA_HEADS, GLA_DK, GLA_DV), F32),
        ],
        input_output_aliases={6: 0},
        compiler_params=_params(("parallel", "arbitrary")),
        name="gla_prompt",
    )(proj, proj, proj, proj, g, onorm, og)


def _gla_sample_kernel(t_len, q_ref, k_ref, v_ref, gate_ref, g_ref, onorm_ref, s0_ref,
                       og_in_ref, *rest):
    og_ref, s_ref = rest[-2], rest[-1]
    n = SEQ_GROUP * t_len
    row = lax.broadcasted_iota(jnp.int32, (n, LANES), 0)
    col = lax.broadcasted_iota(jnp.int32, (n, LANES), 1)
    causal = ((row // t_len) == (col // t_len)) & (row >= col)
    row_seq = lax.broadcasted_iota(jnp.int32, (n, 1), 0) // t_len

    qs = q_ref[...].astype(F32) * (GLA_DK ** -0.5)
    k = k_ref[...].astype(F32)
    pad_k = jnp.zeros((LANES - n, GLA_DK), F32)
    v_pad = jnp.concatenate([v_ref[...], jnp.zeros((LANES - n, GLA_DV), BF16)], axis=0)
    b = _masked_row_sums(jnp.where(causal, 1.0, 0.0).astype(BF16),
                         jnp.concatenate([g_ref[...], pad_k], axis=0))
    att = jnp.where(causal, _exact_block(qs, k, b, n, LANES, 0), 0.0)
    o = jnp.dot(att.astype(BF16), v_pad, preferred_element_type=F32)

    q_dec = (qs * jnp.exp(b)).astype(BF16)
    for u in range(SEQ_GROUP):
        state = s0_ref[0, u, 0]
        o_u = jnp.dot(q_dec, state.astype(BF16), preferred_element_type=F32)
        o = o + jnp.where(row_seq == u, o_u, 0.0)
        b_last = b[(u + 1) * t_len - 1:(u + 1) * t_len]
        kd = jnp.where(row_seq == u, k * jnp.exp(jnp.minimum(b_last - b, 0.0)), 0.0)
        kd_t = jnp.concatenate([kd, pad_k], axis=0).T.astype(BF16)
        s_ref[0, u, 0] = (state * _column_of(jnp.exp(b_last), GLA_DK)
                          + jnp.dot(kd_t, v_pad, preferred_element_type=F32))

    og_ref[...] = _head_norm_gate(o, onorm_ref[...], gate_ref[...].astype(F32))


def _gla_sample(proj, g, onorm, state_in, og, s_out, layer, n_seq, t_len, row0):
    n = SEQ_GROUP * t_len
    rb0 = row0 // n
    n_layers = state_in.shape[0]
    key_blocks = GLA_KEY_DIM // GLA_DK
    val_blocks = GLA_VAL_DIM // GLA_DV
    st_spec = pl.BlockSpec((1, SEQ_GROUP, 1, GLA_DK, GLA_DV),
                           lambda s, h: (layer, s, h, 0, 0))
    in_specs = [
        pl.BlockSpec((n, GLA_DK), lambda s, h: (rb0 + s, h)),
        pl.BlockSpec((n, GLA_DK), lambda s, h: (rb0 + s, key_blocks + h)),
        pl.BlockSpec((n, GLA_DV), lambda s, h: (rb0 + s, val_blocks + h)),
        pl.BlockSpec((n, GLA_DV), lambda s, h: (rb0 + s, 2 * val_blocks + h)),
        pl.BlockSpec((n, GLA_DK), lambda s, h: (rb0 + s, h)),
        pl.BlockSpec((1, GLA_DV), lambda s, h: (0, 0)),
        st_spec,
        pl.BlockSpec(memory_space=pl.ANY),
    ]
    args = [proj, proj, proj, proj, g, onorm, state_in, og]
    aliases = {7: 0}
    if s_out is not None:
        in_specs.append(pl.BlockSpec(memory_space=pl.ANY))
        args.append(s_out)
        aliases[8] = 1
    return pl.pallas_call(
        functools.partial(_gla_sample_kernel, t_len),
        grid=(n_seq // SEQ_GROUP, GLA_HEADS),
        in_specs=in_specs,
        out_specs=[
            pl.BlockSpec((n, GLA_DV), lambda s, h: (rb0 + s, h)),
            st_spec,
        ],
        out_shape=[
            jax.ShapeDtypeStruct(og.shape, BF16),
            jax.ShapeDtypeStruct((n_layers, n_seq, GLA_HEADS, GLA_DK, GLA_DV), F32),
        ],
        input_output_aliases=aliases,
        compiler_params=_params(("parallel", "parallel")),
        name="gla_sample",
    )(*args)


def _half_lane_copies(kc, parity):
    kr = pltpu.roll(kc, HEAD_DIM, axis=1)
    return (kc, kr) if parity == 0 else (kr, kc)


def _attend_rows(q_ref, gate_ref, og_ref, sink_ref, kv_sets, s_scr, p_scr):
    n_rows = q_ref.shape[0]
    n_keys = s_scr.shape[2]
    group = N_Q_HEADS // N_KV_HEADS
    low_out = lax.broadcasted_iota(jnp.int32, (n_rows, LANES), 1) < HEAD_DIM
    lane0 = lax.broadcasted_iota(jnp.int32, (n_rows, LANES), 1) == 0
    key_lane = lax.broadcasted_iota(jnp.int32, (n_keys, LANES), 1)
    key_row = lax.broadcasted_iota(jnp.int32, (n_keys, LANES), 0)
    low = key_lane < HEAD_DIM
    slot0 = key_row == 0

    def head_order():
        for cpair in range(KV_WIDTH // LANES):
            for parity in range(2):
                for pr in range(group // 2):
                    yield cpair, parity, (2 * cpair + parity) * (group // 2) + pr

    idx = 0
    for keys, _, _ in kv_sets:
        k_pads = None
        for cpair, parity, qc in head_order():
            if qc % (group // 2) == 0:
                lo_src, hi_src = _half_lane_copies(keys[cpair], parity)
                k_pads = (jnp.where(low, lo_src, 0.0).astype(BF16),
                          jnp.where(low, 0.0, hi_src).astype(BF16))
            q2 = (q_ref[:, qc * LANES:(qc + 1) * LANES].astype(F32)
                  * (HEAD_DIM ** -0.5 * LOG2_E)).astype(BF16)
            for k_pad in k_pads:
                s_scr[idx] = lax.dot_general(q2, k_pad, NT_DIMS,
                                             preferred_element_type=F32)
                idx += 1

    idx = 0
    for _, _, mask in kv_sets:
        mask_first, mask_rest = mask[:, :LANES], mask[:, LANES:]
        for _, _, qc in head_order():
            for half in range(2):
                sink = sink_ref[2 * qc + half] * LOG2_E
                s = s_scr[idx]
                first = jnp.where(mask_first, s[:, :LANES],
                                  jnp.where(lane0, sink, NEG_INF))
                rest = jnp.where(mask_rest, s[:, LANES:], NEG_INF)
                m = jnp.max(jnp.maximum(first, rest), axis=-1, keepdims=True)
                p_scr[idx] = jnp.exp2(
                    jnp.concatenate([first, rest], axis=1) - m).astype(BF16)
                idx += 1

    idx = 0
    out_cols = [None] * (Q_WIDTH // LANES)
    for _, vals, _ in kv_sets:
        v_pads = None
        for cpair, parity, qc in head_order():
            if qc % (group // 2) == 0:
                lo_src, hi_src = _half_lane_copies(vals[cpair], parity)
                v_pads = (jnp.where(low, jnp.where(slot0, 0.0, lo_src), 1.0).astype(BF16),
                          jnp.where(low, 1.0, jnp.where(slot0, 0.0, hi_src)).astype(BF16))
            halves = []
            for v_pad in v_pads:
                o = jnp.dot(p_scr[idx], v_pad, preferred_element_type=F32)
                halves.append(o * (1.0 / pltpu.roll(o, HEAD_DIM, axis=1)))
                idx += 1
            o2 = jnp.where(low_out, halves[0], halves[1])
            out_cols[qc] = o2 if out_cols[qc] is None else out_cols[qc] + o2

    for qc, o in enumerate(out_cols):
        cols = slice(qc * LANES, (qc + 1) * LANES)
        og_ref[:, cols] = (o * _silu(gate_ref[:, cols].astype(F32))).astype(BF16)


def _attention_scratch(n_sets, n_rows, n_keys):
    n = n_sets * N_Q_HEADS
    return [pltpu.VMEM((n, n_rows, n_keys), F32), pltpu.VMEM((n, n_rows, n_keys), BF16)]


def _swa_prompt_kernel(front, sink_ref, q_ref, gate_ref, kc_ref, kp_ref, vc_ref, vp_ref,
                       og_in_ref, og_ref, s_scr, p_scr):
    del og_in_ref
    blk = pl.program_id(1)
    row = lax.broadcasted_iota(jnp.int32, (CHUNK, 2 * CHUNK), 0)
    col = lax.broadcasted_iota(jnp.int32, (CHUNK, 2 * CHUNK), 1)
    mask = (col > row) & (col <= row + WINDOW) & ((blk - 1) * CHUNK + col >= front)
    keys = [jnp.concatenate([kp_ref[:, c * LANES:(c + 1) * LANES],
                             kc_ref[:, c * LANES:(c + 1) * LANES]], axis=0)
            for c in range(KV_WIDTH // LANES)]
    vals = [jnp.concatenate([vp_ref[:, c * LANES:(c + 1) * LANES],
                             vc_ref[:, c * LANES:(c + 1) * LANES]], axis=0)
            for c in range(KV_WIDTH // LANES)]
    _attend_rows(q_ref, gate_ref, og_ref, sink_ref, [(keys, vals, mask)], s_scr, p_scr)


def _swa_prompt(pq, kv, sinks, og, n_batch, chunks, front):
    cur = lambda col: (lambda b, i: (b * chunks + i, col))
    prev = lambda col: (lambda b, i: (b * chunks + jnp.maximum(i - 1, 0), col))
    return pl.pallas_call(
        functools.partial(_swa_prompt_kernel, front),
        grid=(n_batch, chunks),
        in_specs=[
            pl.BlockSpec(memory_space=pltpu.SMEM),
            pl.BlockSpec((CHUNK, Q_WIDTH), cur(0)),
            pl.BlockSpec((CHUNK, Q_WIDTH), cur(1)),
            pl.BlockSpec((CHUNK, KV_WIDTH), cur(0)),
            pl.BlockSpec((CHUNK, KV_WIDTH), prev(0)),
            pl.BlockSpec((CHUNK, KV_WIDTH), cur(1)),
            pl.BlockSpec((CHUNK, KV_WIDTH), prev(1)),
            pl.BlockSpec(memory_space=pl.ANY),
        ],
        out_specs=pl.BlockSpec((CHUNK, Q_WIDTH), cur(0)),
        out_shape=jax.ShapeDtypeStruct(og.shape, BF16),
        scratch_shapes=_attention_scratch(1, CHUNK, 2 * CHUNK),
        input_output_aliases={7: 0},
        compiler_params=_params(("parallel", "parallel")),
        name="swa_prompt",
    )(sinks, pq, pq, kv, kv, kv, kv, og)


def _swa_sample_kernel(t_len, sink_ref, q_ref, gate_ref, kn_ref, vn_ref, ck_ref, cv_ref,
                       og_in_ref, og_ref, s_scr, p_scr):
    del og_in_ref
    n = SEQ_GROUP * t_len
    win = ck_ref.shape[1]
    row = lax.broadcasted_iota(jnp.int32, (n, 2 * win), 0)
    col = lax.broadcasted_iota(jnp.int32, (n, 2 * win), 1)
    t = row % t_len
    new = col - win
    in_window = ((col < win) & (col > t)) | (
        (new >= 0) & (new < n) & ((new // t_len) == (row // t_len)) & ((new % t_len) <= t))
    pad = jnp.zeros((win - n, LANES), F32)
    kv_pairs = []
    for u in range(SEQ_GROUP):
        keys = [jnp.concatenate([ck_ref[u, :, c * LANES:(c + 1) * LANES],
                                 kn_ref[:, c * LANES:(c + 1) * LANES], pad], axis=0)
                for c in range(KV_WIDTH // LANES)]
        vals = [jnp.concatenate([cv_ref[u, :, c * LANES:(c + 1) * LANES],
                                 vn_ref[:, c * LANES:(c + 1) * LANES], pad], axis=0)
                for c in range(KV_WIDTH // LANES)]
        kv_pairs.append((keys, vals, in_window & ((row // t_len) == u)))
    _attend_rows(q_ref, gate_ref, og_ref, sink_ref, kv_pairs, s_scr, p_scr)


def _swa_sample(pq, kv, cache_k, cache_v, sinks, og, n_seq, t_len, row0):
    n = SEQ_GROUP * t_len
    rb0 = row0 // n
    win = cache_k.shape[1]
    return pl.pallas_call(
        functools.partial(_swa_sample_kernel, t_len),
        grid=(n_seq // SEQ_GROUP,),
        in_specs=[
            pl.BlockSpec(memory_space=pltpu.SMEM),
            pl.BlockSpec((n, Q_WIDTH), lambda s: (rb0 + s, 0)),
            pl.BlockSpec((n, Q_WIDTH), lambda s: (rb0 + s, 1)),
            pl.BlockSpec((n, KV_WIDTH), lambda s: (rb0 + s, 0)),
            pl.BlockSpec((n, KV_WIDTH), lambda s: (rb0 + s, 1)),
            pl.BlockSpec((SEQ_GROUP, win, KV_WIDTH), lambda s: (s, 0, 0)),
            pl.BlockSpec((SEQ_GROUP, win, KV_WIDTH), lambda s: (s, 0, 0)),
            pl.BlockSpec(memory_space=pl.ANY),
        ],
        out_specs=pl.BlockSpec((n, Q_WIDTH), lambda s: (rb0 + s, 0)),
        out_shape=jax.ShapeDtypeStruct(og.shape, BF16),
        scratch_shapes=_attention_scratch(SEQ_GROUP, n, 2 * win),
        input_output_aliases={7: 0},
        compiler_params=_params(("parallel",)),
        name="swa_sample",
    )(sinks, pq, pq, kv, kv, cache_k, cache_v, og)


def kernel(x_prompt, x_sample, state_gla, cache_k_win, cache_v_win, meta_tokens, norm_a, w_in_a, w_gk2_a, b_gk2_a, onorm_a, w_out_a, norm_kv, w_kv, norm_b, w_in_b, sinks_b, w_out_b, norm_f):
    n_batch, seq, _ = x_prompt.shape
    n_seq, t_len, _ = x_sample.shape
    win = cache_k_win.shape[1]
    n_a = w_in_a.shape[0]
    n_b = w_in_b.shape[0]
    assert win == WINDOW == CHUNK and (n_seq * t_len) % (SEQ_GROUP * t_len) == 0

    length = seq + N_META
    front = (-length) % CHUNK
    lp = length + front
    chunks = lp // CHUNK
    srow = n_batch * lp
    ns = n_seq * t_len
    rows = -(-(srow + ns) // ROW_TILE) * ROW_TILE

    assert front + N_META == CHUNK and ns == CHUNK and seq % CHUNK == 0
    assert rows % CHUNK == 0 and rows % (16 * IN_PROJ_ROW_STEPS) == 0

    w_in_a_t = jnp.swapaxes(w_in_a, 1, 2)

    def gate_weights(l):
        w_gate = jnp.pad(w_in_a_t[l, GLA_MAIN_WIDTH:],
                         ((0, LANES - GATE_RANK), (0, 0))).astype(BF16)
        w_gk = jnp.pad(w_gk2_a[l], ((0, LANES - GATE_RANK), (0, 0))).astype(BF16)
        return w_gate, w_gk, b_gk2_a[l][None]

    x, hn, g = _embed(x_prompt.reshape(n_batch * seq, D_MODEL), x_sample.reshape(ns, D_MODEL),
                      meta_tokens, norm_a[0], gate_weights(0), rows, chunks, n_batch, front)
    og = jnp.zeros((rows, D_MODEL), BF16)

    states_p = []
    states_s = None
    for l in range(n_a):
        proj = _in_proj(hn, w_in_a_t, l, GLA_MAIN_WIDTH, BF16, True, "gla_in_proj")
        onorm = onorm_a[l][None]
        og, s_p = _gla_prompt(proj, g, onorm, og, n_batch, chunks)
        og, states_s = _gla_sample(proj, g, onorm, state_gla, og, states_s, l,
                                   n_seq, t_len, srow)
        states_p.append(s_p)
        w_out = w_out_a[l].astype(BF16)
        if l + 1 < n_a:
            x, hn, g = _out_proj(og, w_out, x, [norm_a[l + 1]], gate_weights(l + 1), True,
                                 BF16, "gla_out_proj")
        else:
            x, hn_kv, hn = _out_proj(og, w_out, x, [norm_kv, norm_b[0]], None, True,
                                     BF16, "gla_out_proj")

    kv = _in_proj(hn_kv, w_kv[None], 0, 2 * KV_WIDTH, F32, False, "kv_proj")
    cache_k = cache_k_win.reshape(n_seq, win, KV_WIDTH)
    cache_v = cache_v_win.reshape(n_seq, win, KV_WIDTH)
    for j in range(n_b):
        pq = _in_proj(hn, w_in_b, j, 2 * Q_WIDTH, BF16, False, "swa_in_proj")
        og = _swa_prompt(pq, kv, sinks_b[j], og, n_batch, chunks, front)
        og = _swa_sample(pq, kv, cache_k, cache_v, sinks_b[j], og, n_seq, t_len, srow)
        w_out = w_out_b[j].astype(BF16)
        if j + 1 < n_b:
            x, hn = _out_proj(og, w_out, x, [norm_b[j + 1]], None, True, BF16, "swa_out_proj")

    tiles_per_batch = seq // ROW_TILE
    lead_blocks = (lp - seq) // CHUNK

    def prompt_first_block(t):
        return ((t // tiles_per_batch) * chunks + lead_blocks
                + (t % tiles_per_batch) * (ROW_TILE // CHUNK))

    (y_prompt,) = _out_proj(og, w_out, x, [norm_f], None, False, F32, "final_prompt",
                            gather=(ROW_TILE, n_batch * tiles_per_batch, prompt_first_block))
    (y_sample,) = _out_proj(og, w_out, x, [norm_f], None, False, F32, "final_sample",
                            gather=(CHUNK, ns // CHUNK, lambda t: srow // CHUNK + t))
    y_prompt = y_prompt.reshape(n_batch, seq, D_MODEL)
    y_sample = y_sample.reshape(n_seq, t_len, D_MODEL)
    kv_p = jnp.stack([kv[(b + 1) * lp - win:(b + 1) * lp] for b in range(n_batch)])
    kv_p = kv_p.reshape(n_batch, win, 2, N_KV_HEADS, HEAD_DIM)
    kv_s = kv[srow:srow + ns].reshape(n_seq, t_len, 2, N_KV_HEADS, HEAD_DIM)
    k_win_s = jnp.concatenate([cache_k_win, kv_s[:, :, 0]], axis=1)[:, -win:]
    v_win_s = jnp.concatenate([cache_v_win, kv_s[:, :, 1]], axis=1)[:, -win:]
    return (y_prompt, y_sample, jnp.stack(states_p), states_s,
            kv_p[:, :, 0], kv_p[:, :, 1], k_win_s, v_win_s)
```

```python
import functools

import jax
import jax.numpy as jnp
from jax import lax
from jax.experimental import pallas as pl
from jax.experimental.pallas import tpu as pltpu

F32 = jnp.float32
BF16 = jnp.bfloat16

D_MODEL = 2048
N_META = 16
GLA_HEADS = 4
GLA_DK = 256
GLA_DV = 512
GLA_KEY_DIM = GLA_HEADS * GLA_DK
GLA_VAL_DIM = GLA_HEADS * GLA_DV
GLA_MAIN_WIDTH = 2 * GLA_KEY_DIM + 2 * GLA_VAL_DIM
GATE_RANK = 16
GATE_LOGIT_NORM = 16.0
HEAD_DIM = 64
N_Q_HEADS = 32
N_KV_HEADS = 8
Q_WIDTH = N_Q_HEADS * HEAD_DIM
KV_WIDTH = N_KV_HEADS * HEAD_DIM
WINDOW = 128
RMS_EPS = 1e-6
NEG_INF = -1e30
LOG2_E = 1.4426950408889634

LANES = 128
CHUNK = 128
GLA_LEAF = 8
LEAF_BATCH = 4
GLA_BOUNDED_RANGE = 60.0
ROW_TILE = 512
IN_PROJ_COLS = 1024
IN_PROJ_ROW_STEPS = 8
SEQ_GROUP = 4
VMEM_LIMIT = 48 * 1024 * 1024

NT_DIMS = (((1,), (1,)), ((), ()))


def _silu(x):
    return x * (1.0 / (1.0 + jnp.exp(-x)))


def _params(sem):
    return pltpu.CompilerParams(dimension_semantics=sem, vmem_limit_bytes=VMEM_LIMIT)


def _emit_normed(y, nw_refs, gate_refs, hn_refs, g_ref):
    ms = jnp.mean(y * y, axis=-1, keepdims=True)
    yn = y * lax.rsqrt(ms + RMS_EPS)
    for nw_ref, hn_ref in zip(nw_refs, hn_refs):
        hn_ref[...] = (yn * nw_ref[...]).astype(hn_ref.dtype)
    if gate_refs:
        wg_ref, wgk_ref, bgk_ref = gate_refs
        glr = lax.dot_general(hn_refs[0][...], wg_ref[...], NT_DIMS,
                              preferred_element_type=F32)
        z = jnp.dot(glr.astype(BF16), wgk_ref[...], preferred_element_type=F32) + bgk_ref[...]
        log_sig = jnp.minimum(z, 0.0) - jnp.log(1.0 + jnp.exp(-jnp.abs(z)))
        g_ref[...] = log_sig * (1.0 / GATE_LOGIT_NORM)


def _split_refs(refs, n_norms, with_gate):
    nw_refs = refs[:n_norms]
    gate_refs = refs[n_norms:n_norms + 3] if with_gate else None
    return nw_refs, gate_refs, refs[n_norms + 3 * with_gate:]


def _norm_specs(norms, gate, index):
    specs = [pl.BlockSpec((1, D_MODEL), index) for _ in norms]
    args = [nw[None] for nw in norms]
    if gate is not None:
        specs += [pl.BlockSpec((LANES, D_MODEL), index),
                  pl.BlockSpec((LANES, GLA_KEY_DIM), index),
                  pl.BlockSpec((1, GLA_KEY_DIM), index)]
        args += list(gate)
    return specs, args


def _embed_kernel(sub, lead_blocks, sample_block, front, with_gate, *refs):
    xp_refs, (xs_ref, meta_ref), refs = refs[:sub], refs[sub:sub + 2], refs[sub + 2:]
    nw_refs, gate_refs, outs = _split_refs(refs, 1, with_gate)
    x_ref = outs[0]
    for k in range(sub):
        blk = pl.program_id(0) * sub + k
        rows = slice(k * CHUNK, (k + 1) * CHUNK)
        is_lead = functools.reduce(jnp.logical_or, [blk == b for b in lead_blocks])

        @pl.when(is_lead)
        def _():
            x_ref[rows, :] = jnp.concatenate(
                [jnp.zeros((front, D_MODEL), F32), meta_ref[...]], axis=0)

        @pl.when(jnp.logical_and(jnp.logical_not(is_lead), blk < sample_block))
        def _():
            x_ref[rows, :] = xp_refs[k][...]

        @pl.when(blk == sample_block)
        def _():
            x_ref[rows, :] = xs_ref[...]

        @pl.when(blk > sample_block)
        def _():
            x_ref[rows, :] = jnp.zeros((CHUNK, D_MODEL), F32)

    _emit_normed(x_ref[...], nw_refs, gate_refs, outs[1:2], outs[2] if with_gate else None)


def _embed(x_prompt2d, x_sample2d, meta, norm, gate, rows, chunks, n_batch, front):
    sub = ROW_TILE // CHUNK
    prompt_blocks = x_prompt2d.shape[0] // CHUNK
    per_batch = prompt_blocks // n_batch

    def prompt_index(k):
        def index(i):
            blk = i * sub + k
            src = (blk // chunks) * per_batch + (blk % chunks) - 1
            return (jnp.clip(src, 0, prompt_blocks - 1), 0)
        return index

    norm_specs, norm_args = _norm_specs([norm], gate, lambda i: (0, 0))
    row_spec = pl.BlockSpec((ROW_TILE, D_MODEL), lambda i: (i, 0))
    out_specs = [row_spec, row_spec]
    out_shape = [jax.ShapeDtypeStruct((rows, D_MODEL), F32),
                 jax.ShapeDtypeStruct((rows, D_MODEL), BF16)]
    if gate is not None:
        out_specs.append(pl.BlockSpec((ROW_TILE, GLA_KEY_DIM), lambda i: (i, 0)))
        out_shape.append(jax.ShapeDtypeStruct((rows, GLA_KEY_DIM), F32))
    return pl.pallas_call(
        functools.partial(_embed_kernel, sub, tuple(b * chunks for b in range(n_batch)),
                          n_batch * chunks, front, gate is not None),
        grid=(rows // ROW_TILE,),
        in_specs=[pl.BlockSpec((CHUNK, D_MODEL), prompt_index(k)) for k in range(sub)] + [
            pl.BlockSpec((CHUNK, D_MODEL), lambda i: (0, 0)),
            pl.BlockSpec((N_META, D_MODEL), lambda i: (0, 0)),
        ] + norm_specs,
        out_specs=out_specs,
        out_shape=out_shape,
        compiler_params=_params(("parallel",)),
        name="embed",
    )(*([x_prompt2d] * sub), x_sample2d, meta, *norm_args)


def _in_proj_kernel(transposed, hn_ref, w_ref, o_ref, w_bf16_ref):
    @pl.when(pl.program_id(1) == 0)
    def _():
        w_bf16_ref[...] = w_ref[...].astype(BF16)

    dims = NT_DIMS if transposed else (((1,), (0,)), ((), ()))
    o_ref[...] = lax.dot_general(hn_ref[...], w_bf16_ref[...], dims,
                                 preferred_element_type=F32).astype(o_ref.dtype)


def _in_proj(hn, w_stack, layer, n_cols, out_dtype, transposed, name):
    rows = hn.shape[0]
    row_tile = rows // IN_PROJ_ROW_STEPS
    if transposed:
        slab = (IN_PROJ_COLS, D_MODEL)
        w_spec = pl.BlockSpec((None,) + slab, lambda j, i: (layer, j, 0))
    else:
        slab = (D_MODEL, IN_PROJ_COLS)
        w_spec = pl.BlockSpec((None,) + slab, lambda j, i: (layer, 0, j))
    return pl.pallas_call(
        functools.partial(_in_proj_kernel, transposed),
        grid=(n_cols // IN_PROJ_COLS, IN_PROJ_ROW_STEPS),
        in_specs=[pl.BlockSpec((row_tile, D_MODEL), lambda j, i: (i, 0)), w_spec],
        out_specs=pl.BlockSpec((row_tile, IN_PROJ_COLS), lambda j, i: (i, j)),
        out_shape=jax.ShapeDtypeStruct((rows, n_cols), out_dtype),
        scratch_shapes=[pltpu.VMEM(slab, BF16)],
        compiler_params=_params(("parallel", "arbitrary")),
        name=name,
    )(hn, w_stack)


def _out_proj_kernel(sub, n_norms, with_gate, keep_x, *refs):
    og_refs, w_ref, x_refs, refs = refs[:sub], refs[sub], refs[sub + 1:2 * sub + 1], refs[2 * sub + 1:]
    nw_refs, gate_refs, outs = _split_refs(refs, n_norms, with_gate)
    if sub == 1:
        og, x = og_refs[0][...], x_refs[0][...]
    else:
        og = jnp.concatenate([r[...] for r in og_refs], axis=0)
        x = jnp.concatenate([r[...] for r in x_refs], axis=0)
    y = x + jnp.dot(og, w_ref[...], preferred_element_type=F32)
    if keep_x:
        outs[0][...] = y
        outs = outs[1:]
    _emit_normed(y, nw_refs, gate_refs, outs[:n_norms], outs[n_norms] if with_gate else None)


def _out_proj(og, w, x, norms, gate, keep_x, normed_dtype, name, gather=None):
    if gather is None:
        tile_rows, n_tiles, sub = ROW_TILE, x.shape[0] // ROW_TILE, 1
        in_row_specs = [pl.BlockSpec((ROW_TILE, D_MODEL), lambda i: (i, 0))]
    else:
        tile_rows, n_tiles, first_block = gather
        sub = tile_rows // CHUNK
        in_row_specs = [pl.BlockSpec((CHUNK, D_MODEL),
                                     functools.partial(lambda k, i: (first_block(i) + k, 0), k))
                        for k in range(sub)]
    rows = tile_rows * n_tiles
    row_spec = pl.BlockSpec((tile_rows, D_MODEL), lambda i: (i, 0))
    norm_specs, norm_args = _norm_specs(norms, gate, lambda i: (0, 0))
    out_specs, out_shape = [], []
    if keep_x:
        out_specs.append(row_spec)
        out_shape.append(jax.ShapeDtypeStruct((rows, D_MODEL), F32))
    for _ in norms:
        out_specs.append(row_spec)
        out_shape.append(jax.ShapeDtypeStruct((rows, D_MODEL), normed_dtype))
    if gate is not None:
        out_specs.append(pl.BlockSpec((tile_rows, GLA_KEY_DIM), lambda i: (i, 0)))
        out_shape.append(jax.ShapeDtypeStruct((rows, GLA_KEY_DIM), F32))
    w_spec = pl.BlockSpec((D_MODEL, D_MODEL), lambda i: (0, 0), pipeline_mode=pl.Buffered(1))
    return pl.pallas_call(
        functools.partial(_out_proj_kernel, sub, len(norms), gate is not None, keep_x),
        grid=(n_tiles,),
        in_specs=in_row_specs + [w_spec] + in_row_specs + norm_specs,
        out_specs=out_specs,
        out_shape=out_shape,
        compiler_params=_params(("parallel",)),
        name=name,
    )(*([og] * sub), w, *([x] * sub), *norm_args)


def _exact_block(qs, k, b, n, width, col0):
    lane = lax.broadcasted_iota(jnp.int32, (n, width), 1)
    acc = jnp.zeros((n, width), F32)
    for j in range(n):
        t = qs * k[j:j + 1] * jnp.exp(jnp.minimum(b - b[j:j + 1], 0.0))
        acc = jnp.where(lane == col0 + j, jnp.sum(t, axis=-1, keepdims=True), acc)
    return acc


def _masked_row_sums(mask_bf16, g):
    g_hi = g.astype(BF16)
    r1 = g - g_hi.astype(F32)
    g_mid = r1.astype(BF16)
    g_lo = (r1 - g_mid.astype(F32)).astype(BF16)
    return (jnp.dot(mask_bf16, g_hi, preferred_element_type=F32)
            + jnp.dot(mask_bf16, g_mid, preferred_element_type=F32)
            + jnp.dot(mask_bf16, g_lo, preferred_element_type=F32))


def _column_of(row_vec, n):
    return jnp.broadcast_to(row_vec, (LANES, n)).T[:, 0:1]


def _head_norm_gate(o, onorm, gate):
    ms = jnp.mean(o * o, axis=-1, keepdims=True)
    return (o * lax.rsqrt(ms + RMS_EPS) * onorm * _silu(gate)).astype(BF16)


def _intra_chunk_scores(qs, k, k_bf, b, row_xor_col):
    c_len = qs.shape[0]
    leaf = GLA_LEAF
    leaf_row = lax.broadcasted_iota(jnp.int32, (leaf, c_len), 0)
    leaf_col = lax.broadcasted_iota(jnp.int32, (leaf, c_len), 1)
    lane_in_leaf = [(leaf_col & (leaf - 1)) == j for j in range(leaf)]

    att_rows = []
    for blk0 in range(0, c_len // leaf, LEAF_BATCH):
        lhs = []
        for blk in range(blk0, blk0 + LEAF_BATCH):
            q_blk = qs[blk * leaf:(blk + 1) * leaf]
            b_blk = b[blk * leaf:(blk + 1) * leaf]
            lhs += [q_blk * jnp.exp2(jnp.minimum(b_blk - b_blk[j:j + 1], 0.0))
                    for j in range(leaf)]
        r = lax.dot_general(jnp.concatenate(lhs, axis=0).astype(BF16), k_bf, NT_DIMS,
                            preferred_element_type=F32)
        for i, blk in enumerate(range(blk0, blk0 + LEAF_BATCH)):
            acc = jnp.zeros((leaf, c_len), F32)
            for j in range(leaf):
                r0 = (i * leaf + j) * leaf
                acc = jnp.where(lane_in_leaf[j], r[r0:r0 + leaf], acc)
            rel = leaf_col - blk * leaf
            att_rows.append(jnp.where((rel >= 0) & (rel <= leaf_row), acc, 0.0))
    att = jnp.concatenate(att_rows, axis=0)

    m = c_len // 2
    while m >= leaf:
        q_parts, k_parts = [], []
        zeros = jnp.zeros((m, GLA_DK), F32)
        for base in range(0, c_len, 2 * m):
            ref = b[base + m - 1:base + m]
            lo = slice(base, base + m)
            hi = slice(base + m, base + 2 * m)
            k_parts += [k[lo] * jnp.exp2(ref - b[lo]), zeros]
            q_parts += [zeros, qs[hi] * jnp.exp2(b[hi] - ref)]
        qt = jnp.concatenate(q_parts, axis=0).astype(BF16)
        kt = jnp.concatenate(k_parts, axis=0).astype(BF16)
        a = lax.dot_general(qt, kt, NT_DIMS, preferred_element_type=F32)
        att = att + jnp.where(row_xor_col < 2 * m, a, 0.0)
        m //= 2
    return att


def _gla_prompt_kernel(q_ref, k_ref, v_ref, gate_ref, g_ref, onorm_ref, og_in_ref,
                       og_ref, s_ref, b_scr, qd_scr, att_scr):
    del og_in_ref
    c_len = CHUNK
    heads = [(slice(h * GLA_DK, (h + 1) * GLA_DK), slice(h * GLA_DV, (h + 1) * GLA_DV))
             for h in range(GLA_HEADS)]

    @pl.when(pl.program_id(1) == 0)
    def _():
        s_ref[...] = jnp.zeros_like(s_ref)

    row = lax.broadcasted_iota(jnp.int32, (c_len, c_len), 0)
    col = lax.broadcasted_iota(jnp.int32, (c_len, c_len), 1)
    lower = row >= col
    onorm = onorm_ref[...]
    scale = GLA_DK ** -0.5

    b_all = _masked_row_sums(jnp.where(lower, 1.0, 0.0).astype(BF16), g_ref[...]) * LOG2_E
    b_scr[...] = b_all
    qd_scr[...] = (q_ref[...].astype(F32) * scale * jnp.exp2(b_all)).astype(BF16)

    bounded = jnp.max(-b_all[c_len - 1:c_len]) <= GLA_BOUNDED_RANGE

    @pl.when(bounded)
    def _():
        for h, (ks, _) in enumerate(heads):
            k_inv = (k_ref[:, ks].astype(F32) * jnp.exp2(-b_scr[:, ks])).astype(BF16)
            a = lax.dot_general(qd_scr[:, ks], k_inv, NT_DIMS, preferred_element_type=F32)
            att_scr[h] = jnp.where(lower, a, 0.0).astype(BF16)

    @pl.when(jnp.logical_not(bounded))
    def _():
        for h, (ks, _) in enumerate(heads):
            k_bf = k_ref[:, ks]
            att_scr[h] = _intra_chunk_scores(
                q_ref[:, ks].astype(F32) * scale, k_bf.astype(F32), k_bf, b_scr[:, ks],
                row ^ col).astype(BF16)

    for h, (ks, vs) in enumerate(heads):
        k = k_ref[:, ks].astype(F32)
        v = v_ref[:, vs]
        b = b_scr[:, ks]
        b_last = b[c_len - 1:c_len]
        state = s_ref[0, h]

        o = jnp.dot(qd_scr[:, ks], state.astype(BF16), preferred_element_type=F32)
        o = o + jnp.dot(att_scr[h], v, preferred_element_type=F32)

        kd_t = (k * jnp.exp2(b_last - b)).T.astype(BF16)
        s_ref[0, h] = (state * _column_of(jnp.exp2(b_last), GLA_DK)
                       + jnp.dot(kd_t, v, preferred_element_type=F32))

        og_ref[:, vs] = _head_norm_gate(o, onorm, gate_ref[:, vs].astype(F32))


def _gla_prompt(proj, g, onorm, og, n_batch, chunks):
    rows = proj.shape[0]
    blk = lambda col: (lambda b, c: (b * chunks + c, col))
    return pl.pallas_call(
        _gla_prompt_kernel,
        grid=(n_batch, chunks),
        in_specs=[
            pl.BlockSpec((CHUNK, GLA_KEY_DIM), blk(0)),
            pl.BlockSpec((CHUNK, GLA_KEY_DIM), blk(1)),
            pl.BlockSpec((CHUNK, GLA_VAL_DIM), blk(1)),
            pl.BlockSpec((CHUNK, GLA_VAL_DIM), blk(2)),
            pl.BlockSpec((CHUNK, GLA_KEY_DIM), blk(0)),
            pl.BlockSpec((1, GLA_DV), lambda b, c: (0, 0)),
            pl.BlockSpec(memory_space=pl.ANY),
        ],
        out_specs=[
            pl.BlockSpec((CHUNK, GLA_VAL_DIM), blk(0)),
            pl.BlockSpec((1, GLA_HEADS, GLA_DK, GLA_DV), lambda b, c: (b, 0, 0, 0)),
        ],
        out_shape=[
            jax.ShapeDtypeStruct((rows, GLA_VAL_DIM), BF16),
            jax.ShapeDtypeStruct((n_batch, GLA_HEADS, GLA_DK, GLA_DV), F32),
        ],
        scratch_shapes=[pltpu.VMEM((CHUNK, GLA_KEY_DIM), F32),
                        pltpu.VMEM((CHUNK, GLA_KEY_DIM), BF16),
                        pltpu.VMEM((GLA_HEADS, CHUNK, CHUNK), BF16)],
        input_output_aliases={6: 0},
        compiler_params=_params(("parallel", "arbitrary")),
        name="gla_prompt",
    )(proj, proj, proj, proj, g, onorm, og)


def _gla_sample_kernel(t_len, q_ref, k_ref, v_ref, gate_ref, g_ref, onorm_ref, s0_ref,
                       og_in_ref, *rest):
    og_ref, s_ref = rest[-2], rest[-1]
    n = SEQ_GROUP * t_len
    row = lax.broadcasted_iota(jnp.int32, (n, LANES), 0)
    col = lax.broadcasted_iota(jnp.int32, (n, LANES), 1)
    causal = ((row // t_len) == (col // t_len)) & (row >= col)
    row_seq = lax.broadcasted_iota(jnp.int32, (n, 1), 0) // t_len

    qs = q_ref[...].astype(F32) * (GLA_DK ** -0.5)
    k = k_ref[...].astype(F32)
    pad_k = jnp.zeros((LANES - n, GLA_DK), F32)
    v_pad = jnp.concatenate([v_ref[...], jnp.zeros((LANES - n, GLA_DV), BF16)], axis=0)
    b = _masked_row_sums(jnp.where(causal, 1.0, 0.0).astype(BF16),
                         jnp.concatenate([g_ref[...], pad_k], axis=0))
    att = jnp.where(causal, _exact_block(qs, k, b, n, LANES, 0), 0.0)
    o = jnp.dot(att.astype(BF16), v_pad, preferred_element_type=F32)

    q_dec = (qs * jnp.exp(b)).astype(BF16)
    for u in range(SEQ_GROUP):
        state = s0_ref[0, u, 0]
        o_u = jnp.dot(q_dec, state.astype(BF16), preferred_element_type=F32)
        o = o + jnp.where(row_seq == u, o_u, 0.0)
        b_last = b[(u + 1) * t_len - 1:(u + 1) * t_len]
        kd = jnp.where(row_seq == u, k * jnp.exp(jnp.minimum(b_last - b, 0.0)), 0.0)
        kd_t = jnp.concatenate([kd, pad_k], axis=0).T.astype(BF16)
        s_ref[0, u, 0] = (state * _column_of(jnp.exp(b_last), GLA_DK)
                          + jnp.dot(kd_t, v_pad, preferred_element_type=F32))

    og_ref[...] = _head_norm_gate(o, onorm_ref[...], gate_ref[...].astype(F32))


def _gla_sample(proj, g, onorm, state_in, og, s_out, layer, n_seq, t_len, row0):
    n = SEQ_GROUP * t_len
    rb0 = row0 // n
    n_layers = state_in.shape[0]
    key_blocks = GLA_KEY_DIM // GLA_DK
    val_blocks = GLA_VAL_DIM // GLA_DV
    st_spec = pl.BlockSpec((1, SEQ_GROUP, 1, GLA_DK, GLA_DV),
                           lambda s, h: (layer, s, h, 0, 0))
    in_specs = [
        pl.BlockSpec((n, GLA_DK), lambda s, h: (rb0 + s, h)),
        pl.BlockSpec((n, GLA_DK), lambda s, h: (rb0 + s, key_blocks + h)),
        pl.BlockSpec((n, GLA_DV), lambda s, h: (rb0 + s, val_blocks + h)),
        pl.BlockSpec((n, GLA_DV), lambda s, h: (rb0 + s, 2 * val_blocks + h)),
        pl.BlockSpec((n, GLA_DK), lambda s, h: (rb0 + s, h)),
        pl.BlockSpec((1, GLA_DV), lambda s, h: (0, 0)),
        st_spec,
        pl.BlockSpec(memory_space=pl.ANY),
    ]
    args = [proj, proj, proj, proj, g, onorm, state_in, og]
    aliases = {7: 0}
    if s_out is not None:
        in_specs.append(pl.BlockSpec(memory_space=pl.ANY))
        args.append(s_out)
        aliases[8] = 1
    return pl.pallas_call(
        functools.partial(_gla_sample_kernel, t_len),
        grid=(n_seq // SEQ_GROUP, GLA_HEADS),
        in_specs=in_specs,
        out_specs=[
            pl.BlockSpec((n, GLA_DV), lambda s, h: (rb0 + s, h)),
            st_spec,
        ],
        out_shape=[
            jax.ShapeDtypeStruct(og.shape, BF16),
            jax.ShapeDtypeStruct((n_layers, n_seq, GLA_HEADS, GLA_DK, GLA_DV), F32),
        ],
        input_output_aliases=aliases,
        compiler_params=_params(("parallel", "parallel")),
        name="gla_sample",
    )(*args)


def _half_lane_copies(kc, parity):
    kr = pltpu.roll(kc, HEAD_DIM, axis=1)
    return (kc, kr) if parity == 0 else (kr, kc)


def _attend_rows(q_ref, gate_ref, og_ref, sink_ref, kv_sets, s_scr, p_scr):
    n_rows = q_ref.shape[0]
    n_keys = s_scr.shape[2]
    group = N_Q_HEADS // N_KV_HEADS
    low_out = lax.broadcasted_iota(jnp.int32, (n_rows, LANES), 1) < HEAD_DIM
    lane0 = lax.broadcasted_iota(jnp.int32, (n_rows, LANES), 1) == 0
    key_lane = lax.broadcasted_iota(jnp.int32, (n_keys, LANES), 1)
    key_row = lax.broadcasted_iota(jnp.int32, (n_keys, LANES), 0)
    low = key_lane < HEAD_DIM
    slot0 = key_row == 0

    def head_order():
        for cpair in range(KV_WIDTH // LANES):
            for parity in range(2):
                for pr in range(group // 2):
                    yield cpair, parity, (2 * cpair + parity) * (group // 2) + pr

    idx = 0
    for keys, _, _ in kv_sets:
        k_pads = None
        for cpair, parity, qc in head_order():
            if qc % (group // 2) == 0:
                lo_src, hi_src = _half_lane_copies(keys[cpair], parity)
                k_pads = (jnp.where(low, lo_src, 0.0).astype(BF16),
                          jnp.where(low, 0.0, hi_src).astype(BF16))
            q2 = (q_ref[:, qc * LANES:(qc + 1) * LANES].astype(F32)
                  * (HEAD_DIM ** -0.5 * LOG2_E)).astype(BF16)
            for k_pad in k_pads:
                s_scr[idx] = lax.dot_general(q2, k_pad, NT_DIMS,
                                             preferred_element_type=F32)
                idx += 1

    idx = 0
    for _, _, mask in kv_sets:
        mask_first, mask_rest = mask[:, :LANES], mask[:, LANES:]
        for _, _, qc in head_order():
            for half in range(2):
                sink = sink_ref[2 * qc + half] * LOG2_E
                s = s_scr[idx]
                first = jnp.where(mask_first, s[:, :LANES],
                                  jnp.where(lane0, sink, NEG_INF))
                rest = jnp.where(mask_rest, s[:, LANES:], NEG_INF)
                m = jnp.max(jnp.maximum(first, rest), axis=-1, keepdims=True)
                p_scr[idx] = jnp.exp2(
                    jnp.concatenate([first, rest], axis=1) - m).astype(BF16)
                idx += 1

    def store(qc, o):
        cols = slice(qc * LANES, (qc + 1) * LANES)
        og_ref[:, cols] = (o * _silu(gate_ref[:, cols].astype(F32))).astype(BF16)

    idx = 0
    out_cols = [None] * (Q_WIDTH // LANES)
    for _, vals, _ in kv_sets:
        v_pads = None
        for cpair, parity, qc in head_order():
            if qc % (group // 2) == 0:
                lo_src, hi_src = _half_lane_copies(vals[cpair], parity)
                v_pads = (jnp.where(low, jnp.where(slot0, 0.0, lo_src), 1.0).astype(BF16),
                          jnp.where(low, 1.0, jnp.where(slot0, 0.0, hi_src)).astype(BF16))
            halves = []
            for v_pad in v_pads:
                o = jnp.dot(p_scr[idx], v_pad, preferred_element_type=F32)
                halves.append(o * (1.0 / pltpu.roll(o, HEAD_DIM, axis=1)))
                idx += 1
            o2 = jnp.where(low_out, halves[0], halves[1])
            if len(kv_sets) == 1:
                store(qc, o2)
            else:
                out_cols[qc] = o2 if out_cols[qc] is None else out_cols[qc] + o2

    if len(kv_sets) > 1:
        for qc, o in enumerate(out_cols):
            store(qc, o)


def _attention_scratch(n_sets, n_rows, n_keys):
    n = n_sets * N_Q_HEADS
    return [pltpu.VMEM((n, n_rows, n_keys), F32), pltpu.VMEM((n, n_rows, n_keys), BF16)]


def _swa_prompt_kernel(front, sink_ref, q_ref, gate_ref, kc_ref, kp_ref, vc_ref, vp_ref,
                       og_in_ref, og_ref, s_scr, p_scr):
    del og_in_ref
    blk = pl.program_id(1)
    row = lax.broadcasted_iota(jnp.int32, (CHUNK, 2 * CHUNK), 0)
    col = lax.broadcasted_iota(jnp.int32, (CHUNK, 2 * CHUNK), 1)
    mask = (col > row) & (col <= row + WINDOW) & ((blk - 1) * CHUNK + col >= front)
    keys = [jnp.concatenate([kp_ref[:, c * LANES:(c + 1) * LANES],
                             kc_ref[:, c * LANES:(c + 1) * LANES]], axis=0)
            for c in range(KV_WIDTH // LANES)]
    vals = [jnp.concatenate([vp_ref[:, c * LANES:(c + 1) * LANES],
                             vc_ref[:, c * LANES:(c + 1) * LANES]], axis=0)
            for c in range(KV_WIDTH // LANES)]
    _attend_rows(q_ref, gate_ref, og_ref, sink_ref, [(keys, vals, mask)], s_scr, p_scr)


def _swa_prompt(pq, kv, sinks, og, n_batch, chunks, front):
    cur = lambda col: (lambda b, i: (b * chunks + i, col))
    prev = lambda col: (lambda b, i: (b * chunks + jnp.maximum(i - 1, 0), col))
    return pl.pallas_call(
        functools.partial(_swa_prompt_kernel, front),
        grid=(n_batch, chunks),
        in_specs=[
            pl.BlockSpec(memory_space=pltpu.SMEM),
            pl.BlockSpec((CHUNK, Q_WIDTH), cur(0)),
            pl.BlockSpec((CHUNK, Q_WIDTH), cur(1)),
            pl.BlockSpec((CHUNK, KV_WIDTH), cur(0)),
            pl.BlockSpec((CHUNK, KV_WIDTH), prev(0)),
            pl.BlockSpec((CHUNK, KV_WIDTH), cur(1)),
            pl.BlockSpec((CHUNK, KV_WIDTH), prev(1)),
            pl.BlockSpec(memory_space=pl.ANY),
        ],
        out_specs=pl.BlockSpec((CHUNK, Q_WIDTH), cur(0)),
        out_shape=jax.ShapeDtypeStruct(og.shape, BF16),
        scratch_shapes=_attention_scratch(1, CHUNK, 2 * CHUNK),
        input_output_aliases={7: 0},
        compiler_params=_params(("parallel", "parallel")),
        name="swa_prompt",
    )(sinks, pq, pq, kv, kv, kv, kv, og)


def _swa_sample_kernel(t_len, sink_ref, q_ref, gate_ref, kn_ref, vn_ref, ck_ref, cv_ref,
                       og_in_ref, og_ref, s_scr, p_scr):
    del og_in_ref
    n = SEQ_GROUP * t_len
    win = ck_ref.shape[1]
    row = lax.broadcasted_iota(jnp.int32, (n, 2 * win), 0)
    col = lax.broadcasted_iota(jnp.int32, (n, 2 * win), 1)
    t = row % t_len
    new = col - win
    in_window = ((col < win) & (col > t)) | (
        (new >= 0) & (new < n) & ((new // t_len) == (row // t_len)) & ((new % t_len) <= t))
    pad = jnp.zeros((win - n, LANES), F32)
    kv_pairs = []
    for u in range(SEQ_GROUP):
        keys = [jnp.concatenate([ck_ref[u, :, c * LANES:(c + 1) * LANES],
                                 kn_ref[:, c * LANES:(c + 1) * LANES], pad], axis=0)
                for c in range(KV_WIDTH // LANES)]
        vals = [jnp.concatenate([cv_ref[u, :, c * LANES:(c + 1) * LANES],
                                 vn_ref[:, c * LANES:(c + 1) * LANES], pad], axis=0)
                for c in range(KV_WIDTH // LANES)]
        kv_pairs.append((keys, vals, in_window & ((row // t_len) == u)))
    _attend_rows(q_ref, gate_ref, og_ref, sink_ref, kv_pairs, s_scr, p_scr)


def _swa_sample(pq, kv, cache_k, cache_v, sinks, og, n_seq, t_len, row0):
    n = SEQ_GROUP * t_len
    rb0 = row0 // n
    win = cache_k.shape[1]
    return pl.pallas_call(
        functools.partial(_swa_sample_kernel, t_len),
        grid=(n_seq // SEQ_GROUP,),
        in_specs=[
            pl.BlockSpec(memory_space=pltpu.SMEM),
            pl.BlockSpec((n, Q_WIDTH), lambda s: (rb0 + s, 0)),
            pl.BlockSpec((n, Q_WIDTH), lambda s: (rb0 + s, 1)),
            pl.BlockSpec((n, KV_WIDTH), lambda s: (rb0 + s, 0)),
            pl.BlockSpec((n, KV_WIDTH), lambda s: (rb0 + s, 1)),
            pl.BlockSpec((SEQ_GROUP, win, KV_WIDTH), lambda s: (s, 0, 0)),
            pl.BlockSpec((SEQ_GROUP, win, KV_WIDTH), lambda s: (s, 0, 0)),
            pl.BlockSpec(memory_space=pl.ANY),
        ],
        out_specs=pl.BlockSpec((n, Q_WIDTH), lambda s: (rb0 + s, 0)),
        out_shape=jax.ShapeDtypeStruct(og.shape, BF16),
        scratch_shapes=_attention_scratch(SEQ_GROUP, n, 2 * win),
        input_output_aliases={7: 0},
        compiler_params=_params(("parallel",)),
        name="swa_sample",
    )(sinks, pq, pq, kv, kv, cache_k, cache_v, og)


def kernel(x_prompt, x_sample, state_gla, cache_k_win, cache_v_win, meta_tokens, norm_a, w_in_a, w_gk2_a, b_gk2_a, onorm_a, w_out_a, norm_kv, w_kv, norm_b, w_in_b, sinks_b, w_out_b, norm_f):
    n_batch, seq, _ = x_prompt.shape
    n_seq, t_len, _ = x_sample.shape
    win = cache_k_win.shape[1]
    n_a = w_in_a.shape[0]
    n_b = w_in_b.shape[0]
    assert win == WINDOW == CHUNK and (n_seq * t_len) % (SEQ_GROUP * t_len) == 0

    length = seq + N_META
    front = (-length) % CHUNK
    lp = length + front
    chunks = lp // CHUNK
    srow = n_batch * lp
    ns = n_seq * t_len
    rows = -(-(srow + ns) // ROW_TILE) * ROW_TILE

    assert front + N_META == CHUNK and ns == CHUNK and seq % CHUNK == 0
    assert rows % CHUNK == 0 and rows % (16 * IN_PROJ_ROW_STEPS) == 0

    w_in_a_t = jnp.swapaxes(w_in_a, 1, 2)

    def gate_weights(l):
        w_gate = jnp.pad(w_in_a_t[l, GLA_MAIN_WIDTH:],
                         ((0, LANES - GATE_RANK), (0, 0))).astype(BF16)
        w_gk = jnp.pad(w_gk2_a[l], ((0, LANES - GATE_RANK), (0, 0))).astype(BF16)
        return w_gate, w_gk, b_gk2_a[l][None]

    x, hn, g = _embed(x_prompt.reshape(n_batch * seq, D_MODEL), x_sample.reshape(ns, D_MODEL),
                      meta_tokens, norm_a[0], gate_weights(0), rows, chunks, n_batch, front)
    og = jnp.zeros((rows, D_MODEL), BF16)

    states_p = []
    states_s = None
    for l in range(n_a):
        proj = _in_proj(hn, w_in_a_t, l, GLA_MAIN_WIDTH, BF16, True, "gla_in_proj")
        onorm = onorm_a[l][None]
        og, s_p = _gla_prompt(proj, g, onorm, og, n_batch, chunks)
        og, states_s = _gla_sample(proj, g, onorm, state_gla, og, states_s, l,
                                   n_seq, t_len, srow)
        states_p.append(s_p)
        w_out = w_out_a[l].astype(BF16)
        if l + 1 < n_a:
            x, hn, g = _out_proj(og, w_out, x, [norm_a[l + 1]], gate_weights(l + 1), True,
                                 BF16, "gla_out_proj")
        else:
            x, hn_kv, hn = _out_proj(og, w_out, x, [norm_kv, norm_b[0]], None, True,
                                     BF16, "gla_out_proj")

    kv = _in_proj(hn_kv, w_kv[None], 0, 2 * KV_WIDTH, F32, False, "kv_proj")
    cache_k = cache_k_win.reshape(n_seq, win, KV_WIDTH)
    cache_v = cache_v_win.reshape(n_seq, win, KV_WIDTH)
    for j in range(n_b):
        pq = _in_proj(hn, w_in_b, j, 2 * Q_WIDTH, BF16, False, "swa_in_proj")
        og = _swa_prompt(pq, kv, sinks_b[j], og, n_batch, chunks, front)
        og = _swa_sample(pq, kv, cache_k, cache_v, sinks_b[j], og, n_seq, t_len, srow)
        w_out = w_out_b[j].astype(BF16)
        if j + 1 < n_b:
            x, hn = _out_proj(og, w_out, x, [norm_b[j + 1]], None, True, BF16, "swa_out_proj")

    tiles_per_batch = seq // ROW_TILE
    lead_blocks = (lp - seq) // CHUNK

    def prompt_first_block(t):
        return ((t // tiles_per_batch) * chunks + lead_blocks
                + (t % tiles_per_batch) * (ROW_TILE // CHUNK))

    (y_prompt,) = _out_proj(og, w_out, x, [norm_f], None, False, F32, "final_prompt",
                            gather=(ROW_TILE, n_batch * tiles_per_batch, prompt_first_block))
    (y_sample,) = _out_proj(og, w_out, x, [norm_f], None, False, F32, "final_sample",
                            gather=(CHUNK, ns // CHUNK, lambda t: srow // CHUNK + t))
    y_prompt = y_prompt.reshape(n_batch, seq, D_MODEL)
    y_sample = y_sample.reshape(n_seq, t_len, D_MODEL)
    kv_p = jnp.stack([kv[(b + 1) * lp - win:(b + 1) * lp] for b in range(n_batch)])
    kv_p = kv_p.reshape(n_batch, win, 2, N_KV_HEADS, HEAD_DIM)
    kv_s = kv[srow:srow + ns].reshape(n_seq, t_len, 2, N_KV_HEADS, HEAD_DIM)
    k_win_s = jnp.concatenate([cache_k_win, kv_s[:, :, 0]], axis=1)[:, -win:]
    v_win_s = jnp.concatenate([cache_v_win, kv_s[:, :, 1]], axis=1)[:, -win:]
    return (y_prompt, y_sample, jnp.stack(states_p), states_s,
            kv_p[:, :, 0], kv_p[:, :, 1], k_win_s, v_win_s)
```

```python
import functools

import jax
import jax.numpy as jnp
from jax import lax
from jax.experimental import pallas as pl
from jax.experimental.pallas import tpu as pltpu

F32 = jnp.float32
BF16 = jnp.bfloat16

D_MODEL = 2048
N_META = 16
GLA_HEADS = 4
GLA_DK = 256
GLA_DV = 512
GLA_KEY_DIM = GLA_HEADS * GLA_DK
GLA_VAL_DIM = GLA_HEADS * GLA_DV
GLA_MAIN_WIDTH = 2 * GLA_KEY_DIM + 2 * GLA_VAL_DIM
GATE_RANK = 16
GATE_LOGIT_NORM = 16.0
HEAD_DIM = 64
N_Q_HEADS = 32
N_KV_HEADS = 8
Q_WIDTH = N_Q_HEADS * HEAD_DIM
KV_WIDTH = N_KV_HEADS * HEAD_DIM
WINDOW = 128
RMS_EPS = 1e-6
NEG_INF = -1e30
LOG2_E = 1.4426950408889634

LANES = 128
CHUNK = 128
GLA_LEAF = 8
LEAF_BATCH = 4
GLA_BOUNDED_RANGE = 60.0
ROW_TILE = 512
IN_PROJ_COLS = 1024
IN_PROJ_ROW_STEPS = 8
SEQ_GROUP = 4
GLA_SEQ_GROUP = 8
VMEM_LIMIT = 48 * 1024 * 1024

NT_DIMS = (((1,), (1,)), ((), ()))


def _silu(x):
    return x * (1.0 / (1.0 + jnp.exp(-x)))


def _params(sem):
    return pltpu.CompilerParams(dimension_semantics=sem, vmem_limit_bytes=VMEM_LIMIT)


def _emit_normed(y, nw_refs, gate_refs, hn_refs, g_ref):
    ms = jnp.mean(y * y, axis=-1, keepdims=True)
    yn = y * lax.rsqrt(ms + RMS_EPS)
    for nw_ref, hn_ref in zip(nw_refs, hn_refs):
        hn_ref[...] = (yn * nw_ref[...]).astype(hn_ref.dtype)
    if gate_refs:
        wg_ref, wgk_ref, bgk_ref = gate_refs
        glr = lax.dot_general(hn_refs[0][...], wg_ref[...].astype(BF16), NT_DIMS,
                              preferred_element_type=F32)
        z = jnp.dot(glr.astype(BF16), wgk_ref[...], preferred_element_type=F32) + bgk_ref[...]
        log_sig = jnp.minimum(z, 0.0) - jnp.log(1.0 + jnp.exp(-jnp.abs(z)))
        g_ref[...] = log_sig * (1.0 / GATE_LOGIT_NORM)


def _split_refs(refs, n_norms, with_gate):
    nw_refs = refs[:n_norms]
    gate_refs = refs[n_norms:n_norms + 3] if with_gate else None
    return nw_refs, gate_refs, refs[n_norms + 3 * with_gate:]


def _norm_specs(norms, gate, index):
    specs = [pl.BlockSpec((1, D_MODEL), index) for _ in norms]
    args = [nw[None] for nw in norms]
    if gate is not None:
        specs += [pl.BlockSpec((LANES, D_MODEL), index),
                  pl.BlockSpec((LANES, GLA_KEY_DIM), index),
                  pl.BlockSpec((1, GLA_KEY_DIM), index)]
        args += list(gate)
    return specs, args


def _embed_kernel(sub, lead_blocks, sample_block, front, with_gate, *refs):
    xp_refs, (xs_ref, meta_ref), refs = refs[:sub], refs[sub:sub + 2], refs[sub + 2:]
    nw_refs, gate_refs, outs = _split_refs(refs, 1, with_gate)
    x_ref = outs[0]
    for k in range(sub):
        blk = pl.program_id(0) * sub + k
        rows = slice(k * CHUNK, (k + 1) * CHUNK)
        is_lead = functools.reduce(jnp.logical_or, [blk == b for b in lead_blocks])

        @pl.when(is_lead)
        def _():
            x_ref[rows, :] = jnp.concatenate(
                [jnp.zeros((front, D_MODEL), F32), meta_ref[...]], axis=0)

        @pl.when(jnp.logical_and(jnp.logical_not(is_lead), blk < sample_block))
        def _():
            x_ref[rows, :] = xp_refs[k][...]

        @pl.when(blk == sample_block)
        def _():
            x_ref[rows, :] = xs_ref[...]

        @pl.when(blk > sample_block)
        def _():
            x_ref[rows, :] = jnp.zeros((CHUNK, D_MODEL), F32)

    _emit_normed(x_ref[...], nw_refs, gate_refs, outs[1:2], outs[2] if with_gate else None)


def _embed(x_prompt2d, x_sample2d, meta, norm, gate, rows, chunks, n_batch, front):
    sub = ROW_TILE // CHUNK
    prompt_blocks = x_prompt2d.shape[0] // CHUNK
    per_batch = prompt_blocks // n_batch

    def prompt_index(k):
        def index(i):
            blk = i * sub + k
            src = (blk // chunks) * per_batch + (blk % chunks) - 1
            return (jnp.clip(src, 0, prompt_blocks - 1), 0)
        return index

    norm_specs, norm_args = _norm_specs([norm], gate, lambda i: (0, 0))
    row_spec = pl.BlockSpec((ROW_TILE, D_MODEL), lambda i: (i, 0))
    out_specs = [row_spec, row_spec]
    out_shape = [jax.ShapeDtypeStruct((rows, D_MODEL), F32),
                 jax.ShapeDtypeStruct((rows, D_MODEL), BF16)]
    if gate is not None:
        out_specs.append(pl.BlockSpec((ROW_TILE, GLA_KEY_DIM), lambda i: (i, 0)))
        out_shape.append(jax.ShapeDtypeStruct((rows, GLA_KEY_DIM), F32))
    return pl.pallas_call(
        functools.partial(_embed_kernel, sub, tuple(b * chunks for b in range(n_batch)),
                          n_batch * chunks, front, gate is not None),
        grid=(rows // ROW_TILE,),
        in_specs=[pl.BlockSpec((CHUNK, D_MODEL), prompt_index(k)) for k in range(sub)] + [
            pl.BlockSpec((CHUNK, D_MODEL), lambda i: (0, 0)),
            pl.BlockSpec((N_META, D_MODEL), lambda i: (0, 0)),
        ] + norm_specs,
        out_specs=out_specs,
        out_shape=out_shape,
        compiler_params=_params(("parallel",)),
        name="embed",
    )(*([x_prompt2d] * sub), x_sample2d, meta, *norm_args)


def _in_proj_kernel(transposed, hn_ref, w_ref, o_ref, w_bf16_ref):
    @pl.when(pl.program_id(1) == 0)
    def _():
        w_bf16_ref[...] = w_ref[...].astype(BF16)

    dims = NT_DIMS if transposed else (((1,), (0,)), ((), ()))
    o_ref[...] = lax.dot_general(hn_ref[...], w_bf16_ref[...], dims,
                                 preferred_element_type=F32).astype(o_ref.dtype)


def _in_proj(hn, w_stack, layer, n_cols, out_dtype, transposed, name):
    rows = hn.shape[0]
    row_tile = rows // IN_PROJ_ROW_STEPS
    if transposed:
        slab = (IN_PROJ_COLS, D_MODEL)
        w_spec = pl.BlockSpec((None,) + slab, lambda j, i: (layer, j, 0))
    else:
        slab = (D_MODEL, IN_PROJ_COLS)
        w_spec = pl.BlockSpec((None,) + slab, lambda j, i: (layer, 0, j))
    return pl.pallas_call(
        functools.partial(_in_proj_kernel, transposed),
        grid=(n_cols // IN_PROJ_COLS, IN_PROJ_ROW_STEPS),
        in_specs=[pl.BlockSpec((row_tile, D_MODEL), lambda j, i: (i, 0)), w_spec],
        out_specs=pl.BlockSpec((row_tile, IN_PROJ_COLS), lambda j, i: (i, j)),
        out_shape=jax.ShapeDtypeStruct((rows, n_cols), out_dtype),
        scratch_shapes=[pltpu.VMEM(slab, BF16)],
        compiler_params=_params(("parallel", "arbitrary")),
        name=name,
    )(hn, w_stack)


def _out_proj_kernel(sub, n_norms, with_gate, keep_x, *refs):
    og_refs, w_ref, x_refs, refs = refs[:sub], refs[sub], refs[sub + 1:2 * sub + 1], refs[2 * sub + 1:]
    nw_refs, gate_refs, outs = _split_refs(refs, n_norms, with_gate)
    if sub == 1:
        og, x = og_refs[0][...], x_refs[0][...]
    else:
        og = jnp.concatenate([r[...] for r in og_refs], axis=0)
        x = jnp.concatenate([r[...] for r in x_refs], axis=0)
    y = x + jnp.dot(og, w_ref[...], preferred_element_type=F32)
    if keep_x:
        outs[0][...] = y
        outs = outs[1:]
    _emit_normed(y, nw_refs, gate_refs, outs[:n_norms], outs[n_norms] if with_gate else None)


def _out_proj(og, w, x, norms, gate, keep_x, normed_dtype, name, gather=None):
    if gather is None:
        tile_rows, n_tiles, sub = ROW_TILE, x.shape[0] // ROW_TILE, 1
        in_row_specs = [pl.BlockSpec((ROW_TILE, D_MODEL), lambda i: (i, 0))]
    else:
        tile_rows, n_tiles, first_block = gather
        sub = tile_rows // CHUNK
        in_row_specs = [pl.BlockSpec((CHUNK, D_MODEL),
                                     functools.partial(lambda k, i: (first_block(i) + k, 0), k))
                        for k in range(sub)]
    rows = tile_rows * n_tiles
    row_spec = pl.BlockSpec((tile_rows, D_MODEL), lambda i: (i, 0))
    norm_specs, norm_args = _norm_specs(norms, gate, lambda i: (0, 0))
    out_specs, out_shape = [], []
    if keep_x:
        out_specs.append(row_spec)
        out_shape.append(jax.ShapeDtypeStruct((rows, D_MODEL), F32))
    for _ in norms:
        out_specs.append(row_spec)
        out_shape.append(jax.ShapeDtypeStruct((rows, D_MODEL), normed_dtype))
    if gate is not None:
        out_specs.append(pl.BlockSpec((tile_rows, GLA_KEY_DIM), lambda i: (i, 0)))
        out_shape.append(jax.ShapeDtypeStruct((rows, GLA_KEY_DIM), F32))
    w_spec = pl.BlockSpec((D_MODEL, D_MODEL), lambda i: (0, 0), pipeline_mode=pl.Buffered(1))
    return pl.pallas_call(
        functools.partial(_out_proj_kernel, sub, len(norms), gate is not None, keep_x),
        grid=(n_tiles,),
        in_specs=in_row_specs + [w_spec] + in_row_specs + norm_specs,
        out_specs=out_specs,
        out_shape=out_shape,
        compiler_params=_params(("parallel",)),
        name=name,
    )(*([og] * sub), w, *([x] * sub), *norm_args)


def _exact_block(qs, k, b, n, width, col0):
    lane = lax.broadcasted_iota(jnp.int32, (n, width), 1)
    acc = jnp.zeros((n, width), F32)
    for j in range(n):
        t = qs * k[j:j + 1] * jnp.exp(jnp.minimum(b - b[j:j + 1], 0.0))
        acc = jnp.where(lane == col0 + j, jnp.sum(t, axis=-1, keepdims=True), acc)
    return acc


def _masked_row_sums(mask_bf16, g):
    g_hi = g.astype(BF16)
    r1 = g - g_hi.astype(F32)
    g_mid = r1.astype(BF16)
    g_lo = (r1 - g_mid.astype(F32)).astype(BF16)
    return (jnp.dot(mask_bf16, g_hi, preferred_element_type=F32)
            + jnp.dot(mask_bf16, g_mid, preferred_element_type=F32)
            + jnp.dot(mask_bf16, g_lo, preferred_element_type=F32))


def _column_of(row_vec, n):
    return jnp.broadcast_to(row_vec, (LANES, n)).T[:, 0:1]


def _head_norm_gate(o, onorm, gate):
    ms = jnp.mean(o * o, axis=-1, keepdims=True)
    return (o * lax.rsqrt(ms + RMS_EPS) * onorm * _silu(gate)).astype(BF16)


def _intra_chunk_scores(qs, k, k_bf, b, row_xor_col):
    c_len = qs.shape[0]
    leaf = GLA_LEAF
    leaf_row = lax.broadcasted_iota(jnp.int32, (leaf, c_len), 0)
    leaf_col = lax.broadcasted_iota(jnp.int32, (leaf, c_len), 1)
    lane_in_leaf = [(leaf_col & (leaf - 1)) == j for j in range(leaf)]

    att_rows = []
    for blk0 in range(0, c_len // leaf, LEAF_BATCH):
        lhs = []
        for blk in range(blk0, blk0 + LEAF_BATCH):
            q_blk = qs[blk * leaf:(blk + 1) * leaf]
            b_blk = b[blk * leaf:(blk + 1) * leaf]
            lhs += [q_blk * jnp.exp2(jnp.minimum(b_blk - b_blk[j:j + 1], 0.0))
                    for j in range(leaf)]
        r = lax.dot_general(jnp.concatenate(lhs, axis=0).astype(BF16), k_bf, NT_DIMS,
                            preferred_element_type=F32)
        for i, blk in enumerate(range(blk0, blk0 + LEAF_BATCH)):
            acc = jnp.zeros((leaf, c_len), F32)
            for j in range(leaf):
                r0 = (i * leaf + j) * leaf
                acc = jnp.where(lane_in_leaf[j], r[r0:r0 + leaf], acc)
            rel = leaf_col - blk * leaf
            att_rows.append(jnp.where((rel >= 0) & (rel <= leaf_row), acc, 0.0))
    att = jnp.concatenate(att_rows, axis=0)

    m = c_len // 2
    while m >= leaf:
        q_parts, k_parts = [], []
        zeros = jnp.zeros((m, GLA_DK), F32)
        for base in range(0, c_len, 2 * m):
            ref = b[base + m - 1:base + m]
            lo = slice(base, base + m)
            hi = slice(base + m, base + 2 * m)
            k_parts += [k[lo] * jnp.exp2(ref - b[lo]), zeros]
            q_parts += [zeros, qs[hi] * jnp.exp2(b[hi] - ref)]
        qt = jnp.concatenate(q_parts, axis=0).astype(BF16)
        kt = jnp.concatenate(k_parts, axis=0).astype(BF16)
        a = lax.dot_general(qt, kt, NT_DIMS, preferred_element_type=F32)
        att = att + jnp.where(row_xor_col < 2 * m, a, 0.0)
        m //= 2
    return att


def _gla_prompt_kernel(q_ref, k_ref, v_ref, gate_ref, g_ref, onorm_ref, og_in_ref,
                       og_ref, s_ref, b_scr, qd_scr, att_scr):
    del og_in_ref
    c_len = CHUNK
    heads = [(slice(h * GLA_DK, (h + 1) * GLA_DK), slice(h * GLA_DV, (h + 1) * GLA_DV))
             for h in range(GLA_HEADS)]

    @pl.when(pl.program_id(1) == 0)
    def _():
        s_ref[...] = jnp.zeros_like(s_ref)

    row = lax.broadcasted_iota(jnp.int32, (c_len, c_len), 0)
    col = lax.broadcasted_iota(jnp.int32, (c_len, c_len), 1)
    lower = row >= col
    onorm = onorm_ref[...]
    scale = GLA_DK ** -0.5

    b_all = _masked_row_sums(jnp.where(lower, 1.0, 0.0).astype(BF16), g_ref[...]) * LOG2_E
    b_scr[...] = b_all
    qd_scr[...] = (q_ref[...].astype(F32) * scale * jnp.exp2(b_all)).astype(BF16)

    bounded = jnp.max(-b_all[c_len - 1:c_len]) <= GLA_BOUNDED_RANGE

    @pl.when(bounded)
    def _():
        for h, (ks, _) in enumerate(heads):
            k_inv = (k_ref[:, ks].astype(F32) * jnp.exp2(-b_scr[:, ks])).astype(BF16)
            a = lax.dot_general(qd_scr[:, ks], k_inv, NT_DIMS, preferred_element_type=F32)
            att_scr[h] = jnp.where(lower, a, 0.0).astype(BF16)

    @pl.when(jnp.logical_not(bounded))
    def _():
        for h, (ks, _) in enumerate(heads):
            k_bf = k_ref[:, ks]
            att_scr[h] = _intra_chunk_scores(
                q_ref[:, ks].astype(F32) * scale, k_bf.astype(F32), k_bf, b_scr[:, ks],
                row ^ col).astype(BF16)

    for h, (ks, vs) in enumerate(heads):
        k = k_ref[:, ks].astype(F32)
        v = v_ref[:, vs]
        b = b_scr[:, ks]
        b_last = b[c_len - 1:c_len]
        state = s_ref[0, h]

        o = jnp.dot(qd_scr[:, ks], state.astype(BF16), preferred_element_type=F32)
        o = o + jnp.dot(att_scr[h], v, preferred_element_type=F32)

        kd_t = (k * jnp.exp2(b_last - b)).T.astype(BF16)
        s_ref[0, h] = (state * _column_of(jnp.exp2(b_last), GLA_DK)
                       + jnp.dot(kd_t, v, preferred_element_type=F32))

        og_ref[:, vs] = _head_norm_gate(o, onorm, gate_ref[:, vs].astype(F32))


def _gla_prompt(proj, g, onorm, og, n_batch, chunks):
    rows = proj.shape[0]
    blk = lambda col: (lambda b, c: (b * chunks + c, col))
    return pl.pallas_call(
        _gla_prompt_kernel,
        grid=(n_batch, chunks),
        in_specs=[
            pl.BlockSpec((CHUNK, GLA_KEY_DIM), blk(0)),
            pl.BlockSpec((CHUNK, GLA_KEY_DIM), blk(1)),
            pl.BlockSpec((CHUNK, GLA_VAL_DIM), blk(1)),
            pl.BlockSpec((CHUNK, GLA_VAL_DIM), blk(2)),
            pl.BlockSpec((CHUNK, GLA_KEY_DIM), blk(0)),
            pl.BlockSpec((1, GLA_DV), lambda b, c: (0, 0)),
            pl.BlockSpec(memory_space=pl.ANY),
        ],
        out_specs=[
            pl.BlockSpec((CHUNK, GLA_VAL_DIM), blk(0)),
            pl.BlockSpec((1, GLA_HEADS, GLA_DK, GLA_DV), lambda b, c: (b, 0, 0, 0)),
        ],
        out_shape=[
            jax.ShapeDtypeStruct((rows, GLA_VAL_DIM), BF16),
            jax.ShapeDtypeStruct((n_batch, GLA_HEADS, GLA_DK, GLA_DV), F32),
        ],
        scratch_shapes=[pltpu.VMEM((CHUNK, GLA_KEY_DIM), F32),
                        pltpu.VMEM((CHUNK, GLA_KEY_DIM), BF16),
                        pltpu.VMEM((GLA_HEADS, CHUNK, CHUNK), BF16)],
        input_output_aliases={6: 0},
        compiler_params=_params(("parallel", "arbitrary")),
        name="gla_prompt",
    )(proj, proj, proj, proj, g, onorm, og)


def _gla_sample_kernel(t_len, q_ref, k_ref, v_ref, gate_ref, g_ref, onorm_ref, s0_ref,
                       og_in_ref, *rest):
    og_ref, s_ref = rest[-2], rest[-1]
    n = GLA_SEQ_GROUP * t_len
    row = lax.broadcasted_iota(jnp.int32, (n, LANES), 0)
    col = lax.broadcasted_iota(jnp.int32, (n, LANES), 1)
    causal = ((row // t_len) == (col // t_len)) & (row >= col)
    row_seq = lax.broadcasted_iota(jnp.int32, (n, 1), 0) // t_len

    qs = q_ref[...].astype(F32) * (GLA_DK ** -0.5)
    k = k_ref[...].astype(F32)
    pad_k = jnp.zeros((LANES - n, GLA_DK), F32)
    v_pad = jnp.concatenate([v_ref[...], jnp.zeros((LANES - n, GLA_DV), BF16)], axis=0)
    b = _masked_row_sums(jnp.where(causal, 1.0, 0.0).astype(BF16),
                         jnp.concatenate([g_ref[...], pad_k], axis=0))
    att = jnp.where(causal, _exact_block(qs, k, b, n, LANES, 0), 0.0)
    o = jnp.dot(att.astype(BF16), v_pad, preferred_element_type=F32)

    q_dec = (qs * jnp.exp(b)).astype(BF16)
    for u in range(GLA_SEQ_GROUP):
        state = s0_ref[0, u, 0]
        o_u = jnp.dot(q_dec, state.astype(BF16), preferred_element_type=F32)
        o = o + jnp.where(row_seq == u, o_u, 0.0)
        b_last = b[(u + 1) * t_len - 1:(u + 1) * t_len]
        kd = jnp.where(row_seq == u, k * jnp.exp(jnp.minimum(b_last - b, 0.0)), 0.0)
        kd_t = jnp.concatenate([kd, pad_k], axis=0).T.astype(BF16)
        s_ref[0, u, 0] = (state * _column_of(jnp.exp(b_last), GLA_DK)
                          + jnp.dot(kd_t, v_pad, preferred_element_type=F32))

    og_ref[...] = _head_norm_gate(o, onorm_ref[...], gate_ref[...].astype(F32))


def _gla_sample(proj, g, onorm, state_in, og, s_out, layer, n_seq, t_len, row0):
    n = GLA_SEQ_GROUP * t_len
    rb0 = row0 // n
    n_layers = state_in.shape[0]
    key_blocks = GLA_KEY_DIM // GLA_DK
    val_blocks = GLA_VAL_DIM // GLA_DV
    st_spec = pl.BlockSpec((1, GLA_SEQ_GROUP, 1, GLA_DK, GLA_DV),
                           lambda s, h: (layer, s, h, 0, 0))
    in_specs = [
        pl.BlockSpec((n, GLA_DK), lambda s, h: (rb0 + s, h)),
        pl.BlockSpec((n, GLA_DK), lambda s, h: (rb0 + s, key_blocks + h)),
        pl.BlockSpec((n, GLA_DV), lambda s, h: (rb0 + s, val_blocks + h)),
        pl.BlockSpec((n, GLA_DV), lambda s, h: (rb0 + s, 2 * val_blocks + h)),
        pl.BlockSpec((n, GLA_DK), lambda s, h: (rb0 + s, h)),
        pl.BlockSpec((1, GLA_DV), lambda s, h: (0, 0)),
        st_spec,
        pl.BlockSpec(memory_space=pl.ANY),
    ]
    args = [proj, proj, proj, proj, g, onorm, state_in, og]
    aliases = {7: 0}
    if s_out is not None:
        in_specs.append(pl.BlockSpec(memory_space=pl.ANY))
        args.append(s_out)
        aliases[8] = 1
    return pl.pallas_call(
        functools.partial(_gla_sample_kernel, t_len),
        grid=(n_seq // GLA_SEQ_GROUP, GLA_HEADS),
        in_specs=in_specs,
        out_specs=[
            pl.BlockSpec((n, GLA_DV), lambda s, h: (rb0 + s, h)),
            st_spec,
        ],
        out_shape=[
            jax.ShapeDtypeStruct(og.shape, BF16),
            jax.ShapeDtypeStruct((n_layers, n_seq, GLA_HEADS, GLA_DK, GLA_DV), F32),
        ],
        input_output_aliases=aliases,
        compiler_params=_params(("parallel", "parallel")),
        name="gla_sample",
    )(*args)


def _half_lane_copies(kc, parity):
    kr = pltpu.roll(kc, HEAD_DIM, axis=1)
    return (kc, kr) if parity == 0 else (kr, kc)


def _attend_rows(q_ref, gate_ref, og_ref, sink_ref, kv_sets, s_scr, p_scr):
    n_rows = q_ref.shape[0]
    n_keys = s_scr.shape[2]
    group = N_Q_HEADS // N_KV_HEADS
    low_out = lax.broadcasted_iota(jnp.int32, (n_rows, LANES), 1) < HEAD_DIM
    lane0 = lax.broadcasted_iota(jnp.int32, (n_rows, LANES), 1) == 0
    key_lane = lax.broadcasted_iota(jnp.int32, (n_keys, LANES), 1)
    key_row = lax.broadcasted_iota(jnp.int32, (n_keys, LANES), 0)
    low = key_lane < HEAD_DIM
    slot0 = key_row == 0

    def head_order():
        for cpair in range(KV_WIDTH // LANES):
            for parity in range(2):
                for pr in range(group // 2):
                    yield cpair, parity, (2 * cpair + parity) * (group // 2) + pr

    idx = 0
    for keys, _, _ in kv_sets:
        k_pads = None
        for cpair, parity, qc in head_order():
            if qc % (group // 2) == 0:
                lo_src, hi_src = _half_lane_copies(keys[cpair], parity)
                k_pads = (jnp.where(low, lo_src, 0.0).astype(BF16),
                          jnp.where(low, 0.0, hi_src).astype(BF16))
            q2 = (q_ref[:, qc * LANES:(qc + 1) * LANES].astype(F32)
                  * (HEAD_DIM ** -0.5 * LOG2_E)).astype(BF16)
            for k_pad in k_pads:
                s_scr[idx] = lax.dot_general(q2, k_pad, NT_DIMS,
                                             preferred_element_type=F32)
                idx += 1

    idx = 0
    for _, _, mask in kv_sets:
        mask_first, mask_rest = mask[:, :LANES], mask[:, LANES:]
        for _, _, qc in head_order():
            for half in range(2):
                sink = sink_ref[2 * qc + half] * LOG2_E
                s = s_scr[idx]
                first = jnp.where(mask_first, s[:, :LANES],
                                  jnp.where(lane0, sink, NEG_INF))
                rest = jnp.where(mask_rest, s[:, LANES:], NEG_INF)
                m = jnp.max(jnp.maximum(first, rest), axis=-1, keepdims=True)
                p_scr[idx] = jnp.exp2(
                    jnp.concatenate([first, rest], axis=1) - m).astype(BF16)
                idx += 1

    def store(qc, o):
        cols = slice(qc * LANES, (qc + 1) * LANES)
        og_ref[:, cols] = (o * _silu(gate_ref[:, cols].astype(F32))).astype(BF16)

    idx = 0
    out_cols = [None] * (Q_WIDTH // LANES)
    for _, vals, _ in kv_sets:
        v_pads = None
        for cpair, parity, qc in head_order():
            if qc % (group // 2) == 0:
                lo_src, hi_src = _half_lane_copies(vals[cpair], parity)
                v_pads = (jnp.where(low, jnp.where(slot0, 0.0, lo_src), 1.0).astype(BF16),
                          jnp.where(low, 1.0, jnp.where(slot0, 0.0, hi_src)).astype(BF16))
            oa = jnp.dot(p_scr[idx], v_pads[0], preferred_element_type=F32)
            ob = jnp.dot(p_scr[idx + 1], v_pads[1], preferred_element_type=F32)
            idx += 2
            sums = pltpu.roll(jnp.where(low_out, ob, oa), HEAD_DIM, axis=1)
            o2 = jnp.where(low_out, oa, ob) * (1.0 / sums)
            if len(kv_sets) == 1:
                store(qc, o2)
            else:
                out_cols[qc] = o2 if out_cols[qc] is None else out_cols[qc] + o2

    if len(kv_sets) > 1:
        for qc, o in enumerate(out_cols):
            store(qc, o)


def _attention_scratch(n_sets, n_rows, n_keys):
    n = n_sets * N_Q_HEADS
    return [pltpu.VMEM((n, n_rows, n_keys), F32), pltpu.VMEM((n, n_rows, n_keys), BF16)]


def _swa_prompt_kernel(front, sink_ref, q_ref, gate_ref, kc_ref, kp_ref, vc_ref, vp_ref,
                       og_in_ref, og_ref, s_scr, p_scr):
    del og_in_ref
    blk = pl.program_id(1)
    row = lax.broadcasted_iota(jnp.int32, (CHUNK, 2 * CHUNK), 0)
    col = lax.broadcasted_iota(jnp.int32, (CHUNK, 2 * CHUNK), 1)
    mask = (col > row) & (col <= row + WINDOW) & ((blk - 1) * CHUNK + col >= front)
    keys = [jnp.concatenate([kp_ref[:, c * LANES:(c + 1) * LANES],
                             kc_ref[:, c * LANES:(c + 1) * LANES]], axis=0)
            for c in range(KV_WIDTH // LANES)]
    vals = [jnp.concatenate([vp_ref[:, c * LANES:(c + 1) * LANES],
                             vc_ref[:, c * LANES:(c + 1) * LANES]], axis=0)
            for c in range(KV_WIDTH // LANES)]
    _attend_rows(q_ref, gate_ref, og_ref, sink_ref, [(keys, vals, mask)], s_scr, p_scr)


def _swa_prompt(pq, kv, sinks, og, n_batch, chunks, front):
    cur = lambda col: (lambda b, i: (b * chunks + i, col))
    prev = lambda col: (lambda b, i: (b * chunks + jnp.maximum(i - 1, 0), col))
    return pl.pallas_call(
        functools.partial(_swa_prompt_kernel, front),
        grid=(n_batch, chunks),
        in_specs=[
            pl.BlockSpec(memory_space=pltpu.SMEM),
            pl.BlockSpec((CHUNK, Q_WIDTH), cur(0)),
            pl.BlockSpec((CHUNK, Q_WIDTH), cur(1)),
            pl.BlockSpec((CHUNK, KV_WIDTH), cur(0)),
            pl.BlockSpec((CHUNK, KV_WIDTH), prev(0)),
            pl.BlockSpec((CHUNK, KV_WIDTH), cur(1)),
            pl.BlockSpec((CHUNK, KV_WIDTH), prev(1)),
            pl.BlockSpec(memory_space=pl.ANY),
        ],
        out_specs=pl.BlockSpec((CHUNK, Q_WIDTH), cur(0)),
        out_shape=jax.ShapeDtypeStruct(og.shape, BF16),
        scratch_shapes=_attention_scratch(1, CHUNK, 2 * CHUNK),
        input_output_aliases={7: 0},
        compiler_params=_params(("parallel", "parallel")),
        name="swa_prompt",
    )(sinks, pq, pq, kv, kv, kv, kv, og)


def _swa_sample_kernel(t_len, sink_ref, q_ref, gate_ref, kn_ref, vn_ref, ck_ref, cv_ref,
                       og_in_ref, og_ref, s_scr, p_scr):
    del og_in_ref
    n = SEQ_GROUP * t_len
    win = ck_ref.shape[1]
    row = lax.broadcasted_iota(jnp.int32, (n, 2 * win), 0)
    col = lax.broadcasted_iota(jnp.int32, (n, 2 * win), 1)
    t = row % t_len
    new = col - win
    in_window = ((col < win) & (col > t)) | (
        (new >= 0) & (new < n) & ((new // t_len) == (row // t_len)) & ((new % t_len) <= t))
    pad = jnp.zeros((win - n, LANES), F32)
    kv_pairs = []
    for u in range(SEQ_GROUP):
        keys = [jnp.concatenate([ck_ref[u, :, c * LANES:(c + 1) * LANES],
                                 kn_ref[:, c * LANES:(c + 1) * LANES], pad], axis=0)
                for c in range(KV_WIDTH // LANES)]
        vals = [jnp.concatenate([cv_ref[u, :, c * LANES:(c + 1) * LANES],
                                 vn_ref[:, c * LANES:(c + 1) * LANES], pad], axis=0)
                for c in range(KV_WIDTH // LANES)]
        kv_pairs.append((keys, vals, in_window & ((row // t_len) == u)))
    _attend_rows(q_ref, gate_ref, og_ref, sink_ref, kv_pairs, s_scr, p_scr)


def _swa_sample(pq, kv, cache_k, cache_v, sinks, og, n_seq, t_len, row0):
    n = SEQ_GROUP * t_len
    rb0 = row0 // n
    win = cache_k.shape[1]
    return pl.pallas_call(
        functools.partial(_swa_sample_kernel, t_len),
        grid=(n_seq // SEQ_GROUP,),
        in_specs=[
            pl.BlockSpec(memory_space=pltpu.SMEM),
            pl.BlockSpec((n, Q_WIDTH), lambda s: (rb0 + s, 0)),
            pl.BlockSpec((n, Q_WIDTH), lambda s: (rb0 + s, 1)),
            pl.BlockSpec((n, KV_WIDTH), lambda s: (rb0 + s, 0)),
            pl.BlockSpec((n, KV_WIDTH), lambda s: (rb0 + s, 1)),
            pl.BlockSpec((SEQ_GROUP, win, KV_WIDTH), lambda s: (s, 0, 0)),
            pl.BlockSpec((SEQ_GROUP, win, KV_WIDTH), lambda s: (s, 0, 0)),
            pl.BlockSpec(memory_space=pl.ANY),
        ],
        out_specs=pl.BlockSpec((n, Q_WIDTH), lambda s: (rb0 + s, 0)),
        out_shape=jax.ShapeDtypeStruct(og.shape, BF16),
        scratch_shapes=_attention_scratch(SEQ_GROUP, n, 2 * win),
        input_output_aliases={7: 0},
        compiler_params=_params(("parallel",)),
        name="swa_sample",
    )(sinks, pq, pq, kv, kv, cache_k, cache_v, og)


def kernel(x_prompt, x_sample, state_gla, cache_k_win, cache_v_win, meta_tokens, norm_a, w_in_a, w_gk2_a, b_gk2_a, onorm_a, w_out_a, norm_kv, w_kv, norm_b, w_in_b, sinks_b, w_out_b, norm_f):
    n_batch, seq, _ = x_prompt.shape
    n_seq, t_len, _ = x_sample.shape
    win = cache_k_win.shape[1]
    n_a = w_in_a.shape[0]
    n_b = w_in_b.shape[0]
    assert win == WINDOW == CHUNK and n_seq % SEQ_GROUP == 0 and n_seq % GLA_SEQ_GROUP == 0

    length = seq + N_META
    front = (-length) % CHUNK
    lp = length + front
    chunks = lp // CHUNK
    srow = n_batch * lp
    ns = n_seq * t_len
    rows = -(-(srow + ns) // ROW_TILE) * ROW_TILE

    assert front + N_META == CHUNK and ns == CHUNK and seq % CHUNK == 0
    assert rows % CHUNK == 0 and rows % (16 * IN_PROJ_ROW_STEPS) == 0

    w_in_a_t = jnp.swapaxes(w_in_a, 1, 2)

    def gate_weights(l):
        w_gate = jnp.pad(w_in_a_t[l, GLA_MAIN_WIDTH:], ((0, LANES - GATE_RANK), (0, 0)))
        w_gk = jnp.pad(w_gk2_a[l], ((0, LANES - GATE_RANK), (0, 0))).astype(BF16)
        return w_gate, w_gk, b_gk2_a[l][None]

    x, hn, g = _embed(x_prompt.reshape(n_batch * seq, D_MODEL), x_sample.reshape(ns, D_MODEL),
                      meta_tokens, norm_a[0], gate_weights(0), rows, chunks, n_batch, front)
    og = jnp.zeros((rows, D_MODEL), BF16)

    states_p = []
    states_s = None
    for l in range(n_a):
        proj = _in_proj(hn, w_in_a_t, l, GLA_MAIN_WIDTH, BF16, True, "gla_in_proj")
        onorm = onorm_a[l][None]
        og, s_p = _gla_prompt(proj, g, onorm, og, n_batch, chunks)
        og, states_s = _gla_sample(proj, g, onorm, state_gla, og, states_s, l,
                                   n_seq, t_len, srow)
        states_p.append(s_p)
        w_out = w_out_a[l].astype(BF16)
        if l + 1 < n_a:
            x, hn, g = _out_proj(og, w_out, x, [norm_a[l + 1]], gate_weights(l + 1), True,
                                 BF16, "gla_out_proj")
        else:
            x, hn_kv, hn = _out_proj(og, w_out, x, [norm_kv, norm_b[0]], None, True,
                                     BF16, "gla_out_proj")

    kv = _in_proj(hn_kv, w_kv[None], 0, 2 * KV_WIDTH, F32, False, "kv_proj")
    cache_k = cache_k_win.reshape(n_seq, win, KV_WIDTH)
    cache_v = cache_v_win.reshape(n_seq, win, KV_WIDTH)
    for j in range(n_b):
        pq = _in_proj(hn, w_in_b, j, 2 * Q_WIDTH, BF16, False, "swa_in_proj")
        og = _swa_prompt(pq, kv, sinks_b[j], og, n_batch, chunks, front)
        og = _swa_sample(pq, kv, cache_k, cache_v, sinks_b[j], og, n_seq, t_len, srow)
        w_out = w_out_b[j].astype(BF16)
        if j + 1 < n_b:
            x, hn = _out_proj(og, w_out, x, [norm_b[j + 1]], None, True, BF16, "swa_out_proj")

    tiles_per_batch = seq // ROW_TILE
    lead_blocks = (lp - seq) // CHUNK

    def prompt_first_block(t):
        return ((t // tiles_per_batch) * chunks + lead_blocks
                + (t % tiles_per_batch) * (ROW_TILE // CHUNK))

    (y_prompt,) = _out_proj(og, w_out, x, [norm_f], None, False, F32, "final_prompt",
                            gather=(ROW_TILE, n_batch * tiles_per_batch, prompt_first_block))
    (y_sample,) = _out_proj(og, w_out, x, [norm_f], None, False, F32, "final_sample",
                            gather=(CHUNK, ns // CHUNK, lambda t: srow // CHUNK + t))
    y_prompt = y_prompt.reshape(n_batch, seq, D_MODEL)
    y_sample = y_sample.reshape(n_seq, t_len, D_MODEL)
    kv_p = jnp.stack([kv[(b + 1) * lp - win:(b + 1) * lp] for b in range(n_batch)])
    kv_p = kv_p.reshape(n_batch, win, 2, N_KV_HEADS, HEAD_DIM)
    kv_s = kv[srow:srow + ns].reshape(n_seq, t_len, 2, N_KV_HEADS, HEAD_DIM)
    k_win_s = jnp.concatenate([cache_k_win, kv_s[:, :, 0]], axis=1)[:, -win:]
    v_win_s = jnp.concatenate([cache_v_win, kv_s[:, :, 1]], axis=1)[:, -win:]
    return (y_prompt, y_sample, jnp.stack(states_p), states_s,
            kv_p[:, :, 0], kv_p[:, :, 1], k_win_s, v_win_s)
```

```python
import functools

import jax
import jax.numpy as jnp
from jax import lax
from jax.experimental import pallas as pl
from jax.experimental.pallas import tpu as pltpu

F32 = jnp.float32
BF16 = jnp.bfloat16

D_MODEL = 2048
N_META = 16
GLA_HEADS = 4
GLA_DK = 256
GLA_DV = 512
GLA_KEY_DIM = GLA_HEADS * GLA_DK
GLA_VAL_DIM = GLA_HEADS * GLA_DV
GLA_MAIN_WIDTH = 2 * GLA_KEY_DIM + 2 * GLA_VAL_DIM
GATE_RANK = 16
GATE_LOGIT_NORM = 16.0
HEAD_DIM = 64
N_Q_HEADS = 32
N_KV_HEADS = 8
Q_WIDTH = N_Q_HEADS * HEAD_DIM
KV_WIDTH = N_KV_HEADS * HEAD_DIM
WINDOW = 128
RMS_EPS = 1e-6
NEG_INF = -1e30
LOG2_E = 1.4426950408889634

LANES = 128
CHUNK = 128
GLA_LEAF = 8
LEAF_BATCH = 4
GLA_CHUNKS_PER_STEP = 3
GLA_BOUNDED_RANGE = 60.0
ROW_TILE = 512
IN_PROJ_COLS = 1024
IN_PROJ_ROW_STEPS = 8
SEQ_GROUP = 4
GLA_SEQ_GROUP = 8
VMEM_LIMIT = 48 * 1024 * 1024

NT_DIMS = (((1,), (1,)), ((), ()))


def _silu(x):
    return x * (1.0 / (1.0 + jnp.exp(-x)))


def _params(sem):
    return pltpu.CompilerParams(dimension_semantics=sem, vmem_limit_bytes=VMEM_LIMIT)


def _emit_normed(y, nw_refs, gate_refs, hn_refs, g_ref):
    ms = jnp.mean(y * y, axis=-1, keepdims=True)
    yn = y * lax.rsqrt(ms + RMS_EPS)
    for nw_ref, hn_ref in zip(nw_refs, hn_refs):
        hn_ref[...] = (yn * nw_ref[...]).astype(hn_ref.dtype)
    if gate_refs:
        wg_ref, wgk_ref, bgk_ref = gate_refs
        glr = lax.dot_general(hn_refs[0][...], wg_ref[...].astype(BF16), NT_DIMS,
                              preferred_element_type=F32)
        z = jnp.dot(glr.astype(BF16), wgk_ref[...], preferred_element_type=F32) + bgk_ref[...]
        log_sig = jnp.minimum(z, 0.0) - jnp.log(1.0 + jnp.exp(-jnp.abs(z)))
        g_ref[...] = log_sig * (1.0 / GATE_LOGIT_NORM)


def _split_refs(refs, n_norms, with_gate):
    nw_refs = refs[:n_norms]
    gate_refs = refs[n_norms:n_norms + 3] if with_gate else None
    return nw_refs, gate_refs, refs[n_norms + 3 * with_gate:]


def _norm_specs(norms, gate, index):
    specs = [pl.BlockSpec((1, D_MODEL), index) for _ in norms]
    args = [nw[None] for nw in norms]
    if gate is not None:
        specs += [pl.BlockSpec((LANES, D_MODEL), index),
                  pl.BlockSpec((LANES, GLA_KEY_DIM), index),
                  pl.BlockSpec((1, GLA_KEY_DIM), index)]
        args += list(gate)
    return specs, args


def _embed_kernel(sub, lead_blocks, sample_block, front, with_gate, *refs):
    xp_refs, (xs_ref, meta_ref), refs = refs[:sub], refs[sub:sub + 2], refs[sub + 2:]
    nw_refs, gate_refs, outs = _split_refs(refs, 1, with_gate)
    x_ref = outs[0]
    for k in range(sub):
        blk = pl.program_id(0) * sub + k
        rows = slice(k * CHUNK, (k + 1) * CHUNK)
        is_lead = functools.reduce(jnp.logical_or, [blk == b for b in lead_blocks])

        @pl.when(is_lead)
        def _():
            x_ref[rows, :] = jnp.concatenate(
                [jnp.zeros((front, D_MODEL), F32), meta_ref[...]], axis=0)

        @pl.when(jnp.logical_and(jnp.logical_not(is_lead), blk < sample_block))
        def _():
            x_ref[rows, :] = xp_refs[k][...]

        @pl.when(blk == sample_block)
        def _():
            x_ref[rows, :] = xs_ref[...]

        @pl.when(blk > sample_block)
        def _():
            x_ref[rows, :] = jnp.zeros((CHUNK, D_MODEL), F32)

    _emit_normed(x_ref[...], nw_refs, gate_refs, outs[1:2], outs[2] if with_gate else None)


def _embed(x_prompt2d, x_sample2d, meta, norm, gate, rows, chunks, n_batch, front):
    sub = ROW_TILE // CHUNK
    prompt_blocks = x_prompt2d.shape[0] // CHUNK
    per_batch = prompt_blocks // n_batch

    def prompt_index(k):
        def index(i):
            blk = i * sub + k
            src = (blk // chunks) * per_batch + (blk % chunks) - 1
            return (jnp.clip(src, 0, prompt_blocks - 1), 0)
        return index

    norm_specs, norm_args = _norm_specs([norm], gate, lambda i: (0, 0))
    row_spec = pl.BlockSpec((ROW_TILE, D_MODEL), lambda i: (i, 0))
    out_specs = [row_spec, row_spec]
    out_shape = [jax.ShapeDtypeStruct((rows, D_MODEL), F32),
                 jax.ShapeDtypeStruct((rows, D_MODEL), BF16)]
    if gate is not None:
        out_specs.append(pl.BlockSpec((ROW_TILE, GLA_KEY_DIM), lambda i: (i, 0)))
        out_shape.append(jax.ShapeDtypeStruct((rows, GLA_KEY_DIM), F32))
    return pl.pallas_call(
        functools.partial(_embed_kernel, sub, tuple(b * chunks for b in range(n_batch)),
                          n_batch * chunks, front, gate is not None),
        grid=(rows // ROW_TILE,),
        in_specs=[pl.BlockSpec((CHUNK, D_MODEL), prompt_index(k)) for k in range(sub)] + [
            pl.BlockSpec((CHUNK, D_MODEL), lambda i: (0, 0)),
            pl.BlockSpec((N_META, D_MODEL), lambda i: (0, 0)),
        ] + norm_specs,
        out_specs=out_specs,
        out_shape=out_shape,
        compiler_params=_params(("parallel",)),
        name="embed",
    )(*([x_prompt2d] * sub), x_sample2d, meta, *norm_args)


def _in_proj_kernel(transposed, hn_ref, w_ref, o_ref, w_bf16_ref):
    @pl.when(pl.program_id(1) == 0)
    def _():
        w_bf16_ref[...] = w_ref[...].astype(BF16)

    dims = NT_DIMS if transposed else (((1,), (0,)), ((), ()))
    o_ref[...] = lax.dot_general(hn_ref[...], w_bf16_ref[...], dims,
                                 preferred_element_type=F32).astype(o_ref.dtype)


def _in_proj(hn, w_stack, layer, n_cols, out_dtype, transposed, name):
    rows = hn.shape[0]
    row_tile = rows // IN_PROJ_ROW_STEPS
    if transposed:
        slab = (IN_PROJ_COLS, D_MODEL)
        w_spec = pl.BlockSpec((None,) + slab, lambda j, i: (layer, j, 0))
    else:
        slab = (D_MODEL, IN_PROJ_COLS)
        w_spec = pl.BlockSpec((None,) + slab, lambda j, i: (layer, 0, j))
    return pl.pallas_call(
        functools.partial(_in_proj_kernel, transposed),
        grid=(n_cols // IN_PROJ_COLS, IN_PROJ_ROW_STEPS),
        in_specs=[pl.BlockSpec((row_tile, D_MODEL), lambda j, i: (i, 0)), w_spec],
        out_specs=pl.BlockSpec((row_tile, IN_PROJ_COLS), lambda j, i: (i, j)),
        out_shape=jax.ShapeDtypeStruct((rows, n_cols), out_dtype),
        scratch_shapes=[pltpu.VMEM(slab, BF16)],
        compiler_params=_params(("parallel", "arbitrary")),
        name=name,
    )(hn, w_stack)


def _out_proj_kernel(sub, n_norms, with_gate, keep_x, *refs):
    og_refs, w_ref, x_refs, refs = refs[:sub], refs[sub], refs[sub + 1:2 * sub + 1], refs[2 * sub + 1:]
    nw_refs, gate_refs, outs = _split_refs(refs, n_norms, with_gate)
    if sub == 1:
        og, x = og_refs[0][...], x_refs[0][...]
    else:
        og = jnp.concatenate([r[...] for r in og_refs], axis=0)
        x = jnp.concatenate([r[...] for r in x_refs], axis=0)
    y = x + jnp.dot(og, w_ref[...], preferred_element_type=F32)
    if keep_x:
        outs[0][...] = y
        outs = outs[1:]
    _emit_normed(y, nw_refs, gate_refs, outs[:n_norms], outs[n_norms] if with_gate else None)


def _out_proj(og, w, x, norms, gate, keep_x, normed_dtype, name, gather=None):
    if gather is None:
        tile_rows, n_tiles, sub = ROW_TILE, x.shape[0] // ROW_TILE, 1
        in_row_specs = [pl.BlockSpec((ROW_TILE, D_MODEL), lambda i: (i, 0))]
    else:
        tile_rows, n_tiles, first_block = gather
        sub = tile_rows // CHUNK
        in_row_specs = [pl.BlockSpec((CHUNK, D_MODEL),
                                     functools.partial(lambda k, i: (first_block(i) + k, 0), k))
                        for k in range(sub)]
    rows = tile_rows * n_tiles
    row_spec = pl.BlockSpec((tile_rows, D_MODEL), lambda i: (i, 0))
    norm_specs, norm_args = _norm_specs(norms, gate, lambda i: (0, 0))
    out_specs, out_shape = [], []
    if keep_x:
        out_specs.append(row_spec)
        out_shape.append(jax.ShapeDtypeStruct((rows, D_MODEL), F32))
    for _ in norms:
        out_specs.append(row_spec)
        out_shape.append(jax.ShapeDtypeStruct((rows, D_MODEL), normed_dtype))
    if gate is not None:
        out_specs.append(pl.BlockSpec((tile_rows, GLA_KEY_DIM), lambda i: (i, 0)))
        out_shape.append(jax.ShapeDtypeStruct((rows, GLA_KEY_DIM), F32))
    w_spec = pl.BlockSpec((D_MODEL, D_MODEL), lambda i: (0, 0), pipeline_mode=pl.Buffered(1))
    return pl.pallas_call(
        functools.partial(_out_proj_kernel, sub, len(norms), gate is not None, keep_x),
        grid=(n_tiles,),
        in_specs=in_row_specs + [w_spec] + in_row_specs + norm_specs,
        out_specs=out_specs,
        out_shape=out_shape,
        compiler_params=_params(("parallel",)),
        name=name,
    )(*([og] * sub), w, *([x] * sub), *norm_args)


def _exact_block(qs, k, b, n, width, col0):
    lane = lax.broadcasted_iota(jnp.int32, (n, width), 1)
    acc = jnp.zeros((n, width), F32)
    for j in range(n):
        t = qs * k[j:j + 1] * jnp.exp(jnp.minimum(b - b[j:j + 1], 0.0))
        acc = jnp.where(lane == col0 + j, jnp.sum(t, axis=-1, keepdims=True), acc)
    return acc


def _masked_row_sums(mask_bf16, g):
    g_hi = g.astype(BF16)
    r1 = g - g_hi.astype(F32)
    g_mid = r1.astype(BF16)
    g_lo = (r1 - g_mid.astype(F32)).astype(BF16)
    return (jnp.dot(mask_bf16, g_hi, preferred_element_type=F32)
            + jnp.dot(mask_bf16, g_mid, preferred_element_type=F32)
            + jnp.dot(mask_bf16, g_lo, preferred_element_type=F32))


def _column_of(row_vec, n):
    return jnp.broadcast_to(row_vec, (LANES, n)).T[:, 0:1]


def _head_norm_gate(o, onorm, gate):
    ms = jnp.mean(o * o, axis=-1, keepdims=True)
    return (o * lax.rsqrt(ms + RMS_EPS) * onorm * _silu(gate)).astype(BF16)


def _intra_chunk_scores(qs, k, k_bf, b, row_xor_col):
    c_len = qs.shape[0]
    leaf = GLA_LEAF
    leaf_row = lax.broadcasted_iota(jnp.int32, (leaf, c_len), 0)
    leaf_col = lax.broadcasted_iota(jnp.int32, (leaf, c_len), 1)
    lane_in_leaf = [(leaf_col & (leaf - 1)) == j for j in range(leaf)]

    att_rows = []
    for blk0 in range(0, c_len // leaf, LEAF_BATCH):
        lhs = []
        for blk in range(blk0, blk0 + LEAF_BATCH):
            q_blk = qs[blk * leaf:(blk + 1) * leaf]
            b_blk = b[blk * leaf:(blk + 1) * leaf]
            lhs += [q_blk * jnp.exp2(jnp.minimum(b_blk - b_blk[j:j + 1], 0.0))
                    for j in range(leaf)]
        r = lax.dot_general(jnp.concatenate(lhs, axis=0).astype(BF16), k_bf, NT_DIMS,
                            preferred_element_type=F32)
        for i, blk in enumerate(range(blk0, blk0 + LEAF_BATCH)):
            acc = jnp.zeros((leaf, c_len), F32)
            for j in range(leaf):
                r0 = (i * leaf + j) * leaf
                acc = jnp.where(lane_in_leaf[j], r[r0:r0 + leaf], acc)
            rel = leaf_col - blk * leaf
            att_rows.append(jnp.where((rel >= 0) & (rel <= leaf_row), acc, 0.0))
    att = jnp.concatenate(att_rows, axis=0)

    m = c_len // 2
    while m >= leaf:
        q_parts, k_parts = [], []
        zeros = jnp.zeros((m, GLA_DK), F32)
        for base in range(0, c_len, 2 * m):
            ref = b[base + m - 1:base + m]
            lo = slice(base, base + m)
            hi = slice(base + m, base + 2 * m)
            k_parts += [k[lo] * jnp.exp2(ref - b[lo]), zeros]
            q_parts += [zeros, qs[hi] * jnp.exp2(b[hi] - ref)]
        qt = jnp.concatenate(q_parts, axis=0).astype(BF16)
        kt = jnp.concatenate(k_parts, axis=0).astype(BF16)
        a = lax.dot_general(qt, kt, NT_DIMS, preferred_element_type=F32)
        att = att + jnp.where(row_xor_col < 2 * m, a, 0.0)
        m //= 2
    return att


def _gla_prompt_kernel(q_ref, k_ref, v_ref, gate_ref, g_ref, onorm_ref, og_in_ref,
                       og_ref, s_ref, b_scr, qd_scr, att_scr):
    del og_in_ref
    c_len = CHUNK
    heads = [(slice(h * GLA_DK, (h + 1) * GLA_DK), slice(h * GLA_DV, (h + 1) * GLA_DV))
             for h in range(GLA_HEADS)]

    @pl.when(pl.program_id(1) == 0)
    def _():
        s_ref[...] = jnp.zeros_like(s_ref)

    row = lax.broadcasted_iota(jnp.int32, (c_len, c_len), 0)
    col = lax.broadcasted_iota(jnp.int32, (c_len, c_len), 1)
    lower = row >= col
    tri = jnp.where(lower, 1.0, 0.0).astype(BF16)
    onorm = onorm_ref[...]
    scale = GLA_DK ** -0.5
    chunk_rows = [slice(c * c_len, (c + 1) * c_len) for c in range(q_ref.shape[0] // c_len)]

    total = None
    for rows in chunk_rows:
        b_c = _masked_row_sums(tri, g_ref[rows, :]) * LOG2_E
        b_scr[rows, :] = b_c
        qd_scr[rows, :] = (q_ref[rows, :].astype(F32) * scale * jnp.exp2(b_c)).astype(BF16)
        last = b_c[c_len - 1:c_len]
        total = last if total is None else jnp.minimum(total, last)

    bounded = jnp.max(-total) <= GLA_BOUNDED_RANGE

    @pl.when(bounded)
    def _():
        for c, rows in enumerate(chunk_rows):
            for h, (ks, _) in enumerate(heads):
                k_inv = (k_ref[rows, ks].astype(F32) * jnp.exp2(-b_scr[rows, ks])).astype(BF16)
                a = lax.dot_general(qd_scr[rows, ks], k_inv, NT_DIMS,
                                    preferred_element_type=F32)
                att_scr[c * GLA_HEADS + h] = jnp.where(lower, a, 0.0).astype(BF16)

    @pl.when(jnp.logical_not(bounded))
    def _():
        for c, rows in enumerate(chunk_rows):
            for h, (ks, _) in enumerate(heads):
                k_bf = k_ref[rows, ks]
                att_scr[c * GLA_HEADS + h] = _intra_chunk_scores(
                    q_ref[rows, ks].astype(F32) * scale, k_bf.astype(F32), k_bf,
                    b_scr[rows, ks], row ^ col).astype(BF16)

    for c, rows in enumerate(chunk_rows):
        for h, (ks, vs) in enumerate(heads):
            k = k_ref[rows, ks].astype(F32)
            v = v_ref[rows, vs]
            b = b_scr[rows, ks]
            b_last = b[c_len - 1:c_len]
            state = s_ref[0, h]

            o = jnp.dot(qd_scr[rows, ks], state.astype(BF16), preferred_element_type=F32)
            o = o + jnp.dot(att_scr[c * GLA_HEADS + h], v, preferred_element_type=F32)

            kd_t = (k * jnp.exp2(b_last - b)).T.astype(BF16)
            s_ref[0, h] = (state * _column_of(jnp.exp2(b_last), GLA_DK)
                           + jnp.dot(kd_t, v, preferred_element_type=F32))

            og_ref[rows, vs] = _head_norm_gate(o, onorm, gate_ref[rows, vs].astype(F32))


def _gla_prompt(proj, g, onorm, og, n_batch, chunks):
    rows = proj.shape[0]
    per_step = next(n for n in (GLA_CHUNKS_PER_STEP, 1) if chunks % n == 0)
    steps = chunks // per_step
    tile = per_step * CHUNK
    blk = lambda col: (lambda b, c: (b * steps + c, col))
    return pl.pallas_call(
        _gla_prompt_kernel,
        grid=(n_batch, steps),
        in_specs=[
            pl.BlockSpec((tile, GLA_KEY_DIM), blk(0)),
            pl.BlockSpec((tile, GLA_KEY_DIM), blk(1)),
            pl.BlockSpec((tile, GLA_VAL_DIM), blk(1)),
            pl.BlockSpec((tile, GLA_VAL_DIM), blk(2)),
            pl.BlockSpec((tile, GLA_KEY_DIM), blk(0)),
            pl.BlockSpec((1, GLA_DV), lambda b, c: (0, 0)),
            pl.BlockSpec(memory_space=pl.ANY),
        ],
        out_specs=[
            pl.BlockSpec((tile, GLA_VAL_DIM), blk(0)),
            pl.BlockSpec((1, GLA_HEADS, GLA_DK, GLA_DV), lambda b, c: (b, 0, 0, 0)),
        ],
        out_shape=[
            jax.ShapeDtypeStruct((rows, GLA_VAL_DIM), BF16),
            jax.ShapeDtypeStruct((n_batch, GLA_HEADS, GLA_DK, GLA_DV), F32),
        ],
        scratch_shapes=[pltpu.VMEM((tile, GLA_KEY_DIM), F32),
                        pltpu.VMEM((tile, GLA_KEY_DIM), BF16),
                        pltpu.VMEM((per_step * GLA_HEADS, CHUNK, CHUNK), BF16)],
        input_output_aliases={6: 0},
        compiler_params=_params(("parallel", "arbitrary")),
        name="gla_prompt",
    )(proj, proj, proj, proj, g, onorm, og)


def _gla_sample_kernel(t_len, q_ref, k_ref, v_ref, gate_ref, g_ref, onorm_ref, s0_ref,
                       og_in_ref, *rest):
    og_ref, s_ref = rest[-2], rest[-1]
    n = GLA_SEQ_GROUP * t_len
    row = lax.broadcasted_iota(jnp.int32, (n, LANES), 0)
    col = lax.broadcasted_iota(jnp.int32, (n, LANES), 1)
    causal = ((row // t_len) == (col // t_len)) & (row >= col)
    row_seq = lax.broadcasted_iota(jnp.int32, (n, 1), 0) // t_len

    qs = q_ref[...].astype(F32) * (GLA_DK ** -0.5)
    k = k_ref[...].astype(F32)
    pad_k = jnp.zeros((LANES - n, GLA_DK), F32)
    v_pad = jnp.concatenate([v_ref[...], jnp.zeros((LANES - n, GLA_DV), BF16)], axis=0)
    b = _masked_row_sums(jnp.where(causal, 1.0, 0.0).astype(BF16),
                         jnp.concatenate([g_ref[...], pad_k], axis=0))
    att = jnp.where(causal, _exact_block(qs, k, b, n, LANES, 0), 0.0)
    o = jnp.dot(att.astype(BF16), v_pad, preferred_element_type=F32)

    q_dec = (qs * jnp.exp(b)).astype(BF16)
    for u in range(GLA_SEQ_GROUP):
        state = s0_ref[0, u, 0]
        o_u = jnp.dot(q_dec, state.astype(BF16), preferred_element_type=F32)
        o = o + jnp.where(row_seq == u, o_u, 0.0)
        b_last = b[(u + 1) * t_len - 1:(u + 1) * t_len]
        kd = jnp.where(row_seq == u, k * jnp.exp(jnp.minimum(b_last - b, 0.0)), 0.0)
        kd_t = jnp.concatenate([kd, pad_k], axis=0).T.astype(BF16)
        s_ref[0, u, 0] = (state * _column_of(jnp.exp(b_last), GLA_DK)
                          + jnp.dot(kd_t, v_pad, preferred_element_type=F32))

    og_ref[...] = _head_norm_gate(o, onorm_ref[...], gate_ref[...].astype(F32))


def _gla_sample(proj, g, onorm, state_in, og, s_out, layer, n_seq, t_len, row0):
    n = GLA_SEQ_GROUP * t_len
    rb0 = row0 // n
    n_layers = state_in.shape[0]
    key_blocks = GLA_KEY_DIM // GLA_DK
    val_blocks = GLA_VAL_DIM // GLA_DV
    st_spec = pl.BlockSpec((1, GLA_SEQ_GROUP, 1, GLA_DK, GLA_DV),
                           lambda s, h: (layer, s, h, 0, 0))
    in_specs = [
        pl.BlockSpec((n, GLA_DK), lambda s, h: (rb0 + s, h)),
        pl.BlockSpec((n, GLA_DK), lambda s, h: (rb0 + s, key_blocks + h)),
        pl.BlockSpec((n, GLA_DV), lambda s, h: (rb0 + s, val_blocks + h)),
        pl.BlockSpec((n, GLA_DV), lambda s, h: (rb0 + s, 2 * val_blocks + h)),
        pl.BlockSpec((n, GLA_DK), lambda s, h: (rb0 + s, h)),
        pl.BlockSpec((1, GLA_DV), lambda s, h: (0, 0)),
        st_spec,
        pl.BlockSpec(memory_space=pl.ANY),
    ]
    args = [proj, proj, proj, proj, g, onorm, state_in, og]
    aliases = {7: 0}
    if s_out is not None:
        in_specs.append(pl.BlockSpec(memory_space=pl.ANY))
        args.append(s_out)
        aliases[8] = 1
    return pl.pallas_call(
        functools.partial(_gla_sample_kernel, t_len),
        grid=(n_seq // GLA_SEQ_GROUP, GLA_HEADS),
        in_specs=in_specs,
        out_specs=[
            pl.BlockSpec((n, GLA_DV), lambda s, h: (rb0 + s, h)),
            st_spec,
        ],
        out_shape=[
            jax.ShapeDtypeStruct(og.shape, BF16),
            jax.ShapeDtypeStruct((n_layers, n_seq, GLA_HEADS, GLA_DK, GLA_DV), F32),
        ],
        input_output_aliases=aliases,
        compiler_params=_params(("parallel", "parallel")),
        name="gla_sample",
    )(*args)


def _half_lane_copies(kc, parity):
    kr = pltpu.roll(kc, HEAD_DIM, axis=1)
    return (kc, kr) if parity == 0 else (kr, kc)


def _attend_rows(q_ref, gate_ref, og_ref, sink_ref, kv_sets, s_scr, p_scr):
    n_rows = q_ref.shape[0]
    n_keys = s_scr.shape[2]
    group = N_Q_HEADS // N_KV_HEADS
    low_out = lax.broadcasted_iota(jnp.int32, (n_rows, LANES), 1) < HEAD_DIM
    lane0 = lax.broadcasted_iota(jnp.int32, (n_rows, LANES), 1) == 0
    key_lane = lax.broadcasted_iota(jnp.int32, (n_keys, LANES), 1)
    key_row = lax.broadcasted_iota(jnp.int32, (n_keys, LANES), 0)
    low = key_lane < HEAD_DIM
    slot0 = key_row == 0

    def head_order():
        for cpair in range(KV_WIDTH // LANES):
            for parity in range(2):
                for pr in range(group // 2):
                    yield cpair, parity, (2 * cpair + parity) * (group // 2) + pr

    idx = 0
    for keys, _, _ in kv_sets:
        k_pads = None
        for cpair, parity, qc in head_order():
            if qc % (group // 2) == 0:
                lo_src, hi_src = _half_lane_copies(keys[cpair], parity)
                k_pads = (jnp.where(low, lo_src, 0.0).astype(BF16),
                          jnp.where(low, 0.0, hi_src).astype(BF16))
            q2 = (q_ref[:, qc * LANES:(qc + 1) * LANES].astype(F32)
                  * (HEAD_DIM ** -0.5 * LOG2_E)).astype(BF16)
            for k_pad in k_pads:
                s_scr[idx] = lax.dot_general(q2, k_pad, NT_DIMS,
                                             preferred_element_type=F32)
                idx += 1

    idx = 0
    for _, _, mask in kv_sets:
        mask_first, mask_rest = mask[:, :LANES], mask[:, LANES:]
        for _, _, qc in head_order():
            for half in range(2):
                sink = sink_ref[2 * qc + half] * LOG2_E
                s = s_scr[idx]
                first = jnp.where(mask_first, s[:, :LANES],
                                  jnp.where(lane0, sink, NEG_INF))
                rest = jnp.where(mask_rest, s[:, LANES:], NEG_INF)
                m = jnp.max(jnp.maximum(first, rest), axis=-1, keepdims=True)
                p_scr[idx] = jnp.exp2(
                    jnp.concatenate([first, rest], axis=1) - m).astype(BF16)
                idx += 1

    def store(qc, o):
        cols = slice(qc * LANES, (qc + 1) * LANES)
        og_ref[:, cols] = (o * _silu(gate_ref[:, cols].astype(F32))).astype(BF16)

    idx = 0
    out_cols = [None] * (Q_WIDTH // LANES)
    for _, vals, _ in kv_sets:
        v_pads = None
        for cpair, parity, qc in head_order():
            if qc % (group // 2) == 0:
                lo_src, hi_src = _half_lane_copies(vals[cpair], parity)
                v_pads = (jnp.where(low, jnp.where(slot0, 0.0, lo_src), 1.0).astype(BF16),
                          jnp.where(low, 1.0, jnp.where(slot0, 0.0, hi_src)).astype(BF16))
            oa = jnp.dot(p_scr[idx], v_pads[0], preferred_element_type=F32)
            ob = jnp.dot(p_scr[idx + 1], v_pads[1], preferred_element_type=F32)
            idx += 2
            sums = pltpu.roll(jnp.where(low_out, ob, oa), HEAD_DIM, axis=1)
            o2 = jnp.where(low_out, oa, ob) * (1.0 / sums)
            if len(kv_sets) == 1:
                store(qc, o2)
            else:
                out_cols[qc] = o2 if out_cols[qc] is None else out_cols[qc] + o2

    if len(kv_sets) > 1:
        for qc, o in enumerate(out_cols):
            store(qc, o)


def _attention_scratch(n_sets, n_rows, n_keys):
    n = n_sets * N_Q_HEADS
    return [pltpu.VMEM((n, n_rows, n_keys), F32), pltpu.VMEM((n, n_rows, n_keys), BF16)]


def _swa_prompt_kernel(front, sink_ref, q_ref, gate_ref, kc_ref, kp_ref, vc_ref, vp_ref,
                       og_in_ref, og_ref, s_scr, p_scr):
    del og_in_ref
    blk = pl.program_id(1)
    row = lax.broadcasted_iota(jnp.int32, (CHUNK, 2 * CHUNK), 0)
    col = lax.broadcasted_iota(jnp.int32, (CHUNK, 2 * CHUNK), 1)
    mask = (col > row) & (col <= row + WINDOW) & ((blk - 1) * CHUNK + col >= front)
    keys = [jnp.concatenate([kp_ref[:, c * LANES:(c + 1) * LANES],
                             kc_ref[:, c * LANES:(c + 1) * LANES]], axis=0)
            for c in range(KV_WIDTH // LANES)]
    vals = [jnp.concatenate([vp_ref[:, c * LANES:(c + 1) * LANES],
                             vc_ref[:, c * LANES:(c + 1) * LANES]], axis=0)
            for c in range(KV_WIDTH // LANES)]
    _attend_rows(q_ref, gate_ref, og_ref, sink_ref, [(keys, vals, mask)], s_scr, p_scr)


def _swa_prompt(pq, kv, sinks, og, n_batch, chunks, front):
    cur = lambda col: (lambda b, i: (b * chunks + i, col))
    prev = lambda col: (lambda b, i: (b * chunks + jnp.maximum(i - 1, 0), col))
    return pl.pallas_call(
        functools.partial(_swa_prompt_kernel, front),
        grid=(n_batch, chunks),
        in_specs=[
            pl.BlockSpec(memory_space=pltpu.SMEM),
            pl.BlockSpec((CHUNK, Q_WIDTH), cur(0)),
            pl.BlockSpec((CHUNK, Q_WIDTH), cur(1)),
            pl.BlockSpec((CHUNK, KV_WIDTH), cur(0)),
            pl.BlockSpec((CHUNK, KV_WIDTH), prev(0)),
            pl.BlockSpec((CHUNK, KV_WIDTH), cur(1)),
            pl.BlockSpec((CHUNK, KV_WIDTH), prev(1)),
            pl.BlockSpec(memory_space=pl.ANY),
        ],
        out_specs=pl.BlockSpec((CHUNK, Q_WIDTH), cur(0)),
        out_shape=jax.ShapeDtypeStruct(og.shape, BF16),
        scratch_shapes=_attention_scratch(1, CHUNK, 2 * CHUNK),
        input_output_aliases={7: 0},
        compiler_params=_params(("parallel", "parallel")),
        name="swa_prompt",
    )(sinks, pq, pq, kv, kv, kv, kv, og)


def _swa_sample_kernel(t_len, sink_ref, q_ref, gate_ref, kn_ref, vn_ref, ck_ref, cv_ref,
                       og_in_ref, og_ref, s_scr, p_scr):
    del og_in_ref
    n = SEQ_GROUP * t_len
    win = ck_ref.shape[1]
    row = lax.broadcasted_iota(jnp.int32, (n, 2 * win), 0)
    col = lax.broadcasted_iota(jnp.int32, (n, 2 * win), 1)
    t = row % t_len
    new = col - win
    in_window = ((col < win) & (col > t)) | (
        (new >= 0) & (new < n) & ((new // t_len) == (row // t_len)) & ((new % t_len) <= t))
    pad = jnp.zeros((win - n, LANES), F32)
    kv_pairs = []
    for u in range(SEQ_GROUP):
        keys = [jnp.concatenate([ck_ref[u, :, c * LANES:(c + 1) * LANES],
                                 kn_ref[:, c * LANES:(c + 1) * LANES], pad], axis=0)
                for c in range(KV_WIDTH // LANES)]
        vals = [jnp.concatenate([cv_ref[u, :, c * LANES:(c + 1) * LANES],
                                 vn_ref[:, c * LANES:(c + 1) * LANES], pad], axis=0)
                for c in range(KV_WIDTH // LANES)]
        kv_pairs.append((keys, vals, in_window & ((row // t_len) == u)))
    _attend_rows(q_ref, gate_ref, og_ref, sink_ref, kv_pairs, s_scr, p_scr)


def _swa_sample(pq, kv, cache_k, cache_v, sinks, og, n_seq, t_len, row0):
    n = SEQ_GROUP * t_len
    rb0 = row0 // n
    win = cache_k.shape[1]
    return pl.pallas_call(
        functools.partial(_swa_sample_kernel, t_len),
        grid=(n_seq // SEQ_GROUP,),
        in_specs=[
            pl.BlockSpec(memory_space=pltpu.SMEM),
            pl.BlockSpec((n, Q_WIDTH), lambda s: (rb0 + s, 0)),
            pl.BlockSpec((n, Q_WIDTH), lambda s: (rb0 + s, 1)),
            pl.BlockSpec((n, KV_WIDTH), lambda s: (rb0 + s, 0)),
            pl.BlockSpec((n, KV_WIDTH), lambda s: (rb0 + s, 1)),
            pl.BlockSpec((SEQ_GROUP, win, KV_WIDTH), lambda s: (s, 0, 0)),
            pl.BlockSpec((SEQ_GROUP, win, KV_WIDTH), lambda s: (s, 0, 0)),
            pl.BlockSpec(memory_space=pl.ANY),
        ],
        out_specs=pl.BlockSpec((n, Q_WIDTH), lambda s: (rb0 + s, 0)),
        out_shape=jax.ShapeDtypeStruct(og.shape, BF16),
        scratch_shapes=_attention_scratch(SEQ_GROUP, n, 2 * win),
        input_output_aliases={7: 0},
        compiler_params=_params(("parallel",)),
        name="swa_sample",
    )(sinks, pq, pq, kv, kv, cache_k, cache_v, og)


def kernel(x_prompt, x_sample, state_gla, cache_k_win, cache_v_win, meta_tokens, norm_a, w_in_a, w_gk2_a, b_gk2_a, onorm_a, w_out_a, norm_kv, w_kv, norm_b, w_in_b, sinks_b, w_out_b, norm_f):
    n_batch, seq, _ = x_prompt.shape
    n_seq, t_len, _ = x_sample.shape
    win = cache_k_win.shape[1]
    n_a = w_in_a.shape[0]
    n_b = w_in_b.shape[0]
    assert win == WINDOW == CHUNK and n_seq % SEQ_GROUP == 0 and n_seq % GLA_SEQ_GROUP == 0

    length = seq + N_META
    front = (-length) % CHUNK
    lp = length + front
    chunks = lp // CHUNK
    srow = n_batch * lp
    ns = n_seq * t_len
    rows = -(-(srow + ns) // ROW_TILE) * ROW_TILE

    assert front + N_META == CHUNK and ns == CHUNK and seq % CHUNK == 0
    assert rows % CHUNK == 0 and rows % (16 * IN_PROJ_ROW_STEPS) == 0

    w_in_a_t = jnp.swapaxes(w_in_a, 1, 2)

    def gate_weights(l):
        w_gate = jnp.pad(w_in_a_t[l, GLA_MAIN_WIDTH:], ((0, LANES - GATE_RANK), (0, 0)))
        w_gk = jnp.pad(w_gk2_a[l], ((0, LANES - GATE_RANK), (0, 0))).astype(BF16)
        return w_gate, w_gk, b_gk2_a[l][None]

    x, hn, g = _embed(x_prompt.reshape(n_batch * seq, D_MODEL), x_sample.reshape(ns, D_MODEL),
                      meta_tokens, norm_a[0], gate_weights(0), rows, chunks, n_batch, front)
    og = jnp.zeros((rows, D_MODEL), BF16)

    states_p = []
    states_s = None
    for l in range(n_a):
        proj = _in_proj(hn, w_in_a_t, l, GLA_MAIN_WIDTH, BF16, True, "gla_in_proj")
        onorm = onorm_a[l][None]
        og, s_p = _gla_prompt(proj, g, onorm, og, n_batch, chunks)
        og, states_s = _gla_sample(proj, g, onorm, state_gla, og, states_s, l,
                                   n_seq, t_len, srow)
        states_p.append(s_p)
        w_out = w_out_a[l].astype(BF16)
        if l + 1 < n_a:
            x, hn, g = _out_proj(og, w_out, x, [norm_a[l + 1]], gate_weights(l + 1), True,
                                 BF16, "gla_out_proj")
        else:
            x, hn_kv, hn = _out_proj(og, w_out, x, [norm_kv, norm_b[0]], None, True,
                                     BF16, "gla_out_proj")

    kv = _in_proj(hn_kv, w_kv[None], 0, 2 * KV_WIDTH, F32, False, "kv_proj")
    cache_k = cache_k_win.reshape(n_seq, win, KV_WIDTH)
    cache_v = cache_v_win.reshape(n_seq, win, KV_WIDTH)
    for j in range(n_b):
        pq = _in_proj(hn, w_in_b, j, 2 * Q_WIDTH, BF16, False, "swa_in_proj")
        og = _swa_prompt(pq, kv, sinks_b[j], og, n_batch, chunks, front)
        og = _swa_sample(pq, kv, cache_k, cache_v, sinks_b[j], og, n_seq, t_len, srow)
        w_out = w_out_b[j].astype(BF16)
        if j + 1 < n_b:
            x, hn = _out_proj(og, w_out, x, [norm_b[j + 1]], None, True, BF16, "swa_out_proj")

    tiles_per_batch = seq // ROW_TILE
    lead_blocks = (lp - seq) // CHUNK

    def prompt_first_block(t):
        return ((t // tiles_per_batch) * chunks + lead_blocks
                + (t % tiles_per_batch) * (ROW_TILE // CHUNK))

    (y_prompt,) = _out_proj(og, w_out, x, [norm_f], None, False, F32, "final_prompt",
                            gather=(ROW_TILE, n_batch * tiles_per_batch, prompt_first_block))
    (y_sample,) = _out_proj(og, w_out, x, [norm_f], None, False, F32, "final_sample",
                            gather=(CHUNK, ns // CHUNK, lambda t: srow // CHUNK + t))
    y_prompt = y_prompt.reshape(n_batch, seq, D_MODEL)
    y_sample = y_sample.reshape(n_seq, t_len, D_MODEL)
    kv_p = jnp.stack([kv[(b + 1) * lp - win:(b + 1) * lp] for b in range(n_batch)])
    kv_p = kv_p.reshape(n_batch, win, 2, N_KV_HEADS, HEAD_DIM)
    kv_s = kv[srow:srow + ns].reshape(n_seq, t_len, 2, N_KV_HEADS, HEAD_DIM)
    k_win_s = jnp.concatenate([cache_k_win, kv_s[:, :, 0]], axis=1)[:, -win:]
    v_win_s = jnp.concatenate([cache_v_win, kv_s[:, :, 1]], axis=1)[:, -win:]
    return (y_prompt, y_sample, jnp.stack(states_p), states_s,
            kv_p[:, :, 0], kv_p[:, :, 1], k_win_s, v_win_s)
```

```python
import functools

import jax
import jax.numpy as jnp
from jax import lax
from jax.experimental import pallas as pl
from jax.experimental.pallas import tpu as pltpu

F32 = jnp.float32
BF16 = jnp.bfloat16

D_MODEL = 2048
N_META = 16
GLA_HEADS = 4
GLA_DK = 256
GLA_DV = 512
GLA_KEY_DIM = GLA_HEADS * GLA_DK
GLA_VAL_DIM = GLA_HEADS * GLA_DV
GLA_MAIN_WIDTH = 2 * GLA_KEY_DIM + 2 * GLA_VAL_DIM
GATE_RANK = 16
GATE_LOGIT_NORM = 16.0
HEAD_DIM = 64
N_Q_HEADS = 32
N_KV_HEADS = 8
Q_WIDTH = N_Q_HEADS * HEAD_DIM
KV_WIDTH = N_KV_HEADS * HEAD_DIM
WINDOW = 128
RMS_EPS = 1e-6
NEG_INF = -1e30
LOG2_E = 1.4426950408889634

LANES = 128
CHUNK = 128
GLA_LEAF = 8
LEAF_BATCH = 4
GLA_CHUNKS_PER_STEP = 3
GLA_BOUNDED_RANGE = 60.0
ROW_TILE = 512
IN_PROJ_COLS = 1024
IN_PROJ_ROW_STEPS = 8
SEQ_GROUP = 4
GLA_SEQ_GROUP = 8
VMEM_LIMIT = 48 * 1024 * 1024

NT_DIMS = (((1,), (1,)), ((), ()))


def _silu(x):
    return x * (1.0 / (1.0 + jnp.exp(-x)))


def _params(sem):
    return pltpu.CompilerParams(dimension_semantics=sem, vmem_limit_bytes=VMEM_LIMIT)


def _emit_normed(y, nw_refs, gate_refs, hn_refs, g_ref):
    ms = jnp.mean(y * y, axis=-1, keepdims=True)
    yn = y * lax.rsqrt(ms + RMS_EPS)
    for nw_ref, hn_ref in zip(nw_refs, hn_refs):
        hn_ref[...] = (yn * nw_ref[...]).astype(hn_ref.dtype)
    if gate_refs:
        g_ref[...] = _decay_gate(hn_refs[0][...], *gate_refs)


def _split_refs(refs, n_norms, with_gate):
    nw_refs = refs[:n_norms]
    gate_refs = refs[n_norms:n_norms + 3] if with_gate else None
    return nw_refs, gate_refs, refs[n_norms + 3 * with_gate:]


def _norm_specs(norms, gate, index):
    specs = [pl.BlockSpec((1, D_MODEL), index) for _ in norms]
    args = [nw[None] for nw in norms]
    if gate is not None:
        specs += [pl.BlockSpec((LANES, D_MODEL), index),
                  pl.BlockSpec((LANES, GLA_KEY_DIM), index),
                  pl.BlockSpec((1, GLA_KEY_DIM), index)]
        args += list(gate)
    return specs, args


def _embed_kernel(sub, lead_blocks, sample_block, front, with_gate, *refs):
    xp_refs, (xs_ref, meta_ref), refs = refs[:sub], refs[sub:sub + 2], refs[sub + 2:]
    nw_refs, gate_refs, outs = _split_refs(refs, 1, with_gate)
    x_ref = outs[0]
    for k in range(sub):
        blk = pl.program_id(0) * sub + k
        rows = slice(k * CHUNK, (k + 1) * CHUNK)
        is_lead = functools.reduce(jnp.logical_or, [blk == b for b in lead_blocks])

        @pl.when(is_lead)
        def _():
            x_ref[rows, :] = jnp.concatenate(
                [jnp.zeros((front, D_MODEL), F32), meta_ref[...]], axis=0)

        @pl.when(jnp.logical_and(jnp.logical_not(is_lead), blk < sample_block))
        def _():
            x_ref[rows, :] = xp_refs[k][...]

        @pl.when(blk == sample_block)
        def _():
            x_ref[rows, :] = xs_ref[...]

        @pl.when(blk > sample_block)
        def _():
            x_ref[rows, :] = jnp.zeros((CHUNK, D_MODEL), F32)

    _emit_normed(x_ref[...], nw_refs, gate_refs, outs[1:2], outs[2] if with_gate else None)


def _embed(x_prompt2d, x_sample2d, meta, norm, gate, rows, chunks, n_batch, front):
    sub = ROW_TILE // CHUNK
    prompt_blocks = x_prompt2d.shape[0] // CHUNK
    per_batch = prompt_blocks // n_batch

    def prompt_index(k):
        def index(i):
            blk = i * sub + k
            src = (blk // chunks) * per_batch + (blk % chunks) - 1
            return (jnp.clip(src, 0, prompt_blocks - 1), 0)
        return index

    norm_specs, norm_args = _norm_specs([norm], gate, lambda i: (0, 0))
    row_spec = pl.BlockSpec((ROW_TILE, D_MODEL), lambda i: (i, 0))
    out_specs = [row_spec, row_spec]
    out_shape = [jax.ShapeDtypeStruct((rows, D_MODEL), F32),
                 jax.ShapeDtypeStruct((rows, D_MODEL), BF16)]
    if gate is not None:
        out_specs.append(pl.BlockSpec((ROW_TILE, GLA_KEY_DIM), lambda i: (i, 0)))
        out_shape.append(jax.ShapeDtypeStruct((rows, GLA_KEY_DIM), F32))
    return pl.pallas_call(
        functools.partial(_embed_kernel, sub, tuple(b * chunks for b in range(n_batch)),
                          n_batch * chunks, front, gate is not None),
        grid=(rows // ROW_TILE,),
        in_specs=[pl.BlockSpec((CHUNK, D_MODEL), prompt_index(k)) for k in range(sub)] + [
            pl.BlockSpec((CHUNK, D_MODEL), lambda i: (0, 0)),
            pl.BlockSpec((N_META, D_MODEL), lambda i: (0, 0)),
        ] + norm_specs,
        out_specs=out_specs,
        out_shape=out_shape,
        compiler_params=_params(("parallel",)),
        name="embed",
    )(*([x_prompt2d] * sub), x_sample2d, meta, *norm_args)


def _decay_gate(hn, wg_ref, wgk_ref, bgk_ref):
    glr = lax.dot_general(hn, wg_ref[...].astype(BF16), NT_DIMS, preferred_element_type=F32)
    z = jnp.dot(glr.astype(BF16), wgk_ref[...], preferred_element_type=F32) + bgk_ref[...]
    log_sig = jnp.minimum(z, 0.0) - jnp.log(1.0 + jnp.exp(-jnp.abs(z)))
    return log_sig * (1.0 / GATE_LOGIT_NORM)


def _in_proj_kernel(transposed, scale_ref, hn_ref, w_ref, o_ref, w_bf16_ref):
    @pl.when(pl.program_id(1) == 0)
    def _():
        w_bf16_ref[...] = w_ref[...].astype(BF16)

    dims = NT_DIMS if transposed else (((1,), (0,)), ((), ()))
    o = lax.dot_general(hn_ref[...], w_bf16_ref[...], dims, preferred_element_type=F32)
    o_ref[...] = (o * scale_ref[pl.program_id(0)]).astype(o_ref.dtype)


def _in_proj(hn, w_stack, layer, n_cols, out_dtype, transposed, name, q_cols=0, q_scale=1.0):
    rows = hn.shape[0]
    row_tile = rows // IN_PROJ_ROW_STEPS
    n_slabs = n_cols // IN_PROJ_COLS
    assert q_cols % IN_PROJ_COLS == 0
    scales = jnp.where(jnp.arange(n_slabs) < q_cols // IN_PROJ_COLS, q_scale, 1.0).astype(F32)
    if transposed:
        slab = (IN_PROJ_COLS, D_MODEL)
        w_spec = pl.BlockSpec((None,) + slab, lambda j, i: (layer, j, 0))
    else:
        slab = (D_MODEL, IN_PROJ_COLS)
        w_spec = pl.BlockSpec((None,) + slab, lambda j, i: (layer, 0, j))
    return pl.pallas_call(
        functools.partial(_in_proj_kernel, transposed),
        grid=(n_slabs, IN_PROJ_ROW_STEPS),
        in_specs=[pl.BlockSpec(memory_space=pltpu.SMEM),
                  pl.BlockSpec((row_tile, D_MODEL), lambda j, i: (i, 0)), w_spec],
        out_specs=pl.BlockSpec((row_tile, IN_PROJ_COLS), lambda j, i: (i, j)),
        out_shape=jax.ShapeDtypeStruct((rows, n_cols), out_dtype),
        scratch_shapes=[pltpu.VMEM(slab, BF16)],
        compiler_params=_params(("parallel", "arbitrary")),
        name=name,
    )(scales, hn, w_stack)


def _out_proj_kernel(sub, n_norms, with_gate, keep_x, *refs):
    og_refs, w_ref, x_refs, refs = refs[:sub], refs[sub], refs[sub + 1:2 * sub + 1], refs[2 * sub + 1:]
    nw_refs, gate_refs, outs = _split_refs(refs, n_norms, with_gate)
    if sub == 1:
        og, x = og_refs[0][...], x_refs[0][...]
    else:
        og = jnp.concatenate([r[...] for r in og_refs], axis=0)
        x = jnp.concatenate([r[...] for r in x_refs], axis=0)
    y = x + jnp.dot(og, w_ref[...], preferred_element_type=F32)
    if keep_x:
        outs[0][...] = y
        outs = outs[1:]
    _emit_normed(y, nw_refs, gate_refs, outs[:n_norms], outs[n_norms] if with_gate else None)


def _out_proj(og, w, x, norms, gate, keep_x, normed_dtype, name, gather=None):
    if gather is None:
        tile_rows, n_tiles, sub = ROW_TILE, x.shape[0] // ROW_TILE, 1
        in_row_specs = [pl.BlockSpec((ROW_TILE, D_MODEL), lambda i: (i, 0))]
    else:
        tile_rows, n_tiles, first_block = gather
        sub = tile_rows // CHUNK
        in_row_specs = [pl.BlockSpec((CHUNK, D_MODEL),
                                     functools.partial(lambda k, i: (first_block(i) + k, 0), k))
                        for k in range(sub)]
    rows = tile_rows * n_tiles
    row_spec = pl.BlockSpec((tile_rows, D_MODEL), lambda i: (i, 0))
    norm_specs, norm_args = _norm_specs(norms, gate, lambda i: (0, 0))
    out_specs, out_shape = [], []
    if keep_x:
        out_specs.append(row_spec)
        out_shape.append(jax.ShapeDtypeStruct((rows, D_MODEL), F32))
    for _ in norms:
        out_specs.append(row_spec)
        out_shape.append(jax.ShapeDtypeStruct((rows, D_MODEL), normed_dtype))
    if gate is not None:
        out_specs.append(pl.BlockSpec((tile_rows, GLA_KEY_DIM), lambda i: (i, 0)))
        out_shape.append(jax.ShapeDtypeStruct((rows, GLA_KEY_DIM), F32))
    w_spec = pl.BlockSpec((D_MODEL, D_MODEL), lambda i: (0, 0), pipeline_mode=pl.Buffered(1))
    return pl.pallas_call(
        functools.partial(_out_proj_kernel, sub, len(norms), gate is not None, keep_x),
        grid=(n_tiles,),
        in_specs=in_row_specs + [w_spec] + in_row_specs + norm_specs,
        out_specs=out_specs,
        out_shape=out_shape,
        compiler_params=_params(("parallel",)),
        name=name,
    )(*([og] * sub), w, *([x] * sub), *norm_args)


def _exact_block(qs, k, b, n, width, col0):
    lane = lax.broadcasted_iota(jnp.int32, (n, width), 1)
    acc = jnp.zeros((n, width), F32)
    for j in range(n):
        t = qs * k[j:j + 1] * jnp.exp(jnp.minimum(b - b[j:j + 1], 0.0))
        acc = jnp.where(lane == col0 + j, jnp.sum(t, axis=-1, keepdims=True), acc)
    return acc


def _masked_row_sums(mask_bf16, g):
    g_hi = g.astype(BF16)
    r1 = g - g_hi.astype(F32)
    g_mid = r1.astype(BF16)
    g_lo = (r1 - g_mid.astype(F32)).astype(BF16)
    return (jnp.dot(mask_bf16, g_hi, preferred_element_type=F32)
            + jnp.dot(mask_bf16, g_mid, preferred_element_type=F32)
            + jnp.dot(mask_bf16, g_lo, preferred_element_type=F32))


def _column_of(row_vec, n):
    return jnp.broadcast_to(row_vec, (LANES, n)).T[:, 0:1]


def _head_norm_gate(o, onorm, gate):
    ms = jnp.mean(o * o, axis=-1, keepdims=True)
    return (o * lax.rsqrt(ms + RMS_EPS) * onorm * _silu(gate)).astype(BF16)


def _intra_chunk_scores(qs, k, k_bf, b, row_xor_col):
    c_len = qs.shape[0]
    leaf = GLA_LEAF
    leaf_row = lax.broadcasted_iota(jnp.int32, (leaf, c_len), 0)
    leaf_col = lax.broadcasted_iota(jnp.int32, (leaf, c_len), 1)
    lane_in_leaf = [(leaf_col & (leaf - 1)) == j for j in range(leaf)]

    att_rows = []
    for blk0 in range(0, c_len // leaf, LEAF_BATCH):
        lhs = []
        for blk in range(blk0, blk0 + LEAF_BATCH):
            q_blk = qs[blk * leaf:(blk + 1) * leaf]
            b_blk = b[blk * leaf:(blk + 1) * leaf]
            lhs += [q_blk * jnp.exp2(jnp.minimum(b_blk - b_blk[j:j + 1], 0.0))
                    for j in range(leaf)]
        r = lax.dot_general(jnp.concatenate(lhs, axis=0).astype(BF16), k_bf, NT_DIMS,
                            preferred_element_type=F32)
        for i, blk in enumerate(range(blk0, blk0 + LEAF_BATCH)):
            acc = jnp.zeros((leaf, c_len), F32)
            for j in range(leaf):
                r0 = (i * leaf + j) * leaf
                acc = jnp.where(lane_in_leaf[j], r[r0:r0 + leaf], acc)
            rel = leaf_col - blk * leaf
            att_rows.append(jnp.where((rel >= 0) & (rel <= leaf_row), acc, 0.0))
    att = jnp.concatenate(att_rows, axis=0)

    m = c_len // 2
    while m >= leaf:
        q_parts, k_parts = [], []
        zeros = jnp.zeros((m, GLA_DK), F32)
        for base in range(0, c_len, 2 * m):
            ref = b[base + m - 1:base + m]
            lo = slice(base, base + m)
            hi = slice(base + m, base + 2 * m)
            k_parts += [k[lo] * jnp.exp2(ref - b[lo]), zeros]
            q_parts += [zeros, qs[hi] * jnp.exp2(b[hi] - ref)]
        qt = jnp.concatenate(q_parts, axis=0).astype(BF16)
        kt = jnp.concatenate(k_parts, axis=0).astype(BF16)
        a = lax.dot_general(qt, kt, NT_DIMS, preferred_element_type=F32)
        att = att + jnp.where(row_xor_col < 2 * m, a, 0.0)
        m //= 2
    return att


def _gla_prompt_kernel(q_ref, k_ref, v_ref, gate_ref, g_ref, onorm_ref, og_in_ref,
                       og_ref, s_ref, b_scr, qd_scr, att_scr):
    del og_in_ref
    c_len = CHUNK
    heads = [(slice(h * GLA_DK, (h + 1) * GLA_DK), slice(h * GLA_DV, (h + 1) * GLA_DV))
             for h in range(GLA_HEADS)]

    @pl.when(pl.program_id(1) == 0)
    def _():
        s_ref[...] = jnp.zeros_like(s_ref)

    row = lax.broadcasted_iota(jnp.int32, (c_len, c_len), 0)
    col = lax.broadcasted_iota(jnp.int32, (c_len, c_len), 1)
    lower = row >= col
    tri = jnp.where(lower, 1.0, 0.0).astype(BF16)
    onorm = onorm_ref[...]
    chunk_rows = [slice(c * c_len, (c + 1) * c_len) for c in range(q_ref.shape[0] // c_len)]

    total = None
    for rows in chunk_rows:
        b_c = _masked_row_sums(tri, g_ref[rows, :]) * LOG2_E
        b_scr[rows, :] = b_c
        qd_scr[rows, :] = (q_ref[rows, :].astype(F32) * jnp.exp2(b_c)).astype(BF16)
        last = b_c[c_len - 1:c_len]
        total = last if total is None else jnp.minimum(total, last)

    bounded = jnp.max(-total) <= GLA_BOUNDED_RANGE

    @pl.when(bounded)
    def _():
        for c, rows in enumerate(chunk_rows):
            for h, (ks, _) in enumerate(heads):
                k_inv = (k_ref[rows, ks].astype(F32) * jnp.exp2(-b_scr[rows, ks])).astype(BF16)
                a = lax.dot_general(qd_scr[rows, ks], k_inv, NT_DIMS,
                                    preferred_element_type=F32)
                att_scr[c * GLA_HEADS + h] = jnp.where(lower, a, 0.0).astype(BF16)

    @pl.when(jnp.logical_not(bounded))
    def _():
        for c, rows in enumerate(chunk_rows):
            for h, (ks, _) in enumerate(heads):
                k_bf = k_ref[rows, ks]
                att_scr[c * GLA_HEADS + h] = _intra_chunk_scores(
                    q_ref[rows, ks].astype(F32), k_bf.astype(F32), k_bf,
                    b_scr[rows, ks], row ^ col).astype(BF16)

    for c, rows in enumerate(chunk_rows):
        for h, (ks, vs) in enumerate(heads):
            k = k_ref[rows, ks].astype(F32)
            v = v_ref[rows, vs]
            b = b_scr[rows, ks]
            b_last = b[c_len - 1:c_len]
            state = s_ref[0, h]

            o = jnp.dot(qd_scr[rows, ks], state.astype(BF16), preferred_element_type=F32)
            o = o + jnp.dot(att_scr[c * GLA_HEADS + h], v, preferred_element_type=F32)

            kd_t = (k * jnp.exp2(b_last - b)).T.astype(BF16)
            s_ref[0, h] = (state * _column_of(jnp.exp2(b_last), GLA_DK)
                           + jnp.dot(kd_t, v, preferred_element_type=F32))

            og_ref[rows, vs] = _head_norm_gate(o, onorm, gate_ref[rows, vs].astype(F32))


def _gla_prompt(proj, g, onorm, og, n_batch, chunks):
    rows = proj.shape[0]
    per_step = next(n for n in (GLA_CHUNKS_PER_STEP, 1) if chunks % n == 0)
    steps = chunks // per_step
    tile = per_step * CHUNK
    blk = lambda col: (lambda b, c: (b * steps + c, col))
    return pl.pallas_call(
        _gla_prompt_kernel,
        grid=(n_batch, steps),
        in_specs=[
            pl.BlockSpec((tile, GLA_KEY_DIM), blk(0)),
            pl.BlockSpec((tile, GLA_KEY_DIM), blk(1)),
            pl.BlockSpec((tile, GLA_VAL_DIM), blk(1)),
            pl.BlockSpec((tile, GLA_VAL_DIM), blk(2)),
            pl.BlockSpec((tile, GLA_KEY_DIM), blk(0)),
            pl.BlockSpec((1, GLA_DV), lambda b, c: (0, 0)),
            pl.BlockSpec(memory_space=pl.ANY),
        ],
        out_specs=[
            pl.BlockSpec((tile, GLA_VAL_DIM), blk(0)),
            pl.BlockSpec((1, GLA_HEADS, GLA_DK, GLA_DV), lambda b, c: (b, 0, 0, 0)),
        ],
        out_shape=[
            jax.ShapeDtypeStruct((rows, GLA_VAL_DIM), BF16),
            jax.ShapeDtypeStruct((n_batch, GLA_HEADS, GLA_DK, GLA_DV), F32),
        ],
        scratch_shapes=[pltpu.VMEM((tile, GLA_KEY_DIM), F32),
                        pltpu.VMEM((tile, GLA_KEY_DIM), BF16),
                        pltpu.VMEM((per_step * GLA_HEADS, CHUNK, CHUNK), BF16)],
        input_output_aliases={6: 0},
        compiler_params=_params(("parallel", "arbitrary")),
        name="gla_prompt",
    )(proj, proj, proj, proj, g, onorm, og)


def _gla_sample_kernel(t_len, q_ref, k_ref, v_ref, gate_ref, g_ref, onorm_ref, s0_ref,
                       og_in_ref, *rest):
    og_ref, s_ref = rest[-2], rest[-1]
    n = GLA_SEQ_GROUP * t_len
    row = lax.broadcasted_iota(jnp.int32, (n, LANES), 0)
    col = lax.broadcasted_iota(jnp.int32, (n, LANES), 1)
    causal = ((row // t_len) == (col // t_len)) & (row >= col)
    row_seq = lax.broadcasted_iota(jnp.int32, (n, 1), 0) // t_len

    qs = q_ref[...].astype(F32)
    k = k_ref[...].astype(F32)
    pad_k = jnp.zeros((LANES - n, GLA_DK), F32)
    v_pad = jnp.concatenate([v_ref[...], jnp.zeros((LANES - n, GLA_DV), BF16)], axis=0)
    b = _masked_row_sums(jnp.where(causal, 1.0, 0.0).astype(BF16),
                         jnp.concatenate([g_ref[...], pad_k], axis=0))
    att = jnp.where(causal, _exact_block(qs, k, b, n, LANES, 0), 0.0)
    o = jnp.dot(att.astype(BF16), v_pad, preferred_element_type=F32)

    q_dec = (qs * jnp.exp(b)).astype(BF16)
    for u in range(GLA_SEQ_GROUP):
        state = s0_ref[0, u, 0]
        o_u = jnp.dot(q_dec, state.astype(BF16), preferred_element_type=F32)
        o = o + jnp.where(row_seq == u, o_u, 0.0)
        b_last = b[(u + 1) * t_len - 1:(u + 1) * t_len]
        kd = jnp.where(row_seq == u, k * jnp.exp(jnp.minimum(b_last - b, 0.0)), 0.0)
        kd_t = jnp.concatenate([kd, pad_k], axis=0).T.astype(BF16)
        s_ref[0, u, 0] = (state * _column_of(jnp.exp(b_last), GLA_DK)
                          + jnp.dot(kd_t, v_pad, preferred_element_type=F32))

    og_ref[...] = _head_norm_gate(o, onorm_ref[...], gate_ref[...].astype(F32))


def _gla_sample(proj, g, onorm, state_in, og, s_out, layer, n_seq, t_len, row0):
    n = GLA_SEQ_GROUP * t_len
    rb0 = row0 // n
    n_layers = state_in.shape[0]
    key_blocks = GLA_KEY_DIM // GLA_DK
    val_blocks = GLA_VAL_DIM // GLA_DV
    st_spec = pl.BlockSpec((1, GLA_SEQ_GROUP, 1, GLA_DK, GLA_DV),
                           lambda s, h: (layer, s, h, 0, 0))
    in_specs = [
        pl.BlockSpec((n, GLA_DK), lambda s, h: (rb0 + s, h)),
        pl.BlockSpec((n, GLA_DK), lambda s, h: (rb0 + s, key_blocks + h)),
        pl.BlockSpec((n, GLA_DV), lambda s, h: (rb0 + s, val_blocks + h)),
        pl.BlockSpec((n, GLA_DV), lambda s, h: (rb0 + s, 2 * val_blocks + h)),
        pl.BlockSpec((n, GLA_DK), lambda s, h: (rb0 + s, h)),
        pl.BlockSpec((1, GLA_DV), lambda s, h: (0, 0)),
        st_spec,
        pl.BlockSpec(memory_space=pl.ANY),
    ]
    args = [proj, proj, proj, proj, g, onorm, state_in, og]
    aliases = {7: 0}
    if s_out is not None:
        in_specs.append(pl.BlockSpec(memory_space=pl.ANY))
        args.append(s_out)
        aliases[8] = 1
    return pl.pallas_call(
        functools.partial(_gla_sample_kernel, t_len),
        grid=(n_seq // GLA_SEQ_GROUP, GLA_HEADS),
        in_specs=in_specs,
        out_specs=[
            pl.BlockSpec((n, GLA_DV), lambda s, h: (rb0 + s, h)),
            st_spec,
        ],
        out_shape=[
            jax.ShapeDtypeStruct(og.shape, BF16),
            jax.ShapeDtypeStruct((n_layers, n_seq, GLA_HEADS, GLA_DK, GLA_DV), F32),
        ],
        input_output_aliases=aliases,
        compiler_params=_params(("parallel", "parallel")),
        name="gla_sample",
    )(*args)


def _attend_rows(q_ref, gate_ref, og_ref, sink_ref, kv_sets, s_scr, p_scr):
    n_rows = q_ref.shape[0]
    n_keys = s_scr.shape[2]
    group = N_Q_HEADS // N_KV_HEADS
    low_out = lax.broadcasted_iota(jnp.int32, (n_rows, LANES), 1) < HEAD_DIM
    lane0 = lax.broadcasted_iota(jnp.int32, (n_rows, LANES), 1) == 0
    key_lane = lax.broadcasted_iota(jnp.int32, (n_keys, LANES), 1)
    key_row = lax.broadcasted_iota(jnp.int32, (n_keys, LANES), 0)
    low = key_lane < HEAD_DIM
    slot0 = key_row == 0

    def head_order():
        for cpair in range(KV_WIDTH // LANES):
            for parity in range(2):
                for pr in range(group // 2):
                    yield cpair, parity, (2 * cpair + parity) * (group // 2) + pr

    def padded_copies(pair, fill):
        swapped = pltpu.roll(pair, HEAD_DIM, axis=1).astype(BF16)
        pair = pair.astype(BF16)
        fill = jnp.full((), fill, BF16)
        return ((jnp.where(low, pair, fill), jnp.where(low, fill, swapped)),
                (jnp.where(low, swapped, fill), jnp.where(low, fill, pair)))

    idx = 0
    for keys, _, _ in kv_sets:
        k_pads = None
        for cpair, parity, qc in head_order():
            if parity == 0 and qc % (group // 2) == 0:
                k_pads = padded_copies(keys[cpair], 0.0)
            q2 = q_ref[:, qc * LANES:(qc + 1) * LANES]
            for k_pad in k_pads[parity]:
                s_scr[idx] = lax.dot_general(q2, k_pad, NT_DIMS,
                                             preferred_element_type=F32)
                idx += 1

    idx = 0
    for _, _, mask in kv_sets:
        mask_first, mask_rest = mask[:, :LANES], mask[:, LANES:]
        for _, _, qc in head_order():
            for half in range(2):
                sink = sink_ref[2 * qc + half] * LOG2_E
                s = s_scr[idx]
                first = jnp.where(mask_first, s[:, :LANES],
                                  jnp.where(lane0, sink, NEG_INF))
                rest = jnp.where(mask_rest, s[:, LANES:], NEG_INF)
                m = jnp.max(jnp.maximum(first, rest), axis=-1, keepdims=True)
                p_scr[idx] = jnp.exp2(
                    jnp.concatenate([first, rest], axis=1) - m).astype(BF16)
                idx += 1

    def store(qc, o):
        cols = slice(qc * LANES, (qc + 1) * LANES)
        og_ref[:, cols] = (o * _silu(gate_ref[:, cols].astype(F32))).astype(BF16)

    idx = 0
    out_cols = [None] * (Q_WIDTH // LANES)
    for _, vals, _ in kv_sets:
        v_pads = None
        for cpair, parity, qc in head_order():
            if parity == 0 and qc % (group // 2) == 0:
                v_pads = padded_copies(jnp.where(slot0, 0.0, vals[cpair]), 1.0)
            oa = jnp.dot(p_scr[idx], v_pads[parity][0], preferred_element_type=F32)
            ob = jnp.dot(p_scr[idx + 1], v_pads[parity][1], preferred_element_type=F32)
            idx += 2
            sums = pltpu.roll(jnp.where(low_out, ob, oa), HEAD_DIM, axis=1)
            o2 = jnp.where(low_out, oa, ob) * (1.0 / sums)
            if len(kv_sets) == 1:
                store(qc, o2)
            else:
                out_cols[qc] = o2 if out_cols[qc] is None else out_cols[qc] + o2

    if len(kv_sets) > 1:
        for qc, o in enumerate(out_cols):
            store(qc, o)


def _attention_scratch(n_sets, n_rows, n_keys):
    n = n_sets * N_Q_HEADS
    return [pltpu.VMEM((n, n_rows, n_keys), F32), pltpu.VMEM((n, n_rows, n_keys), BF16)]


def _swa_prompt_kernel(front, sink_ref, q_ref, gate_ref, kc_ref, kp_ref, vc_ref, vp_ref,
                       og_in_ref, og_ref, s_scr, p_scr):
    del og_in_ref
    blk = pl.program_id(1)
    row = lax.broadcasted_iota(jnp.int32, (CHUNK, 2 * CHUNK), 0)
    col = lax.broadcasted_iota(jnp.int32, (CHUNK, 2 * CHUNK), 1)
    mask = (col > row) & (col <= row + WINDOW) & ((blk - 1) * CHUNK + col >= front)
    keys = [jnp.concatenate([kp_ref[:, c * LANES:(c + 1) * LANES],
                             kc_ref[:, c * LANES:(c + 1) * LANES]], axis=0)
            for c in range(KV_WIDTH // LANES)]
    vals = [jnp.concatenate([vp_ref[:, c * LANES:(c + 1) * LANES],
                             vc_ref[:, c * LANES:(c + 1) * LANES]], axis=0)
            for c in range(KV_WIDTH // LANES)]
    _attend_rows(q_ref, gate_ref, og_ref, sink_ref, [(keys, vals, mask)], s_scr, p_scr)


def _swa_prompt(pq, kv, sinks, og, n_batch, chunks, front):
    cur = lambda col: (lambda b, i: (b * chunks + i, col))
    prev = lambda col: (lambda b, i: (b * chunks + jnp.maximum(i - 1, 0), col))
    return pl.pallas_call(
        functools.partial(_swa_prompt_kernel, front),
        grid=(n_batch, chunks),
        in_specs=[
            pl.BlockSpec(memory_space=pltpu.SMEM),
            pl.BlockSpec((CHUNK, Q_WIDTH), cur(0)),
            pl.BlockSpec((CHUNK, Q_WIDTH), cur(1)),
            pl.BlockSpec((CHUNK, KV_WIDTH), cur(0)),
            pl.BlockSpec((CHUNK, KV_WIDTH), prev(0)),
            pl.BlockSpec((CHUNK, KV_WIDTH), cur(1)),
            pl.BlockSpec((CHUNK, KV_WIDTH), prev(1)),
            pl.BlockSpec(memory_space=pl.ANY),
        ],
        out_specs=pl.BlockSpec((CHUNK, Q_WIDTH), cur(0)),
        out_shape=jax.ShapeDtypeStruct(og.shape, BF16),
        scratch_shapes=_attention_scratch(1, CHUNK, 2 * CHUNK),
        input_output_aliases={7: 0},
        compiler_params=_params(("parallel", "parallel")),
        name="swa_prompt",
    )(sinks, pq, pq, kv, kv, kv, kv, og)


def _swa_sample_kernel(t_len, sink_ref, q_ref, gate_ref, kn_ref, vn_ref, ck_ref, cv_ref,
                       og_in_ref, og_ref, s_scr, p_scr):
    del og_in_ref
    n = SEQ_GROUP * t_len
    win = ck_ref.shape[1]
    row = lax.broadcasted_iota(jnp.int32, (n, 2 * win), 0)
    col = lax.broadcasted_iota(jnp.int32, (n, 2 * win), 1)
    t = row % t_len
    new = col - win
    in_window = ((col < win) & (col > t)) | (
        (new >= 0) & (new < n) & ((new // t_len) == (row // t_len)) & ((new % t_len) <= t))
    pad = jnp.zeros((win - n, LANES), F32)
    kv_pairs = []
    for u in range(SEQ_GROUP):
        keys = [jnp.concatenate([ck_ref[u, :, c * LANES:(c + 1) * LANES],
                                 kn_ref[:, c * LANES:(c + 1) * LANES], pad], axis=0)
                for c in range(KV_WIDTH // LANES)]
        vals = [jnp.concatenate([cv_ref[u, :, c * LANES:(c + 1) * LANES],
                                 vn_ref[:, c * LANES:(c + 1) * LANES], pad], axis=0)
                for c in range(KV_WIDTH // LANES)]
        kv_pairs.append((keys, vals, in_window & ((row // t_len) == u)))
    _attend_rows(q_ref, gate_ref, og_ref, sink_ref, kv_pairs, s_scr, p_scr)


def _swa_sample(pq, kv, cache_k, cache_v, sinks, og, n_seq, t_len, row0):
    n = SEQ_GROUP * t_len
    rb0 = row0 // n
    win = cache_k.shape[1]
    return pl.pallas_call(
        functools.partial(_swa_sample_kernel, t_len),
        grid=(n_seq // SEQ_GROUP,),
        in_specs=[
            pl.BlockSpec(memory_space=pltpu.SMEM),
            pl.BlockSpec((n, Q_WIDTH), lambda s: (rb0 + s, 0)),
            pl.BlockSpec((n, Q_WIDTH), lambda s: (rb0 + s, 1)),
            pl.BlockSpec((n, KV_WIDTH), lambda s: (rb0 + s, 0)),
            pl.BlockSpec((n, KV_WIDTH), lambda s: (rb0 + s, 1)),
            pl.BlockSpec((SEQ_GROUP, win, KV_WIDTH), lambda s: (s, 0, 0)),
            pl.BlockSpec((SEQ_GROUP, win, KV_WIDTH), lambda s: (s, 0, 0)),
            pl.BlockSpec(memory_space=pl.ANY),
        ],
        out_specs=pl.BlockSpec((n, Q_WIDTH), lambda s: (rb0 + s, 0)),
        out_shape=jax.ShapeDtypeStruct(og.shape, BF16),
        scratch_shapes=_attention_scratch(SEQ_GROUP, n, 2 * win),
        input_output_aliases={7: 0},
        compiler_params=_params(("parallel",)),
        name="swa_sample",
    )(sinks, pq, pq, kv, kv, cache_k, cache_v, og)


def kernel(x_prompt, x_sample, state_gla, cache_k_win, cache_v_win, meta_tokens, norm_a, w_in_a, w_gk2_a, b_gk2_a, onorm_a, w_out_a, norm_kv, w_kv, norm_b, w_in_b, sinks_b, w_out_b, norm_f):
    n_batch, seq, _ = x_prompt.shape
    n_seq, t_len, _ = x_sample.shape
    win = cache_k_win.shape[1]
    n_a = w_in_a.shape[0]
    n_b = w_in_b.shape[0]
    assert win == WINDOW == CHUNK and n_seq % SEQ_GROUP == 0 and n_seq % GLA_SEQ_GROUP == 0

    length = seq + N_META
    front = (-length) % CHUNK
    lp = length + front
    chunks = lp // CHUNK
    srow = n_batch * lp
    ns = n_seq * t_len
    rows = -(-(srow + ns) // ROW_TILE) * ROW_TILE

    assert front + N_META == CHUNK and ns == CHUNK and seq % CHUNK == 0
    assert rows % CHUNK == 0 and rows % (16 * IN_PROJ_ROW_STEPS) == 0

    w_in_a_t = jnp.swapaxes(w_in_a, 1, 2)

    def gate_weights(l):
        w_gate = jnp.pad(w_in_a_t[l, GLA_MAIN_WIDTH:], ((0, LANES - GATE_RANK), (0, 0)))
        w_gk = jnp.pad(w_gk2_a[l], ((0, LANES - GATE_RANK), (0, 0))).astype(BF16)
        return w_gate, w_gk, b_gk2_a[l][None]

    x, hn, g = _embed(x_prompt.reshape(n_batch * seq, D_MODEL), x_sample.reshape(ns, D_MODEL),
                      meta_tokens, norm_a[0], gate_weights(0), rows, chunks, n_batch, front)
    og = jnp.zeros((rows, D_MODEL), BF16)

    states_p = []
    states_s = None
    for l in range(n_a):
        proj = _in_proj(hn, w_in_a_t, l, GLA_MAIN_WIDTH, BF16, True, "gla_in_proj",
                        q_cols=GLA_KEY_DIM, q_scale=GLA_DK ** -0.5)
        onorm = onorm_a[l][None]
        og, s_p = _gla_prompt(proj, g, onorm, og, n_batch, chunks)
        og, states_s = _gla_sample(proj, g, onorm, state_gla, og, states_s, l,
                                   n_seq, t_len, srow)
        states_p.append(s_p)
        w_out = w_out_a[l].astype(BF16)
        if l + 1 < n_a:
            x, hn, g = _out_proj(og, w_out, x, [norm_a[l + 1]], gate_weights(l + 1), True,
                                 BF16, "gla_out_proj")
        else:
            x, hn_kv, hn = _out_proj(og, w_out, x, [norm_kv, norm_b[0]], None, True,
                                     BF16, "gla_out_proj")

    kv = _in_proj(hn_kv, w_kv[None], 0, 2 * KV_WIDTH, F32, False, "kv_proj")
    cache_k = cache_k_win.reshape(n_seq, win, KV_WIDTH)
    cache_v = cache_v_win.reshape(n_seq, win, KV_WIDTH)
    for j in range(n_b):
        pq = _in_proj(hn, w_in_b, j, 2 * Q_WIDTH, BF16, False, "swa_in_proj",
                      q_cols=Q_WIDTH, q_scale=HEAD_DIM ** -0.5 * LOG2_E)
        og = _swa_prompt(pq, kv, sinks_b[j], og, n_batch, chunks, front)
        og = _swa_sample(pq, kv, cache_k, cache_v, sinks_b[j], og, n_seq, t_len, srow)
        w_out = w_out_b[j].astype(BF16)
        if j + 1 < n_b:
            x, hn = _out_proj(og, w_out, x, [norm_b[j + 1]], None, True, BF16, "swa_out_proj")

    tiles_per_batch = seq // ROW_TILE
    lead_blocks = (lp - seq) // CHUNK

    def prompt_first_block(t):
        return ((t // tiles_per_batch) * chunks + lead_blocks
                + (t % tiles_per_batch) * (ROW_TILE // CHUNK))

    (y_prompt,) = _out_proj(og, w_out, x, [norm_f], None, False, F32, "final_prompt",
                            gather=(ROW_TILE, n_batch * tiles_per_batch, prompt_first_block))
    (y_sample,) = _out_proj(og, w_out, x, [norm_f], None, False, F32, "final_sample",
                            gather=(CHUNK, ns // CHUNK, lambda t: srow // CHUNK + t))
    y_prompt = y_prompt.reshape(n_batch, seq, D_MODEL)
    y_sample = y_sample.reshape(n_seq, t_len, D_MODEL)
    kv_p = jnp.stack([kv[(b + 1) * lp - win:(b + 1) * lp] for b in range(n_batch)])
    kv_p = kv_p.reshape(n_batch, win, 2, N_KV_HEADS, HEAD_DIM)
    kv_s = kv[srow:srow + ns].reshape(n_seq, t_len, 2, N_KV_HEADS, HEAD_DIM)
    k_win_s = jnp.concatenate([cache_k_win, kv_s[:, :, 0]], axis=1)[:, -win:]
    v_win_s = jnp.concatenate([cache_v_win, kv_s[:, :, 1]], axis=1)[:, -win:]
    return (y_prompt, y_sample, jnp.stack(states_p), states_s,
            kv_p[:, :, 0], kv_p[:, :, 1], k_win_s, v_win_s)
```

```python
import functools

import jax
import jax.numpy as jnp
from jax import lax
from jax.experimental import pallas as pl
from jax.experimental.pallas import tpu as pltpu

F32 = jnp.float32
BF16 = jnp.bfloat16

D_MODEL = 2048
N_META = 16
GLA_HEADS = 4
GLA_DK = 256
GLA_DV = 512
GLA_KEY_DIM = GLA_HEADS * GLA_DK
GLA_VAL_DIM = GLA_HEADS * GLA_DV
GLA_MAIN_WIDTH = 2 * GLA_KEY_DIM + 2 * GLA_VAL_DIM
GATE_RANK = 16
GATE_LOGIT_NORM = 16.0
HEAD_DIM = 64
N_Q_HEADS = 32
N_KV_HEADS = 8
Q_WIDTH = N_Q_HEADS * HEAD_DIM
KV_WIDTH = N_KV_HEADS * HEAD_DIM
WINDOW = 128
RMS_EPS = 1e-6
NEG_INF = -1e30
LOG2_E = 1.4426950408889634

LANES = 128
CHUNK = 128
GLA_LEAF = 8
LEAF_BATCH = 4
GLA_CHUNKS_PER_STEP = 3
SWA_BLOCKS_PER_STEP = 3
GLA_BOUNDED_RANGE = 60.0
ROW_TILE = 512
IN_PROJ_COLS = 1024
IN_PROJ_ROW_STEPS = 8
SEQ_GROUP = 4
GLA_SEQ_GROUP = 8
VMEM_LIMIT = 48 * 1024 * 1024

NT_DIMS = (((1,), (1,)), ((), ()))


def _silu(x):
    return x * (1.0 / (1.0 + jnp.exp(-x)))


def _params(sem):
    return pltpu.CompilerParams(dimension_semantics=sem, vmem_limit_bytes=VMEM_LIMIT)


def _emit_normed(y, nw_refs, gate_refs, hn_refs, g_ref):
    ms = jnp.mean(y * y, axis=-1, keepdims=True)
    yn = y * lax.rsqrt(ms + RMS_EPS)
    for nw_ref, hn_ref in zip(nw_refs, hn_refs):
        hn_ref[...] = (yn * nw_ref[...]).astype(hn_ref.dtype)
    if gate_refs:
        g_ref[...] = _decay_gate(hn_refs[0][...], *gate_refs)


def _split_refs(refs, n_norms, with_gate):
    nw_refs = refs[:n_norms]
    gate_refs = refs[n_norms:n_norms + 3] if with_gate else None
    return nw_refs, gate_refs, refs[n_norms + 3 * with_gate:]


def _norm_specs(norms, gate, index):
    specs = [pl.BlockSpec((1, D_MODEL), index) for _ in norms]
    args = [nw[None] for nw in norms]
    if gate is not None:
        specs += [pl.BlockSpec((LANES, D_MODEL), index),
                  pl.BlockSpec((LANES, GLA_KEY_DIM), index),
                  pl.BlockSpec((1, GLA_KEY_DIM), index)]
        args += list(gate)
    return specs, args


def _embed_kernel(sub, lead_blocks, sample_block, front, with_gate, *refs):
    xp_refs, (xs_ref, meta_ref), refs = refs[:sub], refs[sub:sub + 2], refs[sub + 2:]
    nw_refs, gate_refs, outs = _split_refs(refs, 1, with_gate)
    x_ref = outs[0]
    for k in range(sub):
        blk = pl.program_id(0) * sub + k
        rows = slice(k * CHUNK, (k + 1) * CHUNK)
        is_lead = functools.reduce(jnp.logical_or, [blk == b for b in lead_blocks])

        @pl.when(is_lead)
        def _():
            x_ref[rows, :] = jnp.concatenate(
                [jnp.zeros((front, D_MODEL), F32), meta_ref[...]], axis=0)

        @pl.when(jnp.logical_and(jnp.logical_not(is_lead), blk < sample_block))
        def _():
            x_ref[rows, :] = xp_refs[k][...]

        @pl.when(blk == sample_block)
        def _():
            x_ref[rows, :] = xs_ref[...]

        @pl.when(blk > sample_block)
        def _():
            x_ref[rows, :] = jnp.zeros((CHUNK, D_MODEL), F32)

    _emit_normed(x_ref[...], nw_refs, gate_refs, outs[1:2], outs[2] if with_gate else None)


def _embed(x_prompt2d, x_sample2d, meta, norm, gate, rows, chunks, n_batch, front):
    sub = ROW_TILE // CHUNK
    prompt_blocks = x_prompt2d.shape[0] // CHUNK
    per_batch = prompt_blocks // n_batch

    def prompt_index(k):
        def index(i):
            blk = i * sub + k
            src = (blk // chunks) * per_batch + (blk % chunks) - 1
            return (jnp.clip(src, 0, prompt_blocks - 1), 0)
        return index

    norm_specs, norm_args = _norm_specs([norm], gate, lambda i: (0, 0))
    row_spec = pl.BlockSpec((ROW_TILE, D_MODEL), lambda i: (i, 0))
    out_specs = [row_spec, row_spec]
    out_shape = [jax.ShapeDtypeStruct((rows, D_MODEL), F32),
                 jax.ShapeDtypeStruct((rows, D_MODEL), BF16)]
    if gate is not None:
        out_specs.append(pl.BlockSpec((ROW_TILE, GLA_KEY_DIM), lambda i: (i, 0)))
        out_shape.append(jax.ShapeDtypeStruct((rows, GLA_KEY_DIM), F32))
    return pl.pallas_call(
        functools.partial(_embed_kernel, sub, tuple(b * chunks for b in range(n_batch)),
                          n_batch * chunks, front, gate is not None),
        grid=(rows // ROW_TILE,),
        in_specs=[pl.BlockSpec((CHUNK, D_MODEL), prompt_index(k)) for k in range(sub)] + [
            pl.BlockSpec((CHUNK, D_MODEL), lambda i: (0, 0)),
            pl.BlockSpec((N_META, D_MODEL), lambda i: (0, 0)),
        ] + norm_specs,
        out_specs=out_specs,
        out_shape=out_shape,
        compiler_params=_params(("parallel",)),
        name="embed",
    )(*([x_prompt2d] * sub), x_sample2d, meta, *norm_args)


def _decay_gate(hn, wg_ref, wgk_ref, bgk_ref):
    glr = lax.dot_general(hn, wg_ref[...].astype(BF16), NT_DIMS, preferred_element_type=F32)
    z = jnp.dot(glr.astype(BF16), wgk_ref[...], preferred_element_type=F32) + bgk_ref[...]
    log_sig = jnp.minimum(z, 0.0) - jnp.log(1.0 + jnp.exp(-jnp.abs(z)))
    return log_sig * (1.0 / GATE_LOGIT_NORM)


def _in_proj_kernel(transposed, scale_ref, hn_ref, w_ref, o_ref, w_bf16_ref):
    @pl.when(pl.program_id(1) == 0)
    def _():
        w_bf16_ref[...] = w_ref[...].astype(BF16)

    dims = NT_DIMS if transposed else (((1,), (0,)), ((), ()))
    o = lax.dot_general(hn_ref[...], w_bf16_ref[...], dims, preferred_element_type=F32)
    o_ref[...] = (o * scale_ref[pl.program_id(0)]).astype(o_ref.dtype)


def _in_proj(hn, w_stack, layer, n_cols, out_dtype, transposed, name, q_cols=0, q_scale=1.0):
    rows = hn.shape[0]
    row_tile = rows // IN_PROJ_ROW_STEPS
    n_slabs = n_cols // IN_PROJ_COLS
    assert q_cols % IN_PROJ_COLS == 0
    scales = jnp.where(jnp.arange(n_slabs) < q_cols // IN_PROJ_COLS, q_scale, 1.0).astype(F32)
    if transposed:
        slab = (IN_PROJ_COLS, D_MODEL)
        w_spec = pl.BlockSpec((None,) + slab, lambda j, i: (layer, j, 0))
    else:
        slab = (D_MODEL, IN_PROJ_COLS)
        w_spec = pl.BlockSpec((None,) + slab, lambda j, i: (layer, 0, j))
    return pl.pallas_call(
        functools.partial(_in_proj_kernel, transposed),
        grid=(n_slabs, IN_PROJ_ROW_STEPS),
        in_specs=[pl.BlockSpec(memory_space=pltpu.SMEM),
                  pl.BlockSpec((row_tile, D_MODEL), lambda j, i: (i, 0)), w_spec],
        out_specs=pl.BlockSpec((row_tile, IN_PROJ_COLS), lambda j, i: (i, j)),
        out_shape=jax.ShapeDtypeStruct((rows, n_cols), out_dtype),
        scratch_shapes=[pltpu.VMEM(slab, BF16)],
        compiler_params=_params(("parallel", "arbitrary")),
        name=name,
    )(scales, hn, w_stack)


def _out_proj_kernel(sub, n_norms, with_gate, keep_x, *refs):
    og_refs, w_ref, x_refs, refs = refs[:sub], refs[sub], refs[sub + 1:2 * sub + 1], refs[2 * sub + 1:]
    nw_refs, gate_refs, outs = _split_refs(refs, n_norms, with_gate)
    if sub == 1:
        og, x = og_refs[0][...], x_refs[0][...]
    else:
        og = jnp.concatenate([r[...] for r in og_refs], axis=0)
        x = jnp.concatenate([r[...] for r in x_refs], axis=0)
    y = x + jnp.dot(og, w_ref[...], preferred_element_type=F32)
    if keep_x:
        outs[0][...] = y
        outs = outs[1:]
    _emit_normed(y, nw_refs, gate_refs, outs[:n_norms], outs[n_norms] if with_gate else None)


def _out_proj(og, w, x, norms, gate, keep_x, normed_dtype, name, gather=None):
    if gather is None:
        tile_rows, n_tiles, sub = ROW_TILE, x.shape[0] // ROW_TILE, 1
        in_row_specs = [pl.BlockSpec((ROW_TILE, D_MODEL), lambda i: (i, 0))]
    else:
        tile_rows, n_tiles, first_block = gather
        sub = tile_rows // CHUNK
        in_row_specs = [pl.BlockSpec((CHUNK, D_MODEL),
                                     functools.partial(lambda k, i: (first_block(i) + k, 0), k))
                        for k in range(sub)]
    rows = tile_rows * n_tiles
    row_spec = pl.BlockSpec((tile_rows, D_MODEL), lambda i: (i, 0))
    norm_specs, norm_args = _norm_specs(norms, gate, lambda i: (0, 0))
    out_specs, out_shape = [], []
    if keep_x:
        out_specs.append(row_spec)
        out_shape.append(jax.ShapeDtypeStruct((rows, D_MODEL), F32))
    for _ in norms:
        out_specs.append(row_spec)
        out_shape.append(jax.ShapeDtypeStruct((rows, D_MODEL), normed_dtype))
    if gate is not None:
        out_specs.append(pl.BlockSpec((tile_rows, GLA_KEY_DIM), lambda i: (i, 0)))
        out_shape.append(jax.ShapeDtypeStruct((rows, GLA_KEY_DIM), F32))
    w_spec = pl.BlockSpec((D_MODEL, D_MODEL), lambda i: (0, 0), pipeline_mode=pl.Buffered(1))
    return pl.pallas_call(
        functools.partial(_out_proj_kernel, sub, len(norms), gate is not None, keep_x),
        grid=(n_tiles,),
        in_specs=in_row_specs + [w_spec] + in_row_specs + norm_specs,
        out_specs=out_specs,
        out_shape=out_shape,
        compiler_params=_params(("parallel",)),
        name=name,
    )(*([og] * sub), w, *([x] * sub), *norm_args)


def _exact_block(qs, k, b, n, width, col0):
    lane = lax.broadcasted_iota(jnp.int32, (n, width), 1)
    acc = jnp.zeros((n, width), F32)
    for j in range(n):
        t = qs * k[j:j + 1] * jnp.exp(jnp.minimum(b - b[j:j + 1], 0.0))
        acc = jnp.where(lane == col0 + j, jnp.sum(t, axis=-1, keepdims=True), acc)
    return acc


def _masked_row_sums(mask_bf16, g):
    g_hi = g.astype(BF16)
    r1 = g - g_hi.astype(F32)
    g_mid = r1.astype(BF16)
    g_lo = (r1 - g_mid.astype(F32)).astype(BF16)
    return (jnp.dot(mask_bf16, g_hi, preferred_element_type=F32)
            + jnp.dot(mask_bf16, g_mid, preferred_element_type=F32)
            + jnp.dot(mask_bf16, g_lo, preferred_element_type=F32))


def _column_of(row_vec, n):
    return jnp.broadcast_to(row_vec, (LANES, n)).T[:, 0:1]


def _head_norm_gate(o, onorm, gate):
    ms = jnp.mean(o * o, axis=-1, keepdims=True)
    return (o * lax.rsqrt(ms + RMS_EPS) * onorm * _silu(gate)).astype(BF16)


def _intra_chunk_scores(qs, k, k_bf, b, row_xor_col):
    c_len = qs.shape[0]
    leaf = GLA_LEAF
    leaf_row = lax.broadcasted_iota(jnp.int32, (leaf, c_len), 0)
    leaf_col = lax.broadcasted_iota(jnp.int32, (leaf, c_len), 1)
    lane_in_leaf = [(leaf_col & (leaf - 1)) == j for j in range(leaf)]

    att_rows = []
    for blk0 in range(0, c_len // leaf, LEAF_BATCH):
        lhs = []
        for blk in range(blk0, blk0 + LEAF_BATCH):
            q_blk = qs[blk * leaf:(blk + 1) * leaf]
            b_blk = b[blk * leaf:(blk + 1) * leaf]
            lhs += [q_blk * jnp.exp2(jnp.minimum(b_blk - b_blk[j:j + 1], 0.0))
                    for j in range(leaf)]
        r = lax.dot_general(jnp.concatenate(lhs, axis=0).astype(BF16), k_bf, NT_DIMS,
                            preferred_element_type=F32)
        for i, blk in enumerate(range(blk0, blk0 + LEAF_BATCH)):
            acc = jnp.zeros((leaf, c_len), F32)
            for j in range(leaf):
                r0 = (i * leaf + j) * leaf
                acc = jnp.where(lane_in_leaf[j], r[r0:r0 + leaf], acc)
            rel = leaf_col - blk * leaf
            att_rows.append(jnp.where((rel >= 0) & (rel <= leaf_row), acc, 0.0))
    att = jnp.concatenate(att_rows, axis=0)

    m = c_len // 2
    while m >= leaf:
        q_parts, k_parts = [], []
        zeros = jnp.zeros((m, GLA_DK), F32)
        for base in range(0, c_len, 2 * m):
            ref = b[base + m - 1:base + m]
            lo = slice(base, base + m)
            hi = slice(base + m, base + 2 * m)
            k_parts += [k[lo] * jnp.exp2(ref - b[lo]), zeros]
            q_parts += [zeros, qs[hi] * jnp.exp2(b[hi] - ref)]
        qt = jnp.concatenate(q_parts, axis=0).astype(BF16)
        kt = jnp.concatenate(k_parts, axis=0).astype(BF16)
        a = lax.dot_general(qt, kt, NT_DIMS, preferred_element_type=F32)
        att = att + jnp.where(row_xor_col < 2 * m, a, 0.0)
        m //= 2
    return att


def _gla_prompt_kernel(q_ref, k_ref, v_ref, gate_ref, g_ref, onorm_ref, og_in_ref,
                       og_ref, s_ref, b_scr, qd_scr, att_scr):
    del og_in_ref
    c_len = CHUNK
    heads = [(slice(h * GLA_DK, (h + 1) * GLA_DK), slice(h * GLA_DV, (h + 1) * GLA_DV))
             for h in range(GLA_HEADS)]

    @pl.when(pl.program_id(1) == 0)
    def _():
        s_ref[...] = jnp.zeros_like(s_ref)

    row = lax.broadcasted_iota(jnp.int32, (c_len, c_len), 0)
    col = lax.broadcasted_iota(jnp.int32, (c_len, c_len), 1)
    lower = row >= col
    tri = jnp.where(lower, 1.0, 0.0).astype(BF16)
    onorm = onorm_ref[...]
    chunk_rows = [slice(c * c_len, (c + 1) * c_len) for c in range(q_ref.shape[0] // c_len)]

    total = None
    for rows in chunk_rows:
        b_c = _masked_row_sums(tri, g_ref[rows, :]) * LOG2_E
        b_scr[rows, :] = b_c
        qd_scr[rows, :] = (q_ref[rows, :].astype(F32) * jnp.exp2(b_c)).astype(BF16)
        last = b_c[c_len - 1:c_len]
        total = last if total is None else jnp.minimum(total, last)

    bounded = jnp.max(-total) <= GLA_BOUNDED_RANGE

    @pl.when(bounded)
    def _():
        for c, rows in enumerate(chunk_rows):
            for h, (ks, _) in enumerate(heads):
                k_inv = (k_ref[rows, ks].astype(F32) * jnp.exp2(-b_scr[rows, ks])).astype(BF16)
                a = lax.dot_general(qd_scr[rows, ks], k_inv, NT_DIMS,
                                    preferred_element_type=F32)
                att_scr[c * GLA_HEADS + h] = jnp.where(lower, a, 0.0).astype(BF16)

    @pl.when(jnp.logical_not(bounded))
    def _():
        for c, rows in enumerate(chunk_rows):
            for h, (ks, _) in enumerate(heads):
                k_bf = k_ref[rows, ks]
                att_scr[c * GLA_HEADS + h] = _intra_chunk_scores(
                    q_ref[rows, ks].astype(F32), k_bf.astype(F32), k_bf,
                    b_scr[rows, ks], row ^ col).astype(BF16)

    for c, rows in enumerate(chunk_rows):
        for h, (ks, vs) in enumerate(heads):
            k = k_ref[rows, ks].astype(F32)
            v = v_ref[rows, vs]
            b = b_scr[rows, ks]
            b_last = b[c_len - 1:c_len]
            state = s_ref[0, h]

            o = jnp.dot(qd_scr[rows, ks], state.astype(BF16), preferred_element_type=F32)
            o = o + jnp.dot(att_scr[c * GLA_HEADS + h], v, preferred_element_type=F32)

            kd_t = (k * jnp.exp2(b_last - b)).T.astype(BF16)
            s_ref[0, h] = (state * _column_of(jnp.exp2(b_last), GLA_DK)
                           + jnp.dot(kd_t, v, preferred_element_type=F32))

            og_ref[rows, vs] = _head_norm_gate(o, onorm, gate_ref[rows, vs].astype(F32))


def _gla_prompt(proj, g, onorm, og, n_batch, chunks):
    rows = proj.shape[0]
    per_step = next(n for n in (GLA_CHUNKS_PER_STEP, 1) if chunks % n == 0)
    steps = chunks // per_step
    tile = per_step * CHUNK
    blk = lambda col: (lambda b, c: (b * steps + c, col))
    return pl.pallas_call(
        _gla_prompt_kernel,
        grid=(n_batch, steps),
        in_specs=[
            pl.BlockSpec((tile, GLA_KEY_DIM), blk(0)),
            pl.BlockSpec((tile, GLA_KEY_DIM), blk(1)),
            pl.BlockSpec((tile, GLA_VAL_DIM), blk(1)),
            pl.BlockSpec((tile, GLA_VAL_DIM), blk(2)),
            pl.BlockSpec((tile, GLA_KEY_DIM), blk(0)),
            pl.BlockSpec((1, GLA_DV), lambda b, c: (0, 0)),
            pl.BlockSpec(memory_space=pl.ANY),
        ],
        out_specs=[
            pl.BlockSpec((tile, GLA_VAL_DIM), blk(0)),
            pl.BlockSpec((1, GLA_HEADS, GLA_DK, GLA_DV), lambda b, c: (b, 0, 0, 0)),
        ],
        out_shape=[
            jax.ShapeDtypeStruct((rows, GLA_VAL_DIM), BF16),
            jax.ShapeDtypeStruct((n_batch, GLA_HEADS, GLA_DK, GLA_DV), F32),
        ],
        scratch_shapes=[pltpu.VMEM((tile, GLA_KEY_DIM), F32),
                        pltpu.VMEM((tile, GLA_KEY_DIM), BF16),
                        pltpu.VMEM((per_step * GLA_HEADS, CHUNK, CHUNK), BF16)],
        input_output_aliases={6: 0},
        compiler_params=_params(("parallel", "arbitrary")),
        name="gla_prompt",
    )(proj, proj, proj, proj, g, onorm, og)


def _gla_sample_kernel(t_len, q_ref, k_ref, v_ref, gate_ref, g_ref, onorm_ref, s0_ref,
                       og_in_ref, *rest):
    og_ref, s_ref = rest[-2], rest[-1]
    n = GLA_SEQ_GROUP * t_len
    row = lax.broadcasted_iota(jnp.int32, (n, LANES), 0)
    col = lax.broadcasted_iota(jnp.int32, (n, LANES), 1)
    causal = ((row // t_len) == (col // t_len)) & (row >= col)
    row_seq = lax.broadcasted_iota(jnp.int32, (n, 1), 0) // t_len

    qs = q_ref[...].astype(F32)
    k = k_ref[...].astype(F32)
    pad_k = jnp.zeros((LANES - n, GLA_DK), F32)
    v_pad = jnp.concatenate([v_ref[...], jnp.zeros((LANES - n, GLA_DV), BF16)], axis=0)
    b = _masked_row_sums(jnp.where(causal, 1.0, 0.0).astype(BF16),
                         jnp.concatenate([g_ref[...], pad_k], axis=0))
    att = jnp.where(causal, _exact_block(qs, k, b, n, LANES, 0), 0.0)
    o = jnp.dot(att.astype(BF16), v_pad, preferred_element_type=F32)

    q_dec = (qs * jnp.exp(b)).astype(BF16)
    for u in range(GLA_SEQ_GROUP):
        state = s0_ref[0, u, 0]
        o_u = jnp.dot(q_dec, state.astype(BF16), preferred_element_type=F32)
        o = o + jnp.where(row_seq == u, o_u, 0.0)
        b_last = b[(u + 1) * t_len - 1:(u + 1) * t_len]
        kd = jnp.where(row_seq == u, k * jnp.exp(jnp.minimum(b_last - b, 0.0)), 0.0)
        kd_t = jnp.concatenate([kd, pad_k], axis=0).T.astype(BF16)
        s_ref[0, u, 0] = (state * _column_of(jnp.exp(b_last), GLA_DK)
                          + jnp.dot(kd_t, v_pad, preferred_element_type=F32))

    og_ref[...] = _head_norm_gate(o, onorm_ref[...], gate_ref[...].astype(F32))


def _gla_sample(proj, g, onorm, state_in, og, s_out, layer, n_seq, t_len, row0):
    n = GLA_SEQ_GROUP * t_len
    rb0 = row0 // n
    n_layers = state_in.shape[0]
    key_blocks = GLA_KEY_DIM // GLA_DK
    val_blocks = GLA_VAL_DIM // GLA_DV
    st_spec = pl.BlockSpec((1, GLA_SEQ_GROUP, 1, GLA_DK, GLA_DV),
                           lambda s, h: (layer, s, h, 0, 0))
    in_specs = [
        pl.BlockSpec((n, GLA_DK), lambda s, h: (rb0 + s, h)),
        pl.BlockSpec((n, GLA_DK), lambda s, h: (rb0 + s, key_blocks + h)),
        pl.BlockSpec((n, GLA_DV), lambda s, h: (rb0 + s, val_blocks + h)),
        pl.BlockSpec((n, GLA_DV), lambda s, h: (rb0 + s, 2 * val_blocks + h)),
        pl.BlockSpec((n, GLA_DK), lambda s, h: (rb0 + s, h)),
        pl.BlockSpec((1, GLA_DV), lambda s, h: (0, 0)),
        st_spec,
        pl.BlockSpec(memory_space=pl.ANY),
    ]
    args = [proj, proj, proj, proj, g, onorm, state_in, og]
    aliases = {7: 0}
    if s_out is not None:
        in_specs.append(pl.BlockSpec(memory_space=pl.ANY))
        args.append(s_out)
        aliases[8] = 1
    return pl.pallas_call(
        functools.partial(_gla_sample_kernel, t_len),
        grid=(n_seq // GLA_SEQ_GROUP, GLA_HEADS),
        in_specs=in_specs,
        out_specs=[
            pl.BlockSpec((n, GLA_DV), lambda s, h: (rb0 + s, h)),
            st_spec,
        ],
        out_shape=[
            jax.ShapeDtypeStruct(og.shape, BF16),
            jax.ShapeDtypeStruct((n_layers, n_seq, GLA_HEADS, GLA_DK, GLA_DV), F32),
        ],
        input_output_aliases=aliases,
        compiler_params=_params(("parallel", "parallel")),
        name="gla_sample",
    )(*args)


def _attend_rows(q_ref, gate_ref, og_ref, sink_ref, kv_sets, s_scr, p_scr):
    n_rows = q_ref.shape[0]
    n_keys = s_scr.shape[2]
    group = N_Q_HEADS // N_KV_HEADS
    low_out = lax.broadcasted_iota(jnp.int32, (n_rows, LANES), 1) < HEAD_DIM
    lane0 = lax.broadcasted_iota(jnp.int32, (n_rows, LANES), 1) == 0
    key_lane = lax.broadcasted_iota(jnp.int32, (n_keys, LANES), 1)
    key_row = lax.broadcasted_iota(jnp.int32, (n_keys, LANES), 0)
    low = key_lane < HEAD_DIM
    slot0 = key_row == 0

    def kv_heads():
        for cpair in range(KV_WIDTH // LANES):
            for parity in range(2):
                first = (2 * cpair + parity) * (group // 2)
                yield cpair, parity, list(range(first, first + group // 2))

    def slot(set_index, qc, half):
        return (set_index * (Q_WIDTH // LANES) + qc) * 2 + half

    def padded_copies(pair, fill):
        swapped = pltpu.roll(pair, HEAD_DIM, axis=1).astype(BF16)
        pair = pair.astype(BF16)
        fill = jnp.full((), fill, BF16)
        return ((jnp.where(low, pair, fill), jnp.where(low, fill, swapped)),
                (jnp.where(low, swapped, fill), jnp.where(low, fill, pair)))

    for si, (keys, _, _) in enumerate(kv_sets):
        k_pads = None
        for cpair, parity, qcs in kv_heads():
            if parity == 0:
                k_pads = padded_copies(keys[cpair], 0.0)
            q_stack = jnp.concatenate(
                [q_ref[:, qc * LANES:(qc + 1) * LANES] for qc in qcs], axis=0)
            for half, k_pad in enumerate(k_pads[parity]):
                s = lax.dot_general(q_stack, k_pad, NT_DIMS, preferred_element_type=F32)
                for j, qc in enumerate(qcs):
                    s_scr[slot(si, qc, half)] = s[j * n_rows:(j + 1) * n_rows]

    for si, (_, _, mask) in enumerate(kv_sets):
        mask_first, mask_rest = mask[:, :LANES], mask[:, LANES:]
        for qc in range(Q_WIDTH // LANES):
            for half in range(2):
                sink = sink_ref[2 * qc + half] * LOG2_E
                s = s_scr[slot(si, qc, half)]
                first = jnp.where(mask_first, s[:, :LANES],
                                  jnp.where(lane0, sink, NEG_INF))
                rest = jnp.where(mask_rest, s[:, LANES:], NEG_INF)
                m = jnp.max(jnp.maximum(first, rest), axis=-1, keepdims=True)
                p_scr[slot(si, qc, half)] = jnp.exp2(
                    jnp.concatenate([first, rest], axis=1) - m).astype(BF16)

    def store(qc, o):
        cols = slice(qc * LANES, (qc + 1) * LANES)
        og_ref[:, cols] = (o * _silu(gate_ref[:, cols].astype(F32))).astype(BF16)

    out_cols = [None] * (Q_WIDTH // LANES)
    for si, (_, vals, _) in enumerate(kv_sets):
        v_pads = None
        for cpair, parity, qcs in kv_heads():
            if parity == 0:
                v_pads = padded_copies(jnp.where(slot0, 0.0, vals[cpair]), 1.0)
            stacked = [jnp.dot(jnp.concatenate([p_scr[slot(si, qc, half)] for qc in qcs],
                                               axis=0),
                               v_pads[parity][half], preferred_element_type=F32)
                       for half in range(2)]
            for j, qc in enumerate(qcs):
                oa, ob = (o[j * n_rows:(j + 1) * n_rows] for o in stacked)
                sums = pltpu.roll(jnp.where(low_out, ob, oa), HEAD_DIM, axis=1)
                o2 = jnp.where(low_out, oa, ob) * (1.0 / sums)
                if len(kv_sets) == 1:
                    store(qc, o2)
                else:
                    out_cols[qc] = o2 if out_cols[qc] is None else out_cols[qc] + o2

    if len(kv_sets) > 1:
        for qc, o in enumerate(out_cols):
            store(qc, o)


def _attention_scratch(n_sets, n_rows, n_keys):
    n = n_sets * N_Q_HEADS
    return [pltpu.VMEM((n, n_rows, n_keys), F32), pltpu.VMEM((n, n_rows, n_keys), BF16)]


def _swa_prompt_kernel(front, sink_ref, q_ref, gate_ref, kc_ref, kp_ref, vc_ref, vp_ref,
                       og_in_ref, og_ref, s_scr, p_scr):
    del og_in_ref
    row = lax.broadcasted_iota(jnp.int32, (CHUNK, 2 * CHUNK), 0)
    col = lax.broadcasted_iota(jnp.int32, (CHUNK, 2 * CHUNK), 1)
    band = (col > row) & (col <= row + WINDOW)
    n_sub = q_ref.shape[0] // CHUNK
    lane_pairs = [slice(c * LANES, (c + 1) * LANES) for c in range(KV_WIDTH // LANES)]
    for r in range(n_sub):
        blk = pl.program_id(1) * n_sub + r
        rows = slice(r * CHUNK, (r + 1) * CHUNK)
        before = slice((r - 1) * CHUNK, r * CHUNK)
        mask = band & ((blk - 1) * CHUNK + col >= front)
        keys = [jnp.concatenate([kp_ref[:, c] if r == 0 else kc_ref[before, c],
                                 kc_ref[rows, c]], axis=0) for c in lane_pairs]
        vals = [jnp.concatenate([vp_ref[:, c] if r == 0 else vc_ref[before, c],
                                 vc_ref[rows, c]], axis=0) for c in lane_pairs]
        _attend_rows(q_ref.at[rows, :], gate_ref.at[rows, :], og_ref.at[rows, :], sink_ref,
                     [(keys, vals, mask)], s_scr, p_scr)


def _swa_prompt(pq, kv, sinks, og, n_batch, chunks, front):
    per_step = next(n for n in (SWA_BLOCKS_PER_STEP, 1) if chunks % n == 0)
    steps = chunks // per_step
    tile = per_step * CHUNK
    cur = lambda col: (lambda b, i: (b * steps + i, col))
    prev = lambda col: (lambda b, i: (b * chunks + jnp.maximum(i * per_step - 1, 0), col))
    return pl.pallas_call(
        functools.partial(_swa_prompt_kernel, front),
        grid=(n_batch, steps),
        in_specs=[
            pl.BlockSpec(memory_space=pltpu.SMEM),
            pl.BlockSpec((tile, Q_WIDTH), cur(0)),
            pl.BlockSpec((tile, Q_WIDTH), cur(1)),
            pl.BlockSpec((tile, KV_WIDTH), cur(0)),
            pl.BlockSpec((CHUNK, KV_WIDTH), prev(0)),
            pl.BlockSpec((tile, KV_WIDTH), cur(1)),
            pl.BlockSpec((CHUNK, KV_WIDTH), prev(1)),
            pl.BlockSpec(memory_space=pl.ANY),
        ],
        out_specs=pl.BlockSpec((tile, Q_WIDTH), cur(0)),
        out_shape=jax.ShapeDtypeStruct(og.shape, BF16),
        scratch_shapes=_attention_scratch(1, CHUNK, 2 * CHUNK),
        input_output_aliases={7: 0},
        compiler_params=_params(("parallel", "parallel")),
        name="swa_prompt",
    )(sinks, pq, pq, kv, kv, kv, kv, og)


def _swa_sample_kernel(t_len, sink_ref, q_ref, gate_ref, kn_ref, vn_ref, ck_ref, cv_ref,
                       og_in_ref, og_ref, s_scr, p_scr):
    del og_in_ref
    n = SEQ_GROUP * t_len
    win = ck_ref.shape[1]
    row = lax.broadcasted_iota(jnp.int32, (n, 2 * win), 0)
    col = lax.broadcasted_iota(jnp.int32, (n, 2 * win), 1)
    t = row % t_len
    new = col - win
    in_window = ((col < win) & (col > t)) | (
        (new >= 0) & (new < n) & ((new // t_len) == (row // t_len)) & ((new % t_len) <= t))
    pad = jnp.zeros((win - n, LANES), F32)
    kv_pairs = []
    for u in range(SEQ_GROUP):
        keys = [jnp.concatenate([ck_ref[u, :, c * LANES:(c + 1) * LANES],
                                 kn_ref[:, c * LANES:(c + 1) * LANES], pad], axis=0)
                for c in range(KV_WIDTH // LANES)]
        vals = [jnp.concatenate([cv_ref[u, :, c * LANES:(c + 1) * LANES],
                                 vn_ref[:, c * LANES:(c + 1) * LANES], pad], axis=0)
                for c in range(KV_WIDTH // LANES)]
        kv_pairs.append((keys, vals, in_window & ((row // t_len) == u)))
    _attend_rows(q_ref, gate_ref, og_ref, sink_ref, kv_pairs, s_scr, p_scr)


def _swa_sample(pq, kv, cache_k, cache_v, sinks, og, n_seq, t_len, row0):
    n = SEQ_GROUP * t_len
    rb0 = row0 // n
    win = cache_k.shape[1]
    return pl.pallas_call(
        functools.partial(_swa_sample_kernel, t_len),
        grid=(n_seq // SEQ_GROUP,),
        in_specs=[
            pl.BlockSpec(memory_space=pltpu.SMEM),
            pl.BlockSpec((n, Q_WIDTH), lambda s: (rb0 + s, 0)),
            pl.BlockSpec((n, Q_WIDTH), lambda s: (rb0 + s, 1)),
            pl.BlockSpec((n, KV_WIDTH), lambda s: (rb0 + s, 0)),
            pl.BlockSpec((n, KV_WIDTH), lambda s: (rb0 + s, 1)),
            pl.BlockSpec((SEQ_GROUP, win, KV_WIDTH), lambda s: (s, 0, 0)),
            pl.BlockSpec((SEQ_GROUP, win, KV_WIDTH), lambda s: (s, 0, 0)),
            pl.BlockSpec(memory_space=pl.ANY),
        ],
        out_specs=pl.BlockSpec((n, Q_WIDTH), lambda s: (rb0 + s, 0)),
        out_shape=jax.ShapeDtypeStruct(og.shape, BF16),
        scratch_shapes=_attention_scratch(SEQ_GROUP, n, 2 * win),
        input_output_aliases={7: 0},
        compiler_params=_params(("parallel",)),
        name="swa_sample",
    )(sinks, pq, pq, kv, kv, cache_k, cache_v, og)


def kernel(x_prompt, x_sample, state_gla, cache_k_win, cache_v_win, meta_tokens, norm_a, w_in_a, w_gk2_a, b_gk2_a, onorm_a, w_out_a, norm_kv, w_kv, norm_b, w_in_b, sinks_b, w_out_b, norm_f):
    n_batch, seq, _ = x_prompt.shape
    n_seq, t_len, _ = x_sample.shape
    win = cache_k_win.shape[1]
    n_a = w_in_a.shape[0]
    n_b = w_in_b.shape[0]
    assert win == WINDOW == CHUNK and n_seq % SEQ_GROUP == 0 and n_seq % GLA_SEQ_GROUP == 0

    length = seq + N_META
    front = (-length) % CHUNK
    lp = length + front
    chunks = lp // CHUNK
    srow = n_batch * lp
    ns = n_seq * t_len
    rows = -(-(srow + ns) // ROW_TILE) * ROW_TILE

    assert front + N_META == CHUNK and ns == CHUNK and seq % CHUNK == 0
    assert rows % CHUNK == 0 and rows % (16 * IN_PROJ_ROW_STEPS) == 0

    w_in_a_t = jnp.swapaxes(w_in_a, 1, 2)

    def gate_weights(l):
        w_gate = jnp.pad(w_in_a_t[l, GLA_MAIN_WIDTH:], ((0, LANES - GATE_RANK), (0, 0)))
        w_gk = jnp.pad(w_gk2_a[l], ((0, LANES - GATE_RANK), (0, 0))).astype(BF16)
        return w_gate, w_gk, b_gk2_a[l][None]

    x, hn, g = _embed(x_prompt.reshape(n_batch * seq, D_MODEL), x_sample.reshape(ns, D_MODEL),
                      meta_tokens, norm_a[0], gate_weights(0), rows, chunks, n_batch, front)
    og = jnp.zeros((rows, D_MODEL), BF16)

    states_p = []
    states_s = None
    for l in range(n_a):
        proj = _in_proj(hn, w_in_a_t, l, GLA_MAIN_WIDTH, BF16, True, "gla_in_proj",
                        q_cols=GLA_KEY_DIM, q_scale=GLA_DK ** -0.5)
        onorm = onorm_a[l][None]
        og, s_p = _gla_prompt(proj, g, onorm, og, n_batch, chunks)
        og, states_s = _gla_sample(proj, g, onorm, state_gla, og, states_s, l,
                                   n_seq, t_len, srow)
        states_p.append(s_p)
        w_out = w_out_a[l].astype(BF16)
        if l + 1 < n_a:
            x, hn, g = _out_proj(og, w_out, x, [norm_a[l + 1]], gate_weights(l + 1), True,
                                 BF16, "gla_out_proj")
        else:
            x, hn_kv, hn = _out_proj(og, w_out, x, [norm_kv, norm_b[0]], None, True,
                                     BF16, "gla_out_proj")

    kv = _in_proj(hn_kv, w_kv[None], 0, 2 * KV_WIDTH, F32, False, "kv_proj")
    cache_k = cache_k_win.reshape(n_seq, win, KV_WIDTH)
    cache_v = cache_v_win.reshape(n_seq, win, KV_WIDTH)
    for j in range(n_b):
        pq = _in_proj(hn, w_in_b, j, 2 * Q_WIDTH, BF16, False, "swa_in_proj",
                      q_cols=Q_WIDTH, q_scale=HEAD_DIM ** -0.5 * LOG2_E)
        og = _swa_prompt(pq, kv, sinks_b[j], og, n_batch, chunks, front)
        og = _swa_sample(pq, kv, cache_k, cache_v, sinks_b[j], og, n_seq, t_len, srow)
        w_out = w_out_b[j].astype(BF16)
        if j + 1 < n_b:
            x, hn = _out_proj(og, w_out, x, [norm_b[j + 1]], None, True, BF16, "swa_out_proj")

    tiles_per_batch = seq // ROW_TILE
    lead_blocks = (lp - seq) // CHUNK

    def prompt_first_block(t):
        return ((t // tiles_per_batch) * chunks + lead_blocks
                + (t % tiles_per_batch) * (ROW_TILE // CHUNK))

    (y_prompt,) = _out_proj(og, w_out, x, [norm_f], None, False, F32, "final_prompt",
                            gather=(ROW_TILE, n_batch * tiles_per_batch, prompt_first_block))
    (y_sample,) = _out_proj(og, w_out, x, [norm_f], None, False, F32, "final_sample",
                            gather=(CHUNK, ns // CHUNK, lambda t: srow // CHUNK + t))
    y_prompt = y_prompt.reshape(n_batch, seq, D_MODEL)
    y_sample = y_sample.reshape(n_seq, t_len, D_MODEL)
    kv_p = jnp.stack([kv[(b + 1) * lp - win:(b + 1) * lp] for b in range(n_batch)])
    kv_p = kv_p.reshape(n_batch, win, 2, N_KV_HEADS, HEAD_DIM)
    kv_s = kv[srow:srow + ns].reshape(n_seq, t_len, 2, N_KV_HEADS, HEAD_DIM)
    k_win_s = jnp.concatenate([cache_k_win, kv_s[:, :, 0]], axis=1)[:, -win:]
    v_win_s = jnp.concatenate([cache_v_win, kv_s[:, :, 1]], axis=1)[:, -win:]
    return (y_prompt, y_sample, jnp.stack(states_p), states_s,
            kv_p[:, :, 0], kv_p[:, :, 1], k_win_s, v_win_s)
```

```python
import functools

import jax
import jax.numpy as jnp
from jax import lax
from jax.experimental import pallas as pl
from jax.experimental.pallas import tpu as pltpu

F32 = jnp.float32
BF16 = jnp.bfloat16

D_MODEL = 2048
N_META = 16
GLA_HEADS = 4
GLA_DK = 256
GLA_DV = 512
GLA_KEY_DIM = GLA_HEADS * GLA_DK
GLA_VAL_DIM = GLA_HEADS * GLA_DV
GLA_MAIN_WIDTH = 2 * GLA_KEY_DIM + 2 * GLA_VAL_DIM
GATE_RANK = 16
GATE_LOGIT_NORM = 16.0
HEAD_DIM = 64
N_Q_HEADS = 32
N_KV_HEADS = 8
Q_WIDTH = N_Q_HEADS * HEAD_DIM
KV_WIDTH = N_KV_HEADS * HEAD_DIM
WINDOW = 128
RMS_EPS = 1e-6
NEG_INF = -1e30
LOG2_E = 1.4426950408889634

LANES = 128
CHUNK = 128
GLA_LEAF = 8
LEAF_BATCH = 4
GLA_CHUNKS_PER_STEP = 3
SWA_BLOCKS_PER_STEP = 1
GLA_BOUNDED_RANGE = 60.0
ROW_TILE = 512
IN_PROJ_COLS = 1024
IN_PROJ_ROW_STEPS = 8
SEQ_GROUP = 4
GLA_SEQ_GROUP = 8
VMEM_LIMIT = 48 * 1024 * 1024

NT_DIMS = (((1,), (1,)), ((), ()))


def _silu(x):
    return x * (1.0 / (1.0 + jnp.exp(-x)))


def _params(sem):
    return pltpu.CompilerParams(dimension_semantics=sem, vmem_limit_bytes=VMEM_LIMIT)


def _emit_normed(y, nw_refs, gate_refs, hn_refs, g_ref):
    ms = jnp.mean(y * y, axis=-1, keepdims=True)
    yn = y * lax.rsqrt(ms + RMS_EPS)
    for nw_ref, hn_ref in zip(nw_refs, hn_refs):
        hn_ref[...] = (yn * nw_ref[...]).astype(hn_ref.dtype)
    if gate_refs:
        g_ref[...] = _decay_gate(hn_refs[0][...], *gate_refs)


def _split_refs(refs, n_norms, with_gate):
    nw_refs = refs[:n_norms]
    gate_refs = refs[n_norms:n_norms + 3] if with_gate else None
    return nw_refs, gate_refs, refs[n_norms + 3 * with_gate:]


def _norm_specs(norms, gate, index):
    specs = [pl.BlockSpec((1, D_MODEL), index) for _ in norms]
    args = [nw[None] for nw in norms]
    if gate is not None:
        specs += [pl.BlockSpec((LANES, D_MODEL), index),
                  pl.BlockSpec((LANES, GLA_KEY_DIM), index),
                  pl.BlockSpec((1, GLA_KEY_DIM), index)]
        args += list(gate)
    return specs, args


def _embed_kernel(sub, lead_blocks, sample_block, front, with_gate, *refs):
    n_src = sub + 2
    nw_refs, gate_refs, outs = _split_refs(refs[n_src:], 1, with_gate)
    x_ref = outs[0]
    _assemble_rows(x_ref, sub, lead_blocks, sample_block, front, refs[:n_src])
    _emit_normed(x_ref[...], nw_refs, gate_refs, outs[1:2], outs[2] if with_gate else None)


def _assemble_rows(x_ref, sub, lead_blocks, sample_block, front, src_refs):
    xp_refs, xs_ref, meta_ref = src_refs[:sub], src_refs[sub], src_refs[sub + 1]
    for k in range(sub):
        blk = pl.program_id(0) * sub + k
        rows = slice(k * CHUNK, (k + 1) * CHUNK)
        is_lead = functools.reduce(jnp.logical_or, [blk == b for b in lead_blocks])

        @pl.when(is_lead)
        def _():
            x_ref[rows, :] = jnp.concatenate(
                [jnp.zeros((front, D_MODEL), F32), meta_ref[...]], axis=0)

        @pl.when(jnp.logical_and(jnp.logical_not(is_lead), blk < sample_block))
        def _():
            x_ref[rows, :] = xp_refs[k][...]

        @pl.when(blk == sample_block)
        def _():
            x_ref[rows, :] = xs_ref[...]

        @pl.when(blk > sample_block)
        def _():
            x_ref[rows, :] = jnp.zeros((CHUNK, D_MODEL), F32)


def _token_sources(x_prompt2d, x_sample2d, meta, chunks, n_batch):
    sub = ROW_TILE // CHUNK
    prompt_blocks = x_prompt2d.shape[0] // CHUNK
    per_batch = prompt_blocks // n_batch

    def prompt_index(k):
        def index(i):
            blk = i * sub + k
            src = (blk // chunks) * per_batch + (blk % chunks) - 1
            return (jnp.clip(src, 0, prompt_blocks - 1), 0)
        return index

    specs = [pl.BlockSpec((CHUNK, D_MODEL), prompt_index(k)) for k in range(sub)] + [
        pl.BlockSpec((CHUNK, D_MODEL), lambda i: (0, 0)),
        pl.BlockSpec((N_META, D_MODEL), lambda i: (0, 0))]
    return specs, [x_prompt2d] * sub + [x_sample2d, meta]


def _embed(x_prompt2d, x_sample2d, meta, norm, gate, rows, chunks, n_batch, front):
    sub = ROW_TILE // CHUNK
    src_specs, src_args = _token_sources(x_prompt2d, x_sample2d, meta, chunks, n_batch)
    norm_specs, norm_args = _norm_specs([norm], gate, lambda i: (0, 0))
    row_spec = pl.BlockSpec((ROW_TILE, D_MODEL), lambda i: (i, 0))
    out_specs = [row_spec, row_spec]
    out_shape = [jax.ShapeDtypeStruct((rows, D_MODEL), F32),
                 jax.ShapeDtypeStruct((rows, D_MODEL), BF16)]
    if gate is not None:
        out_specs.append(pl.BlockSpec((ROW_TILE, GLA_KEY_DIM), lambda i: (i, 0)))
        out_shape.append(jax.ShapeDtypeStruct((rows, GLA_KEY_DIM), F32))
    return pl.pallas_call(
        functools.partial(_embed_kernel, sub, tuple(b * chunks for b in range(n_batch)),
                          n_batch * chunks, front, gate is not None),
        grid=(rows // ROW_TILE,),
        in_specs=src_specs + norm_specs,
        out_specs=out_specs,
        out_shape=out_shape,
        compiler_params=_params(("parallel",)),
        name="embed",
    )(*src_args, *norm_args)


def _decay_gate(hn, wg_ref, wgk_ref, bgk_ref):
    glr = lax.dot_general(hn, wg_ref[...].astype(BF16), NT_DIMS, preferred_element_type=F32)
    z = jnp.dot(glr.astype(BF16), wgk_ref[...], preferred_element_type=F32) + bgk_ref[...]
    log_sig = jnp.minimum(z, 0.0) - jnp.log(1.0 + jnp.exp(-jnp.abs(z)))
    return log_sig * (1.0 / GATE_LOGIT_NORM)


def _in_proj_kernel(transposed, scale_ref, hn_ref, w_ref, o_ref, w_bf16_ref):
    @pl.when(pl.program_id(1) == 0)
    def _():
        w_bf16_ref[...] = w_ref[...].astype(BF16)

    dims = NT_DIMS if transposed else (((1,), (0,)), ((), ()))
    o = lax.dot_general(hn_ref[...], w_bf16_ref[...], dims, preferred_element_type=F32)
    o_ref[...] = (o * scale_ref[pl.program_id(0)]).astype(o_ref.dtype)


def _in_proj(hn, w_stack, layer, n_cols, out_dtype, transposed, name, q_cols=0, q_scale=1.0):
    rows = hn.shape[0]
    row_tile = rows // IN_PROJ_ROW_STEPS
    n_slabs = n_cols // IN_PROJ_COLS
    assert q_cols % IN_PROJ_COLS == 0
    scales = jnp.where(jnp.arange(n_slabs) < q_cols // IN_PROJ_COLS, q_scale, 1.0).astype(F32)
    if transposed:
        slab = (IN_PROJ_COLS, D_MODEL)
        w_spec = pl.BlockSpec((None,) + slab, lambda j, i: (layer, j, 0))
    else:
        slab = (D_MODEL, IN_PROJ_COLS)
        w_spec = pl.BlockSpec((None,) + slab, lambda j, i: (layer, 0, j))
    return pl.pallas_call(
        functools.partial(_in_proj_kernel, transposed),
        grid=(n_slabs, IN_PROJ_ROW_STEPS),
        in_specs=[pl.BlockSpec(memory_space=pltpu.SMEM),
                  pl.BlockSpec((row_tile, D_MODEL), lambda j, i: (i, 0)), w_spec],
        out_specs=pl.BlockSpec((row_tile, IN_PROJ_COLS), lambda j, i: (i, j)),
        out_shape=jax.ShapeDtypeStruct((rows, n_cols), out_dtype),
        scratch_shapes=[pltpu.VMEM(slab, BF16)],
        compiler_params=_params(("parallel", "arbitrary")),
        name=name,
    )(scales, hn, w_stack)


def _out_proj_kernel(sub, n_norms, with_gate, keep_x, *refs):
    og_refs, w_ref, x_refs, refs = refs[:sub], refs[sub], refs[sub + 1:2 * sub + 1], refs[2 * sub + 1:]
    nw_refs, gate_refs, outs = _split_refs(refs, n_norms, with_gate)
    if sub == 1:
        og, x = og_refs[0][...], x_refs[0][...]
    else:
        og = jnp.concatenate([r[...] for r in og_refs], axis=0)
        x = jnp.concatenate([r[...] for r in x_refs], axis=0)
    y = x + jnp.dot(og, w_ref[...], preferred_element_type=F32)
    if keep_x:
        outs[0][...] = y
        outs = outs[1:]
    _emit_normed(y, nw_refs, gate_refs, outs[:n_norms], outs[n_norms] if with_gate else None)


def _out_proj(og, w, x, norms, gate, keep_x, normed_dtype, name, gather=None):
    if gather is None:
        tile_rows, n_tiles, sub = ROW_TILE, x.shape[0] // ROW_TILE, 1
        in_row_specs = [pl.BlockSpec((ROW_TILE, D_MODEL), lambda i: (i, 0))]
    else:
        tile_rows, n_tiles, first_block = gather
        sub = tile_rows // CHUNK
        in_row_specs = [pl.BlockSpec((CHUNK, D_MODEL),
                                     functools.partial(lambda k, i: (first_block(i) + k, 0), k))
                        for k in range(sub)]
    rows = tile_rows * n_tiles
    row_spec = pl.BlockSpec((tile_rows, D_MODEL), lambda i: (i, 0))
    norm_specs, norm_args = _norm_specs(norms, gate, lambda i: (0, 0))
    out_specs, out_shape = [], []
    if keep_x:
        out_specs.append(row_spec)
        out_shape.append(jax.ShapeDtypeStruct((rows, D_MODEL), F32))
    for _ in norms:
        out_specs.append(row_spec)
        out_shape.append(jax.ShapeDtypeStruct((rows, D_MODEL), normed_dtype))
    if gate is not None:
        out_specs.append(pl.BlockSpec((tile_rows, GLA_KEY_DIM), lambda i: (i, 0)))
        out_shape.append(jax.ShapeDtypeStruct((rows, GLA_KEY_DIM), F32))
    w_spec = pl.BlockSpec((D_MODEL, D_MODEL), lambda i: (0, 0), pipeline_mode=pl.Buffered(1))
    return pl.pallas_call(
        functools.partial(_out_proj_kernel, sub, len(norms), gate is not None, keep_x),
        grid=(n_tiles,),
        in_specs=in_row_specs + [w_spec] + in_row_specs + norm_specs,
        out_specs=out_specs,
        out_shape=out_shape,
        compiler_params=_params(("parallel",)),
        name=name,
    )(*([og] * sub), w, *([x] * sub), *norm_args)


def _exact_block(qs, k, b, n, width, col0):
    lane = lax.broadcasted_iota(jnp.int32, (n, width), 1)
    acc = jnp.zeros((n, width), F32)
    for j in range(n):
        t = qs * k[j:j + 1] * jnp.exp(jnp.minimum(b - b[j:j + 1], 0.0))
        acc = jnp.where(lane == col0 + j, jnp.sum(t, axis=-1, keepdims=True), acc)
    return acc


def _masked_row_sums(mask_bf16, g):
    g_hi = g.astype(BF16)
    r1 = g - g_hi.astype(F32)
    g_mid = r1.astype(BF16)
    g_lo = (r1 - g_mid.astype(F32)).astype(BF16)
    return (jnp.dot(mask_bf16, g_hi, preferred_element_type=F32)
            + jnp.dot(mask_bf16, g_mid, preferred_element_type=F32)
            + jnp.dot(mask_bf16, g_lo, preferred_element_type=F32))


def _column_of(row_vec, n):
    return jnp.broadcast_to(row_vec, (LANES, n)).T[:, 0:1]


def _head_norm_gate(o, onorm, gate):
    ms = jnp.mean(o * o, axis=-1, keepdims=True)
    return (o * lax.rsqrt(ms + RMS_EPS) * onorm * _silu(gate)).astype(BF16)


def _intra_chunk_scores(qs, k, k_bf, b, row_xor_col):
    c_len = qs.shape[0]
    leaf = GLA_LEAF
    leaf_row = lax.broadcasted_iota(jnp.int32, (leaf, c_len), 0)
    leaf_col = lax.broadcasted_iota(jnp.int32, (leaf, c_len), 1)
    lane_in_leaf = [(leaf_col & (leaf - 1)) == j for j in range(leaf)]

    att_rows = []
    for blk0 in range(0, c_len // leaf, LEAF_BATCH):
        lhs = []
        for blk in range(blk0, blk0 + LEAF_BATCH):
            q_blk = qs[blk * leaf:(blk + 1) * leaf]
            b_blk = b[blk * leaf:(blk + 1) * leaf]
            lhs += [q_blk * jnp.exp2(jnp.minimum(b_blk - b_blk[j:j + 1], 0.0))
                    for j in range(leaf)]
        r = lax.dot_general(jnp.concatenate(lhs, axis=0).astype(BF16), k_bf, NT_DIMS,
                            preferred_element_type=F32)
        for i, blk in enumerate(range(blk0, blk0 + LEAF_BATCH)):
            acc = jnp.zeros((leaf, c_len), F32)
            for j in range(leaf):
                r0 = (i * leaf + j) * leaf
                acc = jnp.where(lane_in_leaf[j], r[r0:r0 + leaf], acc)
            rel = leaf_col - blk * leaf
            att_rows.append(jnp.where((rel >= 0) & (rel <= leaf_row), acc, 0.0))
    att = jnp.concatenate(att_rows, axis=0)

    m = c_len // 2
    while m >= leaf:
        q_parts, k_parts = [], []
        zeros = jnp.zeros((m, GLA_DK), F32)
        for base in range(0, c_len, 2 * m):
            ref = b[base + m - 1:base + m]
            lo = slice(base, base + m)
            hi = slice(base + m, base + 2 * m)
            k_parts += [k[lo] * jnp.exp2(ref - b[lo]), zeros]
            q_parts += [zeros, qs[hi] * jnp.exp2(b[hi] - ref)]
        qt = jnp.concatenate(q_parts, axis=0).astype(BF16)
        kt = jnp.concatenate(k_parts, axis=0).astype(BF16)
        a = lax.dot_general(qt, kt, NT_DIMS, preferred_element_type=F32)
        att = att + jnp.where(row_xor_col < 2 * m, a, 0.0)
        m //= 2
    return att


def _gla_prompt_kernel(q_ref, k_ref, v_ref, gate_ref, g_ref, onorm_ref, og_in_ref,
                       og_ref, s_ref, b_scr, qd_scr, att_scr):
    del og_in_ref
    c_len = CHUNK
    heads = [(slice(h * GLA_DK, (h + 1) * GLA_DK), slice(h * GLA_DV, (h + 1) * GLA_DV))
             for h in range(GLA_HEADS)]

    @pl.when(pl.program_id(1) == 0)
    def _():
        s_ref[...] = jnp.zeros_like(s_ref)

    row = lax.broadcasted_iota(jnp.int32, (c_len, c_len), 0)
    col = lax.broadcasted_iota(jnp.int32, (c_len, c_len), 1)
    lower = row >= col
    tri = jnp.where(lower, 1.0, 0.0).astype(BF16)
    onorm = onorm_ref[...]
    chunk_rows = [slice(c * c_len, (c + 1) * c_len) for c in range(q_ref.shape[0] // c_len)]

    total = None
    for rows in chunk_rows:
        b_c = _masked_row_sums(tri, g_ref[rows, :]) * LOG2_E
        b_scr[rows, :] = b_c
        qd_scr[rows, :] = (q_ref[rows, :].astype(F32) * jnp.exp2(b_c)).astype(BF16)
        last = b_c[c_len - 1:c_len]
        total = last if total is None else jnp.minimum(total, last)

    bounded = jnp.max(-total) <= GLA_BOUNDED_RANGE

    @pl.when(bounded)
    def _():
        for c, rows in enumerate(chunk_rows):
            for h, (ks, _) in enumerate(heads):
                k_inv = (k_ref[rows, ks].astype(F32) * jnp.exp2(-b_scr[rows, ks])).astype(BF16)
                a = lax.dot_general(qd_scr[rows, ks], k_inv, NT_DIMS,
                                    preferred_element_type=F32)
                att_scr[c * GLA_HEADS + h] = jnp.where(lower, a, 0.0).astype(BF16)

    @pl.when(jnp.logical_not(bounded))
    def _():
        for c, rows in enumerate(chunk_rows):
            for h, (ks, _) in enumerate(heads):
                k_bf = k_ref[rows, ks]
                att_scr[c * GLA_HEADS + h] = _intra_chunk_scores(
                    q_ref[rows, ks].astype(F32), k_bf.astype(F32), k_bf,
                    b_scr[rows, ks], row ^ col).astype(BF16)

    for c, rows in enumerate(chunk_rows):
        for h, (ks, vs) in enumerate(heads):
            k = k_ref[rows, ks].astype(F32)
            v = v_ref[rows, vs]
            b = b_scr[rows, ks]
            b_last = b[c_len - 1:c_len]
            state = s_ref[0, h]

            o = jnp.dot(qd_scr[rows, ks], state.astype(BF16), preferred_element_type=F32)
            o = o + jnp.dot(att_scr[c * GLA_HEADS + h], v, preferred_element_type=F32)

            kd_t = (k * jnp.exp2(b_last - b)).T.astype(BF16)
            s_ref[0, h] = (state * _column_of(jnp.exp2(b_last), GLA_DK)
                           + jnp.dot(kd_t, v, preferred_element_type=F32))

            og_ref[rows, vs] = _head_norm_gate(o, onorm, gate_ref[rows, vs].astype(F32))


def _gla_prompt(proj, g, onorm, og, n_batch, chunks):
    rows = proj.shape[0]
    per_step = next(n for n in (GLA_CHUNKS_PER_STEP, 1) if chunks % n == 0)
    steps = chunks // per_step
    tile = per_step * CHUNK
    blk = lambda col: (lambda b, c: (b * steps + c, col))
    return pl.pallas_call(
        _gla_prompt_kernel,
        grid=(n_batch, steps),
        in_specs=[
            pl.BlockSpec((tile, GLA_KEY_DIM), blk(0)),
            pl.BlockSpec((tile, GLA_KEY_DIM), blk(1)),
            pl.BlockSpec((tile, GLA_VAL_DIM), blk(1)),
            pl.BlockSpec((tile, GLA_VAL_DIM), blk(2)),
            pl.BlockSpec((tile, GLA_KEY_DIM), blk(0)),
            pl.BlockSpec((1, GLA_DV), lambda b, c: (0, 0)),
            pl.BlockSpec(memory_space=pl.ANY),
        ],
        out_specs=[
            pl.BlockSpec((tile, GLA_VAL_DIM), blk(0)),
            pl.BlockSpec((1, GLA_HEADS, GLA_DK, GLA_DV), lambda b, c: (b, 0, 0, 0)),
        ],
        out_shape=[
            jax.ShapeDtypeStruct((rows, GLA_VAL_DIM), BF16),
            jax.ShapeDtypeStruct((n_batch, GLA_HEADS, GLA_DK, GLA_DV), F32),
        ],
        scratch_shapes=[pltpu.VMEM((tile, GLA_KEY_DIM), F32),
                        pltpu.VMEM((tile, GLA_KEY_DIM), BF16),
                        pltpu.VMEM((per_step * GLA_HEADS, CHUNK, CHUNK), BF16)],
        input_output_aliases={6: 0},
        compiler_params=_params(("parallel", "arbitrary")),
        name="gla_prompt",
    )(proj, proj, proj, proj, g, onorm, og)


def _gla_sample_kernel(t_len, q_ref, k_ref, v_ref, gate_ref, g_ref, onorm_ref, s0_ref,
                       og_in_ref, *rest):
    og_ref, s_ref = rest[-2], rest[-1]
    n = GLA_SEQ_GROUP * t_len
    row = lax.broadcasted_iota(jnp.int32, (n, LANES), 0)
    col = lax.broadcasted_iota(jnp.int32, (n, LANES), 1)
    causal = ((row // t_len) == (col // t_len)) & (row >= col)
    row_seq = lax.broadcasted_iota(jnp.int32, (n, 1), 0) // t_len

    qs = q_ref[...].astype(F32)
    k = k_ref[...].astype(F32)
    pad_k = jnp.zeros((LANES - n, GLA_DK), F32)
    v_pad = jnp.concatenate([v_ref[...], jnp.zeros((LANES - n, GLA_DV), BF16)], axis=0)
    b = _masked_row_sums(jnp.where(causal, 1.0, 0.0).astype(BF16),
                         jnp.concatenate([g_ref[...], pad_k], axis=0))
    att = jnp.where(causal, _exact_block(qs, k, b, n, LANES, 0), 0.0)
    o = jnp.dot(att.astype(BF16), v_pad, preferred_element_type=F32)

    q_dec = (qs * jnp.exp(b)).astype(BF16)
    for u in range(GLA_SEQ_GROUP):
        state = s0_ref[0, u, 0]
        o_u = jnp.dot(q_dec, state.astype(BF16), preferred_element_type=F32)
        o = o + jnp.where(row_seq == u, o_u, 0.0)
        b_last = b[(u + 1) * t_len - 1:(u + 1) * t_len]
        kd = jnp.where(row_seq == u, k * jnp.exp(jnp.minimum(b_last - b, 0.0)), 0.0)
        kd_t = jnp.concatenate([kd, pad_k], axis=0).T.astype(BF16)
        s_ref[0, u, 0] = (state * _column_of(jnp.exp(b_last), GLA_DK)
                          + jnp.dot(kd_t, v_pad, preferred_element_type=F32))

    og_ref[...] = _head_norm_gate(o, onorm_ref[...], gate_ref[...].astype(F32))


def _gla_sample(proj, g, onorm, state_in, og, s_out, layer, n_seq, t_len, row0):
    n = GLA_SEQ_GROUP * t_len
    rb0 = row0 // n
    n_layers = state_in.shape[0]
    key_blocks = GLA_KEY_DIM // GLA_DK
    val_blocks = GLA_VAL_DIM // GLA_DV
    st_spec = pl.BlockSpec((1, GLA_SEQ_GROUP, 1, GLA_DK, GLA_DV),
                           lambda s, h: (layer, s, h, 0, 0))
    in_specs = [
        pl.BlockSpec((n, GLA_DK), lambda s, h: (rb0 + s, h)),
        pl.BlockSpec((n, GLA_DK), lambda s, h: (rb0 + s, key_blocks + h)),
        pl.BlockSpec((n, GLA_DV), lambda s, h: (rb0 + s, val_blocks + h)),
        pl.BlockSpec((n, GLA_DV), lambda s, h: (rb0 + s, 2 * val_blocks + h)),
        pl.BlockSpec((n, GLA_DK), lambda s, h: (rb0 + s, h)),
        pl.BlockSpec((1, GLA_DV), lambda s, h: (0, 0)),
        st_spec,
        pl.BlockSpec(memory_space=pl.ANY),
    ]
    args = [proj, proj, proj, proj, g, onorm, state_in, og]
    aliases = {7: 0}
    if s_out is not None:
        in_specs.append(pl.BlockSpec(memory_space=pl.ANY))
        args.append(s_out)
        aliases[8] = 1
    return pl.pallas_call(
        functools.partial(_gla_sample_kernel, t_len),
        grid=(n_seq // GLA_SEQ_GROUP, GLA_HEADS),
        in_specs=in_specs,
        out_specs=[
            pl.BlockSpec((n, GLA_DV), lambda s, h: (rb0 + s, h)),
            st_spec,
        ],
        out_shape=[
            jax.ShapeDtypeStruct(og.shape, BF16),
            jax.ShapeDtypeStruct((n_layers, n_seq, GLA_HEADS, GLA_DK, GLA_DV), F32),
        ],
        input_output_aliases=aliases,
        compiler_params=_params(("parallel", "parallel")),
        name="gla_sample",
    )(*args)


def _attend_rows(q_ref, gate_ref, og_ref, sink_ref, kv_sets, s_scr, p_scr):
    n_rows = q_ref.shape[0]
    n_keys = s_scr.shape[2]
    group = N_Q_HEADS // N_KV_HEADS
    low_out = lax.broadcasted_iota(jnp.int32, (n_rows, LANES), 1) < HEAD_DIM
    lane0 = lax.broadcasted_iota(jnp.int32, (n_rows, LANES), 1) == 0
    key_lane = lax.broadcasted_iota(jnp.int32, (n_keys, LANES), 1)
    key_row = lax.broadcasted_iota(jnp.int32, (n_keys, LANES), 0)
    low = key_lane < HEAD_DIM
    slot0 = key_row == 0

    def kv_heads():
        for cpair in range(KV_WIDTH // LANES):
            for parity in range(2):
                first = (2 * cpair + parity) * (group // 2)
                yield cpair, parity, list(range(first, first + group // 2))

    def slot(set_index, qc, half):
        return (set_index * (Q_WIDTH // LANES) + qc) * 2 + half

    def padded_copies(pair, fill):
        swapped = pltpu.roll(pair, HEAD_DIM, axis=1).astype(BF16)
        pair = pair.astype(BF16)
        fill = jnp.full((), fill, BF16)
        return ((jnp.where(low, pair, fill), jnp.where(low, fill, swapped)),
                (jnp.where(low, swapped, fill), jnp.where(low, fill, pair)))

    for si, (keys, _, _) in enumerate(kv_sets):
        k_pads = None
        for cpair, parity, qcs in kv_heads():
            if parity == 0:
                k_pads = padded_copies(keys[cpair], 0.0)
            q_stack = jnp.concatenate(
                [q_ref[:, qc * LANES:(qc + 1) * LANES] for qc in qcs], axis=0)
            for half, k_pad in enumerate(k_pads[parity]):
                s = lax.dot_general(q_stack, k_pad, NT_DIMS, preferred_element_type=F32)
                for j, qc in enumerate(qcs):
                    s_scr[slot(si, qc, half)] = s[j * n_rows:(j + 1) * n_rows]

    for si, (_, _, mask) in enumerate(kv_sets):
        mask_first, mask_rest = mask[:, :LANES], mask[:, LANES:]
        for qc in range(Q_WIDTH // LANES):
            for half in range(2):
                sink = sink_ref[2 * qc + half] * LOG2_E
                s = s_scr[slot(si, qc, half)]
                first = jnp.where(mask_first, s[:, :LANES],
                                  jnp.where(lane0, sink, NEG_INF))
                rest = jnp.where(mask_rest, s[:, LANES:], NEG_INF)
                m = jnp.max(jnp.maximum(first, rest), axis=-1, keepdims=True)
                p_scr[slot(si, qc, half)] = jnp.exp2(
                    jnp.concatenate([first, rest], axis=1) - m).astype(BF16)

    def store(qc, o):
        cols = slice(qc * LANES, (qc + 1) * LANES)
        og_ref[:, cols] = (o * _silu(gate_ref[:, cols].astype(F32))).astype(BF16)

    out_cols = [None] * (Q_WIDTH // LANES)
    for si, (_, vals, _) in enumerate(kv_sets):
        v_pads = None
        for cpair, parity, qcs in kv_heads():
            if parity == 0:
                v_pads = padded_copies(jnp.where(slot0, 0.0, vals[cpair]), 1.0)
            stacked = [jnp.dot(jnp.concatenate([p_scr[slot(si, qc, half)] for qc in qcs],
                                               axis=0),
                               v_pads[parity][half], preferred_element_type=F32)
                       for half in range(2)]
            for j, qc in enumerate(qcs):
                oa, ob = (o[j * n_rows:(j + 1) * n_rows] for o in stacked)
                sums = pltpu.roll(jnp.where(low_out, ob, oa), HEAD_DIM, axis=1)
                o2 = jnp.where(low_out, oa, ob) * (1.0 / sums)
                if len(kv_sets) == 1:
                    store(qc, o2)
                else:
                    out_cols[qc] = o2 if out_cols[qc] is None else out_cols[qc] + o2

    if len(kv_sets) > 1:
        for qc, o in enumerate(out_cols):
            store(qc, o)


def _attention_scratch(n_sets, n_rows, n_keys):
    n = n_sets * N_Q_HEADS
    return [pltpu.VMEM((n, n_rows, n_keys), F32), pltpu.VMEM((n, n_rows, n_keys), BF16)]


def _swa_prompt_kernel(front, sink_ref, q_ref, gate_ref, kc_ref, kp_ref, vc_ref, vp_ref,
                       og_in_ref, og_ref, s_scr, p_scr):
    del og_in_ref
    row = lax.broadcasted_iota(jnp.int32, (CHUNK, 2 * CHUNK), 0)
    col = lax.broadcasted_iota(jnp.int32, (CHUNK, 2 * CHUNK), 1)
    band = (col > row) & (col <= row + WINDOW)
    n_sub = q_ref.shape[0] // CHUNK
    lane_pairs = [slice(c * LANES, (c + 1) * LANES) for c in range(KV_WIDTH // LANES)]
    for r in range(n_sub):
        blk = pl.program_id(1) * n_sub + r
        rows = slice(r * CHUNK, (r + 1) * CHUNK)
        before = slice((r - 1) * CHUNK, r * CHUNK)
        mask = band & ((blk - 1) * CHUNK + col >= front)
        keys = [jnp.concatenate([kp_ref[:, c] if r == 0 else kc_ref[before, c],
                                 kc_ref[rows, c]], axis=0) for c in lane_pairs]
        vals = [jnp.concatenate([vp_ref[:, c] if r == 0 else vc_ref[before, c],
                                 vc_ref[rows, c]], axis=0) for c in lane_pairs]
        _attend_rows(q_ref.at[rows, :], gate_ref.at[rows, :], og_ref.at[rows, :], sink_ref,
                     [(keys, vals, mask)], s_scr, p_scr)


def _swa_prompt(pq, kv, sinks, og, n_batch, chunks, front):
    per_step = next(n for n in (SWA_BLOCKS_PER_STEP, 1) if chunks % n == 0)
    steps = chunks // per_step
    tile = per_step * CHUNK
    cur = lambda col: (lambda b, i: (b * steps + i, col))
    prev = lambda col: (lambda b, i: (b * chunks + jnp.maximum(i * per_step - 1, 0), col))
    return pl.pallas_call(
        functools.partial(_swa_prompt_kernel, front),
        grid=(n_batch, steps),
        in_specs=[
            pl.BlockSpec(memory_space=pltpu.SMEM),
            pl.BlockSpec((tile, Q_WIDTH), cur(0)),
            pl.BlockSpec((tile, Q_WIDTH), cur(1)),
            pl.BlockSpec((tile, KV_WIDTH), cur(0)),
            pl.BlockSpec((CHUNK, KV_WIDTH), prev(0)),
            pl.BlockSpec((tile, KV_WIDTH), cur(1)),
            pl.BlockSpec((CHUNK, KV_WIDTH), prev(1)),
            pl.BlockSpec(memory_space=pl.ANY),
        ],
        out_specs=pl.BlockSpec((tile, Q_WIDTH), cur(0)),
        out_shape=jax.ShapeDtypeStruct(og.shape, BF16),
        scratch_shapes=_attention_scratch(1, CHUNK, 2 * CHUNK),
        input_output_aliases={7: 0},
        compiler_params=_params(("parallel", "parallel")),
        name="swa_prompt",
    )(sinks, pq, pq, kv, kv, kv, kv, og)


def _swa_sample_kernel(t_len, sink_ref, q_ref, gate_ref, kn_ref, vn_ref, ck_ref, cv_ref,
                       og_in_ref, og_ref, s_scr, p_scr):
    del og_in_ref
    n = SEQ_GROUP * t_len
    win = ck_ref.shape[1]
    row = lax.broadcasted_iota(jnp.int32, (n, 2 * win), 0)
    col = lax.broadcasted_iota(jnp.int32, (n, 2 * win), 1)
    t = row % t_len
    new = col - win
    in_window = ((col < win) & (col > t)) | (
        (new >= 0) & (new < n) & ((new // t_len) == (row // t_len)) & ((new % t_len) <= t))
    pad = jnp.zeros((win - n, LANES), F32)
    kv_pairs = []
    for u in range(SEQ_GROUP):
        keys = [jnp.concatenate([ck_ref[u, :, c * LANES:(c + 1) * LANES],
                                 kn_ref[:, c * LANES:(c + 1) * LANES], pad], axis=0)
                for c in range(KV_WIDTH // LANES)]
        vals = [jnp.concatenate([cv_ref[u, :, c * LANES:(c + 1) * LANES],
                                 vn_ref[:, c * LANES:(c + 1) * LANES], pad], axis=0)
                for c in range(KV_WIDTH // LANES)]
        kv_pairs.append((keys, vals, in_window & ((row // t_len) == u)))
    _attend_rows(q_ref, gate_ref, og_ref, sink_ref, kv_pairs, s_scr, p_scr)


def _swa_sample(pq, kv, cache_k, cache_v, sinks, og, n_seq, t_len, row0):
    n = SEQ_GROUP * t_len
    rb0 = row0 // n
    win = cache_k.shape[1]
    return pl.pallas_call(
        functools.partial(_swa_sample_kernel, t_len),
        grid=(n_seq // SEQ_GROUP,),
        in_specs=[
            pl.BlockSpec(memory_space=pltpu.SMEM),
            pl.BlockSpec((n, Q_WIDTH), lambda s: (rb0 + s, 0)),
            pl.BlockSpec((n, Q_WIDTH), lambda s: (rb0 + s, 1)),
            pl.BlockSpec((n, KV_WIDTH), lambda s: (rb0 + s, 0)),
            pl.BlockSpec((n, KV_WIDTH), lambda s: (rb0 + s, 1)),
            pl.BlockSpec((SEQ_GROUP, win, KV_WIDTH), lambda s: (s, 0, 0)),
            pl.BlockSpec((SEQ_GROUP, win, KV_WIDTH), lambda s: (s, 0, 0)),
            pl.BlockSpec(memory_space=pl.ANY),
        ],
        out_specs=pl.BlockSpec((n, Q_WIDTH), lambda s: (rb0 + s, 0)),
        out_shape=jax.ShapeDtypeStruct(og.shape, BF16),
        scratch_shapes=_attention_scratch(SEQ_GROUP, n, 2 * win),
        input_output_aliases={7: 0},
        compiler_params=_params(("parallel",)),
        name="swa_sample",
    )(sinks, pq, pq, kv, kv, cache_k, cache_v, og)


def kernel(x_prompt, x_sample, state_gla, cache_k_win, cache_v_win, meta_tokens, norm_a, w_in_a, w_gk2_a, b_gk2_a, onorm_a, w_out_a, norm_kv, w_kv, norm_b, w_in_b, sinks_b, w_out_b, norm_f):
    n_batch, seq, _ = x_prompt.shape
    n_seq, t_len, _ = x_sample.shape
    win = cache_k_win.shape[1]
    n_a = w_in_a.shape[0]
    n_b = w_in_b.shape[0]
    assert win == WINDOW == CHUNK and n_seq % SEQ_GROUP == 0 and n_seq % GLA_SEQ_GROUP == 0

    length = seq + N_META
    front = (-length) % CHUNK
    lp = length + front
    chunks = lp // CHUNK
    srow = n_batch * lp
    ns = n_seq * t_len
    rows = -(-(srow + ns) // ROW_TILE) * ROW_TILE

    assert front + N_META == CHUNK and ns == CHUNK and seq % CHUNK == 0
    assert rows % CHUNK == 0 and rows % (16 * IN_PROJ_ROW_STEPS) == 0

    w_in_a_t = jnp.swapaxes(w_in_a, 1, 2)

    def gate_weights(l):
        w_gate = jnp.pad(w_in_a_t[l, GLA_MAIN_WIDTH:], ((0, LANES - GATE_RANK), (0, 0)))
        w_gk = jnp.pad(w_gk2_a[l], ((0, LANES - GATE_RANK), (0, 0))).astype(BF16)
        return w_gate, w_gk, b_gk2_a[l][None]

    x, hn, g = _embed(x_prompt.reshape(n_batch * seq, D_MODEL), x_sample.reshape(ns, D_MODEL),
                      meta_tokens, norm_a[0], gate_weights(0), rows, chunks, n_batch, front)
    og = jnp.zeros((rows, D_MODEL), BF16)

    states_p = []
    states_s = None
    for l in range(n_a):
        proj = _in_proj(hn, w_in_a_t, l, GLA_MAIN_WIDTH, BF16, True, "gla_in_proj",
                        q_cols=GLA_KEY_DIM, q_scale=GLA_DK ** -0.5)
        onorm = onorm_a[l][None]
        og, s_p = _gla_prompt(proj, g, onorm, og, n_batch, chunks)
        og, states_s = _gla_sample(proj, g, onorm, state_gla, og, states_s, l,
                                   n_seq, t_len, srow)
        states_p.append(s_p)
        w_out = w_out_a[l].astype(BF16)
        if l + 1 < n_a:
            x, hn, g = _out_proj(og, w_out, x, [norm_a[l + 1]], gate_weights(l + 1), True,
                                 BF16, "gla_out_proj")
        else:
            x, hn_kv, hn = _out_proj(og, w_out, x, [norm_kv, norm_b[0]], None, True,
                                     BF16, "gla_out_proj")

    kv = _in_proj(hn_kv, w_kv[None], 0, 2 * KV_WIDTH, F32, False, "kv_proj")
    cache_k = cache_k_win.reshape(n_seq, win, KV_WIDTH)
    cache_v = cache_v_win.reshape(n_seq, win, KV_WIDTH)
    for j in range(n_b):
        pq = _in_proj(hn, w_in_b, j, 2 * Q_WIDTH, BF16, False, "swa_in_proj",
                      q_cols=Q_WIDTH, q_scale=HEAD_DIM ** -0.5 * LOG2_E)
        og = _swa_prompt(pq, kv, sinks_b[j], og, n_batch, chunks, front)
        og = _swa_sample(pq, kv, cache_k, cache_v, sinks_b[j], og, n_seq, t_len, srow)
        w_out = w_out_b[j].astype(BF16)
        if j + 1 < n_b:
            x, hn = _out_proj(og, w_out, x, [norm_b[j + 1]], None, True, BF16, "swa_out_proj")

    tiles_per_batch = seq // ROW_TILE
    lead_blocks = (lp - seq) // CHUNK

    def prompt_first_block(t):
        return ((t // tiles_per_batch) * chunks + lead_blocks
                + (t % tiles_per_batch) * (ROW_TILE // CHUNK))

    (y_prompt,) = _out_proj(og, w_out, x, [norm_f], None, False, F32, "final_prompt",
                            gather=(ROW_TILE, n_batch * tiles_per_batch, prompt_first_block))
    (y_sample,) = _out_proj(og, w_out, x, [norm_f], None, False, F32, "final_sample",
                            gather=(CHUNK, ns // CHUNK, lambda t: srow // CHUNK + t))
    y_prompt = y_prompt.reshape(n_batch, seq, D_MODEL)
    y_sample = y_sample.reshape(n_seq, t_len, D_MODEL)
    kv_p = jnp.stack([kv[(b + 1) * lp - win:(b + 1) * lp] for b in range(n_batch)])
    kv_p = kv_p.reshape(n_batch, win, 2, N_KV_HEADS, HEAD_DIM)
    kv_s = kv[srow:srow + ns].reshape(n_seq, t_len, 2, N_KV_HEADS, HEAD_DIM)
    k_win_s = jnp.concatenate([cache_k_win, kv_s[:, :, 0]], axis=1)[:, -win:]
    v_win_s = jnp.concatenate([cache_v_win, kv_s[:, :, 1]], axis=1)[:, -win:]
    return (y_prompt, y_sample, jnp.stack(states_p), states_s,
            kv_p[:, :, 0], kv_p[:, :, 1], k_win_s, v_win_s)
```

```python
import functools

import jax
import jax.numpy as jnp
from jax import lax
from jax.experimental import pallas as pl
from jax.experimental.pallas import tpu as pltpu

F32 = jnp.float32
BF16 = jnp.bfloat16

D_MODEL = 2048
N_META = 16
GLA_HEADS = 4
GLA_DK = 256
GLA_DV = 512
GLA_KEY_DIM = GLA_HEADS * GLA_DK
GLA_VAL_DIM = GLA_HEADS * GLA_DV
GLA_MAIN_WIDTH = 2 * GLA_KEY_DIM + 2 * GLA_VAL_DIM
GATE_RANK = 16
GATE_LOGIT_NORM = 16.0
HEAD_DIM = 64
N_Q_HEADS = 32
N_KV_HEADS = 8
Q_WIDTH = N_Q_HEADS * HEAD_DIM
KV_WIDTH = N_KV_HEADS * HEAD_DIM
WINDOW = 128
RMS_EPS = 1e-6
NEG_INF = -1e30
LOG2_E = 1.4426950408889634

LANES = 128
CHUNK = 128
GLA_LEAF = 8
LEAF_BATCH = 4
GLA_CHUNKS_PER_STEP = 3
SWA_BLOCKS_PER_STEP = 1
GLA_BOUNDED_RANGE = 60.0
ROW_TILE = 512
IN_PROJ_COLS = 1024
IN_PROJ_ROW_STEPS = 8
SEQ_GROUP = 4
GLA_SEQ_GROUP = 8
VMEM_LIMIT = 48 * 1024 * 1024

NT_DIMS = (((1,), (1,)), ((), ()))


def _silu(x):
    return x * (1.0 / (1.0 + jnp.exp(-x)))


def _params(sem):
    return pltpu.CompilerParams(dimension_semantics=sem, vmem_limit_bytes=VMEM_LIMIT)


def _emit_normed(y, nw_refs, gate_refs, hn_refs, g_ref):
    ms = jnp.mean(y * y, axis=-1, keepdims=True)
    yn = y * lax.rsqrt(ms + RMS_EPS)
    for nw_ref, hn_ref in zip(nw_refs, hn_refs):
        hn_ref[...] = (yn * nw_ref[...]).astype(hn_ref.dtype)
    if gate_refs:
        g_ref[...] = _decay_gate(hn_refs[0][...], *gate_refs)


def _split_refs(refs, n_norms, with_gate):
    nw_refs = refs[:n_norms]
    gate_refs = refs[n_norms:n_norms + 3] if with_gate else None
    return nw_refs, gate_refs, refs[n_norms + 3 * with_gate:]


def _norm_specs(norms, gate, index):
    specs = [pl.BlockSpec((1, D_MODEL), index) for _ in norms]
    args = [nw[None] for nw in norms]
    if gate is not None:
        specs += [pl.BlockSpec((LANES, D_MODEL), index),
                  pl.BlockSpec((LANES, GLA_KEY_DIM), index),
                  pl.BlockSpec((1, GLA_KEY_DIM), index)]
        args += list(gate)
    return specs, args


def _embed_kernel(sub, lead_blocks, sample_block, front, with_gate, *refs):
    n_src = sub + 2
    nw_refs, gate_refs, outs = _split_refs(refs[n_src:], 1, with_gate)
    x_ref = outs[0]
    _assemble_rows(x_ref, sub, lead_blocks, sample_block, front, refs[:n_src])
    _emit_normed(x_ref[...], nw_refs, gate_refs, outs[1:2], outs[2] if with_gate else None)


def _assemble_rows(x_ref, sub, lead_blocks, sample_block, front, src_refs):
    xp_refs, xs_ref, meta_ref = src_refs[:sub], src_refs[sub], src_refs[sub + 1]
    for k in range(sub):
        blk = pl.program_id(0) * sub + k
        rows = slice(k * CHUNK, (k + 1) * CHUNK)
        is_lead = functools.reduce(jnp.logical_or, [blk == b for b in lead_blocks])

        @pl.when(is_lead)
        def _():
            x_ref[rows, :] = jnp.concatenate(
                [jnp.zeros((front, D_MODEL), F32), meta_ref[...]], axis=0)

        @pl.when(jnp.logical_and(jnp.logical_not(is_lead), blk < sample_block))
        def _():
            x_ref[rows, :] = xp_refs[k][...]

        @pl.when(blk == sample_block)
        def _():
            x_ref[rows, :] = xs_ref[...]

        @pl.when(blk > sample_block)
        def _():
            x_ref[rows, :] = jnp.zeros((CHUNK, D_MODEL), F32)


def _token_sources(x_prompt2d, x_sample2d, meta, chunks, n_batch):
    sub = ROW_TILE // CHUNK
    prompt_blocks = x_prompt2d.shape[0] // CHUNK
    per_batch = prompt_blocks // n_batch

    def prompt_index(k):
        def index(i):
            blk = i * sub + k
            src = (blk // chunks) * per_batch + (blk % chunks) - 1
            return (jnp.clip(src, 0, prompt_blocks - 1), 0)
        return index

    specs = [pl.BlockSpec((CHUNK, D_MODEL), prompt_index(k)) for k in range(sub)] + [
        pl.BlockSpec((CHUNK, D_MODEL), lambda i: (0, 0)),
        pl.BlockSpec((N_META, D_MODEL), lambda i: (0, 0))]
    return specs, [x_prompt2d] * sub + [x_sample2d, meta]


def _embed(x_prompt2d, x_sample2d, meta, norm, gate, rows, chunks, n_batch, front):
    sub = ROW_TILE // CHUNK
    src_specs, src_args = _token_sources(x_prompt2d, x_sample2d, meta, chunks, n_batch)
    norm_specs, norm_args = _norm_specs([norm], gate, lambda i: (0, 0))
    row_spec = pl.BlockSpec((ROW_TILE, D_MODEL), lambda i: (i, 0))
    out_specs = [row_spec, row_spec]
    out_shape = [jax.ShapeDtypeStruct((rows, D_MODEL), F32),
                 jax.ShapeDtypeStruct((rows, D_MODEL), BF16)]
    if gate is not None:
        out_specs.append(pl.BlockSpec((ROW_TILE, GLA_KEY_DIM), lambda i: (i, 0)))
        out_shape.append(jax.ShapeDtypeStruct((rows, GLA_KEY_DIM), F32))
    return pl.pallas_call(
        functools.partial(_embed_kernel, sub, tuple(b * chunks for b in range(n_batch)),
                          n_batch * chunks, front, gate is not None),
        grid=(rows // ROW_TILE,),
        in_specs=src_specs + norm_specs,
        out_specs=out_specs,
        out_shape=out_shape,
        compiler_params=_params(("parallel",)),
        name="embed",
    )(*src_args, *norm_args)


def _decay_gate(hn, wg_ref, wgk_ref, bgk_ref):
    glr = lax.dot_general(hn, wg_ref[...].astype(BF16), NT_DIMS, preferred_element_type=F32)
    z = jnp.dot(glr.astype(BF16), wgk_ref[...], preferred_element_type=F32) + bgk_ref[...]
    log_sig = jnp.minimum(z, 0.0) - jnp.log(1.0 + jnp.exp(-jnp.abs(z)))
    return log_sig * (1.0 / GATE_LOGIT_NORM)


def _in_proj_kernel(transposed, scale_ref, hn_ref, w_ref, o_ref, w_bf16_ref):
    @pl.when(pl.program_id(1) == 0)
    def _():
        w_bf16_ref[...] = w_ref[...].astype(BF16)

    dims = NT_DIMS if transposed else (((1,), (0,)), ((), ()))
    o = lax.dot_general(hn_ref[...], w_bf16_ref[...], dims, preferred_element_type=F32)
    o_ref[...] = (o * scale_ref[pl.program_id(0)]).astype(o_ref.dtype)


def _in_proj(hn, w_stack, layer, n_cols, out_dtype, transposed, name, q_cols=0, q_scale=1.0):
    rows = hn.shape[0]
    row_tile = rows // IN_PROJ_ROW_STEPS
    n_slabs = n_cols // IN_PROJ_COLS
    assert q_cols % IN_PROJ_COLS == 0
    scales = jnp.where(jnp.arange(n_slabs) < q_cols // IN_PROJ_COLS, q_scale, 1.0).astype(F32)
    if transposed:
        slab = (IN_PROJ_COLS, D_MODEL)
        w_spec = pl.BlockSpec((None,) + slab, lambda j, i: (layer, j, 0))
    else:
        slab = (D_MODEL, IN_PROJ_COLS)
        w_spec = pl.BlockSpec((None,) + slab, lambda j, i: (layer, 0, j))
    return pl.pallas_call(
        functools.partial(_in_proj_kernel, transposed),
        grid=(n_slabs, IN_PROJ_ROW_STEPS),
        in_specs=[pl.BlockSpec(memory_space=pltpu.SMEM),
                  pl.BlockSpec((row_tile, D_MODEL), lambda j, i: (i, 0)), w_spec],
        out_specs=pl.BlockSpec((row_tile, IN_PROJ_COLS), lambda j, i: (i, j)),
        out_shape=jax.ShapeDtypeStruct((rows, n_cols), out_dtype),
        scratch_shapes=[pltpu.VMEM(slab, BF16)],
        compiler_params=_params(("parallel", "arbitrary")),
        name=name,
    )(scales, hn, w_stack)


def _out_proj_kernel(sub, n_norms, with_gate, keep_x, *refs):
    og_refs, w_ref, x_refs, refs = refs[:sub], refs[sub], refs[sub + 1:2 * sub + 1], refs[2 * sub + 1:]
    nw_refs, gate_refs, outs = _split_refs(refs, n_norms, with_gate)
    if sub == 1:
        og, x = og_refs[0][...], x_refs[0][...]
    else:
        og = jnp.concatenate([r[...] for r in og_refs], axis=0)
        x = jnp.concatenate([r[...] for r in x_refs], axis=0)
    y = x + jnp.dot(og, w_ref[...], preferred_element_type=F32)
    if keep_x:
        outs[0][...] = y
        outs = outs[1:]
    _emit_normed(y, nw_refs, gate_refs, outs[:n_norms], outs[n_norms] if with_gate else None)


def _out_proj(og, w, x, norms, gate, keep_x, normed_dtype, name, gather=None):
    if gather is None:
        tile_rows, n_tiles, sub = ROW_TILE, x.shape[0] // ROW_TILE, 1
        in_row_specs = [pl.BlockSpec((ROW_TILE, D_MODEL), lambda i: (i, 0))]
    else:
        tile_rows, n_tiles, first_block = gather
        sub = tile_rows // CHUNK
        in_row_specs = [pl.BlockSpec((CHUNK, D_MODEL),
                                     functools.partial(lambda k, i: (first_block(i) + k, 0), k))
                        for k in range(sub)]
    rows = tile_rows * n_tiles
    row_spec = pl.BlockSpec((tile_rows, D_MODEL), lambda i: (i, 0))
    norm_specs, norm_args = _norm_specs(norms, gate, lambda i: (0, 0))
    out_specs, out_shape = [], []
    if keep_x:
        out_specs.append(row_spec)
        out_shape.append(jax.ShapeDtypeStruct((rows, D_MODEL), F32))
    for _ in norms:
        out_specs.append(row_spec)
        out_shape.append(jax.ShapeDtypeStruct((rows, D_MODEL), normed_dtype))
    if gate is not None:
        out_specs.append(pl.BlockSpec((tile_rows, GLA_KEY_DIM), lambda i: (i, 0)))
        out_shape.append(jax.ShapeDtypeStruct((rows, GLA_KEY_DIM), F32))
    w_spec = pl.BlockSpec((D_MODEL, D_MODEL), lambda i: (0, 0), pipeline_mode=pl.Buffered(1))
    return pl.pallas_call(
        functools.partial(_out_proj_kernel, sub, len(norms), gate is not None, keep_x),
        grid=(n_tiles,),
        in_specs=in_row_specs + [w_spec] + in_row_specs + norm_specs,
        out_specs=out_specs,
        out_shape=out_shape,
        compiler_params=_params(("parallel",)),
        name=name,
    )(*([og] * sub), w, *([x] * sub), *norm_args)


def _exact_block(qs, k, b, n, width, col0):
    lane = lax.broadcasted_iota(jnp.int32, (n, width), 1)
    acc = jnp.zeros((n, width), F32)
    for j in range(n):
        t = qs * k[j:j + 1] * jnp.exp(jnp.minimum(b - b[j:j + 1], 0.0))
        acc = jnp.where(lane == col0 + j, jnp.sum(t, axis=-1, keepdims=True), acc)
    return acc


def _masked_row_sums(mask_bf16, g):
    g_hi = g.astype(BF16)
    r1 = g - g_hi.astype(F32)
    g_mid = r1.astype(BF16)
    g_lo = (r1 - g_mid.astype(F32)).astype(BF16)
    return (jnp.dot(mask_bf16, g_hi, preferred_element_type=F32)
            + jnp.dot(mask_bf16, g_mid, preferred_element_type=F32)
            + jnp.dot(mask_bf16, g_lo, preferred_element_type=F32))


def _column_of(row_vec, n):
    return jnp.broadcast_to(row_vec, (LANES, n)).T[:, 0:1]


def _head_norm_gate(o, onorm, gate):
    ms = jnp.mean(o * o, axis=-1, keepdims=True)
    return (o * lax.rsqrt(ms + RMS_EPS) * onorm * _silu(gate)).astype(BF16)


def _intra_chunk_scores(qs, k, k_bf, b, row_xor_col):
    c_len = qs.shape[0]
    leaf = GLA_LEAF
    leaf_row = lax.broadcasted_iota(jnp.int32, (leaf, c_len), 0)
    leaf_col = lax.broadcasted_iota(jnp.int32, (leaf, c_len), 1)
    lane_in_leaf = [(leaf_col & (leaf - 1)) == j for j in range(leaf)]

    att_rows = []
    for blk0 in range(0, c_len // leaf, LEAF_BATCH):
        lhs = []
        for blk in range(blk0, blk0 + LEAF_BATCH):
            q_blk = qs[blk * leaf:(blk + 1) * leaf]
            b_blk = b[blk * leaf:(blk + 1) * leaf]
            lhs += [q_blk * jnp.exp2(jnp.minimum(b_blk - b_blk[j:j + 1], 0.0))
                    for j in range(leaf)]
        r = lax.dot_general(jnp.concatenate(lhs, axis=0).astype(BF16), k_bf, NT_DIMS,
                            preferred_element_type=F32)
        for i, blk in enumerate(range(blk0, blk0 + LEAF_BATCH)):
            acc = jnp.zeros((leaf, c_len), F32)
            for j in range(leaf):
                r0 = (i * leaf + j) * leaf
                acc = jnp.where(lane_in_leaf[j], r[r0:r0 + leaf], acc)
            rel = leaf_col - blk * leaf
            att_rows.append(jnp.where((rel >= 0) & (rel <= leaf_row), acc, 0.0))
    att = jnp.concatenate(att_rows, axis=0)

    m = c_len // 2
    while m >= leaf:
        q_parts, k_parts = [], []
        zeros = jnp.zeros((m, GLA_DK), F32)
        for base in range(0, c_len, 2 * m):
            ref = b[base + m - 1:base + m]
            lo = slice(base, base + m)
            hi = slice(base + m, base + 2 * m)
            k_parts += [k[lo] * jnp.exp2(ref - b[lo]), zeros]
            q_parts += [zeros, qs[hi] * jnp.exp2(b[hi] - ref)]
        qt = jnp.concatenate(q_parts, axis=0).astype(BF16)
        kt = jnp.concatenate(k_parts, axis=0).astype(BF16)
        a = lax.dot_general(qt, kt, NT_DIMS, preferred_element_type=F32)
        att = att + jnp.where(row_xor_col < 2 * m, a, 0.0)
        m //= 2
    return att


def _gla_prompt_kernel(q_ref, k_ref, v_ref, gate_ref, g_ref, onorm_ref, og_in_ref,
                       og_ref, s_ref, b_scr, qd_scr, att_scr):
    del og_in_ref
    c_len = CHUNK
    heads = [(slice(h * GLA_DK, (h + 1) * GLA_DK), slice(h * GLA_DV, (h + 1) * GLA_DV))
             for h in range(GLA_HEADS)]

    @pl.when(pl.program_id(1) == 0)
    def _():
        s_ref[...] = jnp.zeros_like(s_ref)

    row = lax.broadcasted_iota(jnp.int32, (c_len, c_len), 0)
    col = lax.broadcasted_iota(jnp.int32, (c_len, c_len), 1)
    lower = row >= col
    tri = jnp.where(lower, 1.0, 0.0).astype(BF16)
    onorm = onorm_ref[...]
    chunk_rows = [slice(c * c_len, (c + 1) * c_len) for c in range(q_ref.shape[0] // c_len)]

    total = None
    for rows in chunk_rows:
        b_c = _masked_row_sums(tri, g_ref[rows, :]) * LOG2_E
        b_scr[rows, :] = b_c
        qd_scr[rows, :] = (q_ref[rows, :].astype(F32) * jnp.exp2(b_c)).astype(BF16)
        last = b_c[c_len - 1:c_len]
        total = last if total is None else jnp.minimum(total, last)

    bounded = jnp.max(-total) <= GLA_BOUNDED_RANGE

    @pl.when(bounded)
    def _():
        for c, rows in enumerate(chunk_rows):
            for h, (ks, _) in enumerate(heads):
                k_inv = (k_ref[rows, ks].astype(F32) * jnp.exp2(-b_scr[rows, ks])).astype(BF16)
                a = lax.dot_general(qd_scr[rows, ks], k_inv, NT_DIMS,
                                    preferred_element_type=F32)
                att_scr[c * GLA_HEADS + h] = jnp.where(lower, a, 0.0).astype(BF16)

    @pl.when(jnp.logical_not(bounded))
    def _():
        for c, rows in enumerate(chunk_rows):
            for h, (ks, _) in enumerate(heads):
                k_bf = k_ref[rows, ks]
                att_scr[c * GLA_HEADS + h] = _intra_chunk_scores(
                    q_ref[rows, ks].astype(F32), k_bf.astype(F32), k_bf,
                    b_scr[rows, ks], row ^ col).astype(BF16)

    for c, rows in enumerate(chunk_rows):
        for h, (ks, vs) in enumerate(heads):
            k = k_ref[rows, ks].astype(F32)
            v = v_ref[rows, vs]
            b = b_scr[rows, ks]
            b_last = b[c_len - 1:c_len]
            state = s_ref[0, h]

            o = jnp.dot(qd_scr[rows, ks], state.astype(BF16), preferred_element_type=F32)
            o = o + jnp.dot(att_scr[c * GLA_HEADS + h], v, preferred_element_type=F32)

            kd_t = (k * jnp.exp2(b_last - b)).T.astype(BF16)
            s_ref[0, h] = (state * _column_of(jnp.exp2(b_last), GLA_DK)
                           + jnp.dot(kd_t, v, preferred_element_type=F32))

            og_ref[rows, vs] = _head_norm_gate(o, onorm, gate_ref[rows, vs].astype(F32))


def _gla_prompt(proj, g, onorm, og, n_batch, chunks):
    rows = proj.shape[0]
    per_step = next(n for n in (GLA_CHUNKS_PER_STEP, 1) if chunks % n == 0)
    steps = chunks // per_step
    tile = per_step * CHUNK
    blk = lambda col: (lambda b, c: (b * steps + c, col))
    return pl.pallas_call(
        _gla_prompt_kernel,
        grid=(n_batch, steps),
        in_specs=[
            pl.BlockSpec((tile, GLA_KEY_DIM), blk(0)),
            pl.BlockSpec((tile, GLA_KEY_DIM), blk(1)),
            pl.BlockSpec((tile, GLA_VAL_DIM), blk(1)),
            pl.BlockSpec((tile, GLA_VAL_DIM), blk(2)),
            pl.BlockSpec((tile, GLA_KEY_DIM), blk(0)),
            pl.BlockSpec((1, GLA_DV), lambda b, c: (0, 0)),
            pl.BlockSpec(memory_space=pl.ANY),
        ],
        out_specs=[
            pl.BlockSpec((tile, GLA_VAL_DIM), blk(0)),
            pl.BlockSpec((1, GLA_HEADS, GLA_DK, GLA_DV), lambda b, c: (b, 0, 0, 0)),
        ],
        out_shape=[
            jax.ShapeDtypeStruct((rows, GLA_VAL_DIM), BF16),
            jax.ShapeDtypeStruct((n_batch, GLA_HEADS, GLA_DK, GLA_DV), F32),
        ],
        scratch_shapes=[pltpu.VMEM((tile, GLA_KEY_DIM), F32),
                        pltpu.VMEM((tile, GLA_KEY_DIM), BF16),
                        pltpu.VMEM((per_step * GLA_HEADS, CHUNK, CHUNK), BF16)],
        input_output_aliases={6: 0},
        compiler_params=_params(("parallel", "arbitrary")),
        name="gla_prompt",
    )(proj, proj, proj, proj, g, onorm, og)


def _gla_sample_kernel(t_len, q_ref, k_ref, v_ref, gate_ref, g_ref, onorm_ref, s0_ref,
                       og_in_ref, *rest):
    og_ref, s_ref = rest[-2], rest[-1]
    n = GLA_SEQ_GROUP * t_len
    row = lax.broadcasted_iota(jnp.int32, (n, LANES), 0)
    col = lax.broadcasted_iota(jnp.int32, (n, LANES), 1)
    causal = ((row // t_len) == (col // t_len)) & (row >= col)
    row_seq = lax.broadcasted_iota(jnp.int32, (n, 1), 0) // t_len

    qs = q_ref[...].astype(F32)
    k = k_ref[...].astype(F32)
    pad_k = jnp.zeros((LANES - n, GLA_DK), F32)
    v_pad = jnp.concatenate([v_ref[...], jnp.zeros((LANES - n, GLA_DV), BF16)], axis=0)
    b = _masked_row_sums(jnp.where(causal, 1.0, 0.0).astype(BF16),
                         jnp.concatenate([g_ref[...], pad_k], axis=0))
    att = jnp.where(causal, _exact_block(qs, k, b, n, LANES, 0), 0.0)
    o = jnp.dot(att.astype(BF16), v_pad, preferred_element_type=F32)

    q_dec = (qs * jnp.exp(b)).astype(BF16)
    for u in range(GLA_SEQ_GROUP):
        state = s0_ref[0, u, 0]
        o_u = jnp.dot(q_dec, state.astype(BF16), preferred_element_type=F32)
        o = o + jnp.where(row_seq == u, o_u, 0.0)
        b_last = b[(u + 1) * t_len - 1:(u + 1) * t_len]
        kd = jnp.where(row_seq == u, k * jnp.exp(jnp.minimum(b_last - b, 0.0)), 0.0)
        kd_t = jnp.concatenate([kd, pad_k], axis=0).T.astype(BF16)
        s_ref[0, u, 0] = (state * _column_of(jnp.exp(b_last), GLA_DK)
                          + jnp.dot(kd_t, v_pad, preferred_element_type=F32))

    og_ref[...] = _head_norm_gate(o, onorm_ref[...], gate_ref[...].astype(F32))


def _gla_sample(proj, g, onorm, state_in, og, s_out, layer, n_seq, t_len, row0):
    n = GLA_SEQ_GROUP * t_len
    rb0 = row0 // n
    n_layers = state_in.shape[0]
    key_blocks = GLA_KEY_DIM // GLA_DK
    val_blocks = GLA_VAL_DIM // GLA_DV
    st_spec = pl.BlockSpec((1, GLA_SEQ_GROUP, 1, GLA_DK, GLA_DV),
                           lambda s, h: (layer, s, h, 0, 0))
    in_specs = [
        pl.BlockSpec((n, GLA_DK), lambda s, h: (rb0 + s, h)),
        pl.BlockSpec((n, GLA_DK), lambda s, h: (rb0 + s, key_blocks + h)),
        pl.BlockSpec((n, GLA_DV), lambda s, h: (rb0 + s, val_blocks + h)),
        pl.BlockSpec((n, GLA_DV), lambda s, h: (rb0 + s, 2 * val_blocks + h)),
        pl.BlockSpec((n, GLA_DK), lambda s, h: (rb0 + s, h)),
        pl.BlockSpec((1, GLA_DV), lambda s, h: (0, 0)),
        st_spec,
        pl.BlockSpec(memory_space=pl.ANY),
    ]
    args = [proj, proj, proj, proj, g, onorm, state_in, og]
    aliases = {7: 0}
    if s_out is not None:
        in_specs.append(pl.BlockSpec(memory_space=pl.ANY))
        args.append(s_out)
        aliases[8] = 1
    return pl.pallas_call(
        functools.partial(_gla_sample_kernel, t_len),
        grid=(n_seq // GLA_SEQ_GROUP, GLA_HEADS),
        in_specs=in_specs,
        out_specs=[
            pl.BlockSpec((n, GLA_DV), lambda s, h: (rb0 + s, h)),
            st_spec,
        ],
        out_shape=[
            jax.ShapeDtypeStruct(og.shape, BF16),
            jax.ShapeDtypeStruct((n_layers, n_seq, GLA_HEADS, GLA_DK, GLA_DV), F32),
        ],
        input_output_aliases=aliases,
        compiler_params=_params(("parallel", "parallel")),
        name="gla_sample",
    )(*args)


def _nt_dot(a, b):
    return lax.dot_general(a, b, NT_DIMS, preferred_element_type=F32)


def _lane_padded_copies(pair, fill):
    low = lax.broadcasted_iota(jnp.int32, pair.shape, 1) < HEAD_DIM
    swapped = pltpu.roll(pair, HEAD_DIM, axis=1).astype(BF16)
    pair = pair.astype(BF16)
    fill = jnp.full((), fill, BF16)
    return ((jnp.where(low, pair, fill), jnp.where(low, fill, swapped)),
            (jnp.where(low, swapped, fill), jnp.where(low, fill, pair)))


def _natural_kv_set(keys, vals, mask):
    slot0 = lax.broadcasted_iota(jnp.int32, keys[0].shape, 0) == 0

    def score_fns(cpair):
        return tuple(tuple(functools.partial(_nt_dot, b=k) for k in pads)
                     for pads in _lane_padded_copies(keys[cpair], 0.0))

    def value_fns(cpair):
        pads_by_head = _lane_padded_copies(jnp.where(slot0, 0.0, vals[cpair]), 1.0)
        return tuple(tuple((lambda p, v=v: jnp.dot(p, v, preferred_element_type=F32))
                           for v in pads) for pads in pads_by_head)

    return score_fns, value_fns, mask


def _window_kv_set(kt_ref, vt_ref, k_new_t, v_new_t, mask):
    win = kt_ref.shape[1]
    lane0 = lax.broadcasted_iota(jnp.int32, (HEAD_DIM, win), 1) == 0

    def head_keys(old_ref, new_t, cpair, parity, zero_slot0):
        r0 = (2 * cpair + parity) * HEAD_DIM
        old = old_ref[r0:r0 + HEAD_DIM, :]
        if zero_slot0:
            old = jnp.where(lane0, 0.0, old)
        new = new_t[cpair][parity * HEAD_DIM:(parity + 1) * HEAD_DIM, :]
        return jnp.concatenate([old, new], axis=1).astype(BF16)

    def stacks(head, other):
        return (jnp.concatenate([head, other], axis=0), jnp.concatenate([other, head], axis=0))

    def score_fns(cpair):
        fns = []
        for parity in range(2):
            head = head_keys(kt_ref, k_new_t, cpair, parity, False)
            fns.append(tuple(
                (lambda q, k=k: jnp.dot(q, k, preferred_element_type=F32))
                for k in stacks(head, jnp.zeros_like(head))))
        return tuple(fns)

    def value_fns(cpair):
        fns = []
        for parity in range(2):
            head = head_keys(vt_ref, v_new_t, cpair, parity, True)
            fns.append(tuple(functools.partial(_nt_dot, b=v)
                             for v in stacks(head, jnp.ones_like(head))))
        return tuple(fns)

    return score_fns, value_fns, mask


def _attend_rows(q_ref, gate_ref, og_ref, sink_ref, kv_sets, s_scr, p_scr):
    n_rows = q_ref.shape[0]
    group = N_Q_HEADS // N_KV_HEADS
    low_out = lax.broadcasted_iota(jnp.int32, (n_rows, LANES), 1) < HEAD_DIM
    lane0 = lax.broadcasted_iota(jnp.int32, (n_rows, LANES), 1) == 0

    def kv_heads():
        for cpair in range(KV_WIDTH // LANES):
            for parity in range(2):
                first = (2 * cpair + parity) * (group // 2)
                yield cpair, parity, list(range(first, first + group // 2))

    def slot(set_index, qc, half):
        return (set_index * (Q_WIDTH // LANES) + qc) * 2 + half

    for si, (score_fns, _, _) in enumerate(kv_sets):
        fns = None
        for cpair, parity, qcs in kv_heads():
            if parity == 0:
                fns = score_fns(cpair)
            q_stack = jnp.concatenate(
                [q_ref[:, qc * LANES:(qc + 1) * LANES] for qc in qcs], axis=0)
            for half, score in enumerate(fns[parity]):
                s = score(q_stack)
                for j, qc in enumerate(qcs):
                    s_scr[slot(si, qc, half)] = s[j * n_rows:(j + 1) * n_rows]

    for si, (_, _, mask) in enumerate(kv_sets):
        mask_first, mask_rest = mask[:, :LANES], mask[:, LANES:]
        for qc in range(Q_WIDTH // LANES):
            for half in range(2):
                sink = sink_ref[2 * qc + half] * LOG2_E
                s = s_scr[slot(si, qc, half)]
                first = jnp.where(mask_first, s[:, :LANES],
                                  jnp.where(lane0, sink, NEG_INF))
                rest = jnp.where(mask_rest, s[:, LANES:], NEG_INF)
                m = jnp.max(jnp.maximum(first, rest), axis=-1, keepdims=True)
                p_scr[slot(si, qc, half)] = jnp.exp2(
                    jnp.concatenate([first, rest], axis=1) - m).astype(BF16)

    def store(qc, o):
        cols = slice(qc * LANES, (qc + 1) * LANES)
        og_ref[:, cols] = (o * _silu(gate_ref[:, cols].astype(F32))).astype(BF16)

    out_cols = [None] * (Q_WIDTH // LANES)
    for si, (_, value_fns, _) in enumerate(kv_sets):
        fns = None
        for cpair, parity, qcs in kv_heads():
            if parity == 0:
                fns = value_fns(cpair)
            stacked = [weigh(jnp.concatenate([p_scr[slot(si, qc, half)] for qc in qcs],
                                             axis=0))
                       for half, weigh in enumerate(fns[parity])]
            for j, qc in enumerate(qcs):
                oa, ob = (o[j * n_rows:(j + 1) * n_rows] for o in stacked)
                sums = pltpu.roll(jnp.where(low_out, ob, oa), HEAD_DIM, axis=1)
                o2 = jnp.where(low_out, oa, ob) * (1.0 / sums)
                if len(kv_sets) == 1:
                    store(qc, o2)
                else:
                    out_cols[qc] = o2 if out_cols[qc] is None else out_cols[qc] + o2

    if len(kv_sets) > 1:
        for qc, o in enumerate(out_cols):
            store(qc, o)


def _attention_scratch(n_sets, n_rows, n_keys):
    n = n_sets * N_Q_HEADS
    return [pltpu.VMEM((n, n_rows, n_keys), F32), pltpu.VMEM((n, n_rows, n_keys), BF16)]


def _swa_prompt_kernel(front, sink_ref, q_ref, gate_ref, kc_ref, kp_ref, vc_ref, vp_ref,
                       og_in_ref, og_ref, s_scr, p_scr):
    del og_in_ref
    row = lax.broadcasted_iota(jnp.int32, (CHUNK, 2 * CHUNK), 0)
    col = lax.broadcasted_iota(jnp.int32, (CHUNK, 2 * CHUNK), 1)
    band = (col > row) & (col <= row + WINDOW)
    n_sub = q_ref.shape[0] // CHUNK
    lane_pairs = [slice(c * LANES, (c + 1) * LANES) for c in range(KV_WIDTH // LANES)]
    for r in range(n_sub):
        blk = pl.program_id(1) * n_sub + r
        rows = slice(r * CHUNK, (r + 1) * CHUNK)
        before = slice((r - 1) * CHUNK, r * CHUNK)
        mask = band & ((blk - 1) * CHUNK + col >= front)
        keys = [jnp.concatenate([kp_ref[:, c] if r == 0 else kc_ref[before, c],
                                 kc_ref[rows, c]], axis=0) for c in lane_pairs]
        vals = [jnp.concatenate([vp_ref[:, c] if r == 0 else vc_ref[before, c],
                                 vc_ref[rows, c]], axis=0) for c in lane_pairs]
        _attend_rows(q_ref.at[rows, :], gate_ref.at[rows, :], og_ref.at[rows, :], sink_ref,
                     [_natural_kv_set(keys, vals, mask)], s_scr, p_scr)


def _swa_prompt(pq, kv, sinks, og, n_batch, chunks, front):
    per_step = next(n for n in (SWA_BLOCKS_PER_STEP, 1) if chunks % n == 0)
    steps = chunks // per_step
    tile = per_step * CHUNK
    cur = lambda col: (lambda b, i: (b * steps + i, col))
    prev = lambda col: (lambda b, i: (b * chunks + jnp.maximum(i * per_step - 1, 0), col))
    return pl.pallas_call(
        functools.partial(_swa_prompt_kernel, front),
        grid=(n_batch, steps),
        in_specs=[
            pl.BlockSpec(memory_space=pltpu.SMEM),
            pl.BlockSpec((tile, Q_WIDTH), cur(0)),
            pl.BlockSpec((tile, Q_WIDTH), cur(1)),
            pl.BlockSpec((tile, KV_WIDTH), cur(0)),
            pl.BlockSpec((CHUNK, KV_WIDTH), prev(0)),
            pl.BlockSpec((tile, KV_WIDTH), cur(1)),
            pl.BlockSpec((CHUNK, KV_WIDTH), prev(1)),
            pl.BlockSpec(memory_space=pl.ANY),
        ],
        out_specs=pl.BlockSpec((tile, Q_WIDTH), cur(0)),
        out_shape=jax.ShapeDtypeStruct(og.shape, BF16),
        scratch_shapes=_attention_scratch(1, CHUNK, 2 * CHUNK),
        input_output_aliases={7: 0},
        compiler_params=_params(("parallel", "parallel")),
        name="swa_prompt",
    )(sinks, pq, pq, kv, kv, kv, kv, og)


def _swa_sample_kernel(t_len, sink_ref, q_ref, gate_ref, kn_ref, vn_ref, ckt_ref, cvt_ref,
                       og_in_ref, og_ref, s_scr, p_scr):
    del og_in_ref
    n = SEQ_GROUP * t_len
    win = ckt_ref.shape[2]
    row = lax.broadcasted_iota(jnp.int32, (n, 2 * win), 0)
    col = lax.broadcasted_iota(jnp.int32, (n, 2 * win), 1)
    t = row % t_len
    new = col - win
    in_window = ((col < win) & (col > t)) | (
        (new >= 0) & (new < n) & ((new // t_len) == (row // t_len)) & ((new % t_len) <= t))
    pad = jnp.zeros((win - n, LANES), F32)
    lane_pairs = [slice(c * LANES, (c + 1) * LANES) for c in range(KV_WIDTH // LANES)]
    k_new_t = [jnp.concatenate([kn_ref[:, c], pad], axis=0).T for c in lane_pairs]
    v_new_t = [jnp.concatenate([vn_ref[:, c], pad], axis=0).T for c in lane_pairs]
    kv_sets = [_window_kv_set(ckt_ref.at[u], cvt_ref.at[u], k_new_t, v_new_t,
                              in_window & ((row // t_len) == u))
               for u in range(SEQ_GROUP)]
    _attend_rows(q_ref, gate_ref, og_ref, sink_ref, kv_sets, s_scr, p_scr)


def _swa_sample(pq, kv, cache_kt, cache_vt, sinks, og, n_seq, t_len, row0):
    n = SEQ_GROUP * t_len
    rb0 = row0 // n
    win = cache_kt.shape[2]
    return pl.pallas_call(
        functools.partial(_swa_sample_kernel, t_len),
        grid=(n_seq // SEQ_GROUP,),
        in_specs=[
            pl.BlockSpec(memory_space=pltpu.SMEM),
            pl.BlockSpec((n, Q_WIDTH), lambda s: (rb0 + s, 0)),
            pl.BlockSpec((n, Q_WIDTH), lambda s: (rb0 + s, 1)),
            pl.BlockSpec((n, KV_WIDTH), lambda s: (rb0 + s, 0)),
            pl.BlockSpec((n, KV_WIDTH), lambda s: (rb0 + s, 1)),
            pl.BlockSpec((SEQ_GROUP, KV_WIDTH, win), lambda s: (s, 0, 0)),
            pl.BlockSpec((SEQ_GROUP, KV_WIDTH, win), lambda s: (s, 0, 0)),
            pl.BlockSpec(memory_space=pl.ANY),
        ],
        out_specs=pl.BlockSpec((n, Q_WIDTH), lambda s: (rb0 + s, 0)),
        out_shape=jax.ShapeDtypeStruct(og.shape, BF16),
        scratch_shapes=_attention_scratch(SEQ_GROUP, n, 2 * win),
        input_output_aliases={7: 0},
        compiler_params=_params(("parallel",)),
        name="swa_sample",
    )(sinks, pq, pq, kv, kv, cache_kt, cache_vt, og)


def kernel(x_prompt, x_sample, state_gla, cache_k_win, cache_v_win, meta_tokens, norm_a, w_in_a, w_gk2_a, b_gk2_a, onorm_a, w_out_a, norm_kv, w_kv, norm_b, w_in_b, sinks_b, w_out_b, norm_f):
    n_batch, seq, _ = x_prompt.shape
    n_seq, t_len, _ = x_sample.shape
    win = cache_k_win.shape[1]
    n_a = w_in_a.shape[0]
    n_b = w_in_b.shape[0]
    assert win == WINDOW == CHUNK and n_seq % SEQ_GROUP == 0 and n_seq % GLA_SEQ_GROUP == 0

    length = seq + N_META
    front = (-length) % CHUNK
    lp = length + front
    chunks = lp // CHUNK
    srow = n_batch * lp
    ns = n_seq * t_len
    rows = -(-(srow + ns) // ROW_TILE) * ROW_TILE

    assert front + N_META == CHUNK and ns == CHUNK and seq % CHUNK == 0
    assert rows % CHUNK == 0 and rows % (16 * IN_PROJ_ROW_STEPS) == 0

    w_in_a_t = jnp.swapaxes(w_in_a, 1, 2)

    def gate_weights(l):
        w_gate = jnp.pad(w_in_a_t[l, GLA_MAIN_WIDTH:], ((0, LANES - GATE_RANK), (0, 0)))
        w_gk = jnp.pad(w_gk2_a[l], ((0, LANES - GATE_RANK), (0, 0))).astype(BF16)
        return w_gate, w_gk, b_gk2_a[l][None]

    x, hn, g = _embed(x_prompt.reshape(n_batch * seq, D_MODEL), x_sample.reshape(ns, D_MODEL),
                      meta_tokens, norm_a[0], gate_weights(0), rows, chunks, n_batch, front)
    og = jnp.zeros((rows, D_MODEL), BF16)

    states_p = []
    states_s = None
    for l in range(n_a):
        proj = _in_proj(hn, w_in_a_t, l, GLA_MAIN_WIDTH, BF16, True, "gla_in_proj",
                        q_cols=GLA_KEY_DIM, q_scale=GLA_DK ** -0.5)
        onorm = onorm_a[l][None]
        og, s_p = _gla_prompt(proj, g, onorm, og, n_batch, chunks)
        og, states_s = _gla_sample(proj, g, onorm, state_gla, og, states_s, l,
                                   n_seq, t_len, srow)
        states_p.append(s_p)
        w_out = w_out_a[l].astype(BF16)
        if l + 1 < n_a:
            x, hn, g = _out_proj(og, w_out, x, [norm_a[l + 1]], gate_weights(l + 1), True,
                                 BF16, "gla_out_proj")
        else:
            x, hn_kv, hn = _out_proj(og, w_out, x, [norm_kv, norm_b[0]], None, True,
                                     BF16, "gla_out_proj")

    kv = _in_proj(hn_kv, w_kv[None], 0, 2 * KV_WIDTH, F32, False, "kv_proj")
    cache_k = jnp.transpose(cache_k_win, (0, 2, 3, 1)).reshape(n_seq, KV_WIDTH, win)
    cache_v = jnp.transpose(cache_v_win, (0, 2, 3, 1)).reshape(n_seq, KV_WIDTH, win)
    for j in range(n_b):
        pq = _in_proj(hn, w_in_b, j, 2 * Q_WIDTH, BF16, False, "swa_in_proj",
                      q_cols=Q_WIDTH, q_scale=HEAD_DIM ** -0.5 * LOG2_E)
        og = _swa_prompt(pq, kv, sinks_b[j], og, n_batch, chunks, front)
        og = _swa_sample(pq, kv, cache_k, cache_v, sinks_b[j], og, n_seq, t_len, srow)
        w_out = w_out_b[j].astype(BF16)
        if j + 1 < n_b:
            x, hn = _out_proj(og, w_out, x, [norm_b[j + 1]], None, True, BF16, "swa_out_proj")

    tiles_per_batch = seq // ROW_TILE
    lead_blocks = (lp - seq) // CHUNK

    def prompt_first_block(t):
        return ((t // tiles_per_batch) * chunks + lead_blocks
                + (t % tiles_per_batch) * (ROW_TILE // CHUNK))

    (y_prompt,) = _out_proj(og, w_out, x, [norm_f], None, False, F32, "final_prompt",
                            gather=(ROW_TILE, n_batch * tiles_per_batch, prompt_first_block))
    (y_sample,) = _out_proj(og, w_out, x, [norm_f], None, False, F32, "final_sample",
                            gather=(CHUNK, ns // CHUNK, lambda t: srow // CHUNK + t))
    y_prompt = y_prompt.reshape(n_batch, seq, D_MODEL)
    y_sample = y_sample.reshape(n_seq, t_len, D_MODEL)
    kv_p = jnp.stack([kv[(b + 1) * lp - win:(b + 1) * lp] for b in range(n_batch)])
    kv_p = kv_p.reshape(n_batch, win, 2, N_KV_HEADS, HEAD_DIM)
    kv_s = kv[srow:srow + ns].reshape(n_seq, t_len, 2, N_KV_HEADS, HEAD_DIM)
    k_win_s = jnp.concatenate([cache_k_win, kv_s[:, :, 0]], axis=1)[:, -win:]
    v_win_s = jnp.concatenate([cache_v_win, kv_s[:, :, 1]], axis=1)[:, -win:]
    return (y_prompt, y_sample, jnp.stack(states_p), states_s,
            kv_p[:, :, 0], kv_p[:, :, 1], k_win_s, v_win_s)
```

```python
import functools

import jax
import jax.numpy as jnp
from jax import lax
from jax.experimental import pallas as pl
from jax.experimental.pallas import tpu as pltpu

F32 = jnp.float32
BF16 = jnp.bfloat16

D_MODEL = 2048
N_META = 16
GLA_HEADS = 4
GLA_DK = 256
GLA_DV = 512
GLA_KEY_DIM = GLA_HEADS * GLA_DK
GLA_VAL_DIM = GLA_HEADS * GLA_DV
GLA_MAIN_WIDTH = 2 * GLA_KEY_DIM + 2 * GLA_VAL_DIM
GATE_RANK = 16
GATE_LOGIT_NORM = 16.0
HEAD_DIM = 64
N_Q_HEADS = 32
N_KV_HEADS = 8
Q_WIDTH = N_Q_HEADS * HEAD_DIM
KV_WIDTH = N_KV_HEADS * HEAD_DIM
WINDOW = 128
RMS_EPS = 1e-6
NEG_INF = -1e30
LOG2_E = 1.4426950408889634

LANES = 128
CHUNK = 128
GLA_LEAF = 8
LEAF_BATCH = 4
GLA_CHUNKS_PER_STEP = 3
GLA_BOUNDED_RANGE = 60.0
ROW_TILE = 512
IN_PROJ_COLS = 1024
IN_PROJ_ROW_STEPS = 8
SEQ_GROUP = 4
GLA_SEQ_GROUP = 8
VMEM_LIMIT = 48 * 1024 * 1024

NT_DIMS = (((1,), (1,)), ((), ()))


def _silu(x):
    return x * (1.0 / (1.0 + jnp.exp(-x)))


def _params(sem):
    return pltpu.CompilerParams(dimension_semantics=sem, vmem_limit_bytes=VMEM_LIMIT)


def _emit_normed(y, nw_refs, gate_refs, hn_refs, g_ref):
    ms = jnp.mean(y * y, axis=-1, keepdims=True)
    yn = y * lax.rsqrt(ms + RMS_EPS)
    for nw_ref, hn_ref in zip(nw_refs, hn_refs):
        hn_ref[...] = (yn * nw_ref[...]).astype(hn_ref.dtype)
    if gate_refs:
        g_ref[...] = _decay_gate(hn_refs[0][...], *gate_refs)


def _split_refs(refs, n_norms, with_gate):
    nw_refs = refs[:n_norms]
    gate_refs = refs[n_norms:n_norms + 3] if with_gate else None
    return nw_refs, gate_refs, refs[n_norms + 3 * with_gate:]


def _norm_specs(norms, gate, index):
    specs = [pl.BlockSpec((1, D_MODEL), index) for _ in norms]
    args = [nw[None] for nw in norms]
    if gate is not None:
        specs += [pl.BlockSpec((LANES, D_MODEL), index),
                  pl.BlockSpec((LANES, GLA_KEY_DIM), index),
                  pl.BlockSpec((1, GLA_KEY_DIM), index)]
        args += list(gate)
    return specs, args


def _embed_kernel(sub, lead_blocks, sample_block, front, with_gate, *refs):
    n_src = sub + 2
    nw_refs, gate_refs, outs = _split_refs(refs[n_src:], 1, with_gate)
    x_ref = outs[0]
    _assemble_rows(x_ref, sub, lead_blocks, sample_block, front, refs[:n_src])
    _emit_normed(x_ref[...], nw_refs, gate_refs, outs[1:2], outs[2] if with_gate else None)


def _assemble_rows(x_ref, sub, lead_blocks, sample_block, front, src_refs):
    xp_refs, xs_ref, meta_ref = src_refs[:sub], src_refs[sub], src_refs[sub + 1]
    for k in range(sub):
        blk = pl.program_id(0) * sub + k
        rows = slice(k * CHUNK, (k + 1) * CHUNK)
        is_lead = functools.reduce(jnp.logical_or, [blk == b for b in lead_blocks])

        @pl.when(is_lead)
        def _():
            x_ref[rows, :] = jnp.concatenate(
                [jnp.zeros((front, D_MODEL), F32), meta_ref[...]], axis=0)

        @pl.when(jnp.logical_and(jnp.logical_not(is_lead), blk < sample_block))
        def _():
            x_ref[rows, :] = xp_refs[k][...]

        @pl.when(blk == sample_block)
        def _():
            x_ref[rows, :] = xs_ref[...]

        @pl.when(blk > sample_block)
        def _():
            x_ref[rows, :] = jnp.zeros((CHUNK, D_MODEL), F32)


def _token_sources(x_prompt2d, x_sample2d, meta, chunks, n_batch):
    sub = ROW_TILE // CHUNK
    prompt_blocks = x_prompt2d.shape[0] // CHUNK
    per_batch = prompt_blocks // n_batch

    def prompt_index(k):
        def index(i):
            blk = i * sub + k
            src = (blk // chunks) * per_batch + (blk % chunks) - 1
            return (jnp.clip(src, 0, prompt_blocks - 1), 0)
        return index

    specs = [pl.BlockSpec((CHUNK, D_MODEL), prompt_index(k)) for k in range(sub)] + [
        pl.BlockSpec((CHUNK, D_MODEL), lambda i: (0, 0)),
        pl.BlockSpec((N_META, D_MODEL), lambda i: (0, 0))]
    return specs, [x_prompt2d] * sub + [x_sample2d, meta]


def _embed(x_prompt2d, x_sample2d, meta, norm, gate, rows, chunks, n_batch, front):
    sub = ROW_TILE // CHUNK
    src_specs, src_args = _token_sources(x_prompt2d, x_sample2d, meta, chunks, n_batch)
    norm_specs, norm_args = _norm_specs([norm], gate, lambda i: (0, 0))
    row_spec = pl.BlockSpec((ROW_TILE, D_MODEL), lambda i: (i, 0))
    out_specs = [row_spec, row_spec]
    out_shape = [jax.ShapeDtypeStruct((rows, D_MODEL), F32),
                 jax.ShapeDtypeStruct((rows, D_MODEL), BF16)]
    if gate is not None:
        out_specs.append(pl.BlockSpec((ROW_TILE, GLA_KEY_DIM), lambda i: (i, 0)))
        out_shape.append(jax.ShapeDtypeStruct((rows, GLA_KEY_DIM), F32))
    return pl.pallas_call(
        functools.partial(_embed_kernel, sub, tuple(b * chunks for b in range(n_batch)),
                          n_batch * chunks, front, gate is not None),
        grid=(rows // ROW_TILE,),
        in_specs=src_specs + norm_specs,
        out_specs=out_specs,
        out_shape=out_shape,
        compiler_params=_params(("parallel",)),
        name="embed",
    )(*src_args, *norm_args)


def _decay_gate(hn, wg_ref, wgk_ref, bgk_ref):
    glr = lax.dot_general(hn, wg_ref[...].astype(BF16), NT_DIMS, preferred_element_type=F32)
    z = jnp.dot(glr.astype(BF16), wgk_ref[...], preferred_element_type=F32) + bgk_ref[...]
    log_sig = jnp.minimum(z, 0.0) - jnp.log(1.0 + jnp.exp(-jnp.abs(z)))
    return log_sig * (1.0 / GATE_LOGIT_NORM)


def _in_proj_kernel(transposed, scale_ref, hn_ref, w_ref, o_ref, w_bf16_ref):
    @pl.when(pl.program_id(1) == 0)
    def _():
        w_bf16_ref[...] = w_ref[...].astype(BF16)

    dims = NT_DIMS if transposed else (((1,), (0,)), ((), ()))
    o = lax.dot_general(hn_ref[...], w_bf16_ref[...], dims, preferred_element_type=F32)
    o_ref[...] = (o * scale_ref[pl.program_id(0)]).astype(o_ref.dtype)


def _in_proj(hn, w_stack, layer, n_cols, out_dtype, transposed, name, q_cols=0, q_scale=1.0):
    rows = hn.shape[0]
    row_tile = rows // IN_PROJ_ROW_STEPS
    n_slabs = n_cols // IN_PROJ_COLS
    assert q_cols % IN_PROJ_COLS == 0
    scales = jnp.where(jnp.arange(n_slabs) < q_cols // IN_PROJ_COLS, q_scale, 1.0).astype(F32)
    if transposed:
        slab = (IN_PROJ_COLS, D_MODEL)
        w_spec = pl.BlockSpec((None,) + slab, lambda j, i: (layer, j, 0))
    else:
        slab = (D_MODEL, IN_PROJ_COLS)
        w_spec = pl.BlockSpec((None,) + slab, lambda j, i: (layer, 0, j))
    return pl.pallas_call(
        functools.partial(_in_proj_kernel, transposed),
        grid=(n_slabs, IN_PROJ_ROW_STEPS),
        in_specs=[pl.BlockSpec(memory_space=pltpu.SMEM),
                  pl.BlockSpec((row_tile, D_MODEL), lambda j, i: (i, 0)), w_spec],
        out_specs=pl.BlockSpec((row_tile, IN_PROJ_COLS), lambda j, i: (i, j)),
        out_shape=jax.ShapeDtypeStruct((rows, n_cols), out_dtype),
        scratch_shapes=[pltpu.VMEM(slab, BF16)],
        compiler_params=_params(("parallel", "arbitrary")),
        name=name,
    )(scales, hn, w_stack)


def _out_proj_kernel(sub, n_norms, with_gate, keep_x, *refs):
    og_refs, w_ref, x_refs, refs = refs[:sub], refs[sub], refs[sub + 1:2 * sub + 1], refs[2 * sub + 1:]
    nw_refs, gate_refs, outs = _split_refs(refs, n_norms, with_gate)
    if sub == 1:
        og, x = og_refs[0][...], x_refs[0][...]
    else:
        og = jnp.concatenate([r[...] for r in og_refs], axis=0)
        x = jnp.concatenate([r[...] for r in x_refs], axis=0)
    y = x + jnp.dot(og, w_ref[...], preferred_element_type=F32)
    if keep_x:
        outs[0][...] = y
        outs = outs[1:]
    _emit_normed(y, nw_refs, gate_refs, outs[:n_norms], outs[n_norms] if with_gate else None)


def _out_proj(og, w, x, norms, gate, keep_x, normed_dtype, name, gather=None):
    if gather is None:
        tile_rows, n_tiles, sub = ROW_TILE, x.shape[0] // ROW_TILE, 1
        in_row_specs = [pl.BlockSpec((ROW_TILE, D_MODEL), lambda i: (i, 0))]
    else:
        tile_rows, n_tiles, first_block = gather
        sub = tile_rows // CHUNK
        in_row_specs = [pl.BlockSpec((CHUNK, D_MODEL),
                                     functools.partial(lambda k, i: (first_block(i) + k, 0), k))
                        for k in range(sub)]
    rows = tile_rows * n_tiles
    row_spec = pl.BlockSpec((tile_rows, D_MODEL), lambda i: (i, 0))
    norm_specs, norm_args = _norm_specs(norms, gate, lambda i: (0, 0))
    out_specs, out_shape = [], []
    if keep_x:
        out_specs.append(row_spec)
        out_shape.append(jax.ShapeDtypeStruct((rows, D_MODEL), F32))
    for _ in norms:
        out_specs.append(row_spec)
        out_shape.append(jax.ShapeDtypeStruct((rows, D_MODEL), normed_dtype))
    if gate is not None:
        out_specs.append(pl.BlockSpec((tile_rows, GLA_KEY_DIM), lambda i: (i, 0)))
        out_shape.append(jax.ShapeDtypeStruct((rows, GLA_KEY_DIM), F32))
    w_stack, layer = w
    w_spec = pl.BlockSpec((None, D_MODEL, D_MODEL), lambda i: (layer, 0, 0),
                          pipeline_mode=pl.Buffered(1))
    return pl.pallas_call(
        functools.partial(_out_proj_kernel, sub, len(norms), gate is not None, keep_x),
        grid=(n_tiles,),
        in_specs=in_row_specs + [w_spec] + in_row_specs + norm_specs,
        out_specs=out_specs,
        out_shape=out_shape,
        compiler_params=_params(("parallel",)),
        name=name,
    )(*([og] * sub), w_stack, *([x] * sub), *norm_args)


def _exact_block(qs, k, b, n, width, col0):
    lane = lax.broadcasted_iota(jnp.int32, (n, width), 1)
    acc = jnp.zeros((n, width), F32)
    for j in range(n):
        t = qs * k[j:j + 1] * jnp.exp(jnp.minimum(b - b[j:j + 1], 0.0))
        acc = jnp.where(lane == col0 + j, jnp.sum(t, axis=-1, keepdims=True), acc)
    return acc


def _masked_row_sums(mask_bf16, g):
    g_hi = g.astype(BF16)
    r1 = g - g_hi.astype(F32)
    g_mid = r1.astype(BF16)
    g_lo = (r1 - g_mid.astype(F32)).astype(BF16)
    return (jnp.dot(mask_bf16, g_hi, preferred_element_type=F32)
            + jnp.dot(mask_bf16, g_mid, preferred_element_type=F32)
            + jnp.dot(mask_bf16, g_lo, preferred_element_type=F32))


def _column_of(row_vec, n):
    return jnp.broadcast_to(row_vec, (LANES, n)).T[:, 0:1]


def _head_norm_gate(o, onorm, gate):
    ms = jnp.mean(o * o, axis=-1, keepdims=True)
    return (o * lax.rsqrt(ms + RMS_EPS) * onorm * _silu(gate)).astype(BF16)


def _intra_chunk_scores(qs, k, k_bf, b, row_xor_col):
    c_len = qs.shape[0]
    leaf = GLA_LEAF
    leaf_row = lax.broadcasted_iota(jnp.int32, (leaf, c_len), 0)
    leaf_col = lax.broadcasted_iota(jnp.int32, (leaf, c_len), 1)
    lane_in_leaf = [(leaf_col & (leaf - 1)) == j for j in range(leaf)]

    att_rows = []
    for blk0 in range(0, c_len // leaf, LEAF_BATCH):
        lhs = []
        for blk in range(blk0, blk0 + LEAF_BATCH):
            q_blk = qs[blk * leaf:(blk + 1) * leaf]
            b_blk = b[blk * leaf:(blk + 1) * leaf]
            lhs += [q_blk * jnp.exp2(jnp.minimum(b_blk - b_blk[j:j + 1], 0.0))
                    for j in range(leaf)]
        r = lax.dot_general(jnp.concatenate(lhs, axis=0).astype(BF16), k_bf, NT_DIMS,
                            preferred_element_type=F32)
        for i, blk in enumerate(range(blk0, blk0 + LEAF_BATCH)):
            acc = jnp.zeros((leaf, c_len), F32)
            for j in range(leaf):
                r0 = (i * leaf + j) * leaf
                acc = jnp.where(lane_in_leaf[j], r[r0:r0 + leaf], acc)
            rel = leaf_col - blk * leaf
            att_rows.append(jnp.where((rel >= 0) & (rel <= leaf_row), acc, 0.0))
    att = jnp.concatenate(att_rows, axis=0)

    m = c_len // 2
    while m >= leaf:
        q_parts, k_parts = [], []
        zeros = jnp.zeros((m, GLA_DK), F32)
        for base in range(0, c_len, 2 * m):
            ref = b[base + m - 1:base + m]
            lo = slice(base, base + m)
            hi = slice(base + m, base + 2 * m)
            k_parts += [k[lo] * jnp.exp2(ref - b[lo]), zeros]
            q_parts += [zeros, qs[hi] * jnp.exp2(b[hi] - ref)]
        qt = jnp.concatenate(q_parts, axis=0).astype(BF16)
        kt = jnp.concatenate(k_parts, axis=0).astype(BF16)
        a = lax.dot_general(qt, kt, NT_DIMS, preferred_element_type=F32)
        att = att + jnp.where(row_xor_col < 2 * m, a, 0.0)
        m //= 2
    return att


def _gla_prompt_kernel(q_ref, k_ref, v_ref, gate_ref, g_ref, onorm_ref, og_in_ref,
                       og_ref, s_ref, b_scr, qd_scr, att_scr):
    del og_in_ref
    c_len = CHUNK
    heads = [(slice(h * GLA_DK, (h + 1) * GLA_DK), slice(h * GLA_DV, (h + 1) * GLA_DV))
             for h in range(GLA_HEADS)]

    @pl.when(pl.program_id(1) == 0)
    def _():
        s_ref[...] = jnp.zeros_like(s_ref)

    row = lax.broadcasted_iota(jnp.int32, (c_len, c_len), 0)
    col = lax.broadcasted_iota(jnp.int32, (c_len, c_len), 1)
    lower = row >= col
    tri = jnp.where(lower, 1.0, 0.0).astype(BF16)
    onorm = onorm_ref[...]
    chunk_rows = [slice(c * c_len, (c + 1) * c_len) for c in range(q_ref.shape[0] // c_len)]

    total = None
    for rows in chunk_rows:
        b_c = _masked_row_sums(tri, g_ref[rows, :]) * LOG2_E
        b_scr[rows, :] = b_c
        qd_scr[rows, :] = (q_ref[rows, :].astype(F32) * jnp.exp2(b_c)).astype(BF16)
        last = b_c[c_len - 1:c_len]
        total = last if total is None else jnp.minimum(total, last)

    bounded = jnp.max(-total) <= GLA_BOUNDED_RANGE

    @pl.when(bounded)
    def _():
        for c, rows in enumerate(chunk_rows):
            for h, (ks, _) in enumerate(heads):
                k_inv = (k_ref[rows, ks].astype(F32) * jnp.exp2(-b_scr[rows, ks])).astype(BF16)
                a = lax.dot_general(qd_scr[rows, ks], k_inv, NT_DIMS,
                                    preferred_element_type=F32)
                att_scr[c * GLA_HEADS + h] = jnp.where(lower, a, 0.0).astype(BF16)

    @pl.when(jnp.logical_not(bounded))
    def _():
        for c, rows in enumerate(chunk_rows):
            for h, (ks, _) in enumerate(heads):
                k_bf = k_ref[rows, ks]
                att_scr[c * GLA_HEADS + h] = _intra_chunk_scores(
                    q_ref[rows, ks].astype(F32), k_bf.astype(F32), k_bf,
                    b_scr[rows, ks], row ^ col).astype(BF16)

    for c, rows in enumerate(chunk_rows):
        for h, (ks, vs) in enumerate(heads):
            k = k_ref[rows, ks].astype(F32)
            v = v_ref[rows, vs]
            b = b_scr[rows, ks]
            b_last = b[c_len - 1:c_len]
            state = s_ref[0, h]

            o = jnp.dot(qd_scr[rows, ks], state.astype(BF16), preferred_element_type=F32)
            o = o + jnp.dot(att_scr[c * GLA_HEADS + h], v, preferred_element_type=F32)

            kd_t = (k * jnp.exp2(b_last - b)).T.astype(BF16)
            s_ref[0, h] = (state * _column_of(jnp.exp2(b_last), GLA_DK)
                           + jnp.dot(kd_t, v, preferred_element_type=F32))

            og_ref[rows, vs] = _head_norm_gate(o, onorm, gate_ref[rows, vs].astype(F32))


def _gla_prompt(proj, g, onorm, og, n_batch, chunks):
    rows = proj.shape[0]
    per_step = next(n for n in (GLA_CHUNKS_PER_STEP, 1) if chunks % n == 0)
    steps = chunks // per_step
    tile = per_step * CHUNK
    blk = lambda col: (lambda b, c: (b * steps + c, col))
    return pl.pallas_call(
        _gla_prompt_kernel,
        grid=(n_batch, steps),
        in_specs=[
            pl.BlockSpec((tile, GLA_KEY_DIM), blk(0)),
            pl.BlockSpec((tile, GLA_KEY_DIM), blk(1)),
            pl.BlockSpec((tile, GLA_VAL_DIM), blk(1)),
            pl.BlockSpec((tile, GLA_VAL_DIM), blk(2)),
            pl.BlockSpec((tile, GLA_KEY_DIM), blk(0)),
            pl.BlockSpec((1, GLA_DV), lambda b, c: (0, 0)),
            pl.BlockSpec(memory_space=pl.ANY),
        ],
        out_specs=[
            pl.BlockSpec((tile, GLA_VAL_DIM), blk(0)),
            pl.BlockSpec((1, GLA_HEADS, GLA_DK, GLA_DV), lambda b, c: (b, 0, 0, 0)),
        ],
        out_shape=[
            jax.ShapeDtypeStruct((rows, GLA_VAL_DIM), BF16),
            jax.ShapeDtypeStruct((n_batch, GLA_HEADS, GLA_DK, GLA_DV), F32),
        ],
        scratch_shapes=[pltpu.VMEM((tile, GLA_KEY_DIM), F32),
                        pltpu.VMEM((tile, GLA_KEY_DIM), BF16),
                        pltpu.VMEM((per_step * GLA_HEADS, CHUNK, CHUNK), BF16)],
        input_output_aliases={6: 0},
        compiler_params=_params(("parallel", "arbitrary")),
        name="gla_prompt",
    )(proj, proj, proj, proj, g, onorm, og)


def _gla_sample_kernel(t_len, q_ref, k_ref, v_ref, gate_ref, g_ref, onorm_ref, s0_ref,
                       og_in_ref, *rest):
    og_ref, s_ref = rest[-2], rest[-1]
    n = GLA_SEQ_GROUP * t_len
    row = lax.broadcasted_iota(jnp.int32, (n, LANES), 0)
    col = lax.broadcasted_iota(jnp.int32, (n, LANES), 1)
    causal = ((row // t_len) == (col // t_len)) & (row >= col)
    row_seq = lax.broadcasted_iota(jnp.int32, (n, 1), 0) // t_len

    qs = q_ref[...].astype(F32)
    k = k_ref[...].astype(F32)
    pad_k = jnp.zeros((LANES - n, GLA_DK), F32)
    v_pad = jnp.concatenate([v_ref[...], jnp.zeros((LANES - n, GLA_DV), BF16)], axis=0)
    b = _masked_row_sums(jnp.where(causal, 1.0, 0.0).astype(BF16),
                         jnp.concatenate([g_ref[...], pad_k], axis=0))
    att = jnp.where(causal, _exact_block(qs, k, b, n, LANES, 0), 0.0)
    o = jnp.dot(att.astype(BF16), v_pad, preferred_element_type=F32)

    q_dec = (qs * jnp.exp(b)).astype(BF16)
    for u in range(GLA_SEQ_GROUP):
        state = s0_ref[0, u, 0]
        o_u = jnp.dot(q_dec, state.astype(BF16), preferred_element_type=F32)
        o = o + jnp.where(row_seq == u, o_u, 0.0)
        b_last = b[(u + 1) * t_len - 1:(u + 1) * t_len]
        kd = jnp.where(row_seq == u, k * jnp.exp(jnp.minimum(b_last - b, 0.0)), 0.0)
        kd_t = jnp.concatenate([kd, pad_k], axis=0).T.astype(BF16)
        s_ref[0, u, 0] = (state * _column_of(jnp.exp(b_last), GLA_DK)
                          + jnp.dot(kd_t, v_pad, preferred_element_type=F32))

    og_ref[...] = _head_norm_gate(o, onorm_ref[...], gate_ref[...].astype(F32))


def _gla_sample(proj, g, onorm, state_in, og, s_out, layer, n_seq, t_len, row0):
    n = GLA_SEQ_GROUP * t_len
    rb0 = row0 // n
    n_layers = state_in.shape[0]
    key_blocks = GLA_KEY_DIM // GLA_DK
    val_blocks = GLA_VAL_DIM // GLA_DV
    st_spec = pl.BlockSpec((1, GLA_SEQ_GROUP, 1, GLA_DK, GLA_DV),
                           lambda s, h: (layer, s, h, 0, 0))
    in_specs = [
        pl.BlockSpec((n, GLA_DK), lambda s, h: (rb0 + s, h)),
        pl.BlockSpec((n, GLA_DK), lambda s, h: (rb0 + s, key_blocks + h)),
        pl.BlockSpec((n, GLA_DV), lambda s, h: (rb0 + s, val_blocks + h)),
        pl.BlockSpec((n, GLA_DV), lambda s, h: (rb0 + s, 2 * val_blocks + h)),
        pl.BlockSpec((n, GLA_DK), lambda s, h: (rb0 + s, h)),
        pl.BlockSpec((1, GLA_DV), lambda s, h: (0, 0)),
        st_spec,
        pl.BlockSpec(memory_space=pl.ANY),
    ]
    args = [proj, proj, proj, proj, g, onorm, state_in, og]
    aliases = {7: 0}
    if s_out is not None:
        in_specs.append(pl.BlockSpec(memory_space=pl.ANY))
        args.append(s_out)
        aliases[8] = 1
    return pl.pallas_call(
        functools.partial(_gla_sample_kernel, t_len),
        grid=(n_seq // GLA_SEQ_GROUP, GLA_HEADS),
        in_specs=in_specs,
        out_specs=[
            pl.BlockSpec((n, GLA_DV), lambda s, h: (rb0 + s, h)),
            st_spec,
        ],
        out_shape=[
            jax.ShapeDtypeStruct(og.shape, BF16),
            jax.ShapeDtypeStruct((n_layers, n_seq, GLA_HEADS, GLA_DK, GLA_DV), F32),
        ],
        input_output_aliases=aliases,
        compiler_params=_params(("parallel", "parallel")),
        name="gla_sample",
    )(*args)


def _nt_dot(a, b):
    return lax.dot_general(a, b, NT_DIMS, preferred_element_type=F32)


def _lane_padded_copies(pair, fill):
    low = lax.broadcasted_iota(jnp.int32, pair.shape, 1) < HEAD_DIM
    swapped = pltpu.roll(pair, HEAD_DIM, axis=1).astype(BF16)
    pair = pair.astype(BF16)
    fill = jnp.full((), fill, BF16)
    return ((jnp.where(low, pair, fill), jnp.where(low, fill, swapped)),
            (jnp.where(low, swapped, fill), jnp.where(low, fill, pair)))


def _natural_kv_set(keys, vals, mask):
    slot0 = lax.broadcasted_iota(jnp.int32, keys[0].shape, 0) == 0

    def score_fns(cpair):
        return tuple(tuple(functools.partial(_nt_dot, b=k) for k in pads)
                     for pads in _lane_padded_copies(keys[cpair], 0.0))

    def value_fns(cpair):
        pads_by_head = _lane_padded_copies(jnp.where(slot0, 0.0, vals[cpair]), 1.0)
        return tuple(tuple((lambda p, v=v: jnp.dot(p, v, preferred_element_type=F32))
                           for v in pads) for pads in pads_by_head)

    return score_fns, value_fns, mask


def _window_kv_set(kt_ref, vt_ref, k_new_t, v_new_t, mask):
    win = kt_ref.shape[1]
    lane0 = lax.broadcasted_iota(jnp.int32, (HEAD_DIM, win), 1) == 0

    def head_keys(old_ref, new_t, cpair, parity, zero_slot0):
        r0 = (2 * cpair + parity) * HEAD_DIM
        old = old_ref[r0:r0 + HEAD_DIM, :]
        if zero_slot0:
            old = jnp.where(lane0, 0.0, old)
        new = new_t[cpair][parity * HEAD_DIM:(parity + 1) * HEAD_DIM, :]
        return jnp.concatenate([old, new], axis=1).astype(BF16)

    def stacks(head, other):
        return (jnp.concatenate([head, other], axis=0), jnp.concatenate([other, head], axis=0))

    def score_fns(cpair):
        fns = []
        for parity in range(2):
            head = head_keys(kt_ref, k_new_t, cpair, parity, False)
            fns.append(tuple(
                (lambda q, k=k: jnp.dot(q, k, preferred_element_type=F32))
                for k in stacks(head, jnp.zeros_like(head))))
        return tuple(fns)

    def value_fns(cpair):
        fns = []
        for parity in range(2):
            head = head_keys(vt_ref, v_new_t, cpair, parity, True)
            fns.append(tuple(functools.partial(_nt_dot, b=v)
                             for v in stacks(head, jnp.ones_like(head))))
        return tuple(fns)

    return score_fns, value_fns, mask


def _attend_rows(q_ref, gate_ref, og_ref, sink_ref, kv_sets, s_scr, p_scr):
    n_rows = q_ref.shape[0]
    group = N_Q_HEADS // N_KV_HEADS
    low_out = lax.broadcasted_iota(jnp.int32, (n_rows, LANES), 1) < HEAD_DIM
    lane0 = lax.broadcasted_iota(jnp.int32, (n_rows, LANES), 1) == 0

    def kv_heads():
        for cpair in range(KV_WIDTH // LANES):
            for parity in range(2):
                first = (2 * cpair + parity) * (group // 2)
                yield cpair, parity, list(range(first, first + group // 2))

    def slot(set_index, qc, half):
        return (set_index * (Q_WIDTH // LANES) + qc) * 2 + half

    for si, (score_fns, _, _) in enumerate(kv_sets):
        fns = None
        for cpair, parity, qcs in kv_heads():
            if parity == 0:
                fns = score_fns(cpair)
            q_stack = jnp.concatenate(
                [q_ref[:, qc * LANES:(qc + 1) * LANES] for qc in qcs], axis=0)
            for half, score in enumerate(fns[parity]):
                s = score(q_stack)
                for j, qc in enumerate(qcs):
                    s_scr[slot(si, qc, half)] = s[j * n_rows:(j + 1) * n_rows]

    for si, (_, _, mask) in enumerate(kv_sets):
        mask_first, mask_rest = mask[:, :LANES], mask[:, LANES:]
        for qc in range(Q_WIDTH // LANES):
            for half in range(2):
                sink = sink_ref[2 * qc + half] * LOG2_E
                s = s_scr[slot(si, qc, half)]
                first = jnp.where(mask_first, s[:, :LANES],
                                  jnp.where(lane0, sink, NEG_INF))
                rest = jnp.where(mask_rest, s[:, LANES:], NEG_INF)
                m = jnp.max(jnp.maximum(first, rest), axis=-1, keepdims=True)
                p_scr[slot(si, qc, half)] = jnp.exp2(
                    jnp.concatenate([first, rest], axis=1) - m).astype(BF16)

    def store(qc, o):
        cols = slice(qc * LANES, (qc + 1) * LANES)
        og_ref[:, cols] = (o * _silu(gate_ref[:, cols].astype(F32))).astype(BF16)

    out_cols = [None] * (Q_WIDTH // LANES)
    for si, (_, value_fns, _) in enumerate(kv_sets):
        fns = None
        for cpair, parity, qcs in kv_heads():
            if parity == 0:
                fns = value_fns(cpair)
            stacked = [weigh(jnp.concatenate([p_scr[slot(si, qc, half)] for qc in qcs],
                                             axis=0))
                       for half, weigh in enumerate(fns[parity])]
            for j, qc in enumerate(qcs):
                oa, ob = (o[j * n_rows:(j + 1) * n_rows] for o in stacked)
                sums = pltpu.roll(jnp.where(low_out, ob, oa), HEAD_DIM, axis=1)
                o2 = jnp.where(low_out, oa, ob) * (1.0 / sums)
                if len(kv_sets) == 1:
                    store(qc, o2)
                else:
                    out_cols[qc] = o2 if out_cols[qc] is None else out_cols[qc] + o2

    if len(kv_sets) > 1:
        for qc, o in enumerate(out_cols):
            store(qc, o)


def _attention_scratch(n_sets, n_rows, n_keys):
    n = n_sets * N_Q_HEADS
    return [pltpu.VMEM((n, n_rows, n_keys), F32), pltpu.VMEM((n, n_rows, n_keys), BF16)]


def _swa_prompt_kernel(front, sink_ref, q_ref, gate_ref, kc_ref, kp_ref, vc_ref, vp_ref,
                       og_in_ref, og_ref, s_scr, p_scr):
    del og_in_ref
    blk = pl.program_id(1)
    row = lax.broadcasted_iota(jnp.int32, (CHUNK, 2 * CHUNK), 0)
    col = lax.broadcasted_iota(jnp.int32, (CHUNK, 2 * CHUNK), 1)
    mask = (col > row) & (col <= row + WINDOW) & ((blk - 1) * CHUNK + col >= front)
    lane_pairs = [slice(c * LANES, (c + 1) * LANES) for c in range(KV_WIDTH // LANES)]
    keys = [jnp.concatenate([kp_ref[:, c], kc_ref[:, c]], axis=0) for c in lane_pairs]
    vals = [jnp.concatenate([vp_ref[:, c], vc_ref[:, c]], axis=0) for c in lane_pairs]
    _attend_rows(q_ref, gate_ref, og_ref, sink_ref, [_natural_kv_set(keys, vals, mask)],
                 s_scr, p_scr)


def _swa_prompt(pq, kv, sinks, og, n_batch, chunks, front):
    cur = lambda col: (lambda b, i: (b * chunks + i, col))
    prev = lambda col: (lambda b, i: (b * chunks + jnp.maximum(i - 1, 0), col))
    return pl.pallas_call(
        functools.partial(_swa_prompt_kernel, front),
        grid=(n_batch, chunks),
        in_specs=[
            pl.BlockSpec(memory_space=pltpu.SMEM),
            pl.BlockSpec((CHUNK, Q_WIDTH), cur(0)),
            pl.BlockSpec((CHUNK, Q_WIDTH), cur(1)),
            pl.BlockSpec((CHUNK, KV_WIDTH), cur(0)),
            pl.BlockSpec((CHUNK, KV_WIDTH), prev(0)),
            pl.BlockSpec((CHUNK, KV_WIDTH), cur(1)),
            pl.BlockSpec((CHUNK, KV_WIDTH), prev(1)),
            pl.BlockSpec(memory_space=pl.ANY),
        ],
        out_specs=pl.BlockSpec((CHUNK, Q_WIDTH), cur(0)),
        out_shape=jax.ShapeDtypeStruct(og.shape, BF16),
        scratch_shapes=_attention_scratch(1, CHUNK, 2 * CHUNK),
        input_output_aliases={7: 0},
        compiler_params=_params(("parallel", "parallel")),
        name="swa_prompt",
    )(sinks, pq, pq, kv, kv, kv, kv, og)


def _swa_sample_kernel(t_len, sink_ref, q_ref, gate_ref, kn_ref, vn_ref, ckt_ref, cvt_ref,
                       og_in_ref, og_ref, s_scr, p_scr):
    del og_in_ref
    n = SEQ_GROUP * t_len
    win = ckt_ref.shape[2]
    row = lax.broadcasted_iota(jnp.int32, (n, 2 * win), 0)
    col = lax.broadcasted_iota(jnp.int32, (n, 2 * win), 1)
    t = row % t_len
    new = col - win
    in_window = ((col < win) & (col > t)) | (
        (new >= 0) & (new < n) & ((new // t_len) == (row // t_len)) & ((new % t_len) <= t))
    pad = jnp.zeros((win - n, LANES), F32)
    lane_pairs = [slice(c * LANES, (c + 1) * LANES) for c in range(KV_WIDTH // LANES)]
    k_new_t = [jnp.concatenate([kn_ref[:, c], pad], axis=0).T for c in lane_pairs]
    v_new_t = [jnp.concatenate([vn_ref[:, c], pad], axis=0).T for c in lane_pairs]
    kv_sets = [_window_kv_set(ckt_ref.at[u], cvt_ref.at[u], k_new_t, v_new_t,
                              in_window & ((row // t_len) == u))
               for u in range(SEQ_GROUP)]
    _attend_rows(q_ref, gate_ref, og_ref, sink_ref, kv_sets, s_scr, p_scr)


def _swa_sample(pq, kv, cache_kt, cache_vt, sinks, og, n_seq, t_len, row0):
    n = SEQ_GROUP * t_len
    rb0 = row0 // n
    win = cache_kt.shape[2]
    return pl.pallas_call(
        functools.partial(_swa_sample_kernel, t_len),
        grid=(n_seq // SEQ_GROUP,),
        in_specs=[
            pl.BlockSpec(memory_space=pltpu.SMEM),
            pl.BlockSpec((n, Q_WIDTH), lambda s: (rb0 + s, 0)),
            pl.BlockSpec((n, Q_WIDTH), lambda s: (rb0 + s, 1)),
            pl.BlockSpec((n, KV_WIDTH), lambda s: (rb0 + s, 0)),
            pl.BlockSpec((n, KV_WIDTH), lambda s: (rb0 + s, 1)),
            pl.BlockSpec((SEQ_GROUP, KV_WIDTH, win), lambda s: (s, 0, 0)),
            pl.BlockSpec((SEQ_GROUP, KV_WIDTH, win), lambda s: (s, 0, 0)),
            pl.BlockSpec(memory_space=pl.ANY),
        ],
        out_specs=pl.BlockSpec((n, Q_WIDTH), lambda s: (rb0 + s, 0)),
        out_shape=jax.ShapeDtypeStruct(og.shape, BF16),
        scratch_shapes=_attention_scratch(SEQ_GROUP, n, 2 * win),
        input_output_aliases={7: 0},
        compiler_params=_params(("parallel",)),
        name="swa_sample",
    )(sinks, pq, pq, kv, kv, cache_kt, cache_vt, og)


def kernel(x_prompt, x_sample, state_gla, cache_k_win, cache_v_win, meta_tokens, norm_a, w_in_a, w_gk2_a, b_gk2_a, onorm_a, w_out_a, norm_kv, w_kv, norm_b, w_in_b, sinks_b, w_out_b, norm_f):
    n_batch, seq, _ = x_prompt.shape
    n_seq, t_len, _ = x_sample.shape
    win = cache_k_win.shape[1]
    n_a = w_in_a.shape[0]
    n_b = w_in_b.shape[0]
    assert win == WINDOW == CHUNK and n_seq % SEQ_GROUP == 0 and n_seq % GLA_SEQ_GROUP == 0

    length = seq + N_META
    front = (-length) % CHUNK
    lp = length + front
    chunks = lp // CHUNK
    srow = n_batch * lp
    ns = n_seq * t_len
    rows = -(-(srow + ns) // ROW_TILE) * ROW_TILE

    assert front + N_META == CHUNK and ns == CHUNK and seq % CHUNK == 0
    assert rows % CHUNK == 0 and rows % (16 * IN_PROJ_ROW_STEPS) == 0

    w_in_a_t = jnp.swapaxes(w_in_a, 1, 2)
    w_out_a_bf16 = w_out_a.astype(BF16)
    w_out_b_bf16 = w_out_b.astype(BF16)

    def gate_weights(l):
        w_gate = jnp.pad(w_in_a_t[l, GLA_MAIN_WIDTH:], ((0, LANES - GATE_RANK), (0, 0)))
        w_gk = jnp.pad(w_gk2_a[l], ((0, LANES - GATE_RANK), (0, 0))).astype(BF16)
        return w_gate, w_gk, b_gk2_a[l][None]

    x, hn, g = _embed(x_prompt.reshape(n_batch * seq, D_MODEL), x_sample.reshape(ns, D_MODEL),
                      meta_tokens, norm_a[0], gate_weights(0), rows, chunks, n_batch, front)
    og = jnp.zeros((rows, D_MODEL), BF16)

    states_p = []
    states_s = None
    for l in range(n_a):
        proj = _in_proj(hn, w_in_a_t, l, GLA_MAIN_WIDTH, BF16, True, "gla_in_proj",
                        q_cols=GLA_KEY_DIM, q_scale=GLA_DK ** -0.5)
        onorm = onorm_a[l][None]
        og, s_p = _gla_prompt(proj, g, onorm, og, n_batch, chunks)
        og, states_s = _gla_sample(proj, g, onorm, state_gla, og, states_s, l,
                                   n_seq, t_len, srow)
        states_p.append(s_p)
        w_out = (w_out_a_bf16, l)
        if l + 1 < n_a:
            x, hn, g = _out_proj(og, w_out, x, [norm_a[l + 1]], gate_weights(l + 1), True,
                                 BF16, "gla_out_proj")
        else:
            x, hn_kv, hn = _out_proj(og, w_out, x, [norm_kv, norm_b[0]], None, True,
                                     BF16, "gla_out_proj")

    kv = _in_proj(hn_kv, w_kv[None], 0, 2 * KV_WIDTH, F32, False, "kv_proj")
    cache_k = jnp.transpose(cache_k_win, (0, 2, 3, 1)).reshape(n_seq, KV_WIDTH, win)
    cache_v = jnp.transpose(cache_v_win, (0, 2, 3, 1)).reshape(n_seq, KV_WIDTH, win)
    for j in range(n_b):
        pq = _in_proj(hn, w_in_b, j, 2 * Q_WIDTH, BF16, False, "swa_in_proj",
                      q_cols=Q_WIDTH, q_scale=HEAD_DIM ** -0.5 * LOG2_E)
        og = _swa_prompt(pq, kv, sinks_b[j], og, n_batch, chunks, front)
        og = _swa_sample(pq, kv, cache_k, cache_v, sinks_b[j], og, n_seq, t_len, srow)
        w_out = (w_out_b_bf16, j)
        if j + 1 < n_b:
            x, hn = _out_proj(og, w_out, x, [norm_b[j + 1]], None, True, BF16, "swa_out_proj")

    tiles_per_batch = seq // ROW_TILE
    lead_blocks = (lp - seq) // CHUNK

    def prompt_first_block(t):
        return ((t // tiles_per_batch) * chunks + lead_blocks
                + (t % tiles_per_batch) * (ROW_TILE // CHUNK))

    (y_prompt,) = _out_proj(og, w_out, x, [norm_f], None, False, F32, "final_prompt",
                            gather=(ROW_TILE, n_batch * tiles_per_batch, prompt_first_block))
    (y_sample,) = _out_proj(og, w_out, x, [norm_f], None, False, F32, "final_sample",
                            gather=(CHUNK, ns // CHUNK, lambda t: srow // CHUNK + t))
    y_prompt = y_prompt.reshape(n_batch, seq, D_MODEL)
    y_sample = y_sample.reshape(n_seq, t_len, D_MODEL)
    kv_p = jnp.stack([kv[(b + 1) * lp - win:(b + 1) * lp] for b in range(n_batch)])
    kv_p = kv_p.reshape(n_batch, win, 2, N_KV_HEADS, HEAD_DIM)
    kv_s = kv[srow:srow + ns].reshape(n_seq, t_len, 2, N_KV_HEADS, HEAD_DIM)
    k_win_s = jnp.concatenate([cache_k_win, kv_s[:, :, 0]], axis=1)[:, -win:]
    v_win_s = jnp.concatenate([cache_v_win, kv_s[:, :, 1]], axis=1)[:, -win:]
    return (y_prompt, y_sample, jnp.stack(states_p), states_s,
            kv_p[:, :, 0], kv_p[:, :, 1], k_win_s, v_win_s)
```

```python
import functools

import jax
import jax.numpy as jnp
from jax import lax
from jax.experimental import pallas as pl
from jax.experimental.pallas import tpu as pltpu

F32 = jnp.float32
BF16 = jnp.bfloat16

D_MODEL = 2048
N_META = 16
GLA_HEADS = 4
GLA_DK = 256
GLA_DV = 512
GLA_KEY_DIM = GLA_HEADS * GLA_DK
GLA_VAL_DIM = GLA_HEADS * GLA_DV
GLA_MAIN_WIDTH = 2 * GLA_KEY_DIM + 2 * GLA_VAL_DIM
GATE_RANK = 16
GATE_LOGIT_NORM = 16.0
HEAD_DIM = 64
N_Q_HEADS = 32
N_KV_HEADS = 8
Q_WIDTH = N_Q_HEADS * HEAD_DIM
KV_WIDTH = N_KV_HEADS * HEAD_DIM
WINDOW = 128
RMS_EPS = 1e-6
NEG_INF = -1e30
LOG2_E = 1.4426950408889634

LANES = 128
CHUNK = 128
GLA_LEAF = 8
LEAF_BATCH = 4
GLA_CHUNKS_PER_STEP = 3
GLA_BOUNDED_RANGE = 60.0
ROW_TILE = 512
IN_PROJ_COLS = 1024
IN_PROJ_ROW_STEPS = 8
SEQ_GROUP = 4
GLA_SEQ_GROUP = 8
VMEM_LIMIT = 48 * 1024 * 1024

NT_DIMS = (((1,), (1,)), ((), ()))


def _silu(x):
    return x * (1.0 / (1.0 + jnp.exp(-x)))


def _params(sem):
    return pltpu.CompilerParams(dimension_semantics=sem, vmem_limit_bytes=VMEM_LIMIT)


def _emit_normed(y, nw_refs, gate_refs, hn_refs, g_ref):
    ms = jnp.mean(y * y, axis=-1, keepdims=True)
    yn = y * lax.rsqrt(ms + RMS_EPS)
    for nw_ref, hn_ref in zip(nw_refs, hn_refs):
        hn_ref[...] = (yn * nw_ref[...]).astype(hn_ref.dtype)
    if gate_refs:
        g_ref[...] = _decay_gate(hn_refs[0][...], *gate_refs)


def _split_refs(refs, n_norms, with_gate):
    nw_refs = refs[:n_norms]
    gate_refs = refs[n_norms:n_norms + 3] if with_gate else None
    return nw_refs, gate_refs, refs[n_norms + 3 * with_gate:]


def _norm_specs(norms, gate, index):
    specs = [pl.BlockSpec((1, D_MODEL), index) for _ in norms]
    args = [nw[None] for nw in norms]
    if gate is not None:
        specs += [pl.BlockSpec((LANES, D_MODEL), index),
                  pl.BlockSpec((LANES, GLA_KEY_DIM), index),
                  pl.BlockSpec((1, GLA_KEY_DIM), index)]
        args += list(gate)
    return specs, args


TOKEN_SUB = ROW_TILE // CHUNK


def _token_rows(chunks, n_batch, src_refs):
    xp_refs, ex_refs = src_refs[:TOKEN_SUB], src_refs[TOKEN_SUB:]
    pieces = []
    for k in range(TOKEN_SUB):
        blk = pl.program_id(0) * TOKEN_SUB + k
        from_prompt = jnp.logical_and(blk % chunks != 0, blk < n_batch * chunks)
        pieces.append(jnp.where(from_prompt, xp_refs[k][...], ex_refs[k][...]))
    return jnp.concatenate(pieces, axis=0)


def _token_sources(x_prompt2d, extras, chunks, n_batch):
    prompt_blocks = x_prompt2d.shape[0] // CHUNK
    per_batch = prompt_blocks // n_batch

    def prompt_index(k):
        def index(i):
            blk = i * TOKEN_SUB + k
            src = (blk // chunks) * per_batch + (blk % chunks) - 1
            return (jnp.clip(src, 0, prompt_blocks - 1), 0)
        return index

    def extras_index(k):
        def index(i):
            blk = i * TOKEN_SUB + k
            return (jnp.clip(blk - n_batch * chunks + 1, 0, 2), 0)
        return index

    specs = ([pl.BlockSpec((CHUNK, D_MODEL), prompt_index(k)) for k in range(TOKEN_SUB)]
             + [pl.BlockSpec((CHUNK, D_MODEL), extras_index(k)) for k in range(TOKEN_SUB)])
    return specs, [x_prompt2d] * TOKEN_SUB + [extras] * TOKEN_SUB


def _embed_kernel(chunks, n_batch, with_gate, *refs):
    nw_refs, gate_refs, outs = _split_refs(refs[2 * TOKEN_SUB:], 1, with_gate)
    x = _token_rows(chunks, n_batch, refs[:2 * TOKEN_SUB])
    _emit_normed(x, nw_refs, gate_refs, outs[0:1], outs[1] if with_gate else None)


def _embed(tokens, norm, gate, rows, chunks, n_batch):
    src_specs, src_args = _token_sources(*tokens, chunks, n_batch)
    norm_specs, norm_args = _norm_specs([norm], gate, lambda i: (0, 0))
    out_specs = [pl.BlockSpec((ROW_TILE, D_MODEL), lambda i: (i, 0))]
    out_shape = [jax.ShapeDtypeStruct((rows, D_MODEL), BF16)]
    if gate is not None:
        out_specs.append(pl.BlockSpec((ROW_TILE, GLA_KEY_DIM), lambda i: (i, 0)))
        out_shape.append(jax.ShapeDtypeStruct((rows, GLA_KEY_DIM), F32))
    return pl.pallas_call(
        functools.partial(_embed_kernel, chunks, n_batch, gate is not None),
        grid=(rows // ROW_TILE,),
        in_specs=src_specs + norm_specs,
        out_specs=out_specs,
        out_shape=out_shape,
        compiler_params=_params(("parallel",)),
        name="embed",
    )(*src_args, *norm_args)


def _decay_gate(hn, wg_ref, wgk_ref, bgk_ref):
    glr = lax.dot_general(hn, wg_ref[...].astype(BF16), NT_DIMS, preferred_element_type=F32)
    z = jnp.dot(glr.astype(BF16), wgk_ref[...], preferred_element_type=F32) + bgk_ref[...]
    log_sig = jnp.minimum(z, 0.0) - jnp.log(1.0 + jnp.exp(-jnp.abs(z)))
    return log_sig * (1.0 / GATE_LOGIT_NORM)


def _in_proj_kernel(transposed, scale_ref, hn_ref, w_ref, o_ref, w_bf16_ref):
    @pl.when(pl.program_id(1) == 0)
    def _():
        w_bf16_ref[...] = w_ref[...].astype(BF16)

    dims = NT_DIMS if transposed else (((1,), (0,)), ((), ()))
    o = lax.dot_general(hn_ref[...], w_bf16_ref[...], dims, preferred_element_type=F32)
    o_ref[...] = (o * scale_ref[pl.program_id(0)]).astype(o_ref.dtype)


def _in_proj(hn, w_stack, layer, n_cols, out_dtype, transposed, name, q_cols=0, q_scale=1.0):
    rows = hn.shape[0]
    row_tile = rows // IN_PROJ_ROW_STEPS
    n_slabs = n_cols // IN_PROJ_COLS
    assert q_cols % IN_PROJ_COLS == 0
    scales = jnp.where(jnp.arange(n_slabs) < q_cols // IN_PROJ_COLS, q_scale, 1.0).astype(F32)
    if transposed:
        slab = (IN_PROJ_COLS, D_MODEL)
        w_spec = pl.BlockSpec((None,) + slab, lambda j, i: (layer, j, 0))
    else:
        slab = (D_MODEL, IN_PROJ_COLS)
        w_spec = pl.BlockSpec((None,) + slab, lambda j, i: (layer, 0, j))
    return pl.pallas_call(
        functools.partial(_in_proj_kernel, transposed),
        grid=(n_slabs, IN_PROJ_ROW_STEPS),
        in_specs=[pl.BlockSpec(memory_space=pltpu.SMEM),
                  pl.BlockSpec((row_tile, D_MODEL), lambda j, i: (i, 0)), w_spec],
        out_specs=pl.BlockSpec((row_tile, IN_PROJ_COLS), lambda j, i: (i, j)),
        out_shape=jax.ShapeDtypeStruct((rows, n_cols), out_dtype),
        scratch_shapes=[pltpu.VMEM(slab, BF16)],
        compiler_params=_params(("parallel", "arbitrary")),
        name=name,
    )(scales, hn, w_stack)


def _out_proj_kernel(sub, n_norms, with_gate, keep_x, token_layout, *refs):
    og_refs, w_ref, refs = refs[:sub], refs[sub], refs[sub + 1:]
    n_x = sub if token_layout is None else 2 * TOKEN_SUB
    x_refs, refs = refs[:n_x], refs[n_x:]
    nw_refs, gate_refs, outs = _split_refs(refs, n_norms, with_gate)
    og = og_refs[0][...] if sub == 1 else jnp.concatenate([r[...] for r in og_refs], axis=0)
    if token_layout is not None:
        x = _token_rows(*token_layout, x_refs)
    elif sub == 1:
        x = x_refs[0][...]
    else:
        x = jnp.concatenate([r[...] for r in x_refs], axis=0)
    y = x + jnp.dot(og, w_ref[...], preferred_element_type=F32)
    if keep_x:
        outs[0][...] = y
        outs = outs[1:]
    _emit_normed(y, nw_refs, gate_refs, outs[:n_norms], outs[n_norms] if with_gate else None)


def _out_proj(og, w, x, norms, gate, keep_x, normed_dtype, name, gather=None, tokens=None):
    if gather is None:
        tile_rows, n_tiles, sub = ROW_TILE, og.shape[0] // ROW_TILE, 1
        in_row_specs = [pl.BlockSpec((ROW_TILE, D_MODEL), lambda i: (i, 0))]
    else:
        tile_rows, n_tiles, first_block = gather
        sub = tile_rows // CHUNK
        in_row_specs = [pl.BlockSpec((CHUNK, D_MODEL),
                                     functools.partial(lambda k, i: (first_block(i) + k, 0), k))
                        for k in range(sub)]
    rows = tile_rows * n_tiles
    row_spec = pl.BlockSpec((tile_rows, D_MODEL), lambda i: (i, 0))
    norm_specs, norm_args = _norm_specs(norms, gate, lambda i: (0, 0))
    out_specs, out_shape = [], []
    if keep_x:
        out_specs.append(row_spec)
        out_shape.append(jax.ShapeDtypeStruct((rows, D_MODEL), F32))
    for _ in norms:
        out_specs.append(row_spec)
        out_shape.append(jax.ShapeDtypeStruct((rows, D_MODEL), normed_dtype))
    if gate is not None:
        out_specs.append(pl.BlockSpec((tile_rows, GLA_KEY_DIM), lambda i: (i, 0)))
        out_shape.append(jax.ShapeDtypeStruct((rows, GLA_KEY_DIM), F32))
    w_stack, layer = w
    w_spec = pl.BlockSpec((None, D_MODEL, D_MODEL), lambda i: (layer, 0, 0),
                          pipeline_mode=pl.Buffered(1))
    if x is None:
        sources, chunks, n_batch = tokens
        x_specs, x_args = _token_sources(*sources, chunks, n_batch)
        token_layout = (chunks, n_batch)
    else:
        x_specs, x_args, token_layout = in_row_specs, [x] * sub, None
    return pl.pallas_call(
        functools.partial(_out_proj_kernel, sub, len(norms), gate is not None, keep_x,
                          token_layout),
        grid=(n_tiles,),
        in_specs=in_row_specs + [w_spec] + x_specs + norm_specs,
        out_specs=out_specs,
        out_shape=out_shape,
        compiler_params=_params(("parallel",)),
        name=name,
    )(*([og] * sub), w_stack, *x_args, *norm_args)


def _exact_block(qs, k, b, n, width, col0):
    lane = lax.broadcasted_iota(jnp.int32, (n, width), 1)
    acc = jnp.zeros((n, width), F32)
    for j in range(n):
        t = qs * k[j:j + 1] * jnp.exp(jnp.minimum(b - b[j:j + 1], 0.0))
        acc = jnp.where(lane == col0 + j, jnp.sum(t, axis=-1, keepdims=True), acc)
    return acc


def _masked_row_sums(mask_bf16, g):
    g_hi = g.astype(BF16)
    r1 = g - g_hi.astype(F32)
    g_mid = r1.astype(BF16)
    g_lo = (r1 - g_mid.astype(F32)).astype(BF16)
    return (jnp.dot(mask_bf16, g_hi, preferred_element_type=F32)
            + jnp.dot(mask_bf16, g_mid, preferred_element_type=F32)
            + jnp.dot(mask_bf16, g_lo, preferred_element_type=F32))


def _column_of(row_vec, n):
    return jnp.broadcast_to(row_vec, (LANES, n)).T[:, 0:1]


def _head_norm_gate(o, onorm, gate):
    ms = jnp.mean(o * o, axis=-1, keepdims=True)
    return (o * lax.rsqrt(ms + RMS_EPS) * onorm * _silu(gate)).astype(BF16)


def _intra_chunk_scores(qs, k, k_bf, b, row_xor_col):
    c_len = qs.shape[0]
    leaf = GLA_LEAF
    leaf_row = lax.broadcasted_iota(jnp.int32, (leaf, c_len), 0)
    leaf_col = lax.broadcasted_iota(jnp.int32, (leaf, c_len), 1)
    lane_in_leaf = [(leaf_col & (leaf - 1)) == j for j in range(leaf)]

    att_rows = []
    for blk0 in range(0, c_len // leaf, LEAF_BATCH):
        lhs = []
        for blk in range(blk0, blk0 + LEAF_BATCH):
            q_blk = qs[blk * leaf:(blk + 1) * leaf]
            b_blk = b[blk * leaf:(blk + 1) * leaf]
            lhs += [q_blk * jnp.exp2(jnp.minimum(b_blk - b_blk[j:j + 1], 0.0))
                    for j in range(leaf)]
        r = lax.dot_general(jnp.concatenate(lhs, axis=0).astype(BF16), k_bf, NT_DIMS,
                            preferred_element_type=F32)
        for i, blk in enumerate(range(blk0, blk0 + LEAF_BATCH)):
            acc = jnp.zeros((leaf, c_len), F32)
            for j in range(leaf):
                r0 = (i * leaf + j) * leaf
                acc = jnp.where(lane_in_leaf[j], r[r0:r0 + leaf], acc)
            rel = leaf_col - blk * leaf
            att_rows.append(jnp.where((rel >= 0) & (rel <= leaf_row), acc, 0.0))
    att = jnp.concatenate(att_rows, axis=0)

    m = c_len // 2
    while m >= leaf:
        q_parts, k_parts = [], []
        zeros = jnp.zeros((m, GLA_DK), F32)
        for base in range(0, c_len, 2 * m):
            ref = b[base + m - 1:base + m]
            lo = slice(base, base + m)
            hi = slice(base + m, base + 2 * m)
            k_parts += [k[lo] * jnp.exp2(ref - b[lo]), zeros]
            q_parts += [zeros, qs[hi] * jnp.exp2(b[hi] - ref)]
        qt = jnp.concatenate(q_parts, axis=0).astype(BF16)
        kt = jnp.concatenate(k_parts, axis=0).astype(BF16)
        a = lax.dot_general(qt, kt, NT_DIMS, preferred_element_type=F32)
        att = att + jnp.where(row_xor_col < 2 * m, a, 0.0)
        m //= 2
    return att


def _gla_prompt_kernel(q_ref, k_ref, v_ref, gate_ref, g_ref, onorm_ref, og_in_ref,
                       og_ref, s_ref, b_scr, qd_scr, att_scr):
    del og_in_ref
    c_len = CHUNK
    heads = [(slice(h * GLA_DK, (h + 1) * GLA_DK), slice(h * GLA_DV, (h + 1) * GLA_DV))
             for h in range(GLA_HEADS)]

    @pl.when(pl.program_id(1) == 0)
    def _():
        s_ref[...] = jnp.zeros_like(s_ref)

    row = lax.broadcasted_iota(jnp.int32, (c_len, c_len), 0)
    col = lax.broadcasted_iota(jnp.int32, (c_len, c_len), 1)
    lower = row >= col
    tri = jnp.where(lower, 1.0, 0.0).astype(BF16)
    onorm = onorm_ref[...]
    chunk_rows = [slice(c * c_len, (c + 1) * c_len) for c in range(q_ref.shape[0] // c_len)]

    total = None
    for rows in chunk_rows:
        b_c = _masked_row_sums(tri, g_ref[rows, :]) * LOG2_E
        b_scr[rows, :] = b_c
        qd_scr[rows, :] = (q_ref[rows, :].astype(F32) * jnp.exp2(b_c)).astype(BF16)
        last = b_c[c_len - 1:c_len]
        total = last if total is None else jnp.minimum(total, last)

    bounded = jnp.max(-total) <= GLA_BOUNDED_RANGE

    @pl.when(bounded)
    def _():
        for c, rows in enumerate(chunk_rows):
            for h, (ks, _) in enumerate(heads):
                k_inv = (k_ref[rows, ks].astype(F32) * jnp.exp2(-b_scr[rows, ks])).astype(BF16)
                a = lax.dot_general(qd_scr[rows, ks], k_inv, NT_DIMS,
                                    preferred_element_type=F32)
                att_scr[c * GLA_HEADS + h] = jnp.where(lower, a, 0.0).astype(BF16)

    @pl.when(jnp.logical_not(bounded))
    def _():
        for c, rows in enumerate(chunk_rows):
            for h, (ks, _) in enumerate(heads):
                k_bf = k_ref[rows, ks]
                att_scr[c * GLA_HEADS + h] = _intra_chunk_scores(
                    q_ref[rows, ks].astype(F32), k_bf.astype(F32), k_bf,
                    b_scr[rows, ks], row ^ col).astype(BF16)

    for c, rows in enumerate(chunk_rows):
        for h, (ks, vs) in enumerate(heads):
            k = k_ref[rows, ks].astype(F32)
            v = v_ref[rows, vs]
            b = b_scr[rows, ks]
            b_last = b[c_len - 1:c_len]
            state = s_ref[0, h]

            o = jnp.dot(qd_scr[rows, ks], state.astype(BF16), preferred_element_type=F32)
            o = o + jnp.dot(att_scr[c * GLA_HEADS + h], v, preferred_element_type=F32)

            kd_t = (k * jnp.exp2(b_last - b)).T.astype(BF16)
            s_ref[0, h] = (state * _column_of(jnp.exp2(b_last), GLA_DK)
                           + jnp.dot(kd_t, v, preferred_element_type=F32))

            og_ref[rows, vs] = _head_norm_gate(o, onorm, gate_ref[rows, vs].astype(F32))


def _gla_prompt(proj, g, onorm, og, n_batch, chunks):
    rows = proj.shape[0]
    per_step = next(n for n in (GLA_CHUNKS_PER_STEP, 1) if chunks % n == 0)
    steps = chunks // per_step
    tile = per_step * CHUNK
    blk = lambda col: (lambda b, c: (b * steps + c, col))
    return pl.pallas_call(
        _gla_prompt_kernel,
        grid=(n_batch, steps),
        in_specs=[
            pl.BlockSpec((tile, GLA_KEY_DIM), blk(0)),
            pl.BlockSpec((tile, GLA_KEY_DIM), blk(1)),
            pl.BlockSpec((tile, GLA_VAL_DIM), blk(1)),
            pl.BlockSpec((tile, GLA_VAL_DIM), blk(2)),
            pl.BlockSpec((tile, GLA_KEY_DIM), blk(0)),
            pl.BlockSpec((1, GLA_DV), lambda b, c: (0, 0)),
            pl.BlockSpec(memory_space=pl.ANY),
        ],
        out_specs=[
            pl.BlockSpec((tile, GLA_VAL_DIM), blk(0)),
            pl.BlockSpec((1, GLA_HEADS, GLA_DK, GLA_DV), lambda b, c: (b, 0, 0, 0)),
        ],
        out_shape=[
            jax.ShapeDtypeStruct((rows, GLA_VAL_DIM), BF16),
            jax.ShapeDtypeStruct((n_batch, GLA_HEADS, GLA_DK, GLA_DV), F32),
        ],
        scratch_shapes=[pltpu.VMEM((tile, GLA_KEY_DIM), F32),
                        pltpu.VMEM((tile, GLA_KEY_DIM), BF16),
                        pltpu.VMEM((per_step * GLA_HEADS, CHUNK, CHUNK), BF16)],
        input_output_aliases={6: 0},
        compiler_params=_params(("parallel", "arbitrary")),
        name="gla_prompt",
    )(proj, proj, proj, proj, g, onorm, og)


def _gla_sample_kernel(t_len, q_ref, k_ref, v_ref, gate_ref, g_ref, onorm_ref, s0_ref,
                       og_in_ref, *rest):
    og_ref, s_ref = rest[-2], rest[-1]
    n = GLA_SEQ_GROUP * t_len
    row = lax.broadcasted_iota(jnp.int32, (n, LANES), 0)
    col = lax.broadcasted_iota(jnp.int32, (n, LANES), 1)
    causal = ((row // t_len) == (col // t_len)) & (row >= col)
    row_seq = lax.broadcasted_iota(jnp.int32, (n, 1), 0) // t_len

    qs = q_ref[...].astype(F32)
    k = k_ref[...].astype(F32)
    pad_k = jnp.zeros((LANES - n, GLA_DK), F32)
    v_pad = jnp.concatenate([v_ref[...], jnp.zeros((LANES - n, GLA_DV), BF16)], axis=0)
    b = _masked_row_sums(jnp.where(causal, 1.0, 0.0).astype(BF16),
                         jnp.concatenate([g_ref[...], pad_k], axis=0))
    att = jnp.where(causal, _exact_block(qs, k, b, n, LANES, 0), 0.0)
    o = jnp.dot(att.astype(BF16), v_pad, preferred_element_type=F32)

    q_dec = (qs * jnp.exp(b)).astype(BF16)
    for u in range(GLA_SEQ_GROUP):
        state = s0_ref[0, u, 0]
        o_u = jnp.dot(q_dec, state.astype(BF16), preferred_element_type=F32)
        o = o + jnp.where(row_seq == u, o_u, 0.0)
        b_last = b[(u + 1) * t_len - 1:(u + 1) * t_len]
        kd = jnp.where(row_seq == u, k * jnp.exp(jnp.minimum(b_last - b, 0.0)), 0.0)
        kd_t = jnp.concatenate([kd, pad_k], axis=0).T.astype(BF16)
        s_ref[0, u, 0] = (state * _column_of(jnp.exp(b_last), GLA_DK)
                          + jnp.dot(kd_t, v_pad, preferred_element_type=F32))

    og_ref[...] = _head_norm_gate(o, onorm_ref[...], gate_ref[...].astype(F32))


def _gla_sample(proj, g, onorm, state_in, og, s_out, layer, n_seq, t_len, row0):
    n = GLA_SEQ_GROUP * t_len
    rb0 = row0 // n
    n_layers = state_in.shape[0]
    key_blocks = GLA_KEY_DIM // GLA_DK
    val_blocks = GLA_VAL_DIM // GLA_DV
    st_spec = pl.BlockSpec((1, GLA_SEQ_GROUP, 1, GLA_DK, GLA_DV),
                           lambda s, h: (layer, s, h, 0, 0))
    in_specs = [
        pl.BlockSpec((n, GLA_DK), lambda s, h: (rb0 + s, h)),
        pl.BlockSpec((n, GLA_DK), lambda s, h: (rb0 + s, key_blocks + h)),
        pl.BlockSpec((n, GLA_DV), lambda s, h: (rb0 + s, val_blocks + h)),
        pl.BlockSpec((n, GLA_DV), lambda s, h: (rb0 + s, 2 * val_blocks + h)),
        pl.BlockSpec((n, GLA_DK), lambda s, h: (rb0 + s, h)),
        pl.BlockSpec((1, GLA_DV), lambda s, h: (0, 0)),
        st_spec,
        pl.BlockSpec(memory_space=pl.ANY),
    ]
    args = [proj, proj, proj, proj, g, onorm, state_in, og]
    aliases = {7: 0}
    if s_out is not None:
        in_specs.append(pl.BlockSpec(memory_space=pl.ANY))
        args.append(s_out)
        aliases[8] = 1
    return pl.pallas_call(
        functools.partial(_gla_sample_kernel, t_len),
        grid=(n_seq // GLA_SEQ_GROUP, GLA_HEADS),
        in_specs=in_specs,
        out_specs=[
            pl.BlockSpec((n, GLA_DV), lambda s, h: (rb0 + s, h)),
            st_spec,
        ],
        out_shape=[
            jax.ShapeDtypeStruct(og.shape, BF16),
            jax.ShapeDtypeStruct((n_layers, n_seq, GLA_HEADS, GLA_DK, GLA_DV), F32),
        ],
        input_output_aliases=aliases,
        compiler_params=_params(("parallel", "parallel")),
        name="gla_sample",
    )(*args)


def _nt_dot(a, b):
    return lax.dot_general(a, b, NT_DIMS, preferred_element_type=F32)


def _lane_padded_copies(pair, fill):
    low = lax.broadcasted_iota(jnp.int32, pair.shape, 1) < HEAD_DIM
    swapped = pltpu.roll(pair, HEAD_DIM, axis=1).astype(BF16)
    pair = pair.astype(BF16)
    fill = jnp.full((), fill, BF16)
    return ((jnp.where(low, pair, fill), jnp.where(low, fill, swapped)),
            (jnp.where(low, swapped, fill), jnp.where(low, fill, pair)))


def _natural_kv_set(keys, vals, mask):
    slot0 = lax.broadcasted_iota(jnp.int32, keys[0].shape, 0) == 0

    def score_fns(cpair):
        return tuple(tuple(functools.partial(_nt_dot, b=k) for k in pads)
                     for pads in _lane_padded_copies(keys[cpair], 0.0))

    def value_fns(cpair):
        pads_by_head = _lane_padded_copies(jnp.where(slot0, 0.0, vals[cpair]), 1.0)
        return tuple(tuple((lambda p, v=v: jnp.dot(p, v, preferred_element_type=F32))
                           for v in pads) for pads in pads_by_head)

    return score_fns, value_fns, mask


def _window_kv_set(kt_ref, vt_ref, k_new_t, v_new_t, mask):
    win = kt_ref.shape[1]
    lane0 = lax.broadcasted_iota(jnp.int32, (HEAD_DIM, win), 1) == 0

    def head_keys(old_ref, new_t, cpair, parity, zero_slot0):
        r0 = (2 * cpair + parity) * HEAD_DIM
        old = old_ref[r0:r0 + HEAD_DIM, :]
        if zero_slot0:
            old = jnp.where(lane0, 0.0, old)
        new = new_t[cpair][parity * HEAD_DIM:(parity + 1) * HEAD_DIM, :]
        return jnp.concatenate([old, new], axis=1).astype(BF16)

    def stacks(head, other):
        return (jnp.concatenate([head, other], axis=0), jnp.concatenate([other, head], axis=0))

    def score_fns(cpair):
        fns = []
        for parity in range(2):
            head = head_keys(kt_ref, k_new_t, cpair, parity, False)
            fns.append(tuple(
                (lambda q, k=k: jnp.dot(q, k, preferred_element_type=F32))
                for k in stacks(head, jnp.zeros_like(head))))
        return tuple(fns)

    def value_fns(cpair):
        fns = []
        for parity in range(2):
            head = head_keys(vt_ref, v_new_t, cpair, parity, True)
            fns.append(tuple(functools.partial(_nt_dot, b=v)
                             for v in stacks(head, jnp.ones_like(head))))
        return tuple(fns)

    return score_fns, value_fns, mask


def _attend_rows(q_ref, gate_ref, og_ref, sink_ref, kv_sets, s_scr, p_scr):
    n_rows = q_ref.shape[0]
    group = N_Q_HEADS // N_KV_HEADS
    low_out = lax.broadcasted_iota(jnp.int32, (n_rows, LANES), 1) < HEAD_DIM
    lane0 = lax.broadcasted_iota(jnp.int32, (n_rows, LANES), 1) == 0

    def kv_heads():
        for cpair in range(KV_WIDTH // LANES):
            for parity in range(2):
                first = (2 * cpair + parity) * (group // 2)
                yield cpair, parity, list(range(first, first + group // 2))

    def slot(set_index, qc, half):
        return (set_index * (Q_WIDTH // LANES) + qc) * 2 + half

    for si, (score_fns, _, _) in enumerate(kv_sets):
        fns = None
        for cpair, parity, qcs in kv_heads():
            if parity == 0:
                fns = score_fns(cpair)
            q_stack = jnp.concatenate(
                [q_ref[:, qc * LANES:(qc + 1) * LANES] for qc in qcs], axis=0)
            for half, score in enumerate(fns[parity]):
                s = score(q_stack)
                for j, qc in enumerate(qcs):
                    s_scr[slot(si, qc, half)] = s[j * n_rows:(j + 1) * n_rows]

    for si, (_, _, mask) in enumerate(kv_sets):
        mask_first, mask_rest = mask[:, :LANES], mask[:, LANES:]
        for qc in range(Q_WIDTH // LANES):
            for half in range(2):
                sink = sink_ref[2 * qc + half] * LOG2_E
                s = s_scr[slot(si, qc, half)]
                first = jnp.where(mask_first, s[:, :LANES],
                                  jnp.where(lane0, sink, NEG_INF))
                rest = jnp.where(mask_rest, s[:, LANES:], NEG_INF)
                m = jnp.max(jnp.maximum(first, rest), axis=-1, keepdims=True)
                p_scr[slot(si, qc, half)] = jnp.exp2(
                    jnp.concatenate([first, rest], axis=1) - m).astype(BF16)

    def store(qc, o):
        cols = slice(qc * LANES, (qc + 1) * LANES)
        og_ref[:, cols] = (o * _silu(gate_ref[:, cols].astype(F32))).astype(BF16)

    out_cols = [None] * (Q_WIDTH // LANES)
    for si, (_, value_fns, _) in enumerate(kv_sets):
        fns = None
        for cpair, parity, qcs in kv_heads():
            if parity == 0:
                fns = value_fns(cpair)
            stacked = [weigh(jnp.concatenate([p_scr[slot(si, qc, half)] for qc in qcs],
                                             axis=0))
                       for half, weigh in enumerate(fns[parity])]
            for j, qc in enumerate(qcs):
                oa, ob = (o[j * n_rows:(j + 1) * n_rows] for o in stacked)
                sums = pltpu.roll(jnp.where(low_out, ob, oa), HEAD_DIM, axis=1)
                o2 = jnp.where(low_out, oa, ob) * (1.0 / sums)
                if len(kv_sets) == 1:
                    store(qc, o2)
                else:
                    out_cols[qc] = o2 if out_cols[qc] is None else out_cols[qc] + o2

    if len(kv_sets) > 1:
        for qc, o in enumerate(out_cols):
            store(qc, o)


def _attention_scratch(n_sets, n_rows, n_keys):
    n = n_sets * N_Q_HEADS
    return [pltpu.VMEM((n, n_rows, n_keys), F32), pltpu.VMEM((n, n_rows, n_keys), BF16)]


def _swa_prompt_kernel(front, sink_ref, q_ref, gate_ref, kc_ref, kp_ref, vc_ref, vp_ref,
                       og_in_ref, og_ref, s_scr, p_scr):
    del og_in_ref
    blk = pl.program_id(1)
    row = lax.broadcasted_iota(jnp.int32, (CHUNK, 2 * CHUNK), 0)
    col = lax.broadcasted_iota(jnp.int32, (CHUNK, 2 * CHUNK), 1)
    mask = (col > row) & (col <= row + WINDOW) & ((blk - 1) * CHUNK + col >= front)
    lane_pairs = [slice(c * LANES, (c + 1) * LANES) for c in range(KV_WIDTH // LANES)]
    keys = [jnp.concatenate([kp_ref[:, c], kc_ref[:, c]], axis=0) for c in lane_pairs]
    vals = [jnp.concatenate([vp_ref[:, c], vc_ref[:, c]], axis=0) for c in lane_pairs]
    _attend_rows(q_ref, gate_ref, og_ref, sink_ref, [_natural_kv_set(keys, vals, mask)],
                 s_scr, p_scr)


def _swa_prompt(pq, kv, sinks, og, n_batch, chunks, front):
    cur = lambda col: (lambda b, i: (b * chunks + i, col))
    prev = lambda col: (lambda b, i: (b * chunks + jnp.maximum(i - 1, 0), col))
    return pl.pallas_call(
        functools.partial(_swa_prompt_kernel, front),
        grid=(n_batch, chunks),
        in_specs=[
            pl.BlockSpec(memory_space=pltpu.SMEM),
            pl.BlockSpec((CHUNK, Q_WIDTH), cur(0)),
            pl.BlockSpec((CHUNK, Q_WIDTH), cur(1)),
            pl.BlockSpec((CHUNK, KV_WIDTH), cur(0)),
            pl.BlockSpec((CHUNK, KV_WIDTH), prev(0)),
            pl.BlockSpec((CHUNK, KV_WIDTH), cur(1)),
            pl.BlockSpec((CHUNK, KV_WIDTH), prev(1)),
            pl.BlockSpec(memory_space=pl.ANY),
        ],
        out_specs=pl.BlockSpec((CHUNK, Q_WIDTH), cur(0)),
        out_shape=jax.ShapeDtypeStruct(og.shape, BF16),
        scratch_shapes=_attention_scratch(1, CHUNK, 2 * CHUNK),
        input_output_aliases={7: 0},
        compiler_params=_params(("parallel", "parallel")),
        name="swa_prompt",
    )(sinks, pq, pq, kv, kv, kv, kv, og)


def _swa_sample_kernel(t_len, sink_ref, q_ref, gate_ref, kn_ref, vn_ref, ckt_ref, cvt_ref,
                       og_in_ref, og_ref, s_scr, p_scr):
    del og_in_ref
    n = SEQ_GROUP * t_len
    win = ckt_ref.shape[2]
    row = lax.broadcasted_iota(jnp.int32, (n, 2 * win), 0)
    col = lax.broadcasted_iota(jnp.int32, (n, 2 * win), 1)
    t = row % t_len
    new = col - win
    in_window = ((col < win) & (col > t)) | (
        (new >= 0) & (new < n) & ((new // t_len) == (row // t_len)) & ((new % t_len) <= t))
    pad = jnp.zeros((win - n, LANES), F32)
    lane_pairs = [slice(c * LANES, (c + 1) * LANES) for c in range(KV_WIDTH // LANES)]
    k_new_t = [jnp.concatenate([kn_ref[:, c], pad], axis=0).T for c in lane_pairs]
    v_new_t = [jnp.concatenate([vn_ref[:, c], pad], axis=0).T for c in lane_pairs]
    kv_sets = [_window_kv_set(ckt_ref.at[u], cvt_ref.at[u], k_new_t, v_new_t,
                              in_window & ((row // t_len) == u))
               for u in range(SEQ_GROUP)]
    _attend_rows(q_ref, gate_ref, og_ref, sink_ref, kv_sets, s_scr, p_scr)


def _swa_sample(pq, kv, cache_kt, cache_vt, sinks, og, n_seq, t_len, row0):
    n = SEQ_GROUP * t_len
    rb0 = row0 // n
    win = cache_kt.shape[2]
    return pl.pallas_call(
        functools.partial(_swa_sample_kernel, t_len),
        grid=(n_seq // SEQ_GROUP,),
        in_specs=[
            pl.BlockSpec(memory_space=pltpu.SMEM),
            pl.BlockSpec((n, Q_WIDTH), lambda s: (rb0 + s, 0)),
            pl.BlockSpec((n, Q_WIDTH), lambda s: (rb0 + s, 1)),
            pl.BlockSpec((n, KV_WIDTH), lambda s: (rb0 + s, 0)),
            pl.BlockSpec((n, KV_WIDTH), lambda s: (rb0 + s, 1)),
            pl.BlockSpec((SEQ_GROUP, KV_WIDTH, win), lambda s: (s, 0, 0)),
            pl.BlockSpec((SEQ_GROUP, KV_WIDTH, win), lambda s: (s, 0, 0)),
            pl.BlockSpec(memory_space=pl.ANY),
        ],
        out_specs=pl.BlockSpec((n, Q_WIDTH), lambda s: (rb0 + s, 0)),
        out_shape=jax.ShapeDtypeStruct(og.shape, BF16),
        scratch_shapes=_attention_scratch(SEQ_GROUP, n, 2 * win),
        input_output_aliases={7: 0},
        compiler_params=_params(("parallel",)),
        name="swa_sample",
    )(sinks, pq, pq, kv, kv, cache_kt, cache_vt, og)


def kernel(x_prompt, x_sample, state_gla, cache_k_win, cache_v_win, meta_tokens, norm_a, w_in_a, w_gk2_a, b_gk2_a, onorm_a, w_out_a, norm_kv, w_kv, norm_b, w_in_b, sinks_b, w_out_b, norm_f):
    n_batch, seq, _ = x_prompt.shape
    n_seq, t_len, _ = x_sample.shape
    win = cache_k_win.shape[1]
    n_a = w_in_a.shape[0]
    n_b = w_in_b.shape[0]
    assert win == WINDOW == CHUNK and n_seq % SEQ_GROUP == 0 and n_seq % GLA_SEQ_GROUP == 0

    length = seq + N_META
    front = (-length) % CHUNK
    lp = length + front
    chunks = lp // CHUNK
    srow = n_batch * lp
    ns = n_seq * t_len
    rows = -(-(srow + ns) // ROW_TILE) * ROW_TILE

    assert front + N_META == CHUNK and ns == CHUNK and seq % CHUNK == 0
    assert rows % CHUNK == 0 and rows % (16 * IN_PROJ_ROW_STEPS) == 0

    w_in_a_t = jnp.swapaxes(w_in_a, 1, 2)
    w_out_a_bf16 = w_out_a.astype(BF16)
    w_out_b_bf16 = w_out_b.astype(BF16)

    def gate_weights(l):
        w_gate = jnp.pad(w_in_a_t[l, GLA_MAIN_WIDTH:], ((0, LANES - GATE_RANK), (0, 0)))
        w_gk = jnp.pad(w_gk2_a[l], ((0, LANES - GATE_RANK), (0, 0))).astype(BF16)
        return w_gate, w_gk, b_gk2_a[l][None]

    extras = jnp.concatenate([jnp.zeros((front, D_MODEL), F32), meta_tokens.astype(F32),
                              x_sample.reshape(ns, D_MODEL), jnp.zeros((CHUNK, D_MODEL), F32)])
    tokens = ((x_prompt.reshape(n_batch * seq, D_MODEL), extras), chunks, n_batch)
    hn, g = _embed(tokens[0], norm_a[0], gate_weights(0), rows, chunks, n_batch)
    x = None
    og = jnp.zeros((rows, D_MODEL), BF16)

    states_p = []
    states_s = None
    for l in range(n_a):
        proj = _in_proj(hn, w_in_a_t, l, GLA_MAIN_WIDTH, BF16, True, "gla_in_proj",
                        q_cols=GLA_KEY_DIM, q_scale=GLA_DK ** -0.5)
        onorm = onorm_a[l][None]
        og, s_p = _gla_prompt(proj, g, onorm, og, n_batch, chunks)
        og, states_s = _gla_sample(proj, g, onorm, state_gla, og, states_s, l,
                                   n_seq, t_len, srow)
        states_p.append(s_p)
        w_out = (w_out_a_bf16, l)
        if l + 1 < n_a:
            x, hn, g = _out_proj(og, w_out, x, [norm_a[l + 1]], gate_weights(l + 1), True,
                                 BF16, "gla_out_proj", tokens=tokens)
        else:
            x, hn_kv, hn = _out_proj(og, w_out, x, [norm_kv, norm_b[0]], None, True,
                                     BF16, "gla_out_proj", tokens=tokens)

    kv = _in_proj(hn_kv, w_kv[None], 0, 2 * KV_WIDTH, F32, False, "kv_proj")
    cache_k = jnp.transpose(cache_k_win, (0, 2, 3, 1)).reshape(n_seq, KV_WIDTH, win)
    cache_v = jnp.transpose(cache_v_win, (0, 2, 3, 1)).reshape(n_seq, KV_WIDTH, win)
    for j in range(n_b):
        pq = _in_proj(hn, w_in_b, j, 2 * Q_WIDTH, BF16, False, "swa_in_proj",
                      q_cols=Q_WIDTH, q_scale=HEAD_DIM ** -0.5 * LOG2_E)
        og = _swa_prompt(pq, kv, sinks_b[j], og, n_batch, chunks, front)
        og = _swa_sample(pq, kv, cache_k, cache_v, sinks_b[j], og, n_seq, t_len, srow)
        w_out = (w_out_b_bf16, j)
        if j + 1 < n_b:
            x, hn = _out_proj(og, w_out, x, [norm_b[j + 1]], None, True, BF16, "swa_out_proj")

    tiles_per_batch = seq // ROW_TILE
    lead_blocks = (lp - seq) // CHUNK

    def prompt_first_block(t):
        return ((t // tiles_per_batch) * chunks + lead_blocks
                + (t % tiles_per_batch) * (ROW_TILE // CHUNK))

    (y_prompt,) = _out_proj(og, w_out, x, [norm_f], None, False, F32, "final_prompt",
                            gather=(ROW_TILE, n_batch * tiles_per_batch, prompt_first_block))
    (y_sample,) = _out_proj(og, w_out, x, [norm_f], None, False, F32, "final_sample",
                            gather=(CHUNK, ns // CHUNK, lambda t: srow // CHUNK + t))
    y_prompt = y_prompt.reshape(n_batch, seq, D_MODEL)
    y_sample = y_sample.reshape(n_seq, t_len, D_MODEL)
    kv_p = jnp.stack([kv[(b + 1) * lp - win:(b + 1) * lp] for b in range(n_batch)])
    kv_p = kv_p.reshape(n_batch, win, 2, N_KV_HEADS, HEAD_DIM)
    kv_s = kv[srow:srow + ns].reshape(n_seq, t_len, 2, N_KV_HEADS, HEAD_DIM)
    k_win_s = jnp.concatenate([cache_k_win, kv_s[:, :, 0]], axis=1)[:, -win:]
    v_win_s = jnp.concatenate([cache_v_win, kv_s[:, :, 1]], axis=1)[:, -win:]
    return (y_prompt, y_sample, jnp.stack(states_p), states_s,
            kv_p[:, :, 0], kv_p[:, :, 1], k_win_s, v_win_s)
```

```python
import functools

import jax
import jax.numpy as jnp
from jax import lax
from jax.experimental import pallas as pl
from jax.experimental.pallas import tpu as pltpu

F32 = jnp.float32
BF16 = jnp.bfloat16

D_MODEL = 2048
N_META = 16
GLA_HEADS = 4
GLA_DK = 256
GLA_DV = 512
GLA_KEY_DIM = GLA_HEADS * GLA_DK
GLA_VAL_DIM = GLA_HEADS * GLA_DV
GLA_MAIN_WIDTH = 2 * GLA_KEY_DIM + 2 * GLA_VAL_DIM
GATE_RANK = 16
GATE_LOGIT_NORM = 16.0
HEAD_DIM = 64
N_Q_HEADS = 32
N_KV_HEADS = 8
Q_WIDTH = N_Q_HEADS * HEAD_DIM
KV_WIDTH = N_KV_HEADS * HEAD_DIM
WINDOW = 128
RMS_EPS = 1e-6
NEG_INF = -1e30
LOG2_E = 1.4426950408889634

LANES = 128
CHUNK = 128
GLA_LEAF = 8
LEAF_BATCH = 4
GLA_CHUNKS_PER_STEP = 3
GLA_BOUNDED_RANGE = 60.0
ROW_TILE = 512
IN_PROJ_COLS = 1024
IN_PROJ_ROW_STEPS = 8
SEQ_GROUP = 4
GLA_SEQ_GROUP = 8
VMEM_LIMIT = 48 * 1024 * 1024

NT_DIMS = (((1,), (1,)), ((), ()))


def _silu(x):
    return x * (1.0 / (1.0 + jnp.exp2(x * -LOG2_E)))


def _params(sem):
    return pltpu.CompilerParams(dimension_semantics=sem, vmem_limit_bytes=VMEM_LIMIT)


def _emit_normed(y, nw_refs, gate_refs, hn_refs, g_ref):
    ms = jnp.mean(y * y, axis=-1, keepdims=True)
    yn = y * lax.rsqrt(ms + RMS_EPS)
    for nw_ref, hn_ref in zip(nw_refs, hn_refs):
        hn_ref[...] = (yn * nw_ref[...]).astype(hn_ref.dtype)
    if gate_refs:
        g_ref[...] = _decay_gate(hn_refs[0][...], *gate_refs)


def _split_refs(refs, n_norms, with_gate):
    nw_refs = refs[:n_norms]
    gate_refs = refs[n_norms:n_norms + 3] if with_gate else None
    return nw_refs, gate_refs, refs[n_norms + 3 * with_gate:]


def _norm_specs(norms, gate, index):
    specs = [pl.BlockSpec((1, D_MODEL), index) for _ in norms]
    args = [nw[None] for nw in norms]
    if gate is not None:
        specs += [pl.BlockSpec((LANES, D_MODEL), index),
                  pl.BlockSpec((LANES, GLA_KEY_DIM), index),
                  pl.BlockSpec((1, GLA_KEY_DIM), index)]
        args += list(gate)
    return specs, args


TOKEN_SUB = ROW_TILE // CHUNK


def _token_rows(chunks, n_batch, src_refs):
    xp_refs, ex_refs = src_refs[:TOKEN_SUB], src_refs[TOKEN_SUB:]
    pieces = []
    for k in range(TOKEN_SUB):
        blk = pl.program_id(0) * TOKEN_SUB + k
        from_prompt = jnp.logical_and(blk % chunks != 0, blk < n_batch * chunks)
        pieces.append(jnp.where(from_prompt, xp_refs[k][...], ex_refs[k][...]))
    return jnp.concatenate(pieces, axis=0)


def _token_sources(x_prompt2d, extras, chunks, n_batch):
    prompt_blocks = x_prompt2d.shape[0] // CHUNK
    per_batch = prompt_blocks // n_batch

    def prompt_index(k):
        def index(i):
            blk = i * TOKEN_SUB + k
            src = (blk // chunks) * per_batch + (blk % chunks) - 1
            return (jnp.clip(src, 0, prompt_blocks - 1), 0)
        return index

    def extras_index(k):
        def index(i):
            blk = i * TOKEN_SUB + k
            return (jnp.clip(blk - n_batch * chunks + 1, 0, 2), 0)
        return index

    specs = ([pl.BlockSpec((CHUNK, D_MODEL), prompt_index(k)) for k in range(TOKEN_SUB)]
             + [pl.BlockSpec((CHUNK, D_MODEL), extras_index(k)) for k in range(TOKEN_SUB)])
    return specs, [x_prompt2d] * TOKEN_SUB + [extras] * TOKEN_SUB


def _embed_kernel(chunks, n_batch, with_gate, *refs):
    nw_refs, gate_refs, outs = _split_refs(refs[2 * TOKEN_SUB:], 1, with_gate)
    x = _token_rows(chunks, n_batch, refs[:2 * TOKEN_SUB])
    _emit_normed(x, nw_refs, gate_refs, outs[0:1], outs[1] if with_gate else None)


def _embed(tokens, norm, gate, rows, chunks, n_batch):
    src_specs, src_args = _token_sources(*tokens, chunks, n_batch)
    norm_specs, norm_args = _norm_specs([norm], gate, lambda i: (0, 0))
    out_specs = [pl.BlockSpec((ROW_TILE, D_MODEL), lambda i: (i, 0))]
    out_shape = [jax.ShapeDtypeStruct((rows, D_MODEL), BF16)]
    if gate is not None:
        out_specs.append(pl.BlockSpec((ROW_TILE, GLA_KEY_DIM), lambda i: (i, 0)))
        out_shape.append(jax.ShapeDtypeStruct((rows, GLA_KEY_DIM), F32))
    return pl.pallas_call(
        functools.partial(_embed_kernel, chunks, n_batch, gate is not None),
        grid=(rows // ROW_TILE,),
        in_specs=src_specs + norm_specs,
        out_specs=out_specs,
        out_shape=out_shape,
        compiler_params=_params(("parallel",)),
        name="embed",
    )(*src_args, *norm_args)


def _decay_gate(hn, wg_ref, wgk_ref, bgk_ref):
    glr = lax.dot_general(hn, wg_ref[...].astype(BF16), NT_DIMS, preferred_element_type=F32)
    z = jnp.dot(glr.astype(BF16), wgk_ref[...], preferred_element_type=F32) + bgk_ref[...]
    log_sig = jnp.minimum(z, 0.0) - jnp.log(1.0 + jnp.exp(-jnp.abs(z)))
    return log_sig * (LOG2_E / GATE_LOGIT_NORM)


def _in_proj_kernel(transposed, scale_ref, hn_ref, w_ref, o_ref, w_bf16_ref):
    @pl.when(pl.program_id(1) == 0)
    def _():
        w_bf16_ref[...] = w_ref[...].astype(BF16)

    dims = NT_DIMS if transposed else (((1,), (0,)), ((), ()))
    o = lax.dot_general(hn_ref[...], w_bf16_ref[...], dims, preferred_element_type=F32)
    o_ref[...] = (o * scale_ref[pl.program_id(0)]).astype(o_ref.dtype)


def _in_proj(hn, w_stack, layer, n_cols, out_dtype, transposed, name, q_cols=0, q_scale=1.0):
    rows = hn.shape[0]
    row_tile = rows // IN_PROJ_ROW_STEPS
    n_slabs = n_cols // IN_PROJ_COLS
    assert q_cols % IN_PROJ_COLS == 0
    scales = jnp.where(jnp.arange(n_slabs) < q_cols // IN_PROJ_COLS, q_scale, 1.0).astype(F32)
    if transposed:
        slab = (IN_PROJ_COLS, D_MODEL)
        w_spec = pl.BlockSpec((None,) + slab, lambda j, i: (layer, j, 0))
    else:
        slab = (D_MODEL, IN_PROJ_COLS)
        w_spec = pl.BlockSpec((None,) + slab, lambda j, i: (layer, 0, j))
    return pl.pallas_call(
        functools.partial(_in_proj_kernel, transposed),
        grid=(n_slabs, IN_PROJ_ROW_STEPS),
        in_specs=[pl.BlockSpec(memory_space=pltpu.SMEM),
                  pl.BlockSpec((row_tile, D_MODEL), lambda j, i: (i, 0)), w_spec],
        out_specs=pl.BlockSpec((row_tile, IN_PROJ_COLS), lambda j, i: (i, j)),
        out_shape=jax.ShapeDtypeStruct((rows, n_cols), out_dtype),
        scratch_shapes=[pltpu.VMEM(slab, BF16)],
        compiler_params=_params(("parallel", "arbitrary")),
        name=name,
    )(scales, hn, w_stack)


def _out_proj_kernel(sub, n_norms, with_gate, keep_x, token_layout, *refs):
    og_refs, w_ref, refs = refs[:sub], refs[sub], refs[sub + 1:]
    n_x = sub if token_layout is None else 2 * TOKEN_SUB
    x_refs, refs = refs[:n_x], refs[n_x:]
    nw_refs, gate_refs, outs = _split_refs(refs, n_norms, with_gate)
    og = og_refs[0][...] if sub == 1 else jnp.concatenate([r[...] for r in og_refs], axis=0)
    if token_layout is not None:
        x = _token_rows(*token_layout, x_refs)
    elif sub == 1:
        x = x_refs[0][...]
    else:
        x = jnp.concatenate([r[...] for r in x_refs], axis=0)
    y = x + jnp.dot(og, w_ref[...], preferred_element_type=F32)
    if keep_x:
        outs[0][...] = y
        outs = outs[1:]
    _emit_normed(y, nw_refs, gate_refs, outs[:n_norms], outs[n_norms] if with_gate else None)


def _out_proj(og, w, x, norms, gate, keep_x, normed_dtype, name, gather=None, tokens=None):
    if gather is None:
        tile_rows, n_tiles, sub = ROW_TILE, og.shape[0] // ROW_TILE, 1
        in_row_specs = [pl.BlockSpec((ROW_TILE, D_MODEL), lambda i: (i, 0))]
    else:
        tile_rows, n_tiles, first_block = gather
        sub = tile_rows // CHUNK
        in_row_specs = [pl.BlockSpec((CHUNK, D_MODEL),
                                     functools.partial(lambda k, i: (first_block(i) + k, 0), k))
                        for k in range(sub)]
    rows = tile_rows * n_tiles
    row_spec = pl.BlockSpec((tile_rows, D_MODEL), lambda i: (i, 0))
    norm_specs, norm_args = _norm_specs(norms, gate, lambda i: (0, 0))
    out_specs, out_shape = [], []
    if keep_x:
        out_specs.append(row_spec)
        out_shape.append(jax.ShapeDtypeStruct((rows, D_MODEL), F32))
    for _ in norms:
        out_specs.append(row_spec)
        out_shape.append(jax.ShapeDtypeStruct((rows, D_MODEL), normed_dtype))
    if gate is not None:
        out_specs.append(pl.BlockSpec((tile_rows, GLA_KEY_DIM), lambda i: (i, 0)))
        out_shape.append(jax.ShapeDtypeStruct((rows, GLA_KEY_DIM), F32))
    w_stack, layer = w
    w_spec = pl.BlockSpec((None, D_MODEL, D_MODEL), lambda i: (layer, 0, 0),
                          pipeline_mode=pl.Buffered(1))
    if x is None:
        sources, chunks, n_batch = tokens
        x_specs, x_args = _token_sources(*sources, chunks, n_batch)
        token_layout = (chunks, n_batch)
    else:
        x_specs, x_args, token_layout = in_row_specs, [x] * sub, None
    return pl.pallas_call(
        functools.partial(_out_proj_kernel, sub, len(norms), gate is not None, keep_x,
                          token_layout),
        grid=(n_tiles,),
        in_specs=in_row_specs + [w_spec] + x_specs + norm_specs,
        out_specs=out_specs,
        out_shape=out_shape,
        compiler_params=_params(("parallel",)),
        name=name,
    )(*([og] * sub), w_stack, *x_args, *norm_args)


def _exact_block(qs, k, b, n, width, col0):
    lane = lax.broadcasted_iota(jnp.int32, (n, width), 1)
    acc = jnp.zeros((n, width), F32)
    for j in range(n):
        t = qs * k[j:j + 1] * jnp.exp2(jnp.minimum(b - b[j:j + 1], 0.0))
        acc = jnp.where(lane == col0 + j, jnp.sum(t, axis=-1, keepdims=True), acc)
    return acc


def _masked_row_sums(mask_bf16, g):
    g_hi = g.astype(BF16)
    r1 = g - g_hi.astype(F32)
    g_mid = r1.astype(BF16)
    g_lo = (r1 - g_mid.astype(F32)).astype(BF16)
    return (jnp.dot(mask_bf16, g_hi, preferred_element_type=F32)
            + jnp.dot(mask_bf16, g_mid, preferred_element_type=F32)
            + jnp.dot(mask_bf16, g_lo, preferred_element_type=F32))


def _column_of(row_vec, n):
    return jnp.broadcast_to(row_vec, (LANES, n)).T[:, 0:1]


def _head_norm_gate(o, onorm, gate):
    ms = jnp.mean(o * o, axis=-1, keepdims=True)
    return (o * lax.rsqrt(ms + RMS_EPS) * onorm * _silu(gate)).astype(BF16)


def _intra_chunk_scores(qs, k, k_bf, b, row_xor_col):
    c_len = qs.shape[0]
    leaf = GLA_LEAF
    leaf_row = lax.broadcasted_iota(jnp.int32, (leaf, c_len), 0)
    leaf_col = lax.broadcasted_iota(jnp.int32, (leaf, c_len), 1)
    lane_in_leaf = [(leaf_col & (leaf - 1)) == j for j in range(leaf)]

    att_rows = []
    for blk0 in range(0, c_len // leaf, LEAF_BATCH):
        lhs = []
        for blk in range(blk0, blk0 + LEAF_BATCH):
            q_blk = qs[blk * leaf:(blk + 1) * leaf]
            b_blk = b[blk * leaf:(blk + 1) * leaf]
            lhs += [q_blk * jnp.exp2(jnp.minimum(b_blk - b_blk[j:j + 1], 0.0))
                    for j in range(leaf)]
        r = lax.dot_general(jnp.concatenate(lhs, axis=0).astype(BF16), k_bf, NT_DIMS,
                            preferred_element_type=F32)
        for i, blk in enumerate(range(blk0, blk0 + LEAF_BATCH)):
            acc = jnp.zeros((leaf, c_len), F32)
            for j in range(leaf):
                r0 = (i * leaf + j) * leaf
                acc = jnp.where(lane_in_leaf[j], r[r0:r0 + leaf], acc)
            rel = leaf_col - blk * leaf
            att_rows.append(jnp.where((rel >= 0) & (rel <= leaf_row), acc, 0.0))
    att = jnp.concatenate(att_rows, axis=0)

    m = c_len // 2
    while m >= leaf:
        q_parts, k_parts = [], []
        zeros = jnp.zeros((m, GLA_DK), F32)
        for base in range(0, c_len, 2 * m):
            ref = b[base + m - 1:base + m]
            lo = slice(base, base + m)
            hi = slice(base + m, base + 2 * m)
            k_parts += [k[lo] * jnp.exp2(ref - b[lo]), zeros]
            q_parts += [zeros, qs[hi] * jnp.exp2(b[hi] - ref)]
        qt = jnp.concatenate(q_parts, axis=0).astype(BF16)
        kt = jnp.concatenate(k_parts, axis=0).astype(BF16)
        a = lax.dot_general(qt, kt, NT_DIMS, preferred_element_type=F32)
        att = att + jnp.where(row_xor_col < 2 * m, a, 0.0)
        m //= 2
    return att


def _gla_prompt_kernel(q_ref, k_ref, v_ref, gate_ref, g_ref, onorm_ref, og_in_ref,
                       og_ref, s_ref, b_scr, qd_scr, att_scr):
    del og_in_ref
    c_len = CHUNK
    heads = [(slice(h * GLA_DK, (h + 1) * GLA_DK), slice(h * GLA_DV, (h + 1) * GLA_DV))
             for h in range(GLA_HEADS)]

    @pl.when(pl.program_id(1) == 0)
    def _():
        s_ref[...] = jnp.zeros_like(s_ref)

    row = lax.broadcasted_iota(jnp.int32, (c_len, c_len), 0)
    col = lax.broadcasted_iota(jnp.int32, (c_len, c_len), 1)
    lower = row >= col
    tri = jnp.where(lower, 1.0, 0.0).astype(BF16)
    onorm = onorm_ref[...]
    chunk_rows = [slice(c * c_len, (c + 1) * c_len) for c in range(q_ref.shape[0] // c_len)]

    total = None
    for rows in chunk_rows:
        b_c = _masked_row_sums(tri, g_ref[rows, :])
        b_scr[rows, :] = b_c
        qd_scr[rows, :] = (q_ref[rows, :].astype(F32) * jnp.exp2(b_c)).astype(BF16)
        last = b_c[c_len - 1:c_len]
        total = last if total is None else jnp.minimum(total, last)

    bounded = jnp.max(-total) <= GLA_BOUNDED_RANGE

    @pl.when(bounded)
    def _():
        for c, rows in enumerate(chunk_rows):
            for h, (ks, _) in enumerate(heads):
                k_inv = (k_ref[rows, ks].astype(F32) * jnp.exp2(-b_scr[rows, ks])).astype(BF16)
                a = lax.dot_general(qd_scr[rows, ks], k_inv, NT_DIMS,
                                    preferred_element_type=F32)
                att_scr[c * GLA_HEADS + h] = jnp.where(lower, a, 0.0).astype(BF16)

    @pl.when(jnp.logical_not(bounded))
    def _():
        for c, rows in enumerate(chunk_rows):
            for h, (ks, _) in enumerate(heads):
                k_bf = k_ref[rows, ks]
                att_scr[c * GLA_HEADS + h] = _intra_chunk_scores(
                    q_ref[rows, ks].astype(F32), k_bf.astype(F32), k_bf,
                    b_scr[rows, ks], row ^ col).astype(BF16)

    for c, rows in enumerate(chunk_rows):
        for h, (ks, vs) in enumerate(heads):
            k = k_ref[rows, ks].astype(F32)
            v = v_ref[rows, vs]
            b = b_scr[rows, ks]
            b_last = b[c_len - 1:c_len]
            state = s_ref[0, h]

            o = jnp.dot(qd_scr[rows, ks], state.astype(BF16), preferred_element_type=F32)
            o = o + jnp.dot(att_scr[c * GLA_HEADS + h], v, preferred_element_type=F32)

            kd_t = (k * jnp.exp2(b_last - b)).T.astype(BF16)
            s_ref[0, h] = (state * _column_of(jnp.exp2(b_last), GLA_DK)
                           + jnp.dot(kd_t, v, preferred_element_type=F32))

            og_ref[rows, vs] = _head_norm_gate(o, onorm, gate_ref[rows, vs].astype(F32))


def _gla_prompt(proj, g, onorm, og, n_batch, chunks):
    rows = proj.shape[0]
    per_step = next(n for n in (GLA_CHUNKS_PER_STEP, 1) if chunks % n == 0)
    steps = chunks // per_step
    tile = per_step * CHUNK
    blk = lambda col: (lambda b, c: (b * steps + c, col))
    return pl.pallas_call(
        _gla_prompt_kernel,
        grid=(n_batch, steps),
        in_specs=[
            pl.BlockSpec((tile, GLA_KEY_DIM), blk(0)),
            pl.BlockSpec((tile, GLA_KEY_DIM), blk(1)),
            pl.BlockSpec((tile, GLA_VAL_DIM), blk(1)),
            pl.BlockSpec((tile, GLA_VAL_DIM), blk(2)),
            pl.BlockSpec((tile, GLA_KEY_DIM), blk(0)),
            pl.BlockSpec((1, GLA_DV), lambda b, c: (0, 0)),
            pl.BlockSpec(memory_space=pl.ANY),
        ],
        out_specs=[
            pl.BlockSpec((tile, GLA_VAL_DIM), blk(0)),
            pl.BlockSpec((1, GLA_HEADS, GLA_DK, GLA_DV), lambda b, c: (b, 0, 0, 0)),
        ],
        out_shape=[
            jax.ShapeDtypeStruct((rows, GLA_VAL_DIM), BF16),
            jax.ShapeDtypeStruct((n_batch, GLA_HEADS, GLA_DK, GLA_DV), F32),
        ],
        scratch_shapes=[pltpu.VMEM((tile, GLA_KEY_DIM), F32),
                        pltpu.VMEM((tile, GLA_KEY_DIM), BF16),
                        pltpu.VMEM((per_step * GLA_HEADS, CHUNK, CHUNK), BF16)],
        input_output_aliases={6: 0},
        compiler_params=_params(("parallel", "arbitrary")),
        name="gla_prompt",
    )(proj, proj, proj, proj, g, onorm, og)


def _gla_sample_kernel(t_len, q_ref, k_ref, v_ref, gate_ref, g_ref, onorm_ref, s0_ref,
                       og_in_ref, *rest):
    og_ref, s_ref = rest[-2], rest[-1]
    n = GLA_SEQ_GROUP * t_len
    row = lax.broadcasted_iota(jnp.int32, (n, LANES), 0)
    col = lax.broadcasted_iota(jnp.int32, (n, LANES), 1)
    causal = ((row // t_len) == (col // t_len)) & (row >= col)
    row_seq = lax.broadcasted_iota(jnp.int32, (n, 1), 0) // t_len

    qs = q_ref[...].astype(F32)
    k = k_ref[...].astype(F32)
    pad_k = jnp.zeros((LANES - n, GLA_DK), F32)
    v_pad = jnp.concatenate([v_ref[...], jnp.zeros((LANES - n, GLA_DV), BF16)], axis=0)
    b = _masked_row_sums(jnp.where(causal, 1.0, 0.0).astype(BF16),
                         jnp.concatenate([g_ref[...], pad_k], axis=0))
    att = jnp.where(causal, _exact_block(qs, k, b, n, LANES, 0), 0.0)
    o = jnp.dot(att.astype(BF16), v_pad, preferred_element_type=F32)

    q_dec = (qs * jnp.exp2(b)).astype(BF16)
    for u in range(GLA_SEQ_GROUP):
        state = s0_ref[0, u, 0]
        o_u = jnp.dot(q_dec, state.astype(BF16), preferred_element_type=F32)
        o = o + jnp.where(row_seq == u, o_u, 0.0)
        b_last = b[(u + 1) * t_len - 1:(u + 1) * t_len]
        kd = jnp.where(row_seq == u, k * jnp.exp2(jnp.minimum(b_last - b, 0.0)), 0.0)
        kd_t = jnp.concatenate([kd, pad_k], axis=0).T.astype(BF16)
        s_ref[0, u, 0] = (state * _column_of(jnp.exp2(b_last), GLA_DK)
                          + jnp.dot(kd_t, v_pad, preferred_element_type=F32))

    og_ref[...] = _head_norm_gate(o, onorm_ref[...], gate_ref[...].astype(F32))


def _gla_sample(proj, g, onorm, state_in, og, s_out, layer, n_seq, t_len, row0):
    n = GLA_SEQ_GROUP * t_len
    rb0 = row0 // n
    n_layers = state_in.shape[0]
    key_blocks = GLA_KEY_DIM // GLA_DK
    val_blocks = GLA_VAL_DIM // GLA_DV
    st_spec = pl.BlockSpec((1, GLA_SEQ_GROUP, 1, GLA_DK, GLA_DV),
                           lambda s, h: (layer, s, h, 0, 0))
    in_specs = [
        pl.BlockSpec((n, GLA_DK), lambda s, h: (rb0 + s, h)),
        pl.BlockSpec((n, GLA_DK), lambda s, h: (rb0 + s, key_blocks + h)),
        pl.BlockSpec((n, GLA_DV), lambda s, h: (rb0 + s, val_blocks + h)),
        pl.BlockSpec((n, GLA_DV), lambda s, h: (rb0 + s, 2 * val_blocks + h)),
        pl.BlockSpec((n, GLA_DK), lambda s, h: (rb0 + s, h)),
        pl.BlockSpec((1, GLA_DV), lambda s, h: (0, 0)),
        st_spec,
        pl.BlockSpec(memory_space=pl.ANY),
    ]
    args = [proj, proj, proj, proj, g, onorm, state_in, og]
    aliases = {7: 0}
    if s_out is not None:
        in_specs.append(pl.BlockSpec(memory_space=pl.ANY))
        args.append(s_out)
        aliases[8] = 1
    return pl.pallas_call(
        functools.partial(_gla_sample_kernel, t_len),
        grid=(n_seq // GLA_SEQ_GROUP, GLA_HEADS),
        in_specs=in_specs,
        out_specs=[
            pl.BlockSpec((n, GLA_DV), lambda s, h: (rb0 + s, h)),
            st_spec,
        ],
        out_shape=[
            jax.ShapeDtypeStruct(og.shape, BF16),
            jax.ShapeDtypeStruct((n_layers, n_seq, GLA_HEADS, GLA_DK, GLA_DV), F32),
        ],
        input_output_aliases=aliases,
        compiler_params=_params(("parallel", "parallel")),
        name="gla_sample",
    )(*args)


def _nt_dot(a, b):
    return lax.dot_general(a, b, NT_DIMS, preferred_element_type=F32)


def _lane_padded_copies(pair, fill):
    low = lax.broadcasted_iota(jnp.int32, pair.shape, 1) < HEAD_DIM
    swapped = pltpu.roll(pair, HEAD_DIM, axis=1).astype(BF16)
    pair = pair.astype(BF16)
    fill = jnp.full((), fill, BF16)
    return ((jnp.where(low, pair, fill), jnp.where(low, fill, swapped)),
            (jnp.where(low, swapped, fill), jnp.where(low, fill, pair)))


def _natural_kv_set(keys, vals, mask):
    slot0 = lax.broadcasted_iota(jnp.int32, keys[0].shape, 0) == 0

    def score_fns(cpair):
        return tuple(tuple(functools.partial(_nt_dot, b=k) for k in pads)
                     for pads in _lane_padded_copies(keys[cpair], 0.0))

    def value_fns(cpair):
        pads_by_head = _lane_padded_copies(jnp.where(slot0, 0.0, vals[cpair]), 1.0)
        return tuple(tuple((lambda p, v=v: jnp.dot(p, v, preferred_element_type=F32))
                           for v in pads) for pads in pads_by_head)

    return score_fns, value_fns, mask


def _window_kv_set(kt_ref, vt_ref, k_new_t, v_new_t, mask):
    win = kt_ref.shape[1]
    lane0 = lax.broadcasted_iota(jnp.int32, (HEAD_DIM, win), 1) == 0

    def head_keys(old_ref, new_t, cpair, parity, zero_slot0):
        r0 = (2 * cpair + parity) * HEAD_DIM
        old = old_ref[r0:r0 + HEAD_DIM, :]
        if zero_slot0:
            old = jnp.where(lane0, 0.0, old)
        new = new_t[cpair][parity * HEAD_DIM:(parity + 1) * HEAD_DIM, :]
        return jnp.concatenate([old, new], axis=1).astype(BF16)

    def stacks(head, other):
        return (jnp.concatenate([head, other], axis=0), jnp.concatenate([other, head], axis=0))

    def score_fns(cpair):
        fns = []
        for parity in range(2):
            head = head_keys(kt_ref, k_new_t, cpair, parity, False)
            fns.append(tuple(
                (lambda q, k=k: jnp.dot(q, k, preferred_element_type=F32))
                for k in stacks(head, jnp.zeros_like(head))))
        return tuple(fns)

    def value_fns(cpair):
        fns = []
        for parity in range(2):
            head = head_keys(vt_ref, v_new_t, cpair, parity, True)
            fns.append(tuple(functools.partial(_nt_dot, b=v)
                             for v in stacks(head, jnp.ones_like(head))))
        return tuple(fns)

    return score_fns, value_fns, mask


def _attend_rows(q_ref, gate_ref, og_ref, sink_ref, kv_sets, s_scr, p_scr):
    n_rows = q_ref.shape[0]
    group = N_Q_HEADS // N_KV_HEADS
    low_out = lax.broadcasted_iota(jnp.int32, (n_rows, LANES), 1) < HEAD_DIM
    lane0_row = lax.broadcasted_iota(jnp.int32, (1, LANES), 1) == 0

    def kv_heads():
        for cpair in range(KV_WIDTH // LANES):
            for parity in range(2):
                first = (2 * cpair + parity) * (group // 2)
                yield cpair, parity, list(range(first, first + group // 2))

    def slot(set_index, qc, half):
        return (set_index * (Q_WIDTH // LANES) + qc) * 2 + half

    for si, (score_fns, _, _) in enumerate(kv_sets):
        fns = None
        for cpair, parity, qcs in kv_heads():
            if parity == 0:
                fns = score_fns(cpair)
            q_stack = jnp.concatenate(
                [q_ref[:, qc * LANES:(qc + 1) * LANES] for qc in qcs], axis=0)
            for half, score in enumerate(fns[parity]):
                s = score(q_stack)
                for j, qc in enumerate(qcs):
                    s_scr[slot(si, qc, half)] = s[j * n_rows:(j + 1) * n_rows]

    for si, (_, _, mask) in enumerate(kv_sets):
        mask_first, mask_rest = mask[:, :LANES], mask[:, LANES:]
        for qc in range(Q_WIDTH // LANES):
            for half in range(2):
                sink = sink_ref[2 * qc + half] * LOG2_E
                s = s_scr[slot(si, qc, half)]
                first = jnp.where(mask_first, s[:, :LANES],
                                  jnp.where(lane0_row, sink, NEG_INF))
                rest = jnp.where(mask_rest, s[:, LANES:], NEG_INF)
                m = jnp.max(jnp.maximum(first, rest), axis=-1, keepdims=True)
                p_scr[slot(si, qc, half)] = jnp.exp2(
                    jnp.concatenate([first, rest], axis=1) - m).astype(BF16)

    def store(qc, o):
        cols = slice(qc * LANES, (qc + 1) * LANES)
        og_ref[:, cols] = (o * _silu(gate_ref[:, cols].astype(F32))).astype(BF16)

    out_cols = [None] * (Q_WIDTH // LANES)
    for si, (_, value_fns, _) in enumerate(kv_sets):
        fns = None
        for cpair, parity, qcs in kv_heads():
            if parity == 0:
                fns = value_fns(cpair)
            stacked = [weigh(jnp.concatenate([p_scr[slot(si, qc, half)] for qc in qcs],
                                             axis=0))
                       for half, weigh in enumerate(fns[parity])]
            for j, qc in enumerate(qcs):
                oa, ob = (o[j * n_rows:(j + 1) * n_rows] for o in stacked)
                sums = pltpu.roll(jnp.where(low_out, ob, oa), HEAD_DIM, axis=1)
                o2 = jnp.where(low_out, oa, ob) * (1.0 / sums)
                if len(kv_sets) == 1:
                    store(qc, o2)
                else:
                    out_cols[qc] = o2 if out_cols[qc] is None else out_cols[qc] + o2

    if len(kv_sets) > 1:
        for qc, o in enumerate(out_cols):
            store(qc, o)


def _attention_scratch(n_sets, n_rows, n_keys):
    n = n_sets * N_Q_HEADS
    return [pltpu.VMEM((n, n_rows, n_keys), F32), pltpu.VMEM((n, n_rows, n_keys), BF16)]


def _swa_prompt_kernel(front, sink_ref, q_ref, gate_ref, kc_ref, kp_ref, vc_ref, vp_ref,
                       og_in_ref, og_ref, s_scr, p_scr):
    del og_in_ref
    blk = pl.program_id(1)
    row = lax.broadcasted_iota(jnp.int32, (CHUNK, 2 * CHUNK), 0)
    col = lax.broadcasted_iota(jnp.int32, (CHUNK, 2 * CHUNK), 1)
    mask = (col > row) & (col <= row + WINDOW) & ((blk - 1) * CHUNK + col >= front)
    lane_pairs = [slice(c * LANES, (c + 1) * LANES) for c in range(KV_WIDTH // LANES)]
    keys = [jnp.concatenate([kp_ref[:, c], kc_ref[:, c]], axis=0) for c in lane_pairs]
    vals = [jnp.concatenate([vp_ref[:, c], vc_ref[:, c]], axis=0) for c in lane_pairs]
    _attend_rows(q_ref, gate_ref, og_ref, sink_ref, [_natural_kv_set(keys, vals, mask)],
                 s_scr, p_scr)


def _swa_prompt(pq, kv, sinks, og, n_batch, chunks, front):
    cur = lambda col: (lambda b, i: (b * chunks + i, col))
    prev = lambda col: (lambda b, i: (b * chunks + jnp.maximum(i - 1, 0), col))
    return pl.pallas_call(
        functools.partial(_swa_prompt_kernel, front),
        grid=(n_batch, chunks),
        in_specs=[
            pl.BlockSpec(memory_space=pltpu.SMEM),
            pl.BlockSpec((CHUNK, Q_WIDTH), cur(0)),
            pl.BlockSpec((CHUNK, Q_WIDTH), cur(1)),
            pl.BlockSpec((CHUNK, KV_WIDTH), cur(0)),
            pl.BlockSpec((CHUNK, KV_WIDTH), prev(0)),
            pl.BlockSpec((CHUNK, KV_WIDTH), cur(1)),
            pl.BlockSpec((CHUNK, KV_WIDTH), prev(1)),
            pl.BlockSpec(memory_space=pl.ANY),
        ],
        out_specs=pl.BlockSpec((CHUNK, Q_WIDTH), cur(0)),
        out_shape=jax.ShapeDtypeStruct(og.shape, BF16),
        scratch_shapes=_attention_scratch(1, CHUNK, 2 * CHUNK),
        input_output_aliases={7: 0},
        compiler_params=_params(("parallel", "parallel")),
        name="swa_prompt",
    )(sinks, pq, pq, kv, kv, kv, kv, og)


def _swa_sample_kernel(t_len, sink_ref, q_ref, gate_ref, kn_ref, vn_ref, ckt_ref, cvt_ref,
                       og_in_ref, og_ref, s_scr, p_scr):
    del og_in_ref
    n = SEQ_GROUP * t_len
    win = ckt_ref.shape[2]
    row = lax.broadcasted_iota(jnp.int32, (n, 2 * win), 0)
    col = lax.broadcasted_iota(jnp.int32, (n, 2 * win), 1)
    t = row % t_len
    new = col - win
    in_window = ((col < win) & (col > t)) | (
        (new >= 0) & (new < n) & ((new // t_len) == (row // t_len)) & ((new % t_len) <= t))
    pad = jnp.zeros((win - n, LANES), F32)
    lane_pairs = [slice(c * LANES, (c + 1) * LANES) for c in range(KV_WIDTH // LANES)]
    k_new_t = [jnp.concatenate([kn_ref[:, c], pad], axis=0).T for c in lane_pairs]
    v_new_t = [jnp.concatenate([vn_ref[:, c], pad], axis=0).T for c in lane_pairs]
    kv_sets = [_window_kv_set(ckt_ref.at[u], cvt_ref.at[u], k_new_t, v_new_t,
                              in_window & ((row // t_len) == u))
               for u in range(SEQ_GROUP)]
    _attend_rows(q_ref, gate_ref, og_ref, sink_ref, kv_sets, s_scr, p_scr)


def _swa_sample(pq, kv, cache_kt, cache_vt, sinks, og, n_seq, t_len, row0):
    n = SEQ_GROUP * t_len
    rb0 = row0 // n
    win = cache_kt.shape[2]
    return pl.pallas_call(
        functools.partial(_swa_sample_kernel, t_len),
        grid=(n_seq // SEQ_GROUP,),
        in_specs=[
            pl.BlockSpec(memory_space=pltpu.SMEM),
            pl.BlockSpec((n, Q_WIDTH), lambda s: (rb0 + s, 0)),
            pl.BlockSpec((n, Q_WIDTH), lambda s: (rb0 + s, 1)),
            pl.BlockSpec((n, KV_WIDTH), lambda s: (rb0 + s, 0)),
            pl.BlockSpec((n, KV_WIDTH), lambda s: (rb0 + s, 1)),
            pl.BlockSpec((SEQ_GROUP, KV_WIDTH, win), lambda s: (s, 0, 0)),
            pl.BlockSpec((SEQ_GROUP, KV_WIDTH, win), lambda s: (s, 0, 0)),
            pl.BlockSpec(memory_space=pl.ANY),
        ],
        out_specs=pl.BlockSpec((n, Q_WIDTH), lambda s: (rb0 + s, 0)),
        out_shape=jax.ShapeDtypeStruct(og.shape, BF16),
        scratch_shapes=_attention_scratch(SEQ_GROUP, n, 2 * win),
        input_output_aliases={7: 0},
        compiler_params=_params(("parallel",)),
        name="swa_sample",
    )(sinks, pq, pq, kv, kv, cache_kt, cache_vt, og)


def kernel(x_prompt, x_sample, state_gla, cache_k_win, cache_v_win, meta_tokens, norm_a, w_in_a, w_gk2_a, b_gk2_a, onorm_a, w_out_a, norm_kv, w_kv, norm_b, w_in_b, sinks_b, w_out_b, norm_f):
    n_batch, seq, _ = x_prompt.shape
    n_seq, t_len, _ = x_sample.shape
    win = cache_k_win.shape[1]
    n_a = w_in_a.shape[0]
    n_b = w_in_b.shape[0]
    assert win == WINDOW == CHUNK and n_seq % SEQ_GROUP == 0 and n_seq % GLA_SEQ_GROUP == 0

    length = seq + N_META
    front = (-length) % CHUNK
    lp = length + front
    chunks = lp // CHUNK
    srow = n_batch * lp
    ns = n_seq * t_len
    rows = -(-(srow + ns) // ROW_TILE) * ROW_TILE

    assert front + N_META == CHUNK and ns == CHUNK and seq % CHUNK == 0
    assert rows % CHUNK == 0 and rows % (16 * IN_PROJ_ROW_STEPS) == 0

    w_in_a_t = jnp.swapaxes(w_in_a, 1, 2)
    w_out_a_bf16 = w_out_a.astype(BF16)
    w_out_b_bf16 = w_out_b.astype(BF16)

    def gate_weights(l):
        w_gate = jnp.pad(w_in_a_t[l, GLA_MAIN_WIDTH:], ((0, LANES - GATE_RANK), (0, 0)))
        w_gk = jnp.pad(w_gk2_a[l], ((0, LANES - GATE_RANK), (0, 0))).astype(BF16)
        return w_gate, w_gk, b_gk2_a[l][None]

    extras = jnp.concatenate([jnp.zeros((front, D_MODEL), F32), meta_tokens.astype(F32),
                              x_sample.reshape(ns, D_MODEL), jnp.zeros((CHUNK, D_MODEL), F32)])
    tokens = ((x_prompt.reshape(n_batch * seq, D_MODEL), extras), chunks, n_batch)
    hn, g = _embed(tokens[0], norm_a[0], gate_weights(0), rows, chunks, n_batch)
    x = None
    og = jnp.zeros((rows, D_MODEL), BF16)

    states_p = []
    states_s = None
    for l in range(n_a):
        proj = _in_proj(hn, w_in_a_t, l, GLA_MAIN_WIDTH, BF16, True, "gla_in_proj",
                        q_cols=GLA_KEY_DIM, q_scale=GLA_DK ** -0.5)
        onorm = onorm_a[l][None]
        og, s_p = _gla_prompt(proj, g, onorm, og, n_batch, chunks)
        og, states_s = _gla_sample(proj, g, onorm, state_gla, og, states_s, l,
                                   n_seq, t_len, srow)
        states_p.append(s_p)
        w_out = (w_out_a_bf16, l)
        if l + 1 < n_a:
            x, hn, g = _out_proj(og, w_out, x, [norm_a[l + 1]], gate_weights(l + 1), True,
                                 BF16, "gla_out_proj", tokens=tokens)
        else:
            x, hn_kv, hn = _out_proj(og, w_out, x, [norm_kv, norm_b[0]], None, True,
                                     BF16, "gla_out_proj", tokens=tokens)

    kv = _in_proj(hn_kv, w_kv[None], 0, 2 * KV_WIDTH, F32, False, "kv_proj")
    cache_k = jnp.transpose(cache_k_win, (0, 2, 3, 1)).reshape(n_seq, KV_WIDTH, win)
    cache_v = jnp.transpose(cache_v_win, (0, 2, 3, 1)).reshape(n_seq, KV_WIDTH, win)
    for j in range(n_b):
        pq = _in_proj(hn, w_in_b, j, 2 * Q_WIDTH, BF16, False, "swa_in_proj",
                      q_cols=Q_WIDTH, q_scale=HEAD_DIM ** -0.5 * LOG2_E)
        og = _swa_prompt(pq, kv, sinks_b[j], og, n_batch, chunks, front)
        og = _swa_sample(pq, kv, cache_k, cache_v, sinks_b[j], og, n_seq, t_len, srow)
        w_out = (w_out_b_bf16, j)
        if j + 1 < n_b:
            x, hn = _out_proj(og, w_out, x, [norm_b[j + 1]], None, True, BF16, "swa_out_proj")

    tiles_per_batch = seq // ROW_TILE
    lead_blocks = (lp - seq) // CHUNK

    def prompt_first_block(t):
        return ((t // tiles_per_batch) * chunks + lead_blocks
                + (t % tiles_per_batch) * (ROW_TILE // CHUNK))

    (y_prompt,) = _out_proj(og, w_out, x, [norm_f], None, False, F32, "final_prompt",
                            gather=(ROW_TILE, n_batch * tiles_per_batch, prompt_first_block))
    (y_sample,) = _out_proj(og, w_out, x, [norm_f], None, False, F32, "final_sample",
                            gather=(CHUNK, ns // CHUNK, lambda t: srow // CHUNK + t))
    y_prompt = y_prompt.reshape(n_batch, seq, D_MODEL)
    y_sample = y_sample.reshape(n_seq, t_len, D_MODEL)
    kv_p = jnp.stack([kv[(b + 1) * lp - win:(b + 1) * lp] for b in range(n_batch)])
    kv_p = kv_p.reshape(n_batch, win, 2, N_KV_HEADS, HEAD_DIM)
    kv_s = kv[srow:srow + ns].reshape(n_seq, t_len, 2, N_KV_HEADS, HEAD_DIM)
    k_win_s = jnp.concatenate([cache_k_win, kv_s[:, :, 0]], axis=1)[:, -win:]
    v_win_s = jnp.concatenate([cache_v_win, kv_s[:, :, 1]], axis=1)[:, -win:]
    return (y_prompt, y_sample, jnp.stack(states_p), states_s,
            kv_p[:, :, 0], kv_p[:, :, 1], k_win_s, v_win_s)
```

```python
import functools

import jax
import jax.numpy as jnp
from jax import lax
from jax.experimental import pallas as pl
from jax.experimental.pallas import tpu as pltpu

F32 = jnp.float32
BF16 = jnp.bfloat16

D_MODEL = 2048
N_META = 16
GLA_HEADS = 4
GLA_DK = 256
GLA_DV = 512
GLA_KEY_DIM = GLA_HEADS * GLA_DK
GLA_VAL_DIM = GLA_HEADS * GLA_DV
GLA_MAIN_WIDTH = 2 * GLA_KEY_DIM + 2 * GLA_VAL_DIM
GATE_RANK = 16
GATE_LOGIT_NORM = 16.0
HEAD_DIM = 64
N_Q_HEADS = 32
N_KV_HEADS = 8
Q_WIDTH = N_Q_HEADS * HEAD_DIM
KV_WIDTH = N_KV_HEADS * HEAD_DIM
WINDOW = 128
RMS_EPS = 1e-6
NEG_INF = -1e30
LOG2_E = 1.4426950408889634

LANES = 128
CHUNK = 128
GLA_LEAF = 8
LEAF_BATCH = 4
GLA_CHUNKS_PER_STEP = 3
GLA_BOUNDED_RANGE = 60.0
ROW_TILE = 512
IN_PROJ_COLS = 1024
IN_PROJ_ROW_STEPS = 8
SEQ_GROUP = 4
GLA_SEQ_GROUP = 8
GLA_SAMPLE_HEADS = 2
VMEM_LIMIT = 48 * 1024 * 1024

NT_DIMS = (((1,), (1,)), ((), ()))


def _silu(x):
    return x * (1.0 / (1.0 + jnp.exp2(x * -LOG2_E)))


def _params(sem):
    return pltpu.CompilerParams(dimension_semantics=sem, vmem_limit_bytes=VMEM_LIMIT)


def _emit_normed(y, nw_refs, gate_refs, hn_refs, g_ref):
    ms = jnp.mean(y * y, axis=-1, keepdims=True)
    yn = y * lax.rsqrt(ms + RMS_EPS)
    for nw_ref, hn_ref in zip(nw_refs, hn_refs):
        hn_ref[...] = (yn * nw_ref[...]).astype(hn_ref.dtype)
    if gate_refs:
        g_ref[...] = _decay_gate(hn_refs[0][...], *gate_refs)


def _split_refs(refs, n_norms, with_gate):
    nw_refs = refs[:n_norms]
    gate_refs = refs[n_norms:n_norms + 3] if with_gate else None
    return nw_refs, gate_refs, refs[n_norms + 3 * with_gate:]


def _norm_specs(norms, gate, index):
    specs = [pl.BlockSpec((1, D_MODEL), index) for _ in norms]
    args = [nw[None] for nw in norms]
    if gate is not None:
        specs += [pl.BlockSpec((LANES, D_MODEL), index),
                  pl.BlockSpec((LANES, GLA_KEY_DIM), index),
                  pl.BlockSpec((1, GLA_KEY_DIM), index)]
        args += list(gate)
    return specs, args


TOKEN_SUB = ROW_TILE // CHUNK


def _token_rows(chunks, n_batch, src_refs):
    xp_refs, ex_refs = src_refs[:TOKEN_SUB], src_refs[TOKEN_SUB:]
    pieces = []
    for k in range(TOKEN_SUB):
        blk = pl.program_id(0) * TOKEN_SUB + k
        from_prompt = jnp.logical_and(blk % chunks != 0, blk < n_batch * chunks)
        pieces.append(jnp.where(from_prompt, xp_refs[k][...], ex_refs[k][...]))
    return jnp.concatenate(pieces, axis=0)


def _token_sources(x_prompt2d, extras, chunks, n_batch):
    prompt_blocks = x_prompt2d.shape[0] // CHUNK
    per_batch = prompt_blocks // n_batch

    def prompt_index(k):
        def index(i):
            blk = i * TOKEN_SUB + k
            src = (blk // chunks) * per_batch + (blk % chunks) - 1
            return (jnp.clip(src, 0, prompt_blocks - 1), 0)
        return index

    def extras_index(k):
        def index(i):
            blk = i * TOKEN_SUB + k
            return (jnp.clip(blk - n_batch * chunks + 1, 0, 2), 0)
        return index

    specs = ([pl.BlockSpec((CHUNK, D_MODEL), prompt_index(k)) for k in range(TOKEN_SUB)]
             + [pl.BlockSpec((CHUNK, D_MODEL), extras_index(k)) for k in range(TOKEN_SUB)])
    return specs, [x_prompt2d] * TOKEN_SUB + [extras] * TOKEN_SUB


def _embed_kernel(chunks, n_batch, with_gate, *refs):
    nw_refs, gate_refs, outs = _split_refs(refs[2 * TOKEN_SUB:], 1, with_gate)
    x = _token_rows(chunks, n_batch, refs[:2 * TOKEN_SUB])
    _emit_normed(x, nw_refs, gate_refs, outs[0:1], outs[1] if with_gate else None)


def _embed(tokens, norm, gate, rows, chunks, n_batch):
    src_specs, src_args = _token_sources(*tokens, chunks, n_batch)
    norm_specs, norm_args = _norm_specs([norm], gate, lambda i: (0, 0))
    out_specs = [pl.BlockSpec((ROW_TILE, D_MODEL), lambda i: (i, 0))]
    out_shape = [jax.ShapeDtypeStruct((rows, D_MODEL), BF16)]
    if gate is not None:
        out_specs.append(pl.BlockSpec((ROW_TILE, GLA_KEY_DIM), lambda i: (i, 0)))
        out_shape.append(jax.ShapeDtypeStruct((rows, GLA_KEY_DIM), F32))
    return pl.pallas_call(
        functools.partial(_embed_kernel, chunks, n_batch, gate is not None),
        grid=(rows // ROW_TILE,),
        in_specs=src_specs + norm_specs,
        out_specs=out_specs,
        out_shape=out_shape,
        compiler_params=_params(("parallel",)),
        name="embed",
    )(*src_args, *norm_args)


def _decay_gate(hn, wg_ref, wgk_ref, bgk_ref):
    glr = lax.dot_general(hn, wg_ref[...].astype(BF16), NT_DIMS, preferred_element_type=F32)
    z = jnp.dot(glr.astype(BF16), wgk_ref[...], preferred_element_type=F32) + bgk_ref[...]
    log_sig = jnp.minimum(z, 0.0) - jnp.log(1.0 + jnp.exp(-jnp.abs(z)))
    return log_sig * (LOG2_E / GATE_LOGIT_NORM)


def _in_proj_kernel(transposed, scale_ref, hn_ref, w_ref, o_ref, w_bf16_ref):
    @pl.when(pl.program_id(1) == 0)
    def _():
        w_bf16_ref[...] = w_ref[...].astype(BF16)

    dims = NT_DIMS if transposed else (((1,), (0,)), ((), ()))
    o = lax.dot_general(hn_ref[...], w_bf16_ref[...], dims, preferred_element_type=F32)
    o_ref[...] = (o * scale_ref[pl.program_id(0)]).astype(o_ref.dtype)


def _in_proj(hn, w_stack, layer, n_cols, out_dtype, transposed, name, q_cols=0, q_scale=1.0):
    rows = hn.shape[0]
    row_tile = rows // IN_PROJ_ROW_STEPS
    n_slabs = n_cols // IN_PROJ_COLS
    assert q_cols % IN_PROJ_COLS == 0
    scales = jnp.where(jnp.arange(n_slabs) < q_cols // IN_PROJ_COLS, q_scale, 1.0).astype(F32)
    if transposed:
        slab = (IN_PROJ_COLS, D_MODEL)
        w_spec = pl.BlockSpec((None,) + slab, lambda j, i: (layer, j, 0))
    else:
        slab = (D_MODEL, IN_PROJ_COLS)
        w_spec = pl.BlockSpec((None,) + slab, lambda j, i: (layer, 0, j))
    return pl.pallas_call(
        functools.partial(_in_proj_kernel, transposed),
        grid=(n_slabs, IN_PROJ_ROW_STEPS),
        in_specs=[pl.BlockSpec(memory_space=pltpu.SMEM),
                  pl.BlockSpec((row_tile, D_MODEL), lambda j, i: (i, 0)), w_spec],
        out_specs=pl.BlockSpec((row_tile, IN_PROJ_COLS), lambda j, i: (i, j)),
        out_shape=jax.ShapeDtypeStruct((rows, n_cols), out_dtype),
        scratch_shapes=[pltpu.VMEM(slab, BF16)],
        compiler_params=_params(("parallel", "arbitrary")),
        name=name,
    )(scales, hn, w_stack)


def _out_proj_kernel(sub, n_norms, with_gate, keep_x, token_layout, *refs):
    og_refs, w_ref, refs = refs[:sub], refs[sub], refs[sub + 1:]
    n_x = sub if token_layout is None else 2 * TOKEN_SUB
    x_refs, refs = refs[:n_x], refs[n_x:]
    nw_refs, gate_refs, outs = _split_refs(refs, n_norms, with_gate)
    og = og_refs[0][...] if sub == 1 else jnp.concatenate([r[...] for r in og_refs], axis=0)
    if token_layout is not None:
        x = _token_rows(*token_layout, x_refs)
    elif sub == 1:
        x = x_refs[0][...]
    else:
        x = jnp.concatenate([r[...] for r in x_refs], axis=0)
    y = x + jnp.dot(og, w_ref[...], preferred_element_type=F32)
    if keep_x:
        outs[0][...] = y
        outs = outs[1:]
    _emit_normed(y, nw_refs, gate_refs, outs[:n_norms], outs[n_norms] if with_gate else None)


def _out_proj(og, w, x, norms, gate, keep_x, normed_dtype, name, gather=None, tokens=None):
    if gather is None:
        tile_rows, n_tiles, sub = ROW_TILE, og.shape[0] // ROW_TILE, 1
        in_row_specs = [pl.BlockSpec((ROW_TILE, D_MODEL), lambda i: (i, 0))]
    else:
        tile_rows, n_tiles, first_block = gather
        sub = tile_rows // CHUNK
        in_row_specs = [pl.BlockSpec((CHUNK, D_MODEL),
                                     functools.partial(lambda k, i: (first_block(i) + k, 0), k))
                        for k in range(sub)]
    rows = tile_rows * n_tiles
    row_spec = pl.BlockSpec((tile_rows, D_MODEL), lambda i: (i, 0))
    norm_specs, norm_args = _norm_specs(norms, gate, lambda i: (0, 0))
    out_specs, out_shape = [], []
    if keep_x:
        out_specs.append(row_spec)
        out_shape.append(jax.ShapeDtypeStruct((rows, D_MODEL), F32))
    for _ in norms:
        out_specs.append(row_spec)
        out_shape.append(jax.ShapeDtypeStruct((rows, D_MODEL), normed_dtype))
    if gate is not None:
        out_specs.append(pl.BlockSpec((tile_rows, GLA_KEY_DIM), lambda i: (i, 0)))
        out_shape.append(jax.ShapeDtypeStruct((rows, GLA_KEY_DIM), F32))
    w_stack, layer = w
    w_spec = pl.BlockSpec((None, D_MODEL, D_MODEL), lambda i: (layer, 0, 0),
                          pipeline_mode=pl.Buffered(1))
    if x is None:
        sources, chunks, n_batch = tokens
        x_specs, x_args = _token_sources(*sources, chunks, n_batch)
        token_layout = (chunks, n_batch)
    else:
        x_specs, x_args, token_layout = in_row_specs, [x] * sub, None
    return pl.pallas_call(
        functools.partial(_out_proj_kernel, sub, len(norms), gate is not None, keep_x,
                          token_layout),
        grid=(n_tiles,),
        in_specs=in_row_specs + [w_spec] + x_specs + norm_specs,
        out_specs=out_specs,
        out_shape=out_shape,
        compiler_params=_params(("parallel",)),
        name=name,
    )(*([og] * sub), w_stack, *x_args, *norm_args)


def _exact_block(qs, k, b, n, width, col0):
    lane = lax.broadcasted_iota(jnp.int32, (n, width), 1)
    acc = jnp.zeros((n, width), F32)
    for j in range(n):
        t = qs * k[j:j + 1] * jnp.exp2(jnp.minimum(b - b[j:j + 1], 0.0))
        acc = jnp.where(lane == col0 + j, jnp.sum(t, axis=-1, keepdims=True), acc)
    return acc


def _masked_row_sums(mask_bf16, g):
    g_hi = g.astype(BF16)
    r1 = g - g_hi.astype(F32)
    g_mid = r1.astype(BF16)
    g_lo = (r1 - g_mid.astype(F32)).astype(BF16)
    return (jnp.dot(mask_bf16, g_hi, preferred_element_type=F32)
            + jnp.dot(mask_bf16, g_mid, preferred_element_type=F32)
            + jnp.dot(mask_bf16, g_lo, preferred_element_type=F32))


def _column_of(row_vec, n):
    return jnp.broadcast_to(row_vec, (LANES, n)).T[:, 0:1]


def _head_norm_gate(o, onorm, gate):
    ms = jnp.mean(o * o, axis=-1, keepdims=True)
    return (o * lax.rsqrt(ms + RMS_EPS) * onorm * _silu(gate)).astype(BF16)


def _intra_chunk_scores(qs, k, k_bf, b, row_xor_col):
    c_len = qs.shape[0]
    leaf = GLA_LEAF
    leaf_row = lax.broadcasted_iota(jnp.int32, (leaf, c_len), 0)
    leaf_col = lax.broadcasted_iota(jnp.int32, (leaf, c_len), 1)
    lane_in_leaf = [(leaf_col & (leaf - 1)) == j for j in range(leaf)]

    att_rows = []
    for blk0 in range(0, c_len // leaf, LEAF_BATCH):
        lhs = []
        for blk in range(blk0, blk0 + LEAF_BATCH):
            q_blk = qs[blk * leaf:(blk + 1) * leaf]
            b_blk = b[blk * leaf:(blk + 1) * leaf]
            lhs += [q_blk * jnp.exp2(jnp.minimum(b_blk - b_blk[j:j + 1], 0.0))
                    for j in range(leaf)]
        r = lax.dot_general(jnp.concatenate(lhs, axis=0).astype(BF16), k_bf, NT_DIMS,
                            preferred_element_type=F32)
        for i, blk in enumerate(range(blk0, blk0 + LEAF_BATCH)):
            acc = jnp.zeros((leaf, c_len), F32)
            for j in range(leaf):
                r0 = (i * leaf + j) * leaf
                acc = jnp.where(lane_in_leaf[j], r[r0:r0 + leaf], acc)
            rel = leaf_col - blk * leaf
            att_rows.append(jnp.where((rel >= 0) & (rel <= leaf_row), acc, 0.0))
    att = jnp.concatenate(att_rows, axis=0)

    m = c_len // 2
    while m >= leaf:
        q_parts, k_parts = [], []
        zeros = jnp.zeros((m, GLA_DK), F32)
        for base in range(0, c_len, 2 * m):
            ref = b[base + m - 1:base + m]
            lo = slice(base, base + m)
            hi = slice(base + m, base + 2 * m)
            k_parts += [k[lo] * jnp.exp2(ref - b[lo]), zeros]
            q_parts += [zeros, qs[hi] * jnp.exp2(b[hi] - ref)]
        qt = jnp.concatenate(q_parts, axis=0).astype(BF16)
        kt = jnp.concatenate(k_parts, axis=0).astype(BF16)
        a = lax.dot_general(qt, kt, NT_DIMS, preferred_element_type=F32)
        att = att + jnp.where(row_xor_col < 2 * m, a, 0.0)
        m //= 2
    return att


def _gla_prompt_kernel(q_ref, k_ref, v_ref, gate_ref, g_ref, onorm_ref, og_in_ref,
                       og_ref, s_ref, b_scr, qd_scr, att_scr):
    del og_in_ref
    c_len = CHUNK
    heads = [(slice(h * GLA_DK, (h + 1) * GLA_DK), slice(h * GLA_DV, (h + 1) * GLA_DV))
             for h in range(GLA_HEADS)]

    @pl.when(pl.program_id(1) == 0)
    def _():
        s_ref[...] = jnp.zeros_like(s_ref)

    row = lax.broadcasted_iota(jnp.int32, (c_len, c_len), 0)
    col = lax.broadcasted_iota(jnp.int32, (c_len, c_len), 1)
    lower = row >= col
    tri = jnp.where(lower, 1.0, 0.0).astype(BF16)
    onorm = onorm_ref[...]
    chunk_rows = [slice(c * c_len, (c + 1) * c_len) for c in range(q_ref.shape[0] // c_len)]

    total = None
    for rows in chunk_rows:
        b_c = _masked_row_sums(tri, g_ref[rows, :])
        b_scr[rows, :] = b_c
        qd_scr[rows, :] = (q_ref[rows, :].astype(F32) * jnp.exp2(b_c)).astype(BF16)
        last = b_c[c_len - 1:c_len]
        total = last if total is None else jnp.minimum(total, last)

    bounded = jnp.max(-total) <= GLA_BOUNDED_RANGE

    @pl.when(bounded)
    def _():
        for c, rows in enumerate(chunk_rows):
            for h, (ks, _) in enumerate(heads):
                k_inv = (k_ref[rows, ks].astype(F32) * jnp.exp2(-b_scr[rows, ks])).astype(BF16)
                a = lax.dot_general(qd_scr[rows, ks], k_inv, NT_DIMS,
                                    preferred_element_type=F32)
                att_scr[c * GLA_HEADS + h] = jnp.where(lower, a, 0.0).astype(BF16)

    @pl.when(jnp.logical_not(bounded))
    def _():
        for c, rows in enumerate(chunk_rows):
            for h, (ks, _) in enumerate(heads):
                k_bf = k_ref[rows, ks]
                att_scr[c * GLA_HEADS + h] = _intra_chunk_scores(
                    q_ref[rows, ks].astype(F32), k_bf.astype(F32), k_bf,
                    b_scr[rows, ks], row ^ col).astype(BF16)

    for c, rows in enumerate(chunk_rows):
        for h, (ks, vs) in enumerate(heads):
            k = k_ref[rows, ks].astype(F32)
            v = v_ref[rows, vs]
            b = b_scr[rows, ks]
            b_last = b[c_len - 1:c_len]
            state = s_ref[0, h]

            o = jnp.dot(qd_scr[rows, ks], state.astype(BF16), preferred_element_type=F32)
            o = o + jnp.dot(att_scr[c * GLA_HEADS + h], v, preferred_element_type=F32)

            kd_t = (k * jnp.exp2(b_last - b)).T.astype(BF16)
            s_ref[0, h] = (state * _column_of(jnp.exp2(b_last), GLA_DK)
                           + jnp.dot(kd_t, v, preferred_element_type=F32))

            og_ref[rows, vs] = _head_norm_gate(o, onorm, gate_ref[rows, vs].astype(F32))


def _gla_prompt(proj, g, onorm, og, n_batch, chunks):
    rows = proj.shape[0]
    per_step = next(n for n in (GLA_CHUNKS_PER_STEP, 1) if chunks % n == 0)
    steps = chunks // per_step
    tile = per_step * CHUNK
    blk = lambda col: (lambda b, c: (b * steps + c, col))
    return pl.pallas_call(
        _gla_prompt_kernel,
        grid=(n_batch, steps),
        in_specs=[
            pl.BlockSpec((tile, GLA_KEY_DIM), blk(0)),
            pl.BlockSpec((tile, GLA_KEY_DIM), blk(1)),
            pl.BlockSpec((tile, GLA_VAL_DIM), blk(1)),
            pl.BlockSpec((tile, GLA_VAL_DIM), blk(2)),
            pl.BlockSpec((tile, GLA_KEY_DIM), blk(0)),
            pl.BlockSpec((1, GLA_DV), lambda b, c: (0, 0)),
            pl.BlockSpec(memory_space=pl.ANY),
        ],
        out_specs=[
            pl.BlockSpec((tile, GLA_VAL_DIM), blk(0)),
            pl.BlockSpec((1, GLA_HEADS, GLA_DK, GLA_DV), lambda b, c: (b, 0, 0, 0)),
        ],
        out_shape=[
            jax.ShapeDtypeStruct((rows, GLA_VAL_DIM), BF16),
            jax.ShapeDtypeStruct((n_batch, GLA_HEADS, GLA_DK, GLA_DV), F32),
        ],
        scratch_shapes=[pltpu.VMEM((tile, GLA_KEY_DIM), F32),
                        pltpu.VMEM((tile, GLA_KEY_DIM), BF16),
                        pltpu.VMEM((per_step * GLA_HEADS, CHUNK, CHUNK), BF16)],
        input_output_aliases={6: 0},
        compiler_params=_params(("parallel", "arbitrary")),
        name="gla_prompt",
    )(proj, proj, proj, proj, g, onorm, og)


def _gla_sample_kernel(t_len, q_ref, k_ref, v_ref, gate_ref, g_ref, onorm_ref, s0_ref,
                       og_in_ref, *rest):
    og_ref, s_ref = rest[-2], rest[-1]
    n = GLA_SEQ_GROUP * t_len
    row = lax.broadcasted_iota(jnp.int32, (n, LANES), 0)
    col = lax.broadcasted_iota(jnp.int32, (n, LANES), 1)
    causal = ((row // t_len) == (col // t_len)) & (row >= col)
    causal_bf16 = jnp.where(causal, 1.0, 0.0).astype(BF16)
    row_seq = lax.broadcasted_iota(jnp.int32, (n, 1), 0) // t_len
    pad_k = jnp.zeros((LANES - n, GLA_DK), F32)

    for h in range(GLA_SAMPLE_HEADS):
        ks = slice(h * GLA_DK, (h + 1) * GLA_DK)
        vs = slice(h * GLA_DV, (h + 1) * GLA_DV)
        qs = q_ref[:, ks].astype(F32)
        k = k_ref[:, ks].astype(F32)
        v_pad = jnp.concatenate([v_ref[:, vs], jnp.zeros((LANES - n, GLA_DV), BF16)], axis=0)
        b = _masked_row_sums(causal_bf16, jnp.concatenate([g_ref[:, ks], pad_k], axis=0))
        att = jnp.where(causal, _exact_block(qs, k, b, n, LANES, 0), 0.0)
        o = jnp.dot(att.astype(BF16), v_pad, preferred_element_type=F32)

        q_dec = (qs * jnp.exp2(b)).astype(BF16)
        for u in range(GLA_SEQ_GROUP):
            state = s0_ref[0, u, h]
            o_u = jnp.dot(q_dec, state.astype(BF16), preferred_element_type=F32)
            o = o + jnp.where(row_seq == u, o_u, 0.0)
            b_last = b[(u + 1) * t_len - 1:(u + 1) * t_len]
            kd = jnp.where(row_seq == u, k * jnp.exp2(jnp.minimum(b_last - b, 0.0)), 0.0)
            kd_t = jnp.concatenate([kd, pad_k], axis=0).T.astype(BF16)
            s_ref[0, u, h] = (state * _column_of(jnp.exp2(b_last), GLA_DK)
                              + jnp.dot(kd_t, v_pad, preferred_element_type=F32))

        og_ref[:, vs] = _head_norm_gate(o, onorm_ref[...], gate_ref[:, vs].astype(F32))


def _gla_sample(proj, g, onorm, state_in, og, s_out, layer, n_seq, t_len, row0):
    n = GLA_SEQ_GROUP * t_len
    rb0 = row0 // n
    n_layers = state_in.shape[0]
    hs = GLA_SAMPLE_HEADS
    kw, vw = hs * GLA_DK, hs * GLA_DV
    key_blocks = GLA_KEY_DIM // kw
    val_blocks = GLA_VAL_DIM // vw
    st_spec = pl.BlockSpec((1, GLA_SEQ_GROUP, hs, GLA_DK, GLA_DV),
                           lambda s, h: (layer, s, h, 0, 0))
    in_specs = [
        pl.BlockSpec((n, kw), lambda s, h: (rb0 + s, h)),
        pl.BlockSpec((n, kw), lambda s, h: (rb0 + s, key_blocks + h)),
        pl.BlockSpec((n, vw), lambda s, h: (rb0 + s, val_blocks + h)),
        pl.BlockSpec((n, vw), lambda s, h: (rb0 + s, 2 * val_blocks + h)),
        pl.BlockSpec((n, kw), lambda s, h: (rb0 + s, h)),
        pl.BlockSpec((1, GLA_DV), lambda s, h: (0, 0)),
        st_spec,
        pl.BlockSpec(memory_space=pl.ANY),
    ]
    args = [proj, proj, proj, proj, g, onorm, state_in, og]
    aliases = {7: 0}
    if s_out is not None:
        in_specs.append(pl.BlockSpec(memory_space=pl.ANY))
        args.append(s_out)
        aliases[8] = 1
    return pl.pallas_call(
        functools.partial(_gla_sample_kernel, t_len),
        grid=(n_seq // GLA_SEQ_GROUP, GLA_HEADS // hs),
        in_specs=in_specs,
        out_specs=[
            pl.BlockSpec((n, vw), lambda s, h: (rb0 + s, h)),
            st_spec,
        ],
        out_shape=[
            jax.ShapeDtypeStruct(og.shape, BF16),
            jax.ShapeDtypeStruct((n_layers, n_seq, GLA_HEADS, GLA_DK, GLA_DV), F32),
        ],
        input_output_aliases=aliases,
        compiler_params=_params(("parallel", "parallel")),
        name="gla_sample",
    )(*args)


def _nt_dot(a, b):
    return lax.dot_general(a, b, NT_DIMS, preferred_element_type=F32)


def _lane_padded_copies(pair, fill):
    low = lax.broadcasted_iota(jnp.int32, pair.shape, 1) < HEAD_DIM
    swapped = pltpu.roll(pair, HEAD_DIM, axis=1).astype(BF16)
    pair = pair.astype(BF16)
    fill = jnp.full((), fill, BF16)
    return ((jnp.where(low, pair, fill), jnp.where(low, fill, swapped)),
            (jnp.where(low, swapped, fill), jnp.where(low, fill, pair)))


def _natural_kv_set(keys, vals, mask):
    slot0 = lax.broadcasted_iota(jnp.int32, keys[0].shape, 0) == 0

    def score_fns(cpair):
        return tuple(tuple(functools.partial(_nt_dot, b=k) for k in pads)
                     for pads in _lane_padded_copies(keys[cpair], 0.0))

    def value_fns(cpair):
        pads_by_head = _lane_padded_copies(jnp.where(slot0, 0.0, vals[cpair]), 1.0)
        return tuple(tuple((lambda p, v=v: jnp.dot(p, v, preferred_element_type=F32))
                           for v in pads) for pads in pads_by_head)

    return score_fns, value_fns, mask


def _window_kv_set(kt_ref, vt_ref, k_new_t, v_new_t, mask):
    win = kt_ref.shape[1]
    lane0 = lax.broadcasted_iota(jnp.int32, (HEAD_DIM, win), 1) == 0

    def head_keys(old_ref, new_t, cpair, parity, zero_slot0):
        r0 = (2 * cpair + parity) * HEAD_DIM
        old = old_ref[r0:r0 + HEAD_DIM, :]
        if zero_slot0:
            old = jnp.where(lane0, 0.0, old)
        new = new_t[cpair][parity * HEAD_DIM:(parity + 1) * HEAD_DIM, :]
        return jnp.concatenate([old, new], axis=1).astype(BF16)

    def stacks(head, other):
        return (jnp.concatenate([head, other], axis=0), jnp.concatenate([other, head], axis=0))

    def score_fns(cpair):
        fns = []
        for parity in range(2):
            head = head_keys(kt_ref, k_new_t, cpair, parity, False)
            fns.append(tuple(
                (lambda q, k=k: jnp.dot(q, k, preferred_element_type=F32))
                for k in stacks(head, jnp.zeros_like(head))))
        return tuple(fns)

    def value_fns(cpair):
        fns = []
        for parity in range(2):
            head = head_keys(vt_ref, v_new_t, cpair, parity, True)
            fns.append(tuple(functools.partial(_nt_dot, b=v)
                             for v in stacks(head, jnp.ones_like(head))))
        return tuple(fns)

    return score_fns, value_fns, mask


def _attend_rows(q_ref, gate_ref, og_ref, sink_ref, kv_sets, s_scr, p_scr):
    n_rows = q_ref.shape[0]
    group = N_Q_HEADS // N_KV_HEADS
    low_out = lax.broadcasted_iota(jnp.int32, (n_rows, LANES), 1) < HEAD_DIM
    lane0_row = lax.broadcasted_iota(jnp.int32, (1, LANES), 1) == 0

    def kv_heads():
        for cpair in range(KV_WIDTH // LANES):
            for parity in range(2):
                first = (2 * cpair + parity) * (group // 2)
                yield cpair, parity, list(range(first, first + group // 2))

    def slot(set_index, qc, half):
        return (set_index * (Q_WIDTH // LANES) + qc) * 2 + half

    for si, (score_fns, _, _) in enumerate(kv_sets):
        fns = None
        for cpair, parity, qcs in kv_heads():
            if parity == 0:
                fns = score_fns(cpair)
            q_stack = jnp.concatenate(
                [q_ref[:, qc * LANES:(qc + 1) * LANES] for qc in qcs], axis=0)
            for half, score in enumerate(fns[parity]):
                s = score(q_stack)
                for j, qc in enumerate(qcs):
                    s_scr[slot(si, qc, half)] = s[j * n_rows:(j + 1) * n_rows]

    for si, (_, _, mask) in enumerate(kv_sets):
        mask_first, mask_rest = mask[:, :LANES], mask[:, LANES:]
        for qc in range(Q_WIDTH // LANES):
            for half in range(2):
                sink = sink_ref[2 * qc + half] * LOG2_E
                s = s_scr[slot(si, qc, half)]
                first = jnp.where(mask_first, s[:, :LANES],
                                  jnp.where(lane0_row, sink, NEG_INF))
                rest = jnp.where(mask_rest, s[:, LANES:], NEG_INF)
                m = jnp.max(jnp.maximum(first, rest), axis=-1, keepdims=True)
                p_scr[slot(si, qc, half)] = jnp.exp2(
                    jnp.concatenate([first, rest], axis=1) - m).astype(BF16)

    def store(qc, o):
        cols = slice(qc * LANES, (qc + 1) * LANES)
        og_ref[:, cols] = (o * _silu(gate_ref[:, cols].astype(F32))).astype(BF16)

    out_cols = [None] * (Q_WIDTH // LANES)
    for si, (_, value_fns, _) in enumerate(kv_sets):
        fns = None
        for cpair, parity, qcs in kv_heads():
            if parity == 0:
                fns = value_fns(cpair)
            stacked = [weigh(jnp.concatenate([p_scr[slot(si, qc, half)] for qc in qcs],
                                             axis=0))
                       for half, weigh in enumerate(fns[parity])]
            for j, qc in enumerate(qcs):
                oa, ob = (o[j * n_rows:(j + 1) * n_rows] for o in stacked)
                sums = pltpu.roll(jnp.where(low_out, ob, oa), HEAD_DIM, axis=1)
                o2 = jnp.where(low_out, oa, ob) * (1.0 / sums)
                if len(kv_sets) == 1:
                    store(qc, o2)
                else:
                    out_cols[qc] = o2 if out_cols[qc] is None else out_cols[qc] + o2

    if len(kv_sets) > 1:
        for qc, o in enumerate(out_cols):
            store(qc, o)


def _attention_scratch(n_sets, n_rows, n_keys):
    n = n_sets * N_Q_HEADS
    return [pltpu.VMEM((n, n_rows, n_keys), F32), pltpu.VMEM((n, n_rows, n_keys), BF16)]


def _swa_prompt_kernel(front, sink_ref, q_ref, gate_ref, kc_ref, kp_ref, vc_ref, vp_ref,
                       og_in_ref, og_ref, s_scr, p_scr):
    del og_in_ref
    blk = pl.program_id(1)
    row = lax.broadcasted_iota(jnp.int32, (CHUNK, 2 * CHUNK), 0)
    col = lax.broadcasted_iota(jnp.int32, (CHUNK, 2 * CHUNK), 1)
    mask = (col > row) & (col <= row + WINDOW) & ((blk - 1) * CHUNK + col >= front)
    lane_pairs = [slice(c * LANES, (c + 1) * LANES) for c in range(KV_WIDTH // LANES)]
    keys = [jnp.concatenate([kp_ref[:, c], kc_ref[:, c]], axis=0) for c in lane_pairs]
    vals = [jnp.concatenate([vp_ref[:, c], vc_ref[:, c]], axis=0) for c in lane_pairs]
    _attend_rows(q_ref, gate_ref, og_ref, sink_ref, [_natural_kv_set(keys, vals, mask)],
                 s_scr, p_scr)


def _swa_prompt(pq, kv, sinks, og, n_batch, chunks, front):
    cur = lambda col: (lambda b, i: (b * chunks + i, col))
    prev = lambda col: (lambda b, i: (b * chunks + jnp.maximum(i - 1, 0), col))
    return pl.pallas_call(
        functools.partial(_swa_prompt_kernel, front),
        grid=(n_batch, chunks),
        in_specs=[
            pl.BlockSpec(memory_space=pltpu.SMEM),
            pl.BlockSpec((CHUNK, Q_WIDTH), cur(0)),
            pl.BlockSpec((CHUNK, Q_WIDTH), cur(1)),
            pl.BlockSpec((CHUNK, KV_WIDTH), cur(0)),
            pl.BlockSpec((CHUNK, KV_WIDTH), prev(0)),
            pl.BlockSpec((CHUNK, KV_WIDTH), cur(1)),
            pl.BlockSpec((CHUNK, KV_WIDTH), prev(1)),
            pl.BlockSpec(memory_space=pl.ANY),
        ],
        out_specs=pl.BlockSpec((CHUNK, Q_WIDTH), cur(0)),
        out_shape=jax.ShapeDtypeStruct(og.shape, BF16),
        scratch_shapes=_attention_scratch(1, CHUNK, 2 * CHUNK),
        input_output_aliases={7: 0},
        compiler_params=_params(("parallel", "parallel")),
        name="swa_prompt",
    )(sinks, pq, pq, kv, kv, kv, kv, og)


def _swa_sample_kernel(t_len, sink_ref, q_ref, gate_ref, kn_ref, vn_ref, ckt_ref, cvt_ref,
                       og_in_ref, og_ref, s_scr, p_scr):
    del og_in_ref
    n = SEQ_GROUP * t_len
    win = ckt_ref.shape[2]
    row = lax.broadcasted_iota(jnp.int32, (n, 2 * win), 0)
    col = lax.broadcasted_iota(jnp.int32, (n, 2 * win), 1)
    t = row % t_len
    new = col - win
    in_window = ((col < win) & (col > t)) | (
        (new >= 0) & (new < n) & ((new // t_len) == (row // t_len)) & ((new % t_len) <= t))
    pad = jnp.zeros((win - n, LANES), F32)
    lane_pairs = [slice(c * LANES, (c + 1) * LANES) for c in range(KV_WIDTH // LANES)]
    k_new_t = [jnp.concatenate([kn_ref[:, c], pad], axis=0).T for c in lane_pairs]
    v_new_t = [jnp.concatenate([vn_ref[:, c], pad], axis=0).T for c in lane_pairs]
    kv_sets = [_window_kv_set(ckt_ref.at[u], cvt_ref.at[u], k_new_t, v_new_t,
                              in_window & ((row // t_len) == u))
               for u in range(SEQ_GROUP)]
    _attend_rows(q_ref, gate_ref, og_ref, sink_ref, kv_sets, s_scr, p_scr)


def _swa_sample(pq, kv, cache_kt, cache_vt, sinks, og, n_seq, t_len, row0):
    n = SEQ_GROUP * t_len
    rb0 = row0 // n
    win = cache_kt.shape[2]
    return pl.pallas_call(
        functools.partial(_swa_sample_kernel, t_len),
        grid=(n_seq // SEQ_GROUP,),
        in_specs=[
            pl.BlockSpec(memory_space=pltpu.SMEM),
            pl.BlockSpec((n, Q_WIDTH), lambda s: (rb0 + s, 0)),
            pl.BlockSpec((n, Q_WIDTH), lambda s: (rb0 + s, 1)),
            pl.BlockSpec((n, KV_WIDTH), lambda s: (rb0 + s, 0)),
            pl.BlockSpec((n, KV_WIDTH), lambda s: (rb0 + s, 1)),
            pl.BlockSpec((SEQ_GROUP, KV_WIDTH, win), lambda s: (s, 0, 0)),
            pl.BlockSpec((SEQ_GROUP, KV_WIDTH, win), lambda s: (s, 0, 0)),
            pl.BlockSpec(memory_space=pl.ANY),
        ],
        out_specs=pl.BlockSpec((n, Q_WIDTH), lambda s: (rb0 + s, 0)),
        out_shape=jax.ShapeDtypeStruct(og.shape, BF16),
        scratch_shapes=_attention_scratch(SEQ_GROUP, n, 2 * win),
        input_output_aliases={7: 0},
        compiler_params=_params(("parallel",)),
        name="swa_sample",
    )(sinks, pq, pq, kv, kv, cache_kt, cache_vt, og)


def kernel(x_prompt, x_sample, state_gla, cache_k_win, cache_v_win, meta_tokens, norm_a, w_in_a, w_gk2_a, b_gk2_a, onorm_a, w_out_a, norm_kv, w_kv, norm_b, w_in_b, sinks_b, w_out_b, norm_f):
    n_batch, seq, _ = x_prompt.shape
    n_seq, t_len, _ = x_sample.shape
    win = cache_k_win.shape[1]
    n_a = w_in_a.shape[0]
    n_b = w_in_b.shape[0]
    assert win == WINDOW == CHUNK and n_seq % SEQ_GROUP == 0 and n_seq % GLA_SEQ_GROUP == 0

    length = seq + N_META
    front = (-length) % CHUNK
    lp = length + front
    chunks = lp // CHUNK
    srow = n_batch * lp
    ns = n_seq * t_len
    rows = -(-(srow + ns) // ROW_TILE) * ROW_TILE

    assert front + N_META == CHUNK and ns == CHUNK and seq % CHUNK == 0
    assert rows % CHUNK == 0 and rows % (16 * IN_PROJ_ROW_STEPS) == 0

    w_in_a_t = jnp.swapaxes(w_in_a, 1, 2)
    w_out_a_bf16 = w_out_a.astype(BF16)
    w_out_b_bf16 = w_out_b.astype(BF16)

    def gate_weights(l):
        w_gate = jnp.pad(w_in_a_t[l, GLA_MAIN_WIDTH:], ((0, LANES - GATE_RANK), (0, 0)))
        w_gk = jnp.pad(w_gk2_a[l], ((0, LANES - GATE_RANK), (0, 0))).astype(BF16)
        return w_gate, w_gk, b_gk2_a[l][None]

    extras = jnp.concatenate([jnp.zeros((front, D_MODEL), F32), meta_tokens.astype(F32),
                              x_sample.reshape(ns, D_MODEL), jnp.zeros((CHUNK, D_MODEL), F32)])
    tokens = ((x_prompt.reshape(n_batch * seq, D_MODEL), extras), chunks, n_batch)
    hn, g = _embed(tokens[0], norm_a[0], gate_weights(0), rows, chunks, n_batch)
    x = None
    og = jnp.zeros((rows, D_MODEL), BF16)

    states_p = []
    states_s = None
    for l in range(n_a):
        proj = _in_proj(hn, w_in_a_t, l, GLA_MAIN_WIDTH, BF16, True, "gla_in_proj",
                        q_cols=GLA_KEY_DIM, q_scale=GLA_DK ** -0.5)
        onorm = onorm_a[l][None]
        og, s_p = _gla_prompt(proj, g, onorm, og, n_batch, chunks)
        og, states_s = _gla_sample(proj, g, onorm, state_gla, og, states_s, l,
                                   n_seq, t_len, srow)
        states_p.append(s_p)
        w_out = (w_out_a_bf16, l)
        if l + 1 < n_a:
            x, hn, g = _out_proj(og, w_out, x, [norm_a[l + 1]], gate_weights(l + 1), True,
                                 BF16, "gla_out_proj", tokens=tokens)
        else:
            x, hn_kv, hn = _out_proj(og, w_out, x, [norm_kv, norm_b[0]], None, True,
                                     BF16, "gla_out_proj", tokens=tokens)

    kv = _in_proj(hn_kv, w_kv[None], 0, 2 * KV_WIDTH, F32, False, "kv_proj")
    cache_k = jnp.transpose(cache_k_win, (0, 2, 3, 1)).reshape(n_seq, KV_WIDTH, win)
    cache_v = jnp.transpose(cache_v_win, (0, 2, 3, 1)).reshape(n_seq, KV_WIDTH, win)
    for j in range(n_b):
        pq = _in_proj(hn, w_in_b, j, 2 * Q_WIDTH, BF16, False, "swa_in_proj",
                      q_cols=Q_WIDTH, q_scale=HEAD_DIM ** -0.5 * LOG2_E)
        og = _swa_prompt(pq, kv, sinks_b[j], og, n_batch, chunks, front)
        og = _swa_sample(pq, kv, cache_k, cache_v, sinks_b[j], og, n_seq, t_len, srow)
        w_out = (w_out_b_bf16, j)
        if j + 1 < n_b:
            x, hn = _out_proj(og, w_out, x, [norm_b[j + 1]], None, True, BF16, "swa_out_proj")

    tiles_per_batch = seq // ROW_TILE
    lead_blocks = (lp - seq) // CHUNK

    def prompt_first_block(t):
        return ((t // tiles_per_batch) * chunks + lead_blocks
                + (t % tiles_per_batch) * (ROW_TILE // CHUNK))

    (y_prompt,) = _out_proj(og, w_out, x, [norm_f], None, False, F32, "final_prompt",
                            gather=(ROW_TILE, n_batch * tiles_per_batch, prompt_first_block))
    (y_sample,) = _out_proj(og, w_out, x, [norm_f], None, False, F32, "final_sample",
                            gather=(CHUNK, ns // CHUNK, lambda t: srow // CHUNK + t))
    y_prompt = y_prompt.reshape(n_batch, seq, D_MODEL)
    y_sample = y_sample.reshape(n_seq, t_len, D_MODEL)
    kv_p = jnp.stack([kv[(b + 1) * lp - win:(b + 1) * lp] for b in range(n_batch)])
    kv_p = kv_p.reshape(n_batch, win, 2, N_KV_HEADS, HEAD_DIM)
    kv_s = kv[srow:srow + ns].reshape(n_seq, t_len, 2, N_KV_HEADS, HEAD_DIM)
    k_win_s = jnp.concatenate([cache_k_win, kv_s[:, :, 0]], axis=1)[:, -win:]
    v_win_s = jnp.concatenate([cache_v_win, kv_s[:, :, 1]], axis=1)[:, -win:]
    return (y_prompt, y_sample, jnp.stack(states_p), states_s,
            kv_p[:, :, 0], kv_p[:, :, 1], k_win_s, v_win_s)
```

```python
import functools

import jax
import jax.numpy as jnp
from jax import lax
from jax.experimental import pallas as pl
from jax.experimental.pallas import tpu as pltpu

F32 = jnp.float32
BF16 = jnp.bfloat16

D_MODEL = 2048
N_META = 16
GLA_HEADS = 4
GLA_DK = 256
GLA_DV = 512
GLA_KEY_DIM = GLA_HEADS * GLA_DK
GLA_VAL_DIM = GLA_HEADS * GLA_DV
GLA_MAIN_WIDTH = 2 * GLA_KEY_DIM + 2 * GLA_VAL_DIM
GATE_RANK = 16
GATE_LOGIT_NORM = 16.0
HEAD_DIM = 64
N_Q_HEADS = 32
N_KV_HEADS = 8
Q_WIDTH = N_Q_HEADS * HEAD_DIM
KV_WIDTH = N_KV_HEADS * HEAD_DIM
WINDOW = 128
RMS_EPS = 1e-6
NEG_INF = -1e30
LOG2_E = 1.4426950408889634

LANES = 128
CHUNK = 128
GLA_LEAF = 8
LEAF_BATCH = 4
GLA_CHUNKS_PER_STEP = 3
GLA_BOUNDED_RANGE = 60.0
ROW_TILE = 512
IN_PROJ_COLS = 1024
IN_PROJ_ROW_STEPS = 8
SEQ_GROUP = 4
GLA_SEQ_GROUP = 8
GLA_SAMPLE_HEADS = 2
VMEM_LIMIT = 48 * 1024 * 1024

NT_DIMS = (((1,), (1,)), ((), ()))


def _silu(x):
    return x * (1.0 / (1.0 + jnp.exp2(x * -LOG2_E)))


def _params(sem):
    return pltpu.CompilerParams(dimension_semantics=sem, vmem_limit_bytes=VMEM_LIMIT)


def _emit_normed(y, nw_refs, gate_refs, hn_refs, g_ref):
    ms = jnp.mean(y * y, axis=-1, keepdims=True)
    yn = y * lax.rsqrt(ms + RMS_EPS)
    for nw_ref, hn_ref in zip(nw_refs, hn_refs):
        hn_ref[...] = (yn * nw_ref[...]).astype(hn_ref.dtype)
    if gate_refs:
        g_ref[...] = _decay_gate(hn_refs[0][...], *gate_refs)
    return yn


def _split_refs(refs, n_norms, with_gate):
    nw_refs = refs[:n_norms]
    gate_refs = refs[n_norms:n_norms + 3] if with_gate else None
    return nw_refs, gate_refs, refs[n_norms + 3 * with_gate:]


def _norm_specs(norms, gate, index):
    specs = [pl.BlockSpec((1, D_MODEL), index) for _ in norms]
    args = [nw[None] for nw in norms]
    if gate is not None:
        specs += [pl.BlockSpec((LANES, D_MODEL), index),
                  pl.BlockSpec((LANES, GLA_KEY_DIM), index),
                  pl.BlockSpec((1, GLA_KEY_DIM), index)]
        args += list(gate)
    return specs, args


TOKEN_SUB = ROW_TILE // CHUNK


def _token_rows(chunks, n_batch, src_refs):
    xp_refs, ex_refs = src_refs[:TOKEN_SUB], src_refs[TOKEN_SUB:]
    pieces = []
    for k in range(TOKEN_SUB):
        blk = pl.program_id(0) * TOKEN_SUB + k
        from_prompt = jnp.logical_and(blk % chunks != 0, blk < n_batch * chunks)
        pieces.append(jnp.where(from_prompt, xp_refs[k][...], ex_refs[k][...]))
    return jnp.concatenate(pieces, axis=0)


def _token_sources(x_prompt2d, extras, chunks, n_batch):
    prompt_blocks = x_prompt2d.shape[0] // CHUNK
    per_batch = prompt_blocks // n_batch

    def prompt_index(k):
        def index(i):
            blk = i * TOKEN_SUB + k
            src = (blk // chunks) * per_batch + (blk % chunks) - 1
            return (jnp.clip(src, 0, prompt_blocks - 1), 0)
        return index

    def extras_index(k):
        def index(i):
            blk = i * TOKEN_SUB + k
            return (jnp.clip(blk - n_batch * chunks + 1, 0, 2), 0)
        return index

    specs = ([pl.BlockSpec((CHUNK, D_MODEL), prompt_index(k)) for k in range(TOKEN_SUB)]
             + [pl.BlockSpec((CHUNK, D_MODEL), extras_index(k)) for k in range(TOKEN_SUB)])
    return specs, [x_prompt2d] * TOKEN_SUB + [extras] * TOKEN_SUB


def _embed_kernel(chunks, n_batch, with_gate, *refs):
    nw_refs, gate_refs, outs = _split_refs(refs[2 * TOKEN_SUB:], 1, with_gate)
    x = _token_rows(chunks, n_batch, refs[:2 * TOKEN_SUB])
    _emit_normed(x, nw_refs, gate_refs, outs[0:1], outs[1] if with_gate else None)


def _embed(tokens, norm, gate, rows, chunks, n_batch):
    src_specs, src_args = _token_sources(*tokens, chunks, n_batch)
    norm_specs, norm_args = _norm_specs([norm], gate, lambda i: (0, 0))
    out_specs = [pl.BlockSpec((ROW_TILE, D_MODEL), lambda i: (i, 0))]
    out_shape = [jax.ShapeDtypeStruct((rows, D_MODEL), BF16)]
    if gate is not None:
        out_specs.append(pl.BlockSpec((ROW_TILE, GLA_KEY_DIM), lambda i: (i, 0)))
        out_shape.append(jax.ShapeDtypeStruct((rows, GLA_KEY_DIM), F32))
    return pl.pallas_call(
        functools.partial(_embed_kernel, chunks, n_batch, gate is not None),
        grid=(rows // ROW_TILE,),
        in_specs=src_specs + norm_specs,
        out_specs=out_specs,
        out_shape=out_shape,
        compiler_params=_params(("parallel",)),
        name="embed",
    )(*src_args, *norm_args)


def _decay_gate(hn, wg_ref, wgk_ref, bgk_ref):
    glr = lax.dot_general(hn, wg_ref[...].astype(BF16), NT_DIMS, preferred_element_type=F32)
    z = jnp.dot(glr.astype(BF16), wgk_ref[...], preferred_element_type=F32) + bgk_ref[...]
    log_sig = jnp.minimum(z, 0.0) - jnp.log(1.0 + jnp.exp(-jnp.abs(z)))
    return log_sig * (LOG2_E / GATE_LOGIT_NORM)


def _in_proj_kernel(transposed, scale_ref, hn_ref, w_ref, o_ref, w_bf16_ref):
    @pl.when(pl.program_id(1) == 0)
    def _():
        w_bf16_ref[...] = w_ref[...].astype(BF16)

    dims = NT_DIMS if transposed else (((1,), (0,)), ((), ()))
    o = lax.dot_general(hn_ref[...], w_bf16_ref[...], dims, preferred_element_type=F32)
    o_ref[...] = (o * scale_ref[pl.program_id(0)]).astype(o_ref.dtype)


def _in_proj(hn, w_stack, layer, n_cols, out_dtype, transposed, name, q_cols=0, q_scale=1.0):
    rows = hn.shape[0]
    row_tile = rows // IN_PROJ_ROW_STEPS
    n_slabs = n_cols // IN_PROJ_COLS
    assert q_cols % IN_PROJ_COLS == 0
    scales = jnp.where(jnp.arange(n_slabs) < q_cols // IN_PROJ_COLS, q_scale, 1.0).astype(F32)
    if transposed:
        slab = (IN_PROJ_COLS, D_MODEL)
        w_spec = pl.BlockSpec((None,) + slab, lambda j, i: (layer, j, 0))
    else:
        slab = (D_MODEL, IN_PROJ_COLS)
        w_spec = pl.BlockSpec((None,) + slab, lambda j, i: (layer, 0, j))
    return pl.pallas_call(
        functools.partial(_in_proj_kernel, transposed),
        grid=(n_slabs, IN_PROJ_ROW_STEPS),
        in_specs=[pl.BlockSpec(memory_space=pltpu.SMEM),
                  pl.BlockSpec((row_tile, D_MODEL), lambda j, i: (i, 0)), w_spec],
        out_specs=pl.BlockSpec((row_tile, IN_PROJ_COLS), lambda j, i: (i, j)),
        out_shape=jax.ShapeDtypeStruct((rows, n_cols), out_dtype),
        scratch_shapes=[pltpu.VMEM(slab, BF16)],
        compiler_params=_params(("parallel", "arbitrary")),
        name=name,
    )(scales, hn, w_stack)


def _out_proj_kernel(sub, n_norms, with_gate, keep_x, token_layout, with_kv, *refs):
    og_refs, w_ref, refs = refs[:sub], refs[sub], refs[sub + 1:]
    n_x = sub if token_layout is None else 2 * TOKEN_SUB
    x_refs, refs = refs[:n_x], refs[n_x:]
    nw_refs, gate_refs, outs = _split_refs(refs, n_norms, with_gate)
    if with_kv:
        (nkv_ref, wkv_ref), outs, kv_ref = outs[:2], outs[2:-1], outs[-1]
    og = og_refs[0][...] if sub == 1 else jnp.concatenate([r[...] for r in og_refs], axis=0)
    if token_layout is not None:
        x = _token_rows(*token_layout, x_refs)
    elif sub == 1:
        x = x_refs[0][...]
    else:
        x = jnp.concatenate([r[...] for r in x_refs], axis=0)
    y = x + jnp.dot(og, w_ref[...], preferred_element_type=F32)
    if keep_x:
        outs[0][...] = y
        outs = outs[1:]
    yn = _emit_normed(y, nw_refs, gate_refs, outs[:n_norms],
                      outs[n_norms] if with_gate else None)
    if with_kv:
        kv_ref[...] = jnp.dot((yn * nkv_ref[...]).astype(BF16), wkv_ref[...],
                              preferred_element_type=F32)


def _out_proj(og, w, x, norms, gate, keep_x, normed_dtype, name, gather=None, tokens=None,
              kv=None):
    if gather is None:
        tile_rows, n_tiles, sub = ROW_TILE, og.shape[0] // ROW_TILE, 1
        in_row_specs = [pl.BlockSpec((ROW_TILE, D_MODEL), lambda i: (i, 0))]
    else:
        tile_rows, n_tiles, first_block = gather
        sub = tile_rows // CHUNK
        in_row_specs = [pl.BlockSpec((CHUNK, D_MODEL),
                                     functools.partial(lambda k, i: (first_block(i) + k, 0), k))
                        for k in range(sub)]
    rows = tile_rows * n_tiles
    row_spec = pl.BlockSpec((tile_rows, D_MODEL), lambda i: (i, 0))
    norm_specs, norm_args = _norm_specs(norms, gate, lambda i: (0, 0))
    out_specs, out_shape = [], []
    if keep_x:
        out_specs.append(row_spec)
        out_shape.append(jax.ShapeDtypeStruct((rows, D_MODEL), F32))
    for _ in norms:
        out_specs.append(row_spec)
        out_shape.append(jax.ShapeDtypeStruct((rows, D_MODEL), normed_dtype))
    if gate is not None:
        out_specs.append(pl.BlockSpec((tile_rows, GLA_KEY_DIM), lambda i: (i, 0)))
        out_shape.append(jax.ShapeDtypeStruct((rows, GLA_KEY_DIM), F32))
    w_stack, layer = w
    w_spec = pl.BlockSpec((None, D_MODEL, D_MODEL), lambda i: (layer, 0, 0),
                          pipeline_mode=pl.Buffered(1))
    if x is None:
        sources, chunks, n_batch = tokens
        x_specs, x_args = _token_sources(*sources, chunks, n_batch)
        token_layout = (chunks, n_batch)
    else:
        x_specs, x_args, token_layout = in_row_specs, [x] * sub, None
    if kv is not None:
        norm_kv, w_kv = kv
        norm_specs += [pl.BlockSpec((1, D_MODEL), lambda i: (0, 0)),
                       pl.BlockSpec(w_kv.shape, lambda i: (0, 0), pipeline_mode=pl.Buffered(1))]
        norm_args += [norm_kv[None], w_kv]
        out_specs.append(pl.BlockSpec((tile_rows, w_kv.shape[1]), lambda i: (i, 0)))
        out_shape.append(jax.ShapeDtypeStruct((rows, w_kv.shape[1]), F32))
    return pl.pallas_call(
        functools.partial(_out_proj_kernel, sub, len(norms), gate is not None, keep_x,
                          token_layout, kv is not None),
        grid=(n_tiles,),
        in_specs=in_row_specs + [w_spec] + x_specs + norm_specs,
        out_specs=out_specs,
        out_shape=out_shape,
        compiler_params=_params(("parallel",)),
        name=name,
    )(*([og] * sub), w_stack, *x_args, *norm_args)


def _exact_block(qs, k, b, n, width, col0):
    lane = lax.broadcasted_iota(jnp.int32, (n, width), 1)
    acc = jnp.zeros((n, width), F32)
    for j in range(n):
        t = qs * k[j:j + 1] * jnp.exp2(jnp.minimum(b - b[j:j + 1], 0.0))
        acc = jnp.where(lane == col0 + j, jnp.sum(t, axis=-1, keepdims=True), acc)
    return acc


def _masked_row_sums(mask_bf16, g):
    g_hi = g.astype(BF16)
    r1 = g - g_hi.astype(F32)
    g_mid = r1.astype(BF16)
    g_lo = (r1 - g_mid.astype(F32)).astype(BF16)
    return (jnp.dot(mask_bf16, g_hi, preferred_element_type=F32)
            + jnp.dot(mask_bf16, g_mid, preferred_element_type=F32)
            + jnp.dot(mask_bf16, g_lo, preferred_element_type=F32))


def _column_of(row_vec, n):
    return jnp.broadcast_to(row_vec, (LANES, n)).T[:, 0:1]


def _head_norm_gate(o, onorm, gate):
    ms = jnp.mean(o * o, axis=-1, keepdims=True)
    return (o * lax.rsqrt(ms + RMS_EPS) * onorm * _silu(gate)).astype(BF16)


def _intra_chunk_scores(qs, k, k_bf, b, row_xor_col):
    c_len = qs.shape[0]
    leaf = GLA_LEAF
    leaf_row = lax.broadcasted_iota(jnp.int32, (leaf, c_len), 0)
    leaf_col = lax.broadcasted_iota(jnp.int32, (leaf, c_len), 1)
    lane_in_leaf = [(leaf_col & (leaf - 1)) == j for j in range(leaf)]

    att_rows = []
    for blk0 in range(0, c_len // leaf, LEAF_BATCH):
        lhs = []
        for blk in range(blk0, blk0 + LEAF_BATCH):
            q_blk = qs[blk * leaf:(blk + 1) * leaf]
            b_blk = b[blk * leaf:(blk + 1) * leaf]
            lhs += [q_blk * jnp.exp2(jnp.minimum(b_blk - b_blk[j:j + 1], 0.0))
                    for j in range(leaf)]
        r = lax.dot_general(jnp.concatenate(lhs, axis=0).astype(BF16), k_bf, NT_DIMS,
                            preferred_element_type=F32)
        for i, blk in enumerate(range(blk0, blk0 + LEAF_BATCH)):
            acc = jnp.zeros((leaf, c_len), F32)
            for j in range(leaf):
                r0 = (i * leaf + j) * leaf
                acc = jnp.where(lane_in_leaf[j], r[r0:r0 + leaf], acc)
            rel = leaf_col - blk * leaf
            att_rows.append(jnp.where((rel >= 0) & (rel <= leaf_row), acc, 0.0))
    att = jnp.concatenate(att_rows, axis=0)

    m = c_len // 2
    while m >= leaf:
        q_parts, k_parts = [], []
        zeros = jnp.zeros((m, GLA_DK), F32)
        for base in range(0, c_len, 2 * m):
            ref = b[base + m - 1:base + m]
            lo = slice(base, base + m)
            hi = slice(base + m, base + 2 * m)
            k_parts += [k[lo] * jnp.exp2(ref - b[lo]), zeros]
            q_parts += [zeros, qs[hi] * jnp.exp2(b[hi] - ref)]
        qt = jnp.concatenate(q_parts, axis=0).astype(BF16)
        kt = jnp.concatenate(k_parts, axis=0).astype(BF16)
        a = lax.dot_general(qt, kt, NT_DIMS, preferred_element_type=F32)
        att = att + jnp.where(row_xor_col < 2 * m, a, 0.0)
        m //= 2
    return att


def _gla_prompt_kernel(q_ref, k_ref, v_ref, gate_ref, g_ref, onorm_ref, og_in_ref,
                       og_ref, s_ref, b_scr, qd_scr, att_scr):
    del og_in_ref
    c_len = CHUNK
    heads = [(slice(h * GLA_DK, (h + 1) * GLA_DK), slice(h * GLA_DV, (h + 1) * GLA_DV))
             for h in range(GLA_HEADS)]

    @pl.when(pl.program_id(1) == 0)
    def _():
        s_ref[...] = jnp.zeros_like(s_ref)

    row = lax.broadcasted_iota(jnp.int32, (c_len, c_len), 0)
    col = lax.broadcasted_iota(jnp.int32, (c_len, c_len), 1)
    lower = row >= col
    tri = jnp.where(lower, 1.0, 0.0).astype(BF16)
    onorm = onorm_ref[...]
    chunk_rows = [slice(c * c_len, (c + 1) * c_len) for c in range(q_ref.shape[0] // c_len)]

    total = None
    for rows in chunk_rows:
        b_c = _masked_row_sums(tri, g_ref[rows, :])
        b_scr[rows, :] = b_c
        qd_scr[rows, :] = (q_ref[rows, :].astype(F32) * jnp.exp2(b_c)).astype(BF16)
        last = b_c[c_len - 1:c_len]
        total = last if total is None else jnp.minimum(total, last)

    bounded = jnp.max(-total) <= GLA_BOUNDED_RANGE

    @pl.when(bounded)
    def _():
        for c, rows in enumerate(chunk_rows):
            for h, (ks, _) in enumerate(heads):
                k_inv = (k_ref[rows, ks].astype(F32) * jnp.exp2(-b_scr[rows, ks])).astype(BF16)
                a = lax.dot_general(qd_scr[rows, ks], k_inv, NT_DIMS,
                                    preferred_element_type=F32)
                att_scr[c * GLA_HEADS + h] = jnp.where(lower, a, 0.0).astype(BF16)

    @pl.when(jnp.logical_not(bounded))
    def _():
        for c, rows in enumerate(chunk_rows):
            for h, (ks, _) in enumerate(heads):
                k_bf = k_ref[rows, ks]
                att_scr[c * GLA_HEADS + h] = _intra_chunk_scores(
                    q_ref[rows, ks].astype(F32), k_bf.astype(F32), k_bf,
                    b_scr[rows, ks], row ^ col).astype(BF16)

    for c, rows in enumerate(chunk_rows):
        for h, (ks, vs) in enumerate(heads):
            k = k_ref[rows, ks].astype(F32)
            v = v_ref[rows, vs]
            b = b_scr[rows, ks]
            b_last = b[c_len - 1:c_len]
            state = s_ref[0, h]

            o = jnp.dot(qd_scr[rows, ks], state.astype(BF16), preferred_element_type=F32)
            o = o + jnp.dot(att_scr[c * GLA_HEADS + h], v, preferred_element_type=F32)

            kd_t = (k * jnp.exp2(b_last - b)).T.astype(BF16)
            s_ref[0, h] = (state * _column_of(jnp.exp2(b_last), GLA_DK)
                           + jnp.dot(kd_t, v, preferred_element_type=F32))

            og_ref[rows, vs] = _head_norm_gate(o, onorm, gate_ref[rows, vs].astype(F32))


def _gla_prompt(proj, g, onorm, og, n_batch, chunks):
    rows = proj.shape[0]
    per_step = next(n for n in (GLA_CHUNKS_PER_STEP, 1) if chunks % n == 0)
    steps = chunks // per_step
    tile = per_step * CHUNK
    blk = lambda col: (lambda b, c: (b * steps + c, col))
    return pl.pallas_call(
        _gla_prompt_kernel,
        grid=(n_batch, steps),
        in_specs=[
            pl.BlockSpec((tile, GLA_KEY_DIM), blk(0)),
            pl.BlockSpec((tile, GLA_KEY_DIM), blk(1)),
            pl.BlockSpec((tile, GLA_VAL_DIM), blk(1)),
            pl.BlockSpec((tile, GLA_VAL_DIM), blk(2)),
            pl.BlockSpec((tile, GLA_KEY_DIM), blk(0)),
            pl.BlockSpec((1, GLA_DV), lambda b, c: (0, 0)),
            pl.BlockSpec(memory_space=pl.ANY),
        ],
        out_specs=[
            pl.BlockSpec((tile, GLA_VAL_DIM), blk(0)),
            pl.BlockSpec((1, GLA_HEADS, GLA_DK, GLA_DV), lambda b, c: (b, 0, 0, 0)),
        ],
        out_shape=[
            jax.ShapeDtypeStruct((rows, GLA_VAL_DIM), BF16),
            jax.ShapeDtypeStruct((n_batch, GLA_HEADS, GLA_DK, GLA_DV), F32),
        ],
        scratch_shapes=[pltpu.VMEM((tile, GLA_KEY_DIM), F32),
                        pltpu.VMEM((tile, GLA_KEY_DIM), BF16),
                        pltpu.VMEM((per_step * GLA_HEADS, CHUNK, CHUNK), BF16)],
        input_output_aliases={6: 0},
        compiler_params=_params(("parallel", "arbitrary")),
        name="gla_prompt",
    )(proj, proj, proj, proj, g, onorm, og)


def _gla_sample_kernel(t_len, q_ref, k_ref, v_ref, gate_ref, g_ref, onorm_ref, s0_ref,
                       og_in_ref, *rest):
    og_ref, s_ref = rest[-2], rest[-1]
    n = GLA_SEQ_GROUP * t_len
    row = lax.broadcasted_iota(jnp.int32, (n, LANES), 0)
    col = lax.broadcasted_iota(jnp.int32, (n, LANES), 1)
    causal = ((row // t_len) == (col // t_len)) & (row >= col)
    causal_bf16 = jnp.where(causal, 1.0, 0.0).astype(BF16)
    row_seq = lax.broadcasted_iota(jnp.int32, (n, 1), 0) // t_len
    pad_k = jnp.zeros((LANES - n, GLA_DK), F32)

    for h in range(GLA_SAMPLE_HEADS):
        ks = slice(h * GLA_DK, (h + 1) * GLA_DK)
        vs = slice(h * GLA_DV, (h + 1) * GLA_DV)
        qs = q_ref[:, ks].astype(F32)
        k = k_ref[:, ks].astype(F32)
        v_pad = jnp.concatenate([v_ref[:, vs], jnp.zeros((LANES - n, GLA_DV), BF16)], axis=0)
        b = _masked_row_sums(causal_bf16, jnp.concatenate([g_ref[:, ks], pad_k], axis=0))
        att = jnp.where(causal, _exact_block(qs, k, b, n, LANES, 0), 0.0)
        o = jnp.dot(att.astype(BF16), v_pad, preferred_element_type=F32)

        q_dec = (qs * jnp.exp2(b)).astype(BF16)
        for u in range(GLA_SEQ_GROUP):
            state = s0_ref[0, u, h]
            o_u = jnp.dot(q_dec, state.astype(BF16), preferred_element_type=F32)
            o = o + jnp.where(row_seq == u, o_u, 0.0)
            b_last = b[(u + 1) * t_len - 1:(u + 1) * t_len]
            kd = jnp.where(row_seq == u, k * jnp.exp2(jnp.minimum(b_last - b, 0.0)), 0.0)
            kd_t = jnp.concatenate([kd, pad_k], axis=0).T.astype(BF16)
            s_ref[0, u, h] = (state * _column_of(jnp.exp2(b_last), GLA_DK)
                              + jnp.dot(kd_t, v_pad, preferred_element_type=F32))

        og_ref[:, vs] = _head_norm_gate(o, onorm_ref[...], gate_ref[:, vs].astype(F32))


def _gla_sample(proj, g, onorm, state_in, og, s_out, layer, n_seq, t_len, row0):
    n = GLA_SEQ_GROUP * t_len
    rb0 = row0 // n
    n_layers = state_in.shape[0]
    hs = GLA_SAMPLE_HEADS
    kw, vw = hs * GLA_DK, hs * GLA_DV
    key_blocks = GLA_KEY_DIM // kw
    val_blocks = GLA_VAL_DIM // vw
    st_spec = pl.BlockSpec((1, GLA_SEQ_GROUP, hs, GLA_DK, GLA_DV),
                           lambda s, h: (layer, s, h, 0, 0))
    in_specs = [
        pl.BlockSpec((n, kw), lambda s, h: (rb0 + s, h)),
        pl.BlockSpec((n, kw), lambda s, h: (rb0 + s, key_blocks + h)),
        pl.BlockSpec((n, vw), lambda s, h: (rb0 + s, val_blocks + h)),
        pl.BlockSpec((n, vw), lambda s, h: (rb0 + s, 2 * val_blocks + h)),
        pl.BlockSpec((n, kw), lambda s, h: (rb0 + s, h)),
        pl.BlockSpec((1, GLA_DV), lambda s, h: (0, 0)),
        st_spec,
        pl.BlockSpec(memory_space=pl.ANY),
    ]
    args = [proj, proj, proj, proj, g, onorm, state_in, og]
    aliases = {7: 0}
    if s_out is not None:
        in_specs.append(pl.BlockSpec(memory_space=pl.ANY))
        args.append(s_out)
        aliases[8] = 1
    return pl.pallas_call(
        functools.partial(_gla_sample_kernel, t_len),
        grid=(n_seq // GLA_SEQ_GROUP, GLA_HEADS // hs),
        in_specs=in_specs,
        out_specs=[
            pl.BlockSpec((n, vw), lambda s, h: (rb0 + s, h)),
            st_spec,
        ],
        out_shape=[
            jax.ShapeDtypeStruct(og.shape, BF16),
            jax.ShapeDtypeStruct((n_layers, n_seq, GLA_HEADS, GLA_DK, GLA_DV), F32),
        ],
        input_output_aliases=aliases,
        compiler_params=_params(("parallel", "parallel")),
        name="gla_sample",
    )(*args)


def _nt_dot(a, b):
    return lax.dot_general(a, b, NT_DIMS, preferred_element_type=F32)


def _lane_padded_copies(pair, fill):
    low = lax.broadcasted_iota(jnp.int32, pair.shape, 1) < HEAD_DIM
    swapped = pltpu.roll(pair, HEAD_DIM, axis=1).astype(BF16)
    pair = pair.astype(BF16)
    fill = jnp.full((), fill, BF16)
    return ((jnp.where(low, pair, fill), jnp.where(low, fill, swapped)),
            (jnp.where(low, swapped, fill), jnp.where(low, fill, pair)))


def _natural_kv_set(keys, vals, mask):
    slot0 = lax.broadcasted_iota(jnp.int32, keys[0].shape, 0) == 0

    def score_fns(cpair):
        return tuple(tuple(functools.partial(_nt_dot, b=k) for k in pads)
                     for pads in _lane_padded_copies(keys[cpair], 0.0))

    def value_fns(cpair):
        pads_by_head = _lane_padded_copies(jnp.where(slot0, 0.0, vals[cpair]), 1.0)
        return tuple(tuple((lambda p, v=v: jnp.dot(p, v, preferred_element_type=F32))
                           for v in pads) for pads in pads_by_head)

    return score_fns, value_fns, mask


def _window_kv_set(kt_ref, vt_ref, k_new_t, v_new_t, mask):
    win = kt_ref.shape[1]
    lane0 = lax.broadcasted_iota(jnp.int32, (HEAD_DIM, win), 1) == 0

    def head_keys(old_ref, new_t, cpair, parity, zero_slot0):
        r0 = (2 * cpair + parity) * HEAD_DIM
        old = old_ref[r0:r0 + HEAD_DIM, :]
        if zero_slot0:
            old = jnp.where(lane0, 0.0, old)
        new = new_t[cpair][parity * HEAD_DIM:(parity + 1) * HEAD_DIM, :]
        return jnp.concatenate([old, new], axis=1).astype(BF16)

    def stacks(head, other):
        return (jnp.concatenate([head, other], axis=0), jnp.concatenate([other, head], axis=0))

    def score_fns(cpair):
        fns = []
        for parity in range(2):
            head = head_keys(kt_ref, k_new_t, cpair, parity, False)
            fns.append(tuple(
                (lambda q, k=k: jnp.dot(q, k, preferred_element_type=F32))
                for k in stacks(head, jnp.zeros_like(head))))
        return tuple(fns)

    def value_fns(cpair):
        fns = []
        for parity in range(2):
            head = head_keys(vt_ref, v_new_t, cpair, parity, True)
            fns.append(tuple(functools.partial(_nt_dot, b=v)
                             for v in stacks(head, jnp.ones_like(head))))
        return tuple(fns)

    return score_fns, value_fns, mask


def _attend_rows(q_ref, gate_ref, og_ref, sink_ref, kv_sets, s_scr, p_scr):
    n_rows = q_ref.shape[0]
    group = N_Q_HEADS // N_KV_HEADS
    low_out = lax.broadcasted_iota(jnp.int32, (n_rows, LANES), 1) < HEAD_DIM
    lane0_row = lax.broadcasted_iota(jnp.int32, (1, LANES), 1) == 0

    def kv_heads():
        for cpair in range(KV_WIDTH // LANES):
            for parity in range(2):
                first = (2 * cpair + parity) * (group // 2)
                yield cpair, parity, list(range(first, first + group // 2))

    def slot(set_index, qc, half):
        return (set_index * (Q_WIDTH // LANES) + qc) * 2 + half

    for si, (score_fns, _, _) in enumerate(kv_sets):
        fns = None
        for cpair, parity, qcs in kv_heads():
            if parity == 0:
                fns = score_fns(cpair)
            q_stack = jnp.concatenate(
                [q_ref[:, qc * LANES:(qc + 1) * LANES] for qc in qcs], axis=0)
            for half, score in enumerate(fns[parity]):
                s = score(q_stack)
                for j, qc in enumerate(qcs):
                    s_scr[slot(si, qc, half)] = s[j * n_rows:(j + 1) * n_rows]

    for si, (_, _, mask) in enumerate(kv_sets):
        mask_first, mask_rest = mask[:, :LANES], mask[:, LANES:]
        for qc in range(Q_WIDTH // LANES):
            for half in range(2):
                sink = sink_ref[2 * qc + half] * LOG2_E
                s = s_scr[slot(si, qc, half)]
                first = jnp.where(mask_first, s[:, :LANES],
                                  jnp.where(lane0_row, sink, NEG_INF))
                rest = jnp.where(mask_rest, s[:, LANES:], NEG_INF)
                m = jnp.max(jnp.maximum(first, rest), axis=-1, keepdims=True)
                p_scr[slot(si, qc, half)] = jnp.exp2(
                    jnp.concatenate([first, rest], axis=1) - m).astype(BF16)

    def store(qc, o):
        cols = slice(qc * LANES, (qc + 1) * LANES)
        og_ref[:, cols] = (o * _silu(gate_ref[:, cols].astype(F32))).astype(BF16)

    out_cols = [None] * (Q_WIDTH // LANES)
    for si, (_, value_fns, _) in enumerate(kv_sets):
        fns = None
        for cpair, parity, qcs in kv_heads():
            if parity == 0:
                fns = value_fns(cpair)
            stacked = [weigh(jnp.concatenate([p_scr[slot(si, qc, half)] for qc in qcs],
                                             axis=0))
                       for half, weigh in enumerate(fns[parity])]
            for j, qc in enumerate(qcs):
                oa, ob = (o[j * n_rows:(j + 1) * n_rows] for o in stacked)
                sums = pltpu.roll(jnp.where(low_out, ob, oa), HEAD_DIM, axis=1)
                o2 = jnp.where(low_out, oa, ob) * (1.0 / sums)
                if len(kv_sets) == 1:
                    store(qc, o2)
                else:
                    out_cols[qc] = o2 if out_cols[qc] is None else out_cols[qc] + o2

    if len(kv_sets) > 1:
        for qc, o in enumerate(out_cols):
            store(qc, o)


def _attention_scratch(n_sets, n_rows, n_keys):
    n = n_sets * N_Q_HEADS
    return [pltpu.VMEM((n, n_rows, n_keys), F32), pltpu.VMEM((n, n_rows, n_keys), BF16)]


def _swa_prompt_kernel(front, sink_ref, q_ref, gate_ref, kc_ref, kp_ref, vc_ref, vp_ref,
                       og_in_ref, og_ref, s_scr, p_scr):
    del og_in_ref
    blk = pl.program_id(1)
    row = lax.broadcasted_iota(jnp.int32, (CHUNK, 2 * CHUNK), 0)
    col = lax.broadcasted_iota(jnp.int32, (CHUNK, 2 * CHUNK), 1)
    mask = (col > row) & (col <= row + WINDOW) & ((blk - 1) * CHUNK + col >= front)
    lane_pairs = [slice(c * LANES, (c + 1) * LANES) for c in range(KV_WIDTH // LANES)]
    keys = [jnp.concatenate([kp_ref[:, c], kc_ref[:, c]], axis=0) for c in lane_pairs]
    vals = [jnp.concatenate([vp_ref[:, c], vc_ref[:, c]], axis=0) for c in lane_pairs]
    _attend_rows(q_ref, gate_ref, og_ref, sink_ref, [_natural_kv_set(keys, vals, mask)],
                 s_scr, p_scr)


def _swa_prompt(pq, kv, sinks, og, n_batch, chunks, front):
    cur = lambda col: (lambda b, i: (b * chunks + i, col))
    prev = lambda col: (lambda b, i: (b * chunks + jnp.maximum(i - 1, 0), col))
    return pl.pallas_call(
        functools.partial(_swa_prompt_kernel, front),
        grid=(n_batch, chunks),
        in_specs=[
            pl.BlockSpec(memory_space=pltpu.SMEM),
            pl.BlockSpec((CHUNK, Q_WIDTH), cur(0)),
            pl.BlockSpec((CHUNK, Q_WIDTH), cur(1)),
            pl.BlockSpec((CHUNK, KV_WIDTH), cur(0)),
            pl.BlockSpec((CHUNK, KV_WIDTH), prev(0)),
            pl.BlockSpec((CHUNK, KV_WIDTH), cur(1)),
            pl.BlockSpec((CHUNK, KV_WIDTH), prev(1)),
            pl.BlockSpec(memory_space=pl.ANY),
        ],
        out_specs=pl.BlockSpec((CHUNK, Q_WIDTH), cur(0)),
        out_shape=jax.ShapeDtypeStruct(og.shape, BF16),
        scratch_shapes=_attention_scratch(1, CHUNK, 2 * CHUNK),
        input_output_aliases={7: 0},
        compiler_params=_params(("parallel", "parallel")),
        name="swa_prompt",
    )(sinks, pq, pq, kv, kv, kv, kv, og)


def _swa_sample_kernel(t_len, sink_ref, q_ref, gate_ref, kn_ref, vn_ref, ckt_ref, cvt_ref,
                       og_in_ref, og_ref, s_scr, p_scr):
    del og_in_ref
    n = SEQ_GROUP * t_len
    win = ckt_ref.shape[2]
    row = lax.broadcasted_iota(jnp.int32, (n, 2 * win), 0)
    col = lax.broadcasted_iota(jnp.int32, (n, 2 * win), 1)
    t = row % t_len
    new = col - win
    in_window = ((col < win) & (col > t)) | (
        (new >= 0) & (new < n) & ((new // t_len) == (row // t_len)) & ((new % t_len) <= t))
    pad = jnp.zeros((win - n, LANES), F32)
    lane_pairs = [slice(c * LANES, (c + 1) * LANES) for c in range(KV_WIDTH // LANES)]
    k_new_t = [jnp.concatenate([kn_ref[:, c], pad], axis=0).T for c in lane_pairs]
    v_new_t = [jnp.concatenate([vn_ref[:, c], pad], axis=0).T for c in lane_pairs]
    kv_sets = [_window_kv_set(ckt_ref.at[u], cvt_ref.at[u], k_new_t, v_new_t,
                              in_window & ((row // t_len) == u))
               for u in range(SEQ_GROUP)]
    _attend_rows(q_ref, gate_ref, og_ref, sink_ref, kv_sets, s_scr, p_scr)


def _swa_sample(pq, kv, cache_kt, cache_vt, sinks, og, n_seq, t_len, row0):
    n = SEQ_GROUP * t_len
    rb0 = row0 // n
    win = cache_kt.shape[2]
    return pl.pallas_call(
        functools.partial(_swa_sample_kernel, t_len),
        grid=(n_seq // SEQ_GROUP,),
        in_specs=[
            pl.BlockSpec(memory_space=pltpu.SMEM),
            pl.BlockSpec((n, Q_WIDTH), lambda s: (rb0 + s, 0)),
            pl.BlockSpec((n, Q_WIDTH), lambda s: (rb0 + s, 1)),
            pl.BlockSpec((n, KV_WIDTH), lambda s: (rb0 + s, 0)),
            pl.BlockSpec((n, KV_WIDTH), lambda s: (rb0 + s, 1)),
            pl.BlockSpec((SEQ_GROUP, KV_WIDTH, win), lambda s: (s, 0, 0)),
            pl.BlockSpec((SEQ_GROUP, KV_WIDTH, win), lambda s: (s, 0, 0)),
            pl.BlockSpec(memory_space=pl.ANY),
        ],
        out_specs=pl.BlockSpec((n, Q_WIDTH), lambda s: (rb0 + s, 0)),
        out_shape=jax.ShapeDtypeStruct(og.shape, BF16),
        scratch_shapes=_attention_scratch(SEQ_GROUP, n, 2 * win),
        input_output_aliases={7: 0},
        compiler_params=_params(("parallel",)),
        name="swa_sample",
    )(sinks, pq, pq, kv, kv, cache_kt, cache_vt, og)


def kernel(x_prompt, x_sample, state_gla, cache_k_win, cache_v_win, meta_tokens, norm_a, w_in_a, w_gk2_a, b_gk2_a, onorm_a, w_out_a, norm_kv, w_kv, norm_b, w_in_b, sinks_b, w_out_b, norm_f):
    n_batch, seq, _ = x_prompt.shape
    n_seq, t_len, _ = x_sample.shape
    win = cache_k_win.shape[1]
    n_a = w_in_a.shape[0]
    n_b = w_in_b.shape[0]
    assert win == WINDOW == CHUNK and n_seq % SEQ_GROUP == 0 and n_seq % GLA_SEQ_GROUP == 0

    length = seq + N_META
    front = (-length) % CHUNK
    lp = length + front
    chunks = lp // CHUNK
    srow = n_batch * lp
    ns = n_seq * t_len
    rows = -(-(srow + ns) // ROW_TILE) * ROW_TILE

    assert front + N_META == CHUNK and ns == CHUNK and seq % CHUNK == 0
    assert rows % CHUNK == 0 and rows % (16 * IN_PROJ_ROW_STEPS) == 0

    w_in_a_t = jnp.swapaxes(w_in_a, 1, 2)
    w_out_a_bf16 = w_out_a.astype(BF16)
    w_out_b_bf16 = w_out_b.astype(BF16)

    def gate_weights(l):
        w_gate = jnp.pad(w_in_a_t[l, GLA_MAIN_WIDTH:], ((0, LANES - GATE_RANK), (0, 0)))
        w_gk = jnp.pad(w_gk2_a[l], ((0, LANES - GATE_RANK), (0, 0))).astype(BF16)
        return w_gate, w_gk, b_gk2_a[l][None]

    extras = jnp.concatenate([jnp.zeros((front, D_MODEL), F32), meta_tokens.astype(F32),
                              x_sample.reshape(ns, D_MODEL), jnp.zeros((CHUNK, D_MODEL), F32)])
    tokens = ((x_prompt.reshape(n_batch * seq, D_MODEL), extras), chunks, n_batch)
    hn, g = _embed(tokens[0], norm_a[0], gate_weights(0), rows, chunks, n_batch)
    x = None
    og = jnp.zeros((rows, D_MODEL), BF16)

    states_p = []
    states_s = None
    for l in range(n_a):
        proj = _in_proj(hn, w_in_a_t, l, GLA_MAIN_WIDTH, BF16, True, "gla_in_proj",
                        q_cols=GLA_KEY_DIM, q_scale=GLA_DK ** -0.5)
        onorm = onorm_a[l][None]
        og, s_p = _gla_prompt(proj, g, onorm, og, n_batch, chunks)
        og, states_s = _gla_sample(proj, g, onorm, state_gla, og, states_s, l,
                                   n_seq, t_len, srow)
        states_p.append(s_p)
        w_out = (w_out_a_bf16, l)
        if l + 1 < n_a:
            x, hn, g = _out_proj(og, w_out, x, [norm_a[l + 1]], gate_weights(l + 1), True,
                                 BF16, "gla_out_proj", tokens=tokens)
        else:
            x, hn, kv = _out_proj(og, w_out, x, [norm_b[0]], None, True, BF16, "gla_out_proj",
                                  tokens=tokens, kv=(norm_kv, w_kv.astype(BF16)))

    cache_k = jnp.transpose(cache_k_win, (0, 2, 3, 1)).reshape(n_seq, KV_WIDTH, win)
    cache_v = jnp.transpose(cache_v_win, (0, 2, 3, 1)).reshape(n_seq, KV_WIDTH, win)
    for j in range(n_b):
        pq = _in_proj(hn, w_in_b, j, 2 * Q_WIDTH, BF16, False, "swa_in_proj",
                      q_cols=Q_WIDTH, q_scale=HEAD_DIM ** -0.5 * LOG2_E)
        og = _swa_prompt(pq, kv, sinks_b[j], og, n_batch, chunks, front)
        og = _swa_sample(pq, kv, cache_k, cache_v, sinks_b[j], og, n_seq, t_len, srow)
        w_out = (w_out_b_bf16, j)
        if j + 1 < n_b:
            x, hn = _out_proj(og, w_out, x, [norm_b[j + 1]], None, True, BF16, "swa_out_proj")

    tiles_per_batch = seq // ROW_TILE
    lead_blocks = (lp - seq) // CHUNK

    def prompt_first_block(t):
        return ((t // tiles_per_batch) * chunks + lead_blocks
                + (t % tiles_per_batch) * (ROW_TILE // CHUNK))

    (y_prompt,) = _out_proj(og, w_out, x, [norm_f], None, False, F32, "final_prompt",
                            gather=(ROW_TILE, n_batch * tiles_per_batch, prompt_first_block))
    (y_sample,) = _out_proj(og, w_out, x, [norm_f], None, False, F32, "final_sample",
                            gather=(CHUNK, ns // CHUNK, lambda t: srow // CHUNK + t))
    y_prompt = y_prompt.reshape(n_batch, seq, D_MODEL)
    y_sample = y_sample.reshape(n_seq, t_len, D_MODEL)
    kv_p = jnp.stack([kv[(b + 1) * lp - win:(b + 1) * lp] for b in range(n_batch)])
    kv_p = kv_p.reshape(n_batch, win, 2, N_KV_HEADS, HEAD_DIM)
    kv_s = kv[srow:srow + ns].reshape(n_seq, t_len, 2, N_KV_HEADS, HEAD_DIM)
    k_win_s = jnp.concatenate([cache_k_win, kv_s[:, :, 0]], axis=1)[:, -win:]
    v_win_s = jnp.concatenate([cache_v_win, kv_s[:, :, 1]], axis=1)[:, -win:]
    return (y_prompt, y_sample, jnp.stack(states_p), states_s,
            kv_p[:, :, 0], kv_p[:, :, 1], k_win_s, v_win_s)
```

```python
import functools

import jax
import jax.numpy as jnp
from jax import lax
from jax.experimental import pallas as pl
from jax.experimental.pallas import tpu as pltpu

F32 = jnp.float32
BF16 = jnp.bfloat16

D_MODEL = 2048
N_META = 16
GLA_HEADS = 4
GLA_DK = 256
GLA_DV = 512
GLA_KEY_DIM = GLA_HEADS * GLA_DK
GLA_VAL_DIM = GLA_HEADS * GLA_DV
GLA_MAIN_WIDTH = 2 * GLA_KEY_DIM + 2 * GLA_VAL_DIM
GATE_RANK = 16
GATE_LOGIT_NORM = 16.0
HEAD_DIM = 64
N_Q_HEADS = 32
N_KV_HEADS = 8
Q_WIDTH = N_Q_HEADS * HEAD_DIM
KV_WIDTH = N_KV_HEADS * HEAD_DIM
WINDOW = 128
RMS_EPS = 1e-6
NEG_INF = -1e30
LOG2_E = 1.4426950408889634

LANES = 128
CHUNK = 128
GLA_LEAF = 8
LEAF_BATCH = 4
GLA_CHUNKS_PER_STEP = 3
GLA_BOUNDED_RANGE = 60.0
ROW_TILE = 512
IN_PROJ_COLS = 1024
IN_PROJ_ROW_STEPS = 8
SEQ_GROUP = 4
GLA_SEQ_GROUP = 8
GLA_SAMPLE_HEADS = 2
VMEM_LIMIT = 48 * 1024 * 1024

NT_DIMS = (((1,), (1,)), ((), ()))


def _silu(x):
    return x * (1.0 / (1.0 + jnp.exp2(x * -LOG2_E)))


def _params(sem):
    return pltpu.CompilerParams(dimension_semantics=sem, vmem_limit_bytes=VMEM_LIMIT)


def _emit_normed(y, nw_refs, gate_refs, hn_refs, g_ref):
    ms = jnp.mean(y * y, axis=-1, keepdims=True)
    yn = y * lax.rsqrt(ms + RMS_EPS)
    for nw_ref, hn_ref in zip(nw_refs, hn_refs):
        hn_ref[...] = (yn * nw_ref[...]).astype(hn_ref.dtype)
    if gate_refs:
        g_ref[...] = _decay_gate(hn_refs[0][...], *gate_refs)
    return yn


def _split_refs(refs, n_norms, with_gate):
    nw_refs = refs[:n_norms]
    gate_refs = refs[n_norms:n_norms + 3] if with_gate else None
    return nw_refs, gate_refs, refs[n_norms + 3 * with_gate:]


def _norm_specs(norms, gate, index):
    specs = [pl.BlockSpec((1, D_MODEL), index) for _ in norms]
    args = [nw[None] for nw in norms]
    if gate is not None:
        specs += [pl.BlockSpec((LANES, D_MODEL), index),
                  pl.BlockSpec((LANES, GLA_KEY_DIM), index),
                  pl.BlockSpec((1, GLA_KEY_DIM), index)]
        args += list(gate)
    return specs, args


TOKEN_SUB = ROW_TILE // CHUNK


def _token_rows(chunks, n_batch, src_refs):
    xp_refs, ex_refs = src_refs[:TOKEN_SUB], src_refs[TOKEN_SUB:]
    pieces = []
    for k in range(TOKEN_SUB):
        blk = pl.program_id(0) * TOKEN_SUB + k
        from_prompt = jnp.logical_and(blk % chunks != 0, blk < n_batch * chunks)
        pieces.append(jnp.where(from_prompt, xp_refs[k][...], ex_refs[k][...]))
    return jnp.concatenate(pieces, axis=0)


def _token_sources(x_prompt2d, extras, chunks, n_batch):
    prompt_blocks = x_prompt2d.shape[0] // CHUNK
    per_batch = prompt_blocks // n_batch

    def prompt_index(k):
        def index(i):
            blk = i * TOKEN_SUB + k
            src = (blk // chunks) * per_batch + (blk % chunks) - 1
            return (jnp.clip(src, 0, prompt_blocks - 1), 0)
        return index

    def extras_index(k):
        def index(i):
            blk = i * TOKEN_SUB + k
            return (jnp.clip(blk - n_batch * chunks + 1, 0, 2), 0)
        return index

    specs = ([pl.BlockSpec((CHUNK, D_MODEL), prompt_index(k)) for k in range(TOKEN_SUB)]
             + [pl.BlockSpec((CHUNK, D_MODEL), extras_index(k)) for k in range(TOKEN_SUB)])
    return specs, [x_prompt2d] * TOKEN_SUB + [extras] * TOKEN_SUB


def _embed_kernel(chunks, n_batch, with_gate, *refs):
    nw_refs, gate_refs, outs = _split_refs(refs[2 * TOKEN_SUB:], 1, with_gate)
    x = _token_rows(chunks, n_batch, refs[:2 * TOKEN_SUB])
    _emit_normed(x, nw_refs, gate_refs, outs[0:1], outs[1] if with_gate else None)


def _embed(tokens, norm, gate, rows, chunks, n_batch):
    src_specs, src_args = _token_sources(*tokens, chunks, n_batch)
    norm_specs, norm_args = _norm_specs([norm], gate, lambda i: (0, 0))
    out_specs = [pl.BlockSpec((ROW_TILE, D_MODEL), lambda i: (i, 0))]
    out_shape = [jax.ShapeDtypeStruct((rows, D_MODEL), BF16)]
    if gate is not None:
        out_specs.append(pl.BlockSpec((ROW_TILE, GLA_KEY_DIM), lambda i: (i, 0)))
        out_shape.append(jax.ShapeDtypeStruct((rows, GLA_KEY_DIM), F32))
    return pl.pallas_call(
        functools.partial(_embed_kernel, chunks, n_batch, gate is not None),
        grid=(rows // ROW_TILE,),
        in_specs=src_specs + norm_specs,
        out_specs=out_specs,
        out_shape=out_shape,
        compiler_params=_params(("parallel",)),
        name="embed",
    )(*src_args, *norm_args)


def _decay_gate(hn, wg_ref, wgk_ref, bgk_ref):
    glr = lax.dot_general(hn, wg_ref[...].astype(BF16), NT_DIMS, preferred_element_type=F32)
    z = jnp.dot(glr.astype(BF16), wgk_ref[...], preferred_element_type=F32) + bgk_ref[...]
    log_sig = jnp.minimum(z, 0.0) - jnp.log(1.0 + jnp.exp(-jnp.abs(z)))
    return log_sig * (LOG2_E / GATE_LOGIT_NORM)


def _in_proj_kernel(transposed, scale_ref, hn_ref, w_ref, w_out_ref, o_ref, w_out_bf16_ref,
                    w_bf16_ref):
    @pl.when(pl.program_id(1) == 0)
    def _():
        w_bf16_ref[...] = w_ref[...].astype(BF16)

    w_out_bf16_ref[...] = w_out_ref[...].astype(BF16)

    dims = NT_DIMS if transposed else (((1,), (0,)), ((), ()))
    o = lax.dot_general(hn_ref[...], w_bf16_ref[...], dims, preferred_element_type=F32)
    o_ref[...] = (o * scale_ref[pl.program_id(0)]).astype(o_ref.dtype)


def _in_proj(hn, w_stack, layer, n_cols, out_dtype, transposed, name, w_out_stack,
             q_cols=0, q_scale=1.0):
    rows = hn.shape[0]
    row_tile = rows // IN_PROJ_ROW_STEPS
    n_slabs = n_cols // IN_PROJ_COLS
    assert q_cols % IN_PROJ_COLS == 0
    scales = jnp.where(jnp.arange(n_slabs) < q_cols // IN_PROJ_COLS, q_scale, 1.0).astype(F32)
    if transposed:
        slab = (IN_PROJ_COLS, D_MODEL)
        w_spec = pl.BlockSpec((None,) + slab, lambda j, i: (layer, j, 0))
    else:
        slab = (D_MODEL, IN_PROJ_COLS)
        w_spec = pl.BlockSpec((None,) + slab, lambda j, i: (layer, 0, j))
    cast_blocks = D_MODEL // CHUNK
    assert n_slabs * IN_PROJ_ROW_STEPS >= cast_blocks
    cast_block = lambda j, i: jnp.minimum(j * IN_PROJ_ROW_STEPS + i, cast_blocks - 1)
    return pl.pallas_call(
        functools.partial(_in_proj_kernel, transposed),
        grid=(n_slabs, IN_PROJ_ROW_STEPS),
        in_specs=[pl.BlockSpec(memory_space=pltpu.SMEM),
                  pl.BlockSpec((row_tile, D_MODEL), lambda j, i: (i, 0)), w_spec,
                  pl.BlockSpec((None, CHUNK, D_MODEL),
                               lambda j, i: (layer, cast_block(j, i), 0))],
        out_specs=[pl.BlockSpec((row_tile, IN_PROJ_COLS), lambda j, i: (i, j)),
                   pl.BlockSpec((CHUNK, D_MODEL), lambda j, i: (cast_block(j, i), 0))],
        out_shape=[jax.ShapeDtypeStruct((rows, n_cols), out_dtype),
                   jax.ShapeDtypeStruct((D_MODEL, D_MODEL), BF16)],
        scratch_shapes=[pltpu.VMEM(slab, BF16)],
        compiler_params=_params(("arbitrary", "arbitrary")),
        name=name,
    )(scales, hn, w_stack, w_out_stack)


def _out_proj_kernel(sub, n_norms, with_gate, keep_x, token_layout, with_kv, *refs):
    og_refs, w_ref, refs = refs[:sub], refs[sub], refs[sub + 1:]
    n_x = sub if token_layout is None else 2 * TOKEN_SUB
    x_refs, refs = refs[:n_x], refs[n_x:]
    nw_refs, gate_refs, outs = _split_refs(refs, n_norms, with_gate)
    if with_kv:
        (nkv_ref, wkv_ref), outs, kv_ref = outs[:2], outs[2:-1], outs[-1]
    og = og_refs[0][...] if sub == 1 else jnp.concatenate([r[...] for r in og_refs], axis=0)
    if token_layout is not None:
        x = _token_rows(*token_layout, x_refs)
    elif sub == 1:
        x = x_refs[0][...]
    else:
        x = jnp.concatenate([r[...] for r in x_refs], axis=0)
    y = x + jnp.dot(og, w_ref[...], preferred_element_type=F32)
    if keep_x:
        outs[0][...] = y
        outs = outs[1:]
    yn = _emit_normed(y, nw_refs, gate_refs, outs[:n_norms],
                      outs[n_norms] if with_gate else None)
    if with_kv:
        kv_ref[...] = jnp.dot((yn * nkv_ref[...]).astype(BF16), wkv_ref[...],
                              preferred_element_type=F32)


def _out_proj(og, w, x, norms, gate, keep_x, normed_dtype, name, gather=None, tokens=None,
              kv=None):
    if gather is None:
        tile_rows, n_tiles, sub = ROW_TILE, og.shape[0] // ROW_TILE, 1
        in_row_specs = [pl.BlockSpec((ROW_TILE, D_MODEL), lambda i: (i, 0))]
    else:
        tile_rows, n_tiles, first_block = gather
        sub = tile_rows // CHUNK
        in_row_specs = [pl.BlockSpec((CHUNK, D_MODEL),
                                     functools.partial(lambda k, i: (first_block(i) + k, 0), k))
                        for k in range(sub)]
    rows = tile_rows * n_tiles
    row_spec = pl.BlockSpec((tile_rows, D_MODEL), lambda i: (i, 0))
    norm_specs, norm_args = _norm_specs(norms, gate, lambda i: (0, 0))
    out_specs, out_shape = [], []
    if keep_x:
        out_specs.append(row_spec)
        out_shape.append(jax.ShapeDtypeStruct((rows, D_MODEL), F32))
    for _ in norms:
        out_specs.append(row_spec)
        out_shape.append(jax.ShapeDtypeStruct((rows, D_MODEL), normed_dtype))
    if gate is not None:
        out_specs.append(pl.BlockSpec((tile_rows, GLA_KEY_DIM), lambda i: (i, 0)))
        out_shape.append(jax.ShapeDtypeStruct((rows, GLA_KEY_DIM), F32))
    w_stack, layer = w
    w_spec = pl.BlockSpec((None, D_MODEL, D_MODEL), lambda i: (layer, 0, 0),
                          pipeline_mode=pl.Buffered(1))
    if x is None:
        sources, chunks, n_batch = tokens
        x_specs, x_args = _token_sources(*sources, chunks, n_batch)
        token_layout = (chunks, n_batch)
    else:
        x_specs, x_args, token_layout = in_row_specs, [x] * sub, None
    if kv is not None:
        norm_kv, w_kv = kv
        norm_specs += [pl.BlockSpec((1, D_MODEL), lambda i: (0, 0)),
                       pl.BlockSpec(w_kv.shape, lambda i: (0, 0), pipeline_mode=pl.Buffered(1))]
        norm_args += [norm_kv[None], w_kv]
        out_specs.append(pl.BlockSpec((tile_rows, w_kv.shape[1]), lambda i: (i, 0)))
        out_shape.append(jax.ShapeDtypeStruct((rows, w_kv.shape[1]), F32))
    return pl.pallas_call(
        functools.partial(_out_proj_kernel, sub, len(norms), gate is not None, keep_x,
                          token_layout, kv is not None),
        grid=(n_tiles,),
        in_specs=in_row_specs + [w_spec] + x_specs + norm_specs,
        out_specs=out_specs,
        out_shape=out_shape,
        compiler_params=_params(("parallel",)),
        name=name,
    )(*([og] * sub), w_stack, *x_args, *norm_args)


def _exact_block(qs, k, b, n, width, col0):
    lane = lax.broadcasted_iota(jnp.int32, (n, width), 1)
    acc = jnp.zeros((n, width), F32)
    for j in range(n):
        t = qs * k[j:j + 1] * jnp.exp2(jnp.minimum(b - b[j:j + 1], 0.0))
        acc = jnp.where(lane == col0 + j, jnp.sum(t, axis=-1, keepdims=True), acc)
    return acc


def _masked_row_sums(mask_bf16, g):
    g_hi = g.astype(BF16)
    r1 = g - g_hi.astype(F32)
    g_mid = r1.astype(BF16)
    g_lo = (r1 - g_mid.astype(F32)).astype(BF16)
    return (jnp.dot(mask_bf16, g_hi, preferred_element_type=F32)
            + jnp.dot(mask_bf16, g_mid, preferred_element_type=F32)
            + jnp.dot(mask_bf16, g_lo, preferred_element_type=F32))


def _column_of(row_vec, n):
    return jnp.broadcast_to(row_vec, (LANES, n)).T[:, 0:1]


def _head_norm_gate(o, onorm, gate):
    ms = jnp.mean(o * o, axis=-1, keepdims=True)
    return (o * lax.rsqrt(ms + RMS_EPS) * onorm * _silu(gate)).astype(BF16)


def _intra_chunk_scores(qs, k, k_bf, b, row_xor_col):
    c_len = qs.shape[0]
    leaf = GLA_LEAF
    leaf_row = lax.broadcasted_iota(jnp.int32, (leaf, c_len), 0)
    leaf_col = lax.broadcasted_iota(jnp.int32, (leaf, c_len), 1)
    lane_in_leaf = [(leaf_col & (leaf - 1)) == j for j in range(leaf)]

    att_rows = []
    for blk0 in range(0, c_len // leaf, LEAF_BATCH):
        lhs = []
        for blk in range(blk0, blk0 + LEAF_BATCH):
            q_blk = qs[blk * leaf:(blk + 1) * leaf]
            b_blk = b[blk * leaf:(blk + 1) * leaf]
            lhs += [q_blk * jnp.exp2(jnp.minimum(b_blk - b_blk[j:j + 1], 0.0))
                    for j in range(leaf)]
        r = lax.dot_general(jnp.concatenate(lhs, axis=0).astype(BF16), k_bf, NT_DIMS,
                            preferred_element_type=F32)
        for i, blk in enumerate(range(blk0, blk0 + LEAF_BATCH)):
            acc = jnp.zeros((leaf, c_len), F32)
            for j in range(leaf):
                r0 = (i * leaf + j) * leaf
                acc = jnp.where(lane_in_leaf[j], r[r0:r0 + leaf], acc)
            rel = leaf_col - blk * leaf
            att_rows.append(jnp.where((rel >= 0) & (rel <= leaf_row), acc, 0.0))
    att = jnp.concatenate(att_rows, axis=0)

    m = c_len // 2
    while m >= leaf:
        q_parts, k_parts = [], []
        zeros = jnp.zeros((m, GLA_DK), F32)
        for base in range(0, c_len, 2 * m):
            ref = b[base + m - 1:base + m]
            lo = slice(base, base + m)
            hi = slice(base + m, base + 2 * m)
            k_parts += [k[lo] * jnp.exp2(ref - b[lo]), zeros]
            q_parts += [zeros, qs[hi] * jnp.exp2(b[hi] - ref)]
        qt = jnp.concatenate(q_parts, axis=0).astype(BF16)
        kt = jnp.concatenate(k_parts, axis=0).astype(BF16)
        a = lax.dot_general(qt, kt, NT_DIMS, preferred_element_type=F32)
        att = att + jnp.where(row_xor_col < 2 * m, a, 0.0)
        m //= 2
    return att


def _gla_prompt_kernel(q_ref, k_ref, v_ref, gate_ref, g_ref, onorm_ref, og_in_ref,
                       og_ref, s_ref, b_scr, qd_scr, att_scr):
    del og_in_ref
    c_len = CHUNK
    heads = [(slice(h * GLA_DK, (h + 1) * GLA_DK), slice(h * GLA_DV, (h + 1) * GLA_DV))
             for h in range(GLA_HEADS)]

    @pl.when(pl.program_id(1) == 0)
    def _():
        s_ref[...] = jnp.zeros_like(s_ref)

    row = lax.broadcasted_iota(jnp.int32, (c_len, c_len), 0)
    col = lax.broadcasted_iota(jnp.int32, (c_len, c_len), 1)
    lower = row >= col
    tri = jnp.where(lower, 1.0, 0.0).astype(BF16)
    onorm = onorm_ref[...]
    chunk_rows = [slice(c * c_len, (c + 1) * c_len) for c in range(q_ref.shape[0] // c_len)]

    total = None
    for rows in chunk_rows:
        b_c = _masked_row_sums(tri, g_ref[rows, :])
        b_scr[rows, :] = b_c
        qd_scr[rows, :] = (q_ref[rows, :].astype(F32) * jnp.exp2(b_c)).astype(BF16)
        last = b_c[c_len - 1:c_len]
        total = last if total is None else jnp.minimum(total, last)

    bounded = jnp.max(-total) <= GLA_BOUNDED_RANGE

    @pl.when(bounded)
    def _():
        for c, rows in enumerate(chunk_rows):
            for h, (ks, _) in enumerate(heads):
                k_inv = (k_ref[rows, ks].astype(F32) * jnp.exp2(-b_scr[rows, ks])).astype(BF16)
                a = lax.dot_general(qd_scr[rows, ks], k_inv, NT_DIMS,
                                    preferred_element_type=F32)
                att_scr[c * GLA_HEADS + h] = jnp.where(lower, a, 0.0).astype(BF16)

    @pl.when(jnp.logical_not(bounded))
    def _():
        for c, rows in enumerate(chunk_rows):
            for h, (ks, _) in enumerate(heads):
                k_bf = k_ref[rows, ks]
                att_scr[c * GLA_HEADS + h] = _intra_chunk_scores(
                    q_ref[rows, ks].astype(F32), k_bf.astype(F32), k_bf,
                    b_scr[rows, ks], row ^ col).astype(BF16)

    for c, rows in enumerate(chunk_rows):
        for h, (ks, vs) in enumerate(heads):
            k = k_ref[rows, ks].astype(F32)
            v = v_ref[rows, vs]
            b = b_scr[rows, ks]
            b_last = b[c_len - 1:c_len]
            state = s_ref[0, h]

            o = jnp.dot(qd_scr[rows, ks], state.astype(BF16), preferred_element_type=F32)
            o = o + jnp.dot(att_scr[c * GLA_HEADS + h], v, preferred_element_type=F32)

            kd_t = (k * jnp.exp2(b_last - b)).T.astype(BF16)
            s_ref[0, h] = (state * _column_of(jnp.exp2(b_last), GLA_DK)
                           + jnp.dot(kd_t, v, preferred_element_type=F32))

            og_ref[rows, vs] = _head_norm_gate(o, onorm, gate_ref[rows, vs].astype(F32))


def _gla_prompt(proj, g, onorm, og, n_batch, chunks):
    rows = proj.shape[0]
    per_step = next(n for n in (GLA_CHUNKS_PER_STEP, 1) if chunks % n == 0)
    steps = chunks // per_step
    tile = per_step * CHUNK
    blk = lambda col: (lambda b, c: (b * steps + c, col))
    return pl.pallas_call(
        _gla_prompt_kernel,
        grid=(n_batch, steps),
        in_specs=[
            pl.BlockSpec((tile, GLA_KEY_DIM), blk(0)),
            pl.BlockSpec((tile, GLA_KEY_DIM), blk(1)),
            pl.BlockSpec((tile, GLA_VAL_DIM), blk(1)),
            pl.BlockSpec((tile, GLA_VAL_DIM), blk(2)),
            pl.BlockSpec((tile, GLA_KEY_DIM), blk(0)),
            pl.BlockSpec((1, GLA_DV), lambda b, c: (0, 0)),
            pl.BlockSpec(memory_space=pl.ANY),
        ],
        out_specs=[
            pl.BlockSpec((tile, GLA_VAL_DIM), blk(0)),
            pl.BlockSpec((1, GLA_HEADS, GLA_DK, GLA_DV), lambda b, c: (b, 0, 0, 0)),
        ],
        out_shape=[
            jax.ShapeDtypeStruct((rows, GLA_VAL_DIM), BF16),
            jax.ShapeDtypeStruct((n_batch, GLA_HEADS, GLA_DK, GLA_DV), F32),
        ],
        scratch_shapes=[pltpu.VMEM((tile, GLA_KEY_DIM), F32),
                        pltpu.VMEM((tile, GLA_KEY_DIM), BF16),
                        pltpu.VMEM((per_step * GLA_HEADS, CHUNK, CHUNK), BF16)],
        input_output_aliases={6: 0},
        compiler_params=_params(("parallel", "arbitrary")),
        name="gla_prompt",
    )(proj, proj, proj, proj, g, onorm, og)


def _gla_sample_kernel(t_len, q_ref, k_ref, v_ref, gate_ref, g_ref, onorm_ref, s0_ref,
                       og_in_ref, *rest):
    og_ref, s_ref = rest[-2], rest[-1]
    n = GLA_SEQ_GROUP * t_len
    row = lax.broadcasted_iota(jnp.int32, (n, LANES), 0)
    col = lax.broadcasted_iota(jnp.int32, (n, LANES), 1)
    causal = ((row // t_len) == (col // t_len)) & (row >= col)
    causal_bf16 = jnp.where(causal, 1.0, 0.0).astype(BF16)
    row_seq = lax.broadcasted_iota(jnp.int32, (n, 1), 0) // t_len
    pad_k = jnp.zeros((LANES - n, GLA_DK), F32)

    for h in range(GLA_SAMPLE_HEADS):
        ks = slice(h * GLA_DK, (h + 1) * GLA_DK)
        vs = slice(h * GLA_DV, (h + 1) * GLA_DV)
        qs = q_ref[:, ks].astype(F32)
        k = k_ref[:, ks].astype(F32)
        v_pad = jnp.concatenate([v_ref[:, vs], jnp.zeros((LANES - n, GLA_DV), BF16)], axis=0)
        b = _masked_row_sums(causal_bf16, jnp.concatenate([g_ref[:, ks], pad_k], axis=0))
        att = jnp.where(causal, _exact_block(qs, k, b, n, LANES, 0), 0.0)
        o = jnp.dot(att.astype(BF16), v_pad, preferred_element_type=F32)

        q_dec = (qs * jnp.exp2(b)).astype(BF16)
        for u in range(GLA_SEQ_GROUP):
            state = s0_ref[0, u, h]
            o_u = jnp.dot(q_dec, state.astype(BF16), preferred_element_type=F32)
            o = o + jnp.where(row_seq == u, o_u, 0.0)
            b_last = b[(u + 1) * t_len - 1:(u + 1) * t_len]
            kd = jnp.where(row_seq == u, k * jnp.exp2(jnp.minimum(b_last - b, 0.0)), 0.0)
            kd_t = jnp.concatenate([kd, pad_k], axis=0).T.astype(BF16)
            s_ref[0, u, h] = (state * _column_of(jnp.exp2(b_last), GLA_DK)
                              + jnp.dot(kd_t, v_pad, preferred_element_type=F32))

        og_ref[:, vs] = _head_norm_gate(o, onorm_ref[...], gate_ref[:, vs].astype(F32))


def _gla_sample(proj, g, onorm, state_in, og, s_out, layer, n_seq, t_len, row0):
    n = GLA_SEQ_GROUP * t_len
    rb0 = row0 // n
    n_layers = state_in.shape[0]
    hs = GLA_SAMPLE_HEADS
    kw, vw = hs * GLA_DK, hs * GLA_DV
    key_blocks = GLA_KEY_DIM // kw
    val_blocks = GLA_VAL_DIM // vw
    st_spec = pl.BlockSpec((1, GLA_SEQ_GROUP, hs, GLA_DK, GLA_DV),
                           lambda s, h: (layer, s, h, 0, 0))
    in_specs = [
        pl.BlockSpec((n, kw), lambda s, h: (rb0 + s, h)),
        pl.BlockSpec((n, kw), lambda s, h: (rb0 + s, key_blocks + h)),
        pl.BlockSpec((n, vw), lambda s, h: (rb0 + s, val_blocks + h)),
        pl.BlockSpec((n, vw), lambda s, h: (rb0 + s, 2 * val_blocks + h)),
        pl.BlockSpec((n, kw), lambda s, h: (rb0 + s, h)),
        pl.BlockSpec((1, GLA_DV), lambda s, h: (0, 0)),
        st_spec,
        pl.BlockSpec(memory_space=pl.ANY),
    ]
    args = [proj, proj, proj, proj, g, onorm, state_in, og]
    aliases = {7: 0}
    if s_out is not None:
        in_specs.append(pl.BlockSpec(memory_space=pl.ANY))
        args.append(s_out)
        aliases[8] = 1
    return pl.pallas_call(
        functools.partial(_gla_sample_kernel, t_len),
        grid=(n_seq // GLA_SEQ_GROUP, GLA_HEADS // hs),
        in_specs=in_specs,
        out_specs=[
            pl.BlockSpec((n, vw), lambda s, h: (rb0 + s, h)),
            st_spec,
        ],
        out_shape=[
            jax.ShapeDtypeStruct(og.shape, BF16),
            jax.ShapeDtypeStruct((n_layers, n_seq, GLA_HEADS, GLA_DK, GLA_DV), F32),
        ],
        input_output_aliases=aliases,
        compiler_params=_params(("parallel", "parallel")),
        name="gla_sample",
    )(*args)


def _nt_dot(a, b):
    return lax.dot_general(a, b, NT_DIMS, preferred_element_type=F32)


def _lane_padded_copies(pair, fill):
    low = lax.broadcasted_iota(jnp.int32, pair.shape, 1) < HEAD_DIM
    swapped = pltpu.roll(pair, HEAD_DIM, axis=1).astype(BF16)
    pair = pair.astype(BF16)
    fill = jnp.full((), fill, BF16)
    return ((jnp.where(low, pair, fill), jnp.where(low, fill, swapped)),
            (jnp.where(low, swapped, fill), jnp.where(low, fill, pair)))


def _natural_kv_set(keys, vals, mask):
    slot0 = lax.broadcasted_iota(jnp.int32, keys[0].shape, 0) == 0

    def score_fns(cpair):
        return tuple(tuple(functools.partial(_nt_dot, b=k) for k in pads)
                     for pads in _lane_padded_copies(keys[cpair], 0.0))

    def value_fns(cpair):
        pads_by_head = _lane_padded_copies(jnp.where(slot0, 0.0, vals[cpair]), 1.0)
        return tuple(tuple((lambda p, v=v: jnp.dot(p, v, preferred_element_type=F32))
                           for v in pads) for pads in pads_by_head)

    return score_fns, value_fns, mask


def _window_kv_set(kt_ref, vt_ref, k_new_t, v_new_t, mask):
    win = kt_ref.shape[1]
    lane0 = lax.broadcasted_iota(jnp.int32, (HEAD_DIM, win), 1) == 0

    def head_keys(old_ref, new_t, cpair, parity, zero_slot0):
        r0 = (2 * cpair + parity) * HEAD_DIM
        old = old_ref[r0:r0 + HEAD_DIM, :]
        if zero_slot0:
            old = jnp.where(lane0, 0.0, old)
        new = new_t[cpair][parity * HEAD_DIM:(parity + 1) * HEAD_DIM, :]
        return jnp.concatenate([old, new], axis=1).astype(BF16)

    def stacks(head, other):
        return (jnp.concatenate([head, other], axis=0), jnp.concatenate([other, head], axis=0))

    def score_fns(cpair):
        fns = []
        for parity in range(2):
            head = head_keys(kt_ref, k_new_t, cpair, parity, False)
            fns.append(tuple(
                (lambda q, k=k: jnp.dot(q, k, preferred_element_type=F32))
                for k in stacks(head, jnp.zeros_like(head))))
        return tuple(fns)

    def value_fns(cpair):
        fns = []
        for parity in range(2):
            head = head_keys(vt_ref, v_new_t, cpair, parity, True)
            fns.append(tuple(functools.partial(_nt_dot, b=v)
                             for v in stacks(head, jnp.ones_like(head))))
        return tuple(fns)

    return score_fns, value_fns, mask


def _attend_rows(q_ref, gate_ref, og_ref, sink_ref, kv_sets, s_scr, p_scr):
    n_rows = q_ref.shape[0]
    group = N_Q_HEADS // N_KV_HEADS
    low_out = lax.broadcasted_iota(jnp.int32, (n_rows, LANES), 1) < HEAD_DIM
    lane0_row = lax.broadcasted_iota(jnp.int32, (1, LANES), 1) == 0

    def kv_heads():
        for cpair in range(KV_WIDTH // LANES):
            for parity in range(2):
                first = (2 * cpair + parity) * (group // 2)
                yield cpair, parity, list(range(first, first + group // 2))

    def slot(set_index, qc, half):
        return (set_index * (Q_WIDTH // LANES) + qc) * 2 + half

    for si, (score_fns, _, _) in enumerate(kv_sets):
        fns = None
        for cpair, parity, qcs in kv_heads():
            if parity == 0:
                fns = score_fns(cpair)
            q_stack = jnp.concatenate(
                [q_ref[:, qc * LANES:(qc + 1) * LANES] for qc in qcs], axis=0)
            for half, score in enumerate(fns[parity]):
                s = score(q_stack)
                for j, qc in enumerate(qcs):
                    s_scr[slot(si, qc, half)] = s[j * n_rows:(j + 1) * n_rows]

    for si, (_, _, mask) in enumerate(kv_sets):
        mask_first, mask_rest = mask[:, :LANES], mask[:, LANES:]
        for qc in range(Q_WIDTH // LANES):
            for half in range(2):
                sink = sink_ref[2 * qc + half] * LOG2_E
                s = s_scr[slot(si, qc, half)]
                first = jnp.where(mask_first, s[:, :LANES],
                                  jnp.where(lane0_row, sink, NEG_INF))
                rest = jnp.where(mask_rest, s[:, LANES:], NEG_INF)
                m = jnp.max(jnp.maximum(first, rest), axis=-1, keepdims=True)
                p_scr[slot(si, qc, half)] = jnp.exp2(
                    jnp.concatenate([first, rest], axis=1) - m).astype(BF16)

    def store(qc, o):
        cols = slice(qc * LANES, (qc + 1) * LANES)
        og_ref[:, cols] = (o * _silu(gate_ref[:, cols].astype(F32))).astype(BF16)

    out_cols = [None] * (Q_WIDTH // LANES)
    for si, (_, value_fns, _) in enumerate(kv_sets):
        fns = None
        for cpair, parity, qcs in kv_heads():
            if parity == 0:
                fns = value_fns(cpair)
            stacked = [weigh(jnp.concatenate([p_scr[slot(si, qc, half)] for qc in qcs],
                                             axis=0))
                       for half, weigh in enumerate(fns[parity])]
            for j, qc in enumerate(qcs):
                oa, ob = (o[j * n_rows:(j + 1) * n_rows] for o in stacked)
                sums = pltpu.roll(jnp.where(low_out, ob, oa), HEAD_DIM, axis=1)
                o2 = jnp.where(low_out, oa, ob) * (1.0 / sums)
                if len(kv_sets) == 1:
                    store(qc, o2)
                else:
                    out_cols[qc] = o2 if out_cols[qc] is None else out_cols[qc] + o2

    if len(kv_sets) > 1:
        for qc, o in enumerate(out_cols):
            store(qc, o)


def _attention_scratch(n_sets, n_rows, n_keys):
    n = n_sets * N_Q_HEADS
    return [pltpu.VMEM((n, n_rows, n_keys), F32), pltpu.VMEM((n, n_rows, n_keys), BF16)]


def _swa_prompt_kernel(front, sink_ref, q_ref, gate_ref, kc_ref, kp_ref, vc_ref, vp_ref,
                       og_in_ref, og_ref, s_scr, p_scr):
    del og_in_ref
    blk = pl.program_id(1)
    row = lax.broadcasted_iota(jnp.int32, (CHUNK, 2 * CHUNK), 0)
    col = lax.broadcasted_iota(jnp.int32, (CHUNK, 2 * CHUNK), 1)
    mask = (col > row) & (col <= row + WINDOW) & ((blk - 1) * CHUNK + col >= front)
    lane_pairs = [slice(c * LANES, (c + 1) * LANES) for c in range(KV_WIDTH // LANES)]
    keys = [jnp.concatenate([kp_ref[:, c], kc_ref[:, c]], axis=0) for c in lane_pairs]
    vals = [jnp.concatenate([vp_ref[:, c], vc_ref[:, c]], axis=0) for c in lane_pairs]
    _attend_rows(q_ref, gate_ref, og_ref, sink_ref, [_natural_kv_set(keys, vals, mask)],
                 s_scr, p_scr)


def _swa_prompt(pq, kv, sinks, og, n_batch, chunks, front):
    cur = lambda col: (lambda b, i: (b * chunks + i, col))
    prev = lambda col: (lambda b, i: (b * chunks + jnp.maximum(i - 1, 0), col))
    return pl.pallas_call(
        functools.partial(_swa_prompt_kernel, front),
        grid=(n_batch, chunks),
        in_specs=[
            pl.BlockSpec(memory_space=pltpu.SMEM),
            pl.BlockSpec((CHUNK, Q_WIDTH), cur(0)),
            pl.BlockSpec((CHUNK, Q_WIDTH), cur(1)),
            pl.BlockSpec((CHUNK, KV_WIDTH), cur(0)),
            pl.BlockSpec((CHUNK, KV_WIDTH), prev(0)),
            pl.BlockSpec((CHUNK, KV_WIDTH), cur(1)),
            pl.BlockSpec((CHUNK, KV_WIDTH), prev(1)),
            pl.BlockSpec(memory_space=pl.ANY),
        ],
        out_specs=pl.BlockSpec((CHUNK, Q_WIDTH), cur(0)),
        out_shape=jax.ShapeDtypeStruct(og.shape, BF16),
        scratch_shapes=_attention_scratch(1, CHUNK, 2 * CHUNK),
        input_output_aliases={7: 0},
        compiler_params=_params(("parallel", "parallel")),
        name="swa_prompt",
    )(sinks, pq, pq, kv, kv, kv, kv, og)


def _swa_sample_kernel(t_len, sink_ref, q_ref, gate_ref, kn_ref, vn_ref, ckt_ref, cvt_ref,
                       og_in_ref, og_ref, s_scr, p_scr):
    del og_in_ref
    n = SEQ_GROUP * t_len
    win = ckt_ref.shape[2]
    row = lax.broadcasted_iota(jnp.int32, (n, 2 * win), 0)
    col = lax.broadcasted_iota(jnp.int32, (n, 2 * win), 1)
    t = row % t_len
    new = col - win
    in_window = ((col < win) & (col > t)) | (
        (new >= 0) & (new < n) & ((new // t_len) == (row // t_len)) & ((new % t_len) <= t))
    pad = jnp.zeros((win - n, LANES), F32)
    lane_pairs = [slice(c * LANES, (c + 1) * LANES) for c in range(KV_WIDTH // LANES)]
    k_new_t = [jnp.concatenate([kn_ref[:, c], pad], axis=0).T for c in lane_pairs]
    v_new_t = [jnp.concatenate([vn_ref[:, c], pad], axis=0).T for c in lane_pairs]
    kv_sets = [_window_kv_set(ckt_ref.at[u], cvt_ref.at[u], k_new_t, v_new_t,
                              in_window & ((row // t_len) == u))
               for u in range(SEQ_GROUP)]
    _attend_rows(q_ref, gate_ref, og_ref, sink_ref, kv_sets, s_scr, p_scr)


def _swa_sample(pq, kv, cache_kt, cache_vt, sinks, og, n_seq, t_len, row0):
    n = SEQ_GROUP * t_len
    rb0 = row0 // n
    win = cache_kt.shape[2]
    return pl.pallas_call(
        functools.partial(_swa_sample_kernel, t_len),
        grid=(n_seq // SEQ_GROUP,),
        in_specs=[
            pl.BlockSpec(memory_space=pltpu.SMEM),
            pl.BlockSpec((n, Q_WIDTH), lambda s: (rb0 + s, 0)),
            pl.BlockSpec((n, Q_WIDTH), lambda s: (rb0 + s, 1)),
            pl.BlockSpec((n, KV_WIDTH), lambda s: (rb0 + s, 0)),
            pl.BlockSpec((n, KV_WIDTH), lambda s: (rb0 + s, 1)),
            pl.BlockSpec((SEQ_GROUP, KV_WIDTH, win), lambda s: (s, 0, 0)),
            pl.BlockSpec((SEQ_GROUP, KV_WIDTH, win), lambda s: (s, 0, 0)),
            pl.BlockSpec(memory_space=pl.ANY),
        ],
        out_specs=pl.BlockSpec((n, Q_WIDTH), lambda s: (rb0 + s, 0)),
        out_shape=jax.ShapeDtypeStruct(og.shape, BF16),
        scratch_shapes=_attention_scratch(SEQ_GROUP, n, 2 * win),
        input_output_aliases={7: 0},
        compiler_params=_params(("parallel",)),
        name="swa_sample",
    )(sinks, pq, pq, kv, kv, cache_kt, cache_vt, og)


def kernel(x_prompt, x_sample, state_gla, cache_k_win, cache_v_win, meta_tokens, norm_a, w_in_a, w_gk2_a, b_gk2_a, onorm_a, w_out_a, norm_kv, w_kv, norm_b, w_in_b, sinks_b, w_out_b, norm_f):
    n_batch, seq, _ = x_prompt.shape
    n_seq, t_len, _ = x_sample.shape
    win = cache_k_win.shape[1]
    n_a = w_in_a.shape[0]
    n_b = w_in_b.shape[0]
    assert win == WINDOW == CHUNK and n_seq % SEQ_GROUP == 0 and n_seq % GLA_SEQ_GROUP == 0

    length = seq + N_META
    front = (-length) % CHUNK
    lp = length + front
    chunks = lp // CHUNK
    srow = n_batch * lp
    ns = n_seq * t_len
    rows = -(-(srow + ns) // ROW_TILE) * ROW_TILE

    assert front + N_META == CHUNK and ns == CHUNK and seq % CHUNK == 0
    assert rows % CHUNK == 0 and rows % (16 * IN_PROJ_ROW_STEPS) == 0

    w_in_a_t = jnp.swapaxes(w_in_a, 1, 2)

    def gate_weights(l):
        w_gate = jnp.pad(w_in_a_t[l, GLA_MAIN_WIDTH:], ((0, LANES - GATE_RANK), (0, 0)))
        w_gk = jnp.pad(w_gk2_a[l], ((0, LANES - GATE_RANK), (0, 0))).astype(BF16)
        return w_gate, w_gk, b_gk2_a[l][None]

    extras = jnp.concatenate([jnp.zeros((front, D_MODEL), F32), meta_tokens.astype(F32),
                              x_sample.reshape(ns, D_MODEL), jnp.zeros((CHUNK, D_MODEL), F32)])
    tokens = ((x_prompt.reshape(n_batch * seq, D_MODEL), extras), chunks, n_batch)
    hn, g = _embed(tokens[0], norm_a[0], gate_weights(0), rows, chunks, n_batch)
    x = None
    og = jnp.zeros((rows, D_MODEL), BF16)

    states_p = []
    states_s = None
    for l in range(n_a):
        proj, w_out = _in_proj(hn, w_in_a_t, l, GLA_MAIN_WIDTH, BF16, True, "gla_in_proj",
                               w_out_a, q_cols=GLA_KEY_DIM, q_scale=GLA_DK ** -0.5)
        onorm = onorm_a[l][None]
        og, s_p = _gla_prompt(proj, g, onorm, og, n_batch, chunks)
        og, states_s = _gla_sample(proj, g, onorm, state_gla, og, states_s, l,
                                   n_seq, t_len, srow)
        states_p.append(s_p)
        w_out = (w_out[None], 0)
        if l + 1 < n_a:
            x, hn, g = _out_proj(og, w_out, x, [norm_a[l + 1]], gate_weights(l + 1), True,
                                 BF16, "gla_out_proj", tokens=tokens)
        else:
            x, hn, kv = _out_proj(og, w_out, x, [norm_b[0]], None, True, BF16, "gla_out_proj",
                                  tokens=tokens, kv=(norm_kv, w_kv.astype(BF16)))

    cache_k = jnp.transpose(cache_k_win, (0, 2, 3, 1)).reshape(n_seq, KV_WIDTH, win)
    cache_v = jnp.transpose(cache_v_win, (0, 2, 3, 1)).reshape(n_seq, KV_WIDTH, win)
    for j in range(n_b):
        pq, w_out = _in_proj(hn, w_in_b, j, 2 * Q_WIDTH, BF16, False, "swa_in_proj",
                             w_out_b, q_cols=Q_WIDTH, q_scale=HEAD_DIM ** -0.5 * LOG2_E)
        og = _swa_prompt(pq, kv, sinks_b[j], og, n_batch, chunks, front)
        og = _swa_sample(pq, kv, cache_k, cache_v, sinks_b[j], og, n_seq, t_len, srow)
        w_out = (w_out[None], 0)
        if j + 1 < n_b:
            x, hn = _out_proj(og, w_out, x, [norm_b[j + 1]], None, True, BF16, "swa_out_proj")

    tiles_per_batch = seq // ROW_TILE
    lead_blocks = (lp - seq) // CHUNK

    def prompt_first_block(t):
        return ((t // tiles_per_batch) * chunks + lead_blocks
                + (t % tiles_per_batch) * (ROW_TILE // CHUNK))

    (y_prompt,) = _out_proj(og, w_out, x, [norm_f], None, False, F32, "final_prompt",
                            gather=(ROW_TILE, n_batch * tiles_per_batch, prompt_first_block))
    (y_sample,) = _out_proj(og, w_out, x, [norm_f], None, False, F32, "final_sample",
                            gather=(CHUNK, ns // CHUNK, lambda t: srow // CHUNK + t))
    y_prompt = y_prompt.reshape(n_batch, seq, D_MODEL)
    y_sample = y_sample.reshape(n_seq, t_len, D_MODEL)
    kv_p = jnp.stack([kv[(b + 1) * lp - win:(b + 1) * lp] for b in range(n_batch)])
    kv_p = kv_p.reshape(n_batch, win, 2, N_KV_HEADS, HEAD_DIM)
    kv_s = kv[srow:srow + ns].reshape(n_seq, t_len, 2, N_KV_HEADS, HEAD_DIM)
    k_win_s = jnp.concatenate([cache_k_win, kv_s[:, :, 0]], axis=1)[:, -win:]
    v_win_s = jnp.concatenate([cache_v_win, kv_s[:, :, 1]], axis=1)[:, -win:]
    return (y_prompt, y_sample, jnp.stack(states_p), states_s,
            kv_p[:, :, 0], kv_p[:, :, 1], k_win_s, v_win_s)
```

```python
import functools

import jax
import jax.numpy as jnp
from jax import lax
from jax.experimental import pallas as pl
from jax.experimental.pallas import tpu as pltpu

F32 = jnp.float32
BF16 = jnp.bfloat16

D_MODEL = 2048
N_META = 16
GLA_HEADS = 4
GLA_DK = 256
GLA_DV = 512
GLA_KEY_DIM = GLA_HEADS * GLA_DK
GLA_VAL_DIM = GLA_HEADS * GLA_DV
GLA_MAIN_WIDTH = 2 * GLA_KEY_DIM + 2 * GLA_VAL_DIM
GATE_RANK = 16
GATE_LOGIT_NORM = 16.0
HEAD_DIM = 64
N_Q_HEADS = 32
N_KV_HEADS = 8
Q_WIDTH = N_Q_HEADS * HEAD_DIM
KV_WIDTH = N_KV_HEADS * HEAD_DIM
WINDOW = 128
RMS_EPS = 1e-6
NEG_INF = -1e30
LOG2_E = 1.4426950408889634

LANES = 128
CHUNK = 128
GLA_LEAF = 8
LEAF_BATCH = 4
GLA_CHUNKS_PER_STEP = 3
GLA_BOUNDED_RANGE = 60.0
ROW_TILE = 512
IN_PROJ_COLS = 1024
IN_PROJ_ROW_STEPS = 8
SEQ_GROUP = 4
GLA_SEQ_GROUP = 8
GLA_SAMPLE_HEADS = 2
VMEM_LIMIT = 48 * 1024 * 1024

NT_DIMS = (((1,), (1,)), ((), ()))


def _silu(x):
    return x * (1.0 / (1.0 + jnp.exp2(x * -LOG2_E)))


def _params(sem):
    return pltpu.CompilerParams(dimension_semantics=sem, vmem_limit_bytes=VMEM_LIMIT)


def _emit_normed(y, nw_refs, gate_refs, hn_refs, g_ref):
    ms = jnp.mean(y * y, axis=-1, keepdims=True)
    yn = y * lax.rsqrt(ms + RMS_EPS)
    for nw_ref, hn_ref in zip(nw_refs, hn_refs):
        hn_ref[...] = (yn * nw_ref[...]).astype(hn_ref.dtype)
    if gate_refs:
        g_ref[...] = _decay_gate(hn_refs[0][...], *gate_refs)
    return yn


def _split_refs(refs, n_norms, with_gate):
    nw_refs = refs[:n_norms]
    gate_refs = refs[n_norms:n_norms + 3] if with_gate else None
    return nw_refs, gate_refs, refs[n_norms + 3 * with_gate:]


def _norm_specs(norms, gate, index):
    specs = [pl.BlockSpec((1, D_MODEL), index) for _ in norms]
    args = [nw[None] for nw in norms]
    if gate is not None:
        specs += [pl.BlockSpec((LANES, D_MODEL), index),
                  pl.BlockSpec((LANES, GLA_KEY_DIM), index),
                  pl.BlockSpec((1, GLA_KEY_DIM), index)]
        args += list(gate)
    return specs, args


TOKEN_SUB = ROW_TILE // CHUNK


def _token_rows(chunks, n_batch, src_refs):
    xp_refs, ex_refs = src_refs[:TOKEN_SUB], src_refs[TOKEN_SUB:]
    pieces = []
    for k in range(TOKEN_SUB):
        blk = pl.program_id(0) * TOKEN_SUB + k
        from_prompt = jnp.logical_and(blk % chunks != 0, blk < n_batch * chunks)
        pieces.append(jnp.where(from_prompt, xp_refs[k][...], ex_refs[k][...]))
    return jnp.concatenate(pieces, axis=0)


def _token_sources(x_prompt2d, extras, chunks, n_batch):
    prompt_blocks = x_prompt2d.shape[0] // CHUNK
    per_batch = prompt_blocks // n_batch

    def prompt_index(k):
        def index(i):
            blk = i * TOKEN_SUB + k
            src = (blk // chunks) * per_batch + (blk % chunks) - 1
            return (jnp.clip(src, 0, prompt_blocks - 1), 0)
        return index

    def extras_index(k):
        def index(i):
            blk = i * TOKEN_SUB + k
            return (jnp.clip(blk - n_batch * chunks + 1, 0, 2), 0)
        return index

    specs = ([pl.BlockSpec((CHUNK, D_MODEL), prompt_index(k)) for k in range(TOKEN_SUB)]
             + [pl.BlockSpec((CHUNK, D_MODEL), extras_index(k)) for k in range(TOKEN_SUB)])
    return specs, [x_prompt2d] * TOKEN_SUB + [extras] * TOKEN_SUB


def _embed_kernel(chunks, n_batch, with_gate, *refs):
    nw_refs, gate_refs, outs = _split_refs(refs[2 * TOKEN_SUB:], 1, with_gate)
    x = _token_rows(chunks, n_batch, refs[:2 * TOKEN_SUB])
    _emit_normed(x, nw_refs, gate_refs, outs[0:1], outs[1] if with_gate else None)


def _embed(tokens, norm, gate, rows, chunks, n_batch):
    src_specs, src_args = _token_sources(*tokens, chunks, n_batch)
    norm_specs, norm_args = _norm_specs([norm], gate, lambda i: (0, 0))
    out_specs = [pl.BlockSpec((ROW_TILE, D_MODEL), lambda i: (i, 0))]
    out_shape = [jax.ShapeDtypeStruct((rows, D_MODEL), BF16)]
    if gate is not None:
        out_specs.append(pl.BlockSpec((ROW_TILE, GLA_KEY_DIM), lambda i: (i, 0)))
        out_shape.append(jax.ShapeDtypeStruct((rows, GLA_KEY_DIM), F32))
    return pl.pallas_call(
        functools.partial(_embed_kernel, chunks, n_batch, gate is not None),
        grid=(rows // ROW_TILE,),
        in_specs=src_specs + norm_specs,
        out_specs=out_specs,
        out_shape=out_shape,
        compiler_params=_params(("parallel",)),
        name="embed",
    )(*src_args, *norm_args)


def _decay_gate(hn, wg_ref, wgk_ref, bgk_ref):
    glr = lax.dot_general(hn, wg_ref[...].astype(BF16), NT_DIMS, preferred_element_type=F32)
    z = jnp.dot(glr.astype(BF16), wgk_ref[...], preferred_element_type=F32) + bgk_ref[...]
    log_sig = jnp.minimum(z, 0.0) - jnp.log(1.0 + jnp.exp(-jnp.abs(z)))
    return log_sig * (LOG2_E / GATE_LOGIT_NORM)


def _in_proj_kernel(transposed, scale_ref, hn_ref, w_ref, w_out_ref, o_ref, w_out_bf16_ref,
                    *rest):
    zero_refs, w_bf16_ref = rest[:-1], rest[-1]

    @pl.when(pl.program_id(1) == 0)
    def _():
        w_bf16_ref[...] = w_ref[...].astype(BF16)

    w_out_bf16_ref[...] = w_out_ref[...].astype(BF16)
    for zero_ref in zero_refs:
        zero_ref[...] = jnp.zeros_like(zero_ref)

    dims = NT_DIMS if transposed else (((1,), (0,)), ((), ()))
    o = lax.dot_general(hn_ref[...], w_bf16_ref[...], dims, preferred_element_type=F32)
    o_ref[...] = (o * scale_ref[pl.program_id(0)]).astype(o_ref.dtype)


def _in_proj(hn, w_stack, layer, n_cols, out_dtype, transposed, name, w_out_stack,
             q_cols=0, q_scale=1.0, zero_buffer=False):
    rows = hn.shape[0]
    row_tile = rows // IN_PROJ_ROW_STEPS
    n_slabs = n_cols // IN_PROJ_COLS
    assert q_cols % IN_PROJ_COLS == 0
    scales = jnp.where(jnp.arange(n_slabs) < q_cols // IN_PROJ_COLS, q_scale, 1.0).astype(F32)
    if transposed:
        slab = (IN_PROJ_COLS, D_MODEL)
        w_spec = pl.BlockSpec((None,) + slab, lambda j, i: (layer, j, 0))
    else:
        slab = (D_MODEL, IN_PROJ_COLS)
        w_spec = pl.BlockSpec((None,) + slab, lambda j, i: (layer, 0, j))
    cast_blocks = D_MODEL // CHUNK
    assert n_slabs * IN_PROJ_ROW_STEPS >= cast_blocks
    cast_block = lambda j, i: jnp.minimum(j * IN_PROJ_ROW_STEPS + i, cast_blocks - 1)
    out_specs = [pl.BlockSpec((row_tile, IN_PROJ_COLS), lambda j, i: (i, j)),
                 pl.BlockSpec((CHUNK, D_MODEL), lambda j, i: (cast_block(j, i), 0))]
    out_shape = [jax.ShapeDtypeStruct((rows, n_cols), out_dtype),
                 jax.ShapeDtypeStruct((D_MODEL, D_MODEL), BF16)]
    if zero_buffer:
        zero_rows = 2 * CHUNK
        zero_blocks = rows // zero_rows
        assert rows % zero_rows == 0 and n_slabs * IN_PROJ_ROW_STEPS >= zero_blocks
        out_specs.append(pl.BlockSpec(
            (zero_rows, D_MODEL),
            lambda j, i: (jnp.minimum(j * IN_PROJ_ROW_STEPS + i, zero_blocks - 1), 0)))
        out_shape.append(jax.ShapeDtypeStruct((rows, D_MODEL), BF16))
    return pl.pallas_call(
        functools.partial(_in_proj_kernel, transposed),
        grid=(n_slabs, IN_PROJ_ROW_STEPS),
        in_specs=[pl.BlockSpec(memory_space=pltpu.SMEM),
                  pl.BlockSpec((row_tile, D_MODEL), lambda j, i: (i, 0)), w_spec,
                  pl.BlockSpec((None, CHUNK, D_MODEL),
                               lambda j, i: (layer, cast_block(j, i), 0))],
        out_specs=out_specs,
        out_shape=out_shape,
        scratch_shapes=[pltpu.VMEM(slab, BF16)],
        compiler_params=_params(("arbitrary", "arbitrary")),
        name=name,
    )(scales, hn, w_stack, w_out_stack)


def _out_proj_kernel(sub, n_norms, with_gate, keep_x, token_layout, with_kv, *refs):
    og_refs, w_ref, refs = refs[:sub], refs[sub], refs[sub + 1:]
    n_x = sub if token_layout is None else 2 * TOKEN_SUB
    x_refs, refs = refs[:n_x], refs[n_x:]
    nw_refs, gate_refs, outs = _split_refs(refs, n_norms, with_gate)
    if with_kv:
        (nkv_ref, wkv_ref), outs, kv_ref = outs[:2], outs[2:-1], outs[-1]
    og = og_refs[0][...] if sub == 1 else jnp.concatenate([r[...] for r in og_refs], axis=0)
    if token_layout is not None:
        x = _token_rows(*token_layout, x_refs)
    elif sub == 1:
        x = x_refs[0][...]
    else:
        x = jnp.concatenate([r[...] for r in x_refs], axis=0)
    y = x + jnp.dot(og, w_ref[...], preferred_element_type=F32)
    if keep_x:
        outs[0][...] = y
        outs = outs[1:]
    yn = _emit_normed(y, nw_refs, gate_refs, outs[:n_norms],
                      outs[n_norms] if with_gate else None)
    if with_kv:
        kv_ref[...] = jnp.dot((yn * nkv_ref[...]).astype(BF16), wkv_ref[...],
                              preferred_element_type=F32)


def _out_proj(og, w, x, norms, gate, keep_x, normed_dtype, name, gather=None, tokens=None,
              kv=None):
    if gather is None:
        tile_rows, n_tiles, sub = ROW_TILE, og.shape[0] // ROW_TILE, 1
        in_row_specs = [pl.BlockSpec((ROW_TILE, D_MODEL), lambda i: (i, 0))]
    else:
        tile_rows, n_tiles, first_block = gather
        sub = tile_rows // CHUNK
        in_row_specs = [pl.BlockSpec((CHUNK, D_MODEL),
                                     functools.partial(lambda k, i: (first_block(i) + k, 0), k))
                        for k in range(sub)]
    rows = tile_rows * n_tiles
    row_spec = pl.BlockSpec((tile_rows, D_MODEL), lambda i: (i, 0))
    norm_specs, norm_args = _norm_specs(norms, gate, lambda i: (0, 0))
    out_specs, out_shape = [], []
    if keep_x:
        out_specs.append(row_spec)
        out_shape.append(jax.ShapeDtypeStruct((rows, D_MODEL), F32))
    for _ in norms:
        out_specs.append(row_spec)
        out_shape.append(jax.ShapeDtypeStruct((rows, D_MODEL), normed_dtype))
    if gate is not None:
        out_specs.append(pl.BlockSpec((tile_rows, GLA_KEY_DIM), lambda i: (i, 0)))
        out_shape.append(jax.ShapeDtypeStruct((rows, GLA_KEY_DIM), F32))
    w_stack, layer = w
    w_spec = pl.BlockSpec((None, D_MODEL, D_MODEL), lambda i: (layer, 0, 0),
                          pipeline_mode=pl.Buffered(1))
    if x is None:
        sources, chunks, n_batch = tokens
        x_specs, x_args = _token_sources(*sources, chunks, n_batch)
        token_layout = (chunks, n_batch)
    else:
        x_specs, x_args, token_layout = in_row_specs, [x] * sub, None
    if kv is not None:
        norm_kv, w_kv = kv
        norm_specs += [pl.BlockSpec((1, D_MODEL), lambda i: (0, 0)),
                       pl.BlockSpec(w_kv.shape, lambda i: (0, 0), pipeline_mode=pl.Buffered(1))]
        norm_args += [norm_kv[None], w_kv]
        out_specs.append(pl.BlockSpec((tile_rows, w_kv.shape[1]), lambda i: (i, 0)))
        out_shape.append(jax.ShapeDtypeStruct((rows, w_kv.shape[1]), F32))
    return pl.pallas_call(
        functools.partial(_out_proj_kernel, sub, len(norms), gate is not None, keep_x,
                          token_layout, kv is not None),
        grid=(n_tiles,),
        in_specs=in_row_specs + [w_spec] + x_specs + norm_specs,
        out_specs=out_specs,
        out_shape=out_shape,
        compiler_params=_params(("parallel",)),
        name=name,
    )(*([og] * sub), w_stack, *x_args, *norm_args)


def _exact_block(qs, k, b, n, width, col0):
    lane = lax.broadcasted_iota(jnp.int32, (n, width), 1)
    acc = jnp.zeros((n, width), F32)
    for j in range(n):
        t = qs * k[j:j + 1] * jnp.exp2(jnp.minimum(b - b[j:j + 1], 0.0))
        acc = jnp.where(lane == col0 + j, jnp.sum(t, axis=-1, keepdims=True), acc)
    return acc


def _masked_row_sums(mask_bf16, g):
    g_hi = g.astype(BF16)
    r1 = g - g_hi.astype(F32)
    g_mid = r1.astype(BF16)
    g_lo = (r1 - g_mid.astype(F32)).astype(BF16)
    return (jnp.dot(mask_bf16, g_hi, preferred_element_type=F32)
            + jnp.dot(mask_bf16, g_mid, preferred_element_type=F32)
            + jnp.dot(mask_bf16, g_lo, preferred_element_type=F32))


def _column_of(row_vec, n):
    return jnp.broadcast_to(row_vec, (LANES, n)).T[:, 0:1]


def _head_norm_gate(o, onorm, gate):
    ms = jnp.mean(o * o, axis=-1, keepdims=True)
    return (o * lax.rsqrt(ms + RMS_EPS) * onorm * _silu(gate)).astype(BF16)


def _intra_chunk_scores(qs, k, k_bf, b, row_xor_col):
    c_len = qs.shape[0]
    leaf = GLA_LEAF
    leaf_row = lax.broadcasted_iota(jnp.int32, (leaf, c_len), 0)
    leaf_col = lax.broadcasted_iota(jnp.int32, (leaf, c_len), 1)
    lane_in_leaf = [(leaf_col & (leaf - 1)) == j for j in range(leaf)]

    att_rows = []
    for blk0 in range(0, c_len // leaf, LEAF_BATCH):
        lhs = []
        for blk in range(blk0, blk0 + LEAF_BATCH):
            q_blk = qs[blk * leaf:(blk + 1) * leaf]
            b_blk = b[blk * leaf:(blk + 1) * leaf]
            lhs += [q_blk * jnp.exp2(jnp.minimum(b_blk - b_blk[j:j + 1], 0.0))
                    for j in range(leaf)]
        r = lax.dot_general(jnp.concatenate(lhs, axis=0).astype(BF16), k_bf, NT_DIMS,
                            preferred_element_type=F32)
        for i, blk in enumerate(range(blk0, blk0 + LEAF_BATCH)):
            acc = jnp.zeros((leaf, c_len), F32)
            for j in range(leaf):
                r0 = (i * leaf + j) * leaf
                acc = jnp.where(lane_in_leaf[j], r[r0:r0 + leaf], acc)
            rel = leaf_col - blk * leaf
            att_rows.append(jnp.where((rel >= 0) & (rel <= leaf_row), acc, 0.0))
    att = jnp.concatenate(att_rows, axis=0)

    m = c_len // 2
    while m >= leaf:
        q_parts, k_parts = [], []
        zeros = jnp.zeros((m, GLA_DK), F32)
        for base in range(0, c_len, 2 * m):
            ref = b[base + m - 1:base + m]
            lo = slice(base, base + m)
            hi = slice(base + m, base + 2 * m)
            k_parts += [k[lo] * jnp.exp2(ref - b[lo]), zeros]
            q_parts += [zeros, qs[hi] * jnp.exp2(b[hi] - ref)]
        qt = jnp.concatenate(q_parts, axis=0).astype(BF16)
        kt = jnp.concatenate(k_parts, axis=0).astype(BF16)
        a = lax.dot_general(qt, kt, NT_DIMS, preferred_element_type=F32)
        att = att + jnp.where(row_xor_col < 2 * m, a, 0.0)
        m //= 2
    return att


def _gla_prompt_kernel(q_ref, k_ref, v_ref, gate_ref, g_ref, onorm_ref, og_in_ref,
                       og_ref, s_ref, b_scr, qd_scr, att_scr):
    del og_in_ref
    c_len = CHUNK
    heads = [(slice(h * GLA_DK, (h + 1) * GLA_DK), slice(h * GLA_DV, (h + 1) * GLA_DV))
             for h in range(GLA_HEADS)]

    @pl.when(pl.program_id(1) == 0)
    def _():
        s_ref[...] = jnp.zeros_like(s_ref)

    row = lax.broadcasted_iota(jnp.int32, (c_len, c_len), 0)
    col = lax.broadcasted_iota(jnp.int32, (c_len, c_len), 1)
    lower = row >= col
    tri = jnp.where(lower, 1.0, 0.0).astype(BF16)
    onorm = onorm_ref[...]
    chunk_rows = [slice(c * c_len, (c + 1) * c_len) for c in range(q_ref.shape[0] // c_len)]

    total = None
    for rows in chunk_rows:
        b_c = _masked_row_sums(tri, g_ref[rows, :])
        b_scr[rows, :] = b_c
        qd_scr[rows, :] = (q_ref[rows, :].astype(F32) * jnp.exp2(b_c)).astype(BF16)
        last = b_c[c_len - 1:c_len]
        total = last if total is None else jnp.minimum(total, last)

    bounded = jnp.max(-total) <= GLA_BOUNDED_RANGE

    @pl.when(bounded)
    def _():
        for c, rows in enumerate(chunk_rows):
            for h, (ks, _) in enumerate(heads):
                k_inv = (k_ref[rows, ks].astype(F32) * jnp.exp2(-b_scr[rows, ks])).astype(BF16)
                a = lax.dot_general(qd_scr[rows, ks], k_inv, NT_DIMS,
                                    preferred_element_type=F32)
                att_scr[c * GLA_HEADS + h] = jnp.where(lower, a, 0.0).astype(BF16)

    @pl.when(jnp.logical_not(bounded))
    def _():
        for c, rows in enumerate(chunk_rows):
            for h, (ks, _) in enumerate(heads):
                k_bf = k_ref[rows, ks]
                att_scr[c * GLA_HEADS + h] = _intra_chunk_scores(
                    q_ref[rows, ks].astype(F32), k_bf.astype(F32), k_bf,
                    b_scr[rows, ks], row ^ col).astype(BF16)

    for c, rows in enumerate(chunk_rows):
        for h, (ks, vs) in enumerate(heads):
            k = k_ref[rows, ks].astype(F32)
            v = v_ref[rows, vs]
            b = b_scr[rows, ks]
            b_last = b[c_len - 1:c_len]
            state = s_ref[0, h]

            o = jnp.dot(qd_scr[rows, ks], state.astype(BF16), preferred_element_type=F32)
            o = o + jnp.dot(att_scr[c * GLA_HEADS + h], v, preferred_element_type=F32)

            kd_t = (k * jnp.exp2(b_last - b)).T.astype(BF16)
            s_ref[0, h] = (state * _column_of(jnp.exp2(b_last), GLA_DK)
                           + jnp.dot(kd_t, v, preferred_element_type=F32))

            og_ref[rows, vs] = _head_norm_gate(o, onorm, gate_ref[rows, vs].astype(F32))


def _gla_prompt(proj, g, onorm, og, n_batch, chunks):
    rows = proj.shape[0]
    per_step = next(n for n in (GLA_CHUNKS_PER_STEP, 1) if chunks % n == 0)
    steps = chunks // per_step
    tile = per_step * CHUNK
    blk = lambda col: (lambda b, c: (b * steps + c, col))
    return pl.pallas_call(
        _gla_prompt_kernel,
        grid=(n_batch, steps),
        in_specs=[
            pl.BlockSpec((tile, GLA_KEY_DIM), blk(0)),
            pl.BlockSpec((tile, GLA_KEY_DIM), blk(1)),
            pl.BlockSpec((tile, GLA_VAL_DIM), blk(1)),
            pl.BlockSpec((tile, GLA_VAL_DIM), blk(2)),
            pl.BlockSpec((tile, GLA_KEY_DIM), blk(0)),
            pl.BlockSpec((1, GLA_DV), lambda b, c: (0, 0)),
            pl.BlockSpec(memory_space=pl.ANY),
        ],
        out_specs=[
            pl.BlockSpec((tile, GLA_VAL_DIM), blk(0)),
            pl.BlockSpec((1, GLA_HEADS, GLA_DK, GLA_DV), lambda b, c: (b, 0, 0, 0)),
        ],
        out_shape=[
            jax.ShapeDtypeStruct((rows, GLA_VAL_DIM), BF16),
            jax.ShapeDtypeStruct((n_batch, GLA_HEADS, GLA_DK, GLA_DV), F32),
        ],
        scratch_shapes=[pltpu.VMEM((tile, GLA_KEY_DIM), F32),
                        pltpu.VMEM((tile, GLA_KEY_DIM), BF16),
                        pltpu.VMEM((per_step * GLA_HEADS, CHUNK, CHUNK), BF16)],
        input_output_aliases={6: 0},
        compiler_params=_params(("parallel", "arbitrary")),
        name="gla_prompt",
    )(proj, proj, proj, proj, g, onorm, og)


def _gla_sample_kernel(t_len, q_ref, k_ref, v_ref, gate_ref, g_ref, onorm_ref, s0_ref,
                       og_in_ref, *rest):
    og_ref, s_ref = rest[-2], rest[-1]
    n = GLA_SEQ_GROUP * t_len
    row = lax.broadcasted_iota(jnp.int32, (n, LANES), 0)
    col = lax.broadcasted_iota(jnp.int32, (n, LANES), 1)
    causal = ((row // t_len) == (col // t_len)) & (row >= col)
    causal_bf16 = jnp.where(causal, 1.0, 0.0).astype(BF16)
    row_seq = lax.broadcasted_iota(jnp.int32, (n, 1), 0) // t_len
    pad_k = jnp.zeros((LANES - n, GLA_DK), F32)

    for h in range(GLA_SAMPLE_HEADS):
        ks = slice(h * GLA_DK, (h + 1) * GLA_DK)
        vs = slice(h * GLA_DV, (h + 1) * GLA_DV)
        qs = q_ref[:, ks].astype(F32)
        k = k_ref[:, ks].astype(F32)
        v_pad = jnp.concatenate([v_ref[:, vs], jnp.zeros((LANES - n, GLA_DV), BF16)], axis=0)
        b = _masked_row_sums(causal_bf16, jnp.concatenate([g_ref[:, ks], pad_k], axis=0))
        att = jnp.where(causal, _exact_block(qs, k, b, n, LANES, 0), 0.0)
        o = jnp.dot(att.astype(BF16), v_pad, preferred_element_type=F32)

        q_dec = (qs * jnp.exp2(b)).astype(BF16)
        for u in range(GLA_SEQ_GROUP):
            state = s0_ref[0, u, h]
            o_u = jnp.dot(q_dec, state.astype(BF16), preferred_element_type=F32)
            o = o + jnp.where(row_seq == u, o_u, 0.0)
            b_last = b[(u + 1) * t_len - 1:(u + 1) * t_len]
            kd = jnp.where(row_seq == u, k * jnp.exp2(jnp.minimum(b_last - b, 0.0)), 0.0)
            kd_t = jnp.concatenate([kd, pad_k], axis=0).T.astype(BF16)
            s_ref[0, u, h] = (state * _column_of(jnp.exp2(b_last), GLA_DK)
                              + jnp.dot(kd_t, v_pad, preferred_element_type=F32))

        og_ref[:, vs] = _head_norm_gate(o, onorm_ref[...], gate_ref[:, vs].astype(F32))


def _gla_sample(proj, g, onorm, state_in, og, s_out, layer, n_seq, t_len, row0):
    n = GLA_SEQ_GROUP * t_len
    rb0 = row0 // n
    n_layers = state_in.shape[0]
    hs = GLA_SAMPLE_HEADS
    kw, vw = hs * GLA_DK, hs * GLA_DV
    key_blocks = GLA_KEY_DIM // kw
    val_blocks = GLA_VAL_DIM // vw
    st_spec = pl.BlockSpec((1, GLA_SEQ_GROUP, hs, GLA_DK, GLA_DV),
                           lambda s, h: (layer, s, h, 0, 0))
    in_specs = [
        pl.BlockSpec((n, kw), lambda s, h: (rb0 + s, h)),
        pl.BlockSpec((n, kw), lambda s, h: (rb0 + s, key_blocks + h)),
        pl.BlockSpec((n, vw), lambda s, h: (rb0 + s, val_blocks + h)),
        pl.BlockSpec((n, vw), lambda s, h: (rb0 + s, 2 * val_blocks + h)),
        pl.BlockSpec((n, kw), lambda s, h: (rb0 + s, h)),
        pl.BlockSpec((1, GLA_DV), lambda s, h: (0, 0)),
        st_spec,
        pl.BlockSpec(memory_space=pl.ANY),
    ]
    args = [proj, proj, proj, proj, g, onorm, state_in, og]
    aliases = {7: 0}
    if s_out is not None:
        in_specs.append(pl.BlockSpec(memory_space=pl.ANY))
        args.append(s_out)
        aliases[8] = 1
    return pl.pallas_call(
        functools.partial(_gla_sample_kernel, t_len),
        grid=(n_seq // GLA_SEQ_GROUP, GLA_HEADS // hs),
        in_specs=in_specs,
        out_specs=[
            pl.BlockSpec((n, vw), lambda s, h: (rb0 + s, h)),
            st_spec,
        ],
        out_shape=[
            jax.ShapeDtypeStruct(og.shape, BF16),
            jax.ShapeDtypeStruct((n_layers, n_seq, GLA_HEADS, GLA_DK, GLA_DV), F32),
        ],
        input_output_aliases=aliases,
        compiler_params=_params(("parallel", "parallel")),
        name="gla_sample",
    )(*args)


def _nt_dot(a, b):
    return lax.dot_general(a, b, NT_DIMS, preferred_element_type=F32)


def _lane_padded_copies(pair, fill):
    low = lax.broadcasted_iota(jnp.int32, pair.shape, 1) < HEAD_DIM
    swapped = pltpu.roll(pair, HEAD_DIM, axis=1).astype(BF16)
    pair = pair.astype(BF16)
    fill = jnp.full((), fill, BF16)
    return ((jnp.where(low, pair, fill), jnp.where(low, fill, swapped)),
            (jnp.where(low, swapped, fill), jnp.where(low, fill, pair)))


def _natural_kv_set(keys, vals, mask):
    slot0 = lax.broadcasted_iota(jnp.int32, keys[0].shape, 0) == 0

    def score_fns(cpair):
        return tuple(tuple(functools.partial(_nt_dot, b=k) for k in pads)
                     for pads in _lane_padded_copies(keys[cpair], 0.0))

    def value_fns(cpair):
        pads_by_head = _lane_padded_copies(jnp.where(slot0, 0.0, vals[cpair]), 1.0)
        return tuple(tuple((lambda p, v=v: jnp.dot(p, v, preferred_element_type=F32))
                           for v in pads) for pads in pads_by_head)

    return score_fns, value_fns, mask


def _window_kv_set(kt_ref, vt_ref, k_new_t, v_new_t, mask):
    win = kt_ref.shape[1]
    lane0 = lax.broadcasted_iota(jnp.int32, (HEAD_DIM, win), 1) == 0

    def head_keys(old_ref, new_t, cpair, parity, zero_slot0):
        r0 = (2 * cpair + parity) * HEAD_DIM
        old = old_ref[r0:r0 + HEAD_DIM, :]
        if zero_slot0:
            old = jnp.where(lane0, 0.0, old)
        new = new_t[cpair][parity * HEAD_DIM:(parity + 1) * HEAD_DIM, :]
        return jnp.concatenate([old, new], axis=1).astype(BF16)

    def stacks(head, other):
        return (jnp.concatenate([head, other], axis=0), jnp.concatenate([other, head], axis=0))

    def score_fns(cpair):
        fns = []
        for parity in range(2):
            head = head_keys(kt_ref, k_new_t, cpair, parity, False)
            fns.append(tuple(
                (lambda q, k=k: jnp.dot(q, k, preferred_element_type=F32))
                for k in stacks(head, jnp.zeros_like(head))))
        return tuple(fns)

    def value_fns(cpair):
        fns = []
        for parity in range(2):
            head = head_keys(vt_ref, v_new_t, cpair, parity, True)
            fns.append(tuple(functools.partial(_nt_dot, b=v)
                             for v in stacks(head, jnp.ones_like(head))))
        return tuple(fns)

    return score_fns, value_fns, mask


def _attend_rows(q_ref, gate_ref, og_ref, sink_ref, kv_sets, s_scr, p_scr):
    n_rows = q_ref.shape[0]
    group = N_Q_HEADS // N_KV_HEADS
    low_out = lax.broadcasted_iota(jnp.int32, (n_rows, LANES), 1) < HEAD_DIM
    lane0_row = lax.broadcasted_iota(jnp.int32, (1, LANES), 1) == 0

    def kv_heads():
        for cpair in range(KV_WIDTH // LANES):
            for parity in range(2):
                first = (2 * cpair + parity) * (group // 2)
                yield cpair, parity, list(range(first, first + group // 2))

    def slot(set_index, qc, half):
        return (set_index * (Q_WIDTH // LANES) + qc) * 2 + half

    for si, (score_fns, _, _) in enumerate(kv_sets):
        fns = None
        for cpair, parity, qcs in kv_heads():
            if parity == 0:
                fns = score_fns(cpair)
            q_stack = jnp.concatenate(
                [q_ref[:, qc * LANES:(qc + 1) * LANES] for qc in qcs], axis=0)
            for half, score in enumerate(fns[parity]):
                s = score(q_stack)
                for j, qc in enumerate(qcs):
                    s_scr[slot(si, qc, half)] = s[j * n_rows:(j + 1) * n_rows]

    for si, (_, _, mask) in enumerate(kv_sets):
        mask_first, mask_rest = mask[:, :LANES], mask[:, LANES:]
        for qc in range(Q_WIDTH // LANES):
            for half in range(2):
                sink = sink_ref[2 * qc + half] * LOG2_E
                s = s_scr[slot(si, qc, half)]
                first = jnp.where(mask_first, s[:, :LANES],
                                  jnp.where(lane0_row, sink, NEG_INF))
                rest = jnp.where(mask_rest, s[:, LANES:], NEG_INF)
                m = jnp.max(jnp.maximum(first, rest), axis=-1, keepdims=True)
                p_scr[slot(si, qc, half)] = jnp.exp2(
                    jnp.concatenate([first, rest], axis=1) - m).astype(BF16)

    def store(qc, o):
        cols = slice(qc * LANES, (qc + 1) * LANES)
        og_ref[:, cols] = (o * _silu(gate_ref[:, cols].astype(F32))).astype(BF16)

    out_cols = [None] * (Q_WIDTH // LANES)
    for si, (_, value_fns, _) in enumerate(kv_sets):
        fns = None
        for cpair, parity, qcs in kv_heads():
            if parity == 0:
                fns = value_fns(cpair)
            stacked = [weigh(jnp.concatenate([p_scr[slot(si, qc, half)] for qc in qcs],
                                             axis=0))
                       for half, weigh in enumerate(fns[parity])]
            for j, qc in enumerate(qcs):
                oa, ob = (o[j * n_rows:(j + 1) * n_rows] for o in stacked)
                sums = pltpu.roll(jnp.where(low_out, ob, oa), HEAD_DIM, axis=1)
                o2 = jnp.where(low_out, oa, ob) * (1.0 / sums)
                if len(kv_sets) == 1:
                    store(qc, o2)
                else:
                    out_cols[qc] = o2 if out_cols[qc] is None else out_cols[qc] + o2

    if len(kv_sets) > 1:
        for qc, o in enumerate(out_cols):
            store(qc, o)


def _attention_scratch(n_sets, n_rows, n_keys):
    n = n_sets * N_Q_HEADS
    return [pltpu.VMEM((n, n_rows, n_keys), F32), pltpu.VMEM((n, n_rows, n_keys), BF16)]


def _swa_prompt_kernel(front, sink_ref, q_ref, gate_ref, kc_ref, kp_ref, vc_ref, vp_ref,
                       og_in_ref, og_ref, s_scr, p_scr):
    del og_in_ref
    blk = pl.program_id(1)
    row = lax.broadcasted_iota(jnp.int32, (CHUNK, 2 * CHUNK), 0)
    col = lax.broadcasted_iota(jnp.int32, (CHUNK, 2 * CHUNK), 1)
    mask = (col > row) & (col <= row + WINDOW) & ((blk - 1) * CHUNK + col >= front)
    lane_pairs = [slice(c * LANES, (c + 1) * LANES) for c in range(KV_WIDTH // LANES)]
    keys = [jnp.concatenate([kp_ref[:, c], kc_ref[:, c]], axis=0) for c in lane_pairs]
    vals = [jnp.concatenate([vp_ref[:, c], vc_ref[:, c]], axis=0) for c in lane_pairs]
    _attend_rows(q_ref, gate_ref, og_ref, sink_ref, [_natural_kv_set(keys, vals, mask)],
                 s_scr, p_scr)


def _swa_prompt(pq, kv, sinks, og, n_batch, chunks, front):
    cur = lambda col: (lambda b, i: (b * chunks + i, col))
    prev = lambda col: (lambda b, i: (b * chunks + jnp.maximum(i - 1, 0), col))
    return pl.pallas_call(
        functools.partial(_swa_prompt_kernel, front),
        grid=(n_batch, chunks),
        in_specs=[
            pl.BlockSpec(memory_space=pltpu.SMEM),
            pl.BlockSpec((CHUNK, Q_WIDTH), cur(0)),
            pl.BlockSpec((CHUNK, Q_WIDTH), cur(1)),
            pl.BlockSpec((CHUNK, KV_WIDTH), cur(0)),
            pl.BlockSpec((CHUNK, KV_WIDTH), prev(0)),
            pl.BlockSpec((CHUNK, KV_WIDTH), cur(1)),
            pl.BlockSpec((CHUNK, KV_WIDTH), prev(1)),
            pl.BlockSpec(memory_space=pl.ANY),
        ],
        out_specs=pl.BlockSpec((CHUNK, Q_WIDTH), cur(0)),
        out_shape=jax.ShapeDtypeStruct(og.shape, BF16),
        scratch_shapes=_attention_scratch(1, CHUNK, 2 * CHUNK),
        input_output_aliases={7: 0},
        compiler_params=_params(("parallel", "parallel")),
        name="swa_prompt",
    )(sinks, pq, pq, kv, kv, kv, kv, og)


def _swa_sample_kernel(t_len, sink_ref, q_ref, gate_ref, kn_ref, vn_ref, ckt_ref, cvt_ref,
                       og_in_ref, og_ref, s_scr, p_scr):
    del og_in_ref
    n = SEQ_GROUP * t_len
    win = ckt_ref.shape[2]
    row = lax.broadcasted_iota(jnp.int32, (n, 2 * win), 0)
    col = lax.broadcasted_iota(jnp.int32, (n, 2 * win), 1)
    t = row % t_len
    new = col - win
    in_window = ((col < win) & (col > t)) | (
        (new >= 0) & (new < n) & ((new // t_len) == (row // t_len)) & ((new % t_len) <= t))
    pad = jnp.zeros((win - n, LANES), F32)
    lane_pairs = [slice(c * LANES, (c + 1) * LANES) for c in range(KV_WIDTH // LANES)]
    k_new_t = [jnp.concatenate([kn_ref[:, c], pad], axis=0).T for c in lane_pairs]
    v_new_t = [jnp.concatenate([vn_ref[:, c], pad], axis=0).T for c in lane_pairs]
    kv_sets = [_window_kv_set(ckt_ref.at[u], cvt_ref.at[u], k_new_t, v_new_t,
                              in_window & ((row // t_len) == u))
               for u in range(SEQ_GROUP)]
    _attend_rows(q_ref, gate_ref, og_ref, sink_ref, kv_sets, s_scr, p_scr)


def _swa_sample(pq, kv, cache_kt, cache_vt, sinks, og, n_seq, t_len, row0):
    n = SEQ_GROUP * t_len
    rb0 = row0 // n
    win = cache_kt.shape[2]
    return pl.pallas_call(
        functools.partial(_swa_sample_kernel, t_len),
        grid=(n_seq // SEQ_GROUP,),
        in_specs=[
            pl.BlockSpec(memory_space=pltpu.SMEM),
            pl.BlockSpec((n, Q_WIDTH), lambda s: (rb0 + s, 0)),
            pl.BlockSpec((n, Q_WIDTH), lambda s: (rb0 + s, 1)),
            pl.BlockSpec((n, KV_WIDTH), lambda s: (rb0 + s, 0)),
            pl.BlockSpec((n, KV_WIDTH), lambda s: (rb0 + s, 1)),
            pl.BlockSpec((SEQ_GROUP, KV_WIDTH, win), lambda s: (s, 0, 0)),
            pl.BlockSpec((SEQ_GROUP, KV_WIDTH, win), lambda s: (s, 0, 0)),
            pl.BlockSpec(memory_space=pl.ANY),
        ],
        out_specs=pl.BlockSpec((n, Q_WIDTH), lambda s: (rb0 + s, 0)),
        out_shape=jax.ShapeDtypeStruct(og.shape, BF16),
        scratch_shapes=_attention_scratch(SEQ_GROUP, n, 2 * win),
        input_output_aliases={7: 0},
        compiler_params=_params(("parallel",)),
        name="swa_sample",
    )(sinks, pq, pq, kv, kv, cache_kt, cache_vt, og)


def kernel(x_prompt, x_sample, state_gla, cache_k_win, cache_v_win, meta_tokens, norm_a, w_in_a, w_gk2_a, b_gk2_a, onorm_a, w_out_a, norm_kv, w_kv, norm_b, w_in_b, sinks_b, w_out_b, norm_f):
    n_batch, seq, _ = x_prompt.shape
    n_seq, t_len, _ = x_sample.shape
    win = cache_k_win.shape[1]
    n_a = w_in_a.shape[0]
    n_b = w_in_b.shape[0]
    assert win == WINDOW == CHUNK and n_seq % SEQ_GROUP == 0 and n_seq % GLA_SEQ_GROUP == 0

    length = seq + N_META
    front = (-length) % CHUNK
    lp = length + front
    chunks = lp // CHUNK
    srow = n_batch * lp
    ns = n_seq * t_len
    rows = -(-(srow + ns) // ROW_TILE) * ROW_TILE

    assert front + N_META == CHUNK and ns == CHUNK and seq % CHUNK == 0
    assert rows % CHUNK == 0 and rows % (16 * IN_PROJ_ROW_STEPS) == 0

    w_in_a_t = jnp.swapaxes(w_in_a, 1, 2)

    def gate_weights(l):
        w_gate = jnp.pad(w_in_a_t[l, GLA_MAIN_WIDTH:], ((0, LANES - GATE_RANK), (0, 0)))
        w_gk = jnp.pad(w_gk2_a[l], ((0, LANES - GATE_RANK), (0, 0))).astype(BF16)
        return w_gate, w_gk, b_gk2_a[l][None]

    extras = jnp.concatenate([jnp.zeros((front, D_MODEL), F32), meta_tokens.astype(F32),
                              x_sample.reshape(ns, D_MODEL), jnp.zeros((CHUNK, D_MODEL), F32)])
    tokens = ((x_prompt.reshape(n_batch * seq, D_MODEL), extras), chunks, n_batch)
    hn, g = _embed(tokens[0], norm_a[0], gate_weights(0), rows, chunks, n_batch)
    x = None
    og = None

    states_p = []
    states_s = None
    for l in range(n_a):
        proj, w_out, *fresh = _in_proj(hn, w_in_a_t, l, GLA_MAIN_WIDTH, BF16, True,
                                       "gla_in_proj", w_out_a, q_cols=GLA_KEY_DIM,
                                       q_scale=GLA_DK ** -0.5, zero_buffer=og is None)
        if fresh:
            (og,) = fresh
        onorm = onorm_a[l][None]
        og, s_p = _gla_prompt(proj, g, onorm, og, n_batch, chunks)
        og, states_s = _gla_sample(proj, g, onorm, state_gla, og, states_s, l,
                                   n_seq, t_len, srow)
        states_p.append(s_p)
        w_out = (w_out[None], 0)
        if l + 1 < n_a:
            x, hn, g = _out_proj(og, w_out, x, [norm_a[l + 1]], gate_weights(l + 1), True,
                                 BF16, "gla_out_proj", tokens=tokens)
        else:
            x, hn, kv = _out_proj(og, w_out, x, [norm_b[0]], None, True, BF16, "gla_out_proj",
                                  tokens=tokens, kv=(norm_kv, w_kv.astype(BF16)))

    cache_k = jnp.transpose(cache_k_win, (0, 2, 3, 1)).reshape(n_seq, KV_WIDTH, win)
    cache_v = jnp.transpose(cache_v_win, (0, 2, 3, 1)).reshape(n_seq, KV_WIDTH, win)
    for j in range(n_b):
        pq, w_out = _in_proj(hn, w_in_b, j, 2 * Q_WIDTH, BF16, False, "swa_in_proj",
                             w_out_b, q_cols=Q_WIDTH, q_scale=HEAD_DIM ** -0.5 * LOG2_E)
        og = _swa_prompt(pq, kv, sinks_b[j], og, n_batch, chunks, front)
        og = _swa_sample(pq, kv, cache_k, cache_v, sinks_b[j], og, n_seq, t_len, srow)
        w_out = (w_out[None], 0)
        if j + 1 < n_b:
            x, hn = _out_proj(og, w_out, x, [norm_b[j + 1]], None, True, BF16, "swa_out_proj")

    tiles_per_batch = seq // ROW_TILE
    lead_blocks = (lp - seq) // CHUNK

    def prompt_first_block(t):
        return ((t // tiles_per_batch) * chunks + lead_blocks
                + (t % tiles_per_batch) * (ROW_TILE // CHUNK))

    (y_prompt,) = _out_proj(og, w_out, x, [norm_f], None, False, F32, "final_prompt",
                            gather=(ROW_TILE, n_batch * tiles_per_batch, prompt_first_block))
    (y_sample,) = _out_proj(og, w_out, x, [norm_f], None, False, F32, "final_sample",
                            gather=(CHUNK, ns // CHUNK, lambda t: srow // CHUNK + t))
    y_prompt = y_prompt.reshape(n_batch, seq, D_MODEL)
    y_sample = y_sample.reshape(n_seq, t_len, D_MODEL)
    kv_p = jnp.stack([kv[(b + 1) * lp - win:(b + 1) * lp] for b in range(n_batch)])
    kv_p = kv_p.reshape(n_batch, win, 2, N_KV_HEADS, HEAD_DIM)
    kv_s = kv[srow:srow + ns].reshape(n_seq, t_len, 2, N_KV_HEADS, HEAD_DIM)
    k_win_s = jnp.concatenate([cache_k_win, kv_s[:, :, 0]], axis=1)[:, -win:]
    v_win_s = jnp.concatenate([cache_v_win, kv_s[:, :, 1]], axis=1)[:, -win:]
    return (y_prompt, y_sample, jnp.stack(states_p), states_s,
            kv_p[:, :, 0], kv_p[:, :, 1], k_win_s, v_win_s)
```

```python
import functools

import jax
import jax.numpy as jnp
from jax import lax
from jax.experimental import pallas as pl
from jax.experimental.pallas import tpu as pltpu

F32 = jnp.float32
BF16 = jnp.bfloat16

D_MODEL = 2048
N_META = 16
GLA_HEADS = 4
GLA_DK = 256
GLA_DV = 512
GLA_KEY_DIM = GLA_HEADS * GLA_DK
GLA_VAL_DIM = GLA_HEADS * GLA_DV
GLA_MAIN_WIDTH = 2 * GLA_KEY_DIM + 2 * GLA_VAL_DIM
GATE_RANK = 16
GATE_LOGIT_NORM = 16.0
HEAD_DIM = 64
N_Q_HEADS = 32
N_KV_HEADS = 8
Q_WIDTH = N_Q_HEADS * HEAD_DIM
KV_WIDTH = N_KV_HEADS * HEAD_DIM
WINDOW = 128
RMS_EPS = 1e-6
NEG_INF = -1e30
LOG2_E = 1.4426950408889634

LANES = 128
CHUNK = 128
GLA_LEAF = 8
LEAF_BATCH = 4
GLA_CHUNKS_PER_STEP = 3
GLA_BOUNDED_RANGE = 60.0
ROW_TILE = 512
IN_PROJ_COLS = 1024
IN_PROJ_ROW_STEPS = 8
SEQ_GROUP = 4
GLA_SEQ_GROUP = 8
GLA_SAMPLE_HEADS = 2
VMEM_LIMIT = 48 * 1024 * 1024

NT_DIMS = (((1,), (1,)), ((), ()))


def _silu(x):
    return x * (1.0 / (1.0 + jnp.exp2(x * -LOG2_E)))


def _params(sem):
    return pltpu.CompilerParams(dimension_semantics=sem, vmem_limit_bytes=VMEM_LIMIT)


def _emit_normed(y, nw_refs, gate_refs, hn_refs, g_ref):
    ms = jnp.mean(y * y, axis=-1, keepdims=True)
    yn = y * lax.rsqrt(ms + RMS_EPS)
    for nw_ref, hn_ref in zip(nw_refs, hn_refs):
        hn_ref[...] = (yn * nw_ref[...]).astype(hn_ref.dtype)
    if gate_refs:
        g_ref[...] = _decay_gate(hn_refs[0][...], *gate_refs)
    return yn


def _split_refs(refs, n_norms, with_gate):
    nw_refs = refs[:n_norms]
    gate_refs = refs[n_norms:n_norms + 3] if with_gate else None
    return nw_refs, gate_refs, refs[n_norms + 3 * with_gate:]


def _norm_specs(norms, gate, index):
    specs = [pl.BlockSpec((1, D_MODEL), index) for _ in norms]
    args = [nw[None] for nw in norms]
    if gate is not None:
        specs += [pl.BlockSpec((LANES, D_MODEL), index),
                  pl.BlockSpec((LANES, GLA_KEY_DIM), index),
                  pl.BlockSpec((1, GLA_KEY_DIM), index)]
        args += list(gate)
    return specs, args


TOKEN_SUB = ROW_TILE // CHUNK


def _token_rows(chunks, n_batch, src_refs):
    xp_refs, ex_refs = src_refs[:TOKEN_SUB], src_refs[TOKEN_SUB:]
    pieces = []
    for k in range(TOKEN_SUB):
        blk = pl.program_id(0) * TOKEN_SUB + k
        from_prompt = jnp.logical_and(blk % chunks != 0, blk < n_batch * chunks)
        pieces.append(jnp.where(from_prompt, xp_refs[k][...], ex_refs[k][...]))
    return jnp.concatenate(pieces, axis=0)


def _token_sources(x_prompt2d, extras, chunks, n_batch):
    prompt_blocks = x_prompt2d.shape[0] // CHUNK
    per_batch = prompt_blocks // n_batch

    def prompt_index(k):
        def index(i):
            blk = i * TOKEN_SUB + k
            src = (blk // chunks) * per_batch + (blk % chunks) - 1
            return (jnp.clip(src, 0, prompt_blocks - 1), 0)
        return index

    def extras_index(k):
        def index(i):
            blk = i * TOKEN_SUB + k
            return (jnp.clip(blk - n_batch * chunks + 1, 0, 2), 0)
        return index

    specs = ([pl.BlockSpec((CHUNK, D_MODEL), prompt_index(k)) for k in range(TOKEN_SUB)]
             + [pl.BlockSpec((CHUNK, D_MODEL), extras_index(k)) for k in range(TOKEN_SUB)])
    return specs, [x_prompt2d] * TOKEN_SUB + [extras] * TOKEN_SUB


def _embed_kernel(chunks, n_batch, with_gate, *refs):
    nw_refs, gate_refs, outs = _split_refs(refs[2 * TOKEN_SUB:], 1, with_gate)
    x = _token_rows(chunks, n_batch, refs[:2 * TOKEN_SUB])
    _emit_normed(x, nw_refs, gate_refs, outs[0:1], outs[1] if with_gate else None)


def _embed(tokens, norm, gate, rows, chunks, n_batch):
    src_specs, src_args = _token_sources(*tokens, chunks, n_batch)
    norm_specs, norm_args = _norm_specs([norm], gate, lambda i: (0, 0))
    out_specs = [pl.BlockSpec((ROW_TILE, D_MODEL), lambda i: (i, 0))]
    out_shape = [jax.ShapeDtypeStruct((rows, D_MODEL), BF16)]
    if gate is not None:
        out_specs.append(pl.BlockSpec((ROW_TILE, GLA_KEY_DIM), lambda i: (i, 0)))
        out_shape.append(jax.ShapeDtypeStruct((rows, GLA_KEY_DIM), F32))
    return pl.pallas_call(
        functools.partial(_embed_kernel, chunks, n_batch, gate is not None),
        grid=(rows // ROW_TILE,),
        in_specs=src_specs + norm_specs,
        out_specs=out_specs,
        out_shape=out_shape,
        compiler_params=_params(("parallel",)),
        name="embed",
    )(*src_args, *norm_args)


def _decay_gate(hn, wg_ref, wgk_ref, bgk_ref):
    glr = lax.dot_general(hn, wg_ref[...].astype(BF16), NT_DIMS, preferred_element_type=F32)
    z = jnp.dot(glr.astype(BF16), wgk_ref[...], preferred_element_type=F32) + bgk_ref[...]
    log_sig = jnp.minimum(z, 0.0) - jnp.log(1.0 + jnp.exp(-jnp.abs(z)))
    return log_sig * (LOG2_E / GATE_LOGIT_NORM)


def _in_proj_kernel(transposed, scale_ref, hn_ref, w_ref, w_out_ref, o_ref, w_out_bf16_ref,
                    w_bf16_ref):
    @pl.when(pl.program_id(1) == 0)
    def _():
        w_bf16_ref[...] = w_ref[...].astype(BF16)

    w_out_bf16_ref[...] = w_out_ref[...].astype(BF16)

    dims = NT_DIMS if transposed else (((1,), (0,)), ((), ()))
    o = lax.dot_general(hn_ref[...], w_bf16_ref[...], dims, preferred_element_type=F32)
    o_ref[...] = (o * scale_ref[pl.program_id(0)]).astype(o_ref.dtype)


def _in_proj(hn, w_stack, layer, n_cols, out_dtype, transposed, name, w_out_stack,
             q_cols=0, q_scale=1.0, slab_major=False):
    rows = hn.shape[0]
    row_tile = rows // IN_PROJ_ROW_STEPS
    n_slabs = n_cols // IN_PROJ_COLS
    assert q_cols % IN_PROJ_COLS == 0
    scales = jnp.where(jnp.arange(n_slabs) < q_cols // IN_PROJ_COLS, q_scale, 1.0).astype(F32)
    if transposed:
        slab = (IN_PROJ_COLS, D_MODEL)
        w_spec = pl.BlockSpec((None,) + slab, lambda j, i: (layer, j, 0))
    else:
        slab = (D_MODEL, IN_PROJ_COLS)
        w_spec = pl.BlockSpec((None,) + slab, lambda j, i: (layer, 0, j))
    cast_blocks = D_MODEL // CHUNK
    assert n_slabs * IN_PROJ_ROW_STEPS >= cast_blocks
    cast_block = lambda j, i: jnp.minimum(j * IN_PROJ_ROW_STEPS + i, cast_blocks - 1)
    return pl.pallas_call(
        functools.partial(_in_proj_kernel, transposed),
        grid=(n_slabs, IN_PROJ_ROW_STEPS),
        in_specs=[pl.BlockSpec(memory_space=pltpu.SMEM),
                  pl.BlockSpec((row_tile, D_MODEL), lambda j, i: (i, 0)), w_spec,
                  pl.BlockSpec((None, CHUNK, D_MODEL),
                               lambda j, i: (layer, cast_block(j, i), 0))],
        out_specs=[(pl.BlockSpec((None, row_tile, IN_PROJ_COLS), lambda j, i: (j, i, 0))
                    if slab_major else
                    pl.BlockSpec((row_tile, IN_PROJ_COLS), lambda j, i: (i, j))),
                   pl.BlockSpec((CHUNK, D_MODEL), lambda j, i: (cast_block(j, i), 0))],
        out_shape=[jax.ShapeDtypeStruct(
                       (n_slabs, rows, IN_PROJ_COLS) if slab_major else (rows, n_cols),
                       out_dtype),
                   jax.ShapeDtypeStruct((D_MODEL, D_MODEL), BF16)],
        scratch_shapes=[pltpu.VMEM(slab, BF16)],
        compiler_params=_params(("arbitrary", "arbitrary")),
        name=name,
    )(scales, hn, w_stack, w_out_stack)


def _out_proj_kernel(sub, n_norms, with_gate, keep_x, token_layout, with_kv, *refs):
    og_refs, w_ref, refs = refs[:sub], refs[sub], refs[sub + 1:]
    n_x = sub if token_layout is None else 2 * TOKEN_SUB
    x_refs, refs = refs[:n_x], refs[n_x:]
    nw_refs, gate_refs, outs = _split_refs(refs, n_norms, with_gate)
    if with_kv:
        (nkv_ref, wkv_ref), outs, kv_ref = outs[:2], outs[2:-1], outs[-1]
    og = og_refs[0][...] if sub == 1 else jnp.concatenate([r[...] for r in og_refs], axis=0)
    if token_layout is not None:
        x = _token_rows(*token_layout, x_refs)
    elif sub == 1:
        x = x_refs[0][...]
    else:
        x = jnp.concatenate([r[...] for r in x_refs], axis=0)
    y = x + jnp.dot(og, w_ref[...], preferred_element_type=F32)
    if keep_x:
        outs[0][...] = y
        outs = outs[1:]
    yn = _emit_normed(y, nw_refs, gate_refs, outs[:n_norms],
                      outs[n_norms] if with_gate else None)
    if with_kv:
        kv_ref[...] = jnp.dot((yn * nkv_ref[...]).astype(BF16), wkv_ref[...],
                              preferred_element_type=F32)


def _out_proj(og, w, x, norms, gate, keep_x, normed_dtype, name, gather=None, tokens=None,
              kv=None):
    if gather is None:
        tile_rows, n_tiles, sub = ROW_TILE, og.shape[0] // ROW_TILE, 1
        in_row_specs = [pl.BlockSpec((ROW_TILE, D_MODEL), lambda i: (i, 0))]
    else:
        tile_rows, n_tiles, first_block = gather
        sub = tile_rows // CHUNK
        in_row_specs = [pl.BlockSpec((CHUNK, D_MODEL),
                                     functools.partial(lambda k, i: (first_block(i) + k, 0), k))
                        for k in range(sub)]
    rows = tile_rows * n_tiles
    row_spec = pl.BlockSpec((tile_rows, D_MODEL), lambda i: (i, 0))
    norm_specs, norm_args = _norm_specs(norms, gate, lambda i: (0, 0))
    out_specs, out_shape = [], []
    if keep_x:
        out_specs.append(row_spec)
        out_shape.append(jax.ShapeDtypeStruct((rows, D_MODEL), F32))
    for _ in norms:
        out_specs.append(row_spec)
        out_shape.append(jax.ShapeDtypeStruct((rows, D_MODEL), normed_dtype))
    if gate is not None:
        out_specs.append(pl.BlockSpec((tile_rows, GLA_KEY_DIM), lambda i: (i, 0)))
        out_shape.append(jax.ShapeDtypeStruct((rows, GLA_KEY_DIM), F32))
    w_stack, layer = w
    w_spec = pl.BlockSpec((None, D_MODEL, D_MODEL), lambda i: (layer, 0, 0),
                          pipeline_mode=pl.Buffered(1))
    if x is None:
        sources, chunks, n_batch = tokens
        x_specs, x_args = _token_sources(*sources, chunks, n_batch)
        token_layout = (chunks, n_batch)
    else:
        x_specs, x_args, token_layout = in_row_specs, [x] * sub, None
    if kv is not None:
        norm_kv, w_kv = kv
        norm_specs += [pl.BlockSpec((1, D_MODEL), lambda i: (0, 0)),
                       pl.BlockSpec(w_kv.shape, lambda i: (0, 0), pipeline_mode=pl.Buffered(1))]
        norm_args += [norm_kv[None], w_kv]
        out_specs.append(pl.BlockSpec((tile_rows, w_kv.shape[1]), lambda i: (i, 0)))
        out_shape.append(jax.ShapeDtypeStruct((rows, w_kv.shape[1]), F32))
    return pl.pallas_call(
        functools.partial(_out_proj_kernel, sub, len(norms), gate is not None, keep_x,
                          token_layout, kv is not None),
        grid=(n_tiles,),
        in_specs=in_row_specs + [w_spec] + x_specs + norm_specs,
        out_specs=out_specs,
        out_shape=out_shape,
        compiler_params=_params(("parallel",)),
        name=name,
    )(*([og] * sub), w_stack, *x_args, *norm_args)


def _exact_block(qs, k, b, n, width, col0):
    lane = lax.broadcasted_iota(jnp.int32, (n, width), 1)
    acc = jnp.zeros((n, width), F32)
    for j in range(n):
        t = qs * k[j:j + 1] * jnp.exp2(jnp.minimum(b - b[j:j + 1], 0.0))
        acc = jnp.where(lane == col0 + j, jnp.sum(t, axis=-1, keepdims=True), acc)
    return acc


def _masked_row_sums(mask_bf16, g):
    g_hi = g.astype(BF16)
    r1 = g - g_hi.astype(F32)
    g_mid = r1.astype(BF16)
    g_lo = (r1 - g_mid.astype(F32)).astype(BF16)
    return (jnp.dot(mask_bf16, g_hi, preferred_element_type=F32)
            + jnp.dot(mask_bf16, g_mid, preferred_element_type=F32)
            + jnp.dot(mask_bf16, g_lo, preferred_element_type=F32))


def _column_of(row_vec, n):
    return jnp.broadcast_to(row_vec, (LANES, n)).T[:, 0:1]


def _head_norm_gate(o, onorm, gate):
    ms = jnp.mean(o * o, axis=-1, keepdims=True)
    return (o * lax.rsqrt(ms + RMS_EPS) * onorm * _silu(gate)).astype(BF16)


def _intra_chunk_scores(qs, k, k_bf, b, row_xor_col):
    c_len = qs.shape[0]
    leaf = GLA_LEAF
    leaf_row = lax.broadcasted_iota(jnp.int32, (leaf, c_len), 0)
    leaf_col = lax.broadcasted_iota(jnp.int32, (leaf, c_len), 1)
    lane_in_leaf = [(leaf_col & (leaf - 1)) == j for j in range(leaf)]

    att_rows = []
    for blk0 in range(0, c_len // leaf, LEAF_BATCH):
        lhs = []
        for blk in range(blk0, blk0 + LEAF_BATCH):
            q_blk = qs[blk * leaf:(blk + 1) * leaf]
            b_blk = b[blk * leaf:(blk + 1) * leaf]
            lhs += [q_blk * jnp.exp2(jnp.minimum(b_blk - b_blk[j:j + 1], 0.0))
                    for j in range(leaf)]
        r = lax.dot_general(jnp.concatenate(lhs, axis=0).astype(BF16), k_bf, NT_DIMS,
                            preferred_element_type=F32)
        for i, blk in enumerate(range(blk0, blk0 + LEAF_BATCH)):
            acc = jnp.zeros((leaf, c_len), F32)
            for j in range(leaf):
                r0 = (i * leaf + j) * leaf
                acc = jnp.where(lane_in_leaf[j], r[r0:r0 + leaf], acc)
            rel = leaf_col - blk * leaf
            att_rows.append(jnp.where((rel >= 0) & (rel <= leaf_row), acc, 0.0))
    att = jnp.concatenate(att_rows, axis=0)

    m = c_len // 2
    while m >= leaf:
        q_parts, k_parts = [], []
        zeros = jnp.zeros((m, GLA_DK), F32)
        for base in range(0, c_len, 2 * m):
            ref = b[base + m - 1:base + m]
            lo = slice(base, base + m)
            hi = slice(base + m, base + 2 * m)
            k_parts += [k[lo] * jnp.exp2(ref - b[lo]), zeros]
            q_parts += [zeros, qs[hi] * jnp.exp2(b[hi] - ref)]
        qt = jnp.concatenate(q_parts, axis=0).astype(BF16)
        kt = jnp.concatenate(k_parts, axis=0).astype(BF16)
        a = lax.dot_general(qt, kt, NT_DIMS, preferred_element_type=F32)
        att = att + jnp.where(row_xor_col < 2 * m, a, 0.0)
        m //= 2
    return att


def _gla_prompt_kernel(q_ref, k_ref, v_ref, gate_ref, g_ref, onorm_ref, og_in_ref,
                       og_ref, s_ref, b_scr, qd_scr, att_scr):
    del og_in_ref
    c_len = CHUNK
    heads = [(slice(h * GLA_DK, (h + 1) * GLA_DK), slice(h * GLA_DV, (h + 1) * GLA_DV))
             for h in range(GLA_HEADS)]

    @pl.when(pl.program_id(1) == 0)
    def _():
        s_ref[...] = jnp.zeros_like(s_ref)

    row = lax.broadcasted_iota(jnp.int32, (c_len, c_len), 0)
    col = lax.broadcasted_iota(jnp.int32, (c_len, c_len), 1)
    lower = row >= col
    tri = jnp.where(lower, 1.0, 0.0).astype(BF16)
    onorm = onorm_ref[...]
    chunk_rows = [slice(c * c_len, (c + 1) * c_len) for c in range(q_ref.shape[0] // c_len)]

    total = None
    for rows in chunk_rows:
        b_c = _masked_row_sums(tri, g_ref[rows, :])
        b_scr[rows, :] = b_c
        qd_scr[rows, :] = (q_ref[rows, :].astype(F32) * jnp.exp2(b_c)).astype(BF16)
        last = b_c[c_len - 1:c_len]
        total = last if total is None else jnp.minimum(total, last)

    bounded = jnp.max(-total) <= GLA_BOUNDED_RANGE

    @pl.when(bounded)
    def _():
        for c, rows in enumerate(chunk_rows):
            for h, (ks, _) in enumerate(heads):
                k_inv = (k_ref[rows, ks].astype(F32) * jnp.exp2(-b_scr[rows, ks])).astype(BF16)
                a = lax.dot_general(qd_scr[rows, ks], k_inv, NT_DIMS,
                                    preferred_element_type=F32)
                att_scr[c * GLA_HEADS + h] = jnp.where(lower, a, 0.0).astype(BF16)

    @pl.when(jnp.logical_not(bounded))
    def _():
        for c, rows in enumerate(chunk_rows):
            for h, (ks, _) in enumerate(heads):
                k_bf = k_ref[rows, ks]
                att_scr[c * GLA_HEADS + h] = _intra_chunk_scores(
                    q_ref[rows, ks].astype(F32), k_bf.astype(F32), k_bf,
                    b_scr[rows, ks], row ^ col).astype(BF16)

    for c, rows in enumerate(chunk_rows):
        for h, (ks, vs) in enumerate(heads):
            k = k_ref[rows, ks].astype(F32)
            v_slab, v_cols = vs.start // IN_PROJ_COLS, slice(vs.start % IN_PROJ_COLS,
                                                            vs.start % IN_PROJ_COLS + GLA_DV)
            v = v_ref[v_slab, rows, v_cols]
            b = b_scr[rows, ks]
            b_last = b[c_len - 1:c_len]
            state = s_ref[0, h]

            o = jnp.dot(qd_scr[rows, ks], state.astype(BF16), preferred_element_type=F32)
            o = o + jnp.dot(att_scr[c * GLA_HEADS + h], v, preferred_element_type=F32)

            kd_t = (k * jnp.exp2(b_last - b)).T.astype(BF16)
            s_ref[0, h] = (state * _column_of(jnp.exp2(b_last), GLA_DK)
                           + jnp.dot(kd_t, v, preferred_element_type=F32))

            og_ref[rows, vs] = _head_norm_gate(
                o, onorm, gate_ref[v_slab, rows, v_cols].astype(F32))


def _gla_prompt(proj, g, onorm, og, n_batch, chunks):
    rows = proj.shape[1]
    assert GLA_KEY_DIM == IN_PROJ_COLS and GLA_VAL_DIM == 2 * IN_PROJ_COLS
    per_step = next(n for n in (GLA_CHUNKS_PER_STEP, 1) if chunks % n == 0)
    steps = chunks // per_step
    tile = per_step * CHUNK
    blk = lambda col: (lambda b, c: (b * steps + c, col))
    slabs = lambda first: (lambda b, c: (first, b * steps + c, 0))
    return pl.pallas_call(
        _gla_prompt_kernel,
        grid=(n_batch, steps),
        in_specs=[
            pl.BlockSpec((None, tile, IN_PROJ_COLS), slabs(0)),
            pl.BlockSpec((None, tile, IN_PROJ_COLS), slabs(1)),
            pl.BlockSpec((2, tile, IN_PROJ_COLS), slabs(1)),
            pl.BlockSpec((2, tile, IN_PROJ_COLS), slabs(2)),
            pl.BlockSpec((tile, GLA_KEY_DIM), blk(0)),
            pl.BlockSpec((1, GLA_DV), lambda b, c: (0, 0)),
            pl.BlockSpec(memory_space=pl.ANY),
        ],
        out_specs=[
            pl.BlockSpec((tile, GLA_VAL_DIM), blk(0)),
            pl.BlockSpec((1, GLA_HEADS, GLA_DK, GLA_DV), lambda b, c: (b, 0, 0, 0)),
        ],
        out_shape=[
            jax.ShapeDtypeStruct((rows, GLA_VAL_DIM), BF16),
            jax.ShapeDtypeStruct((n_batch, GLA_HEADS, GLA_DK, GLA_DV), F32),
        ],
        scratch_shapes=[pltpu.VMEM((tile, GLA_KEY_DIM), F32),
                        pltpu.VMEM((tile, GLA_KEY_DIM), BF16),
                        pltpu.VMEM((per_step * GLA_HEADS, CHUNK, CHUNK), BF16)],
        input_output_aliases={6: 0},
        compiler_params=_params(("parallel", "arbitrary")),
        name="gla_prompt",
    )(proj, proj, proj, proj, g, onorm, og)


def _gla_sample_kernel(t_len, q_ref, k_ref, v_ref, gate_ref, g_ref, onorm_ref, s0_ref,
                       og_in_ref, *rest):
    og_ref, s_ref = rest[-2], rest[-1]
    n = GLA_SEQ_GROUP * t_len
    row = lax.broadcasted_iota(jnp.int32, (n, LANES), 0)
    col = lax.broadcasted_iota(jnp.int32, (n, LANES), 1)
    causal = ((row // t_len) == (col // t_len)) & (row >= col)
    causal_bf16 = jnp.where(causal, 1.0, 0.0).astype(BF16)
    row_seq = lax.broadcasted_iota(jnp.int32, (n, 1), 0) // t_len
    pad_k = jnp.zeros((LANES - n, GLA_DK), F32)

    for h in range(GLA_SAMPLE_HEADS):
        ks = slice(h * GLA_DK, (h + 1) * GLA_DK)
        vs = slice(h * GLA_DV, (h + 1) * GLA_DV)
        qs = q_ref[:, ks].astype(F32)
        k = k_ref[:, ks].astype(F32)
        v_pad = jnp.concatenate([v_ref[:, vs], jnp.zeros((LANES - n, GLA_DV), BF16)], axis=0)
        b = _masked_row_sums(causal_bf16, jnp.concatenate([g_ref[:, ks], pad_k], axis=0))
        att = jnp.where(causal, _exact_block(qs, k, b, n, LANES, 0), 0.0)
        o = jnp.dot(att.astype(BF16), v_pad, preferred_element_type=F32)

        q_dec = (qs * jnp.exp2(b)).astype(BF16)
        for u in range(GLA_SEQ_GROUP):
            state = s0_ref[0, u, h]
            o_u = jnp.dot(q_dec, state.astype(BF16), preferred_element_type=F32)
            o = o + jnp.where(row_seq == u, o_u, 0.0)
            b_last = b[(u + 1) * t_len - 1:(u + 1) * t_len]
            kd = jnp.where(row_seq == u, k * jnp.exp2(jnp.minimum(b_last - b, 0.0)), 0.0)
            kd_t = jnp.concatenate([kd, pad_k], axis=0).T.astype(BF16)
            s_ref[0, u, h] = (state * _column_of(jnp.exp2(b_last), GLA_DK)
                              + jnp.dot(kd_t, v_pad, preferred_element_type=F32))

        og_ref[:, vs] = _head_norm_gate(o, onorm_ref[...], gate_ref[:, vs].astype(F32))


def _gla_sample(proj, g, onorm, state_in, og, s_out, layer, n_seq, t_len, row0):
    n = GLA_SEQ_GROUP * t_len
    rb0 = row0 // n
    n_layers = state_in.shape[0]
    hs = GLA_SAMPLE_HEADS
    kw, vw = hs * GLA_DK, hs * GLA_DV
    assert vw == IN_PROJ_COLS and GLA_KEY_DIM == IN_PROJ_COLS
    v_slab0, gate_slab0 = 2, 2 + GLA_VAL_DIM // IN_PROJ_COLS
    st_spec = pl.BlockSpec((1, GLA_SEQ_GROUP, hs, GLA_DK, GLA_DV),
                           lambda s, h: (layer, s, h, 0, 0))
    in_specs = [
        pl.BlockSpec((None, n, kw), lambda s, h: (0, rb0 + s, h)),
        pl.BlockSpec((None, n, kw), lambda s, h: (1, rb0 + s, h)),
        pl.BlockSpec((None, n, vw), lambda s, h: (v_slab0 + h, rb0 + s, 0)),
        pl.BlockSpec((None, n, vw), lambda s, h: (gate_slab0 + h, rb0 + s, 0)),
        pl.BlockSpec((n, kw), lambda s, h: (rb0 + s, h)),
        pl.BlockSpec((1, GLA_DV), lambda s, h: (0, 0)),
        st_spec,
        pl.BlockSpec(memory_space=pl.ANY),
    ]
    args = [proj, proj, proj, proj, g, onorm, state_in, og]
    aliases = {7: 0}
    if s_out is not None:
        in_specs.append(pl.BlockSpec(memory_space=pl.ANY))
        args.append(s_out)
        aliases[8] = 1
    return pl.pallas_call(
        functools.partial(_gla_sample_kernel, t_len),
        grid=(n_seq // GLA_SEQ_GROUP, GLA_HEADS // hs),
        in_specs=in_specs,
        out_specs=[
            pl.BlockSpec((n, vw), lambda s, h: (rb0 + s, h)),
            st_spec,
        ],
        out_shape=[
            jax.ShapeDtypeStruct(og.shape, BF16),
            jax.ShapeDtypeStruct((n_layers, n_seq, GLA_HEADS, GLA_DK, GLA_DV), F32),
        ],
        input_output_aliases=aliases,
        compiler_params=_params(("parallel", "parallel")),
        name="gla_sample",
    )(*args)


def _nt_dot(a, b):
    return lax.dot_general(a, b, NT_DIMS, preferred_element_type=F32)


def _lane_padded_copies(pair, fill):
    low = lax.broadcasted_iota(jnp.int32, pair.shape, 1) < HEAD_DIM
    swapped = pltpu.roll(pair, HEAD_DIM, axis=1).astype(BF16)
    pair = pair.astype(BF16)
    fill = jnp.full((), fill, BF16)
    return ((jnp.where(low, pair, fill), jnp.where(low, fill, swapped)),
            (jnp.where(low, swapped, fill), jnp.where(low, fill, pair)))


def _natural_kv_set(keys, vals, mask):
    slot0 = lax.broadcasted_iota(jnp.int32, keys[0].shape, 0) == 0

    def score_fns(cpair):
        return tuple(tuple(functools.partial(_nt_dot, b=k) for k in pads)
                     for pads in _lane_padded_copies(keys[cpair], 0.0))

    def value_fns(cpair):
        pads_by_head = _lane_padded_copies(jnp.where(slot0, 0.0, vals[cpair]), 1.0)
        return tuple(tuple((lambda p, v=v: jnp.dot(p, v, preferred_element_type=F32))
                           for v in pads) for pads in pads_by_head)

    return score_fns, value_fns, mask


def _window_kv_set(kt_ref, vt_ref, k_new_t, v_new_t, mask):
    win = kt_ref.shape[1]
    lane0 = lax.broadcasted_iota(jnp.int32, (HEAD_DIM, win), 1) == 0

    def head_keys(old_ref, new_t, cpair, parity, zero_slot0):
        r0 = (2 * cpair + parity) * HEAD_DIM
        old = old_ref[r0:r0 + HEAD_DIM, :]
        if zero_slot0:
            old = jnp.where(lane0, 0.0, old)
        new = new_t[cpair][parity * HEAD_DIM:(parity + 1) * HEAD_DIM, :]
        return jnp.concatenate([old, new], axis=1).astype(BF16)

    def stacks(head, other):
        return (jnp.concatenate([head, other], axis=0), jnp.concatenate([other, head], axis=0))

    def score_fns(cpair):
        fns = []
        for parity in range(2):
            head = head_keys(kt_ref, k_new_t, cpair, parity, False)
            fns.append(tuple(
                (lambda q, k=k: jnp.dot(q, k, preferred_element_type=F32))
                for k in stacks(head, jnp.zeros_like(head))))
        return tuple(fns)

    def value_fns(cpair):
        fns = []
        for parity in range(2):
            head = head_keys(vt_ref, v_new_t, cpair, parity, True)
            fns.append(tuple(functools.partial(_nt_dot, b=v)
                             for v in stacks(head, jnp.ones_like(head))))
        return tuple(fns)

    return score_fns, value_fns, mask


def _attend_rows(q_ref, gate_ref, og_ref, sink_ref, kv_sets, s_scr, p_scr):
    n_rows = q_ref.shape[0]
    group = N_Q_HEADS // N_KV_HEADS
    low_out = lax.broadcasted_iota(jnp.int32, (n_rows, LANES), 1) < HEAD_DIM
    lane0_row = lax.broadcasted_iota(jnp.int32, (1, LANES), 1) == 0

    def kv_heads():
        for cpair in range(KV_WIDTH // LANES):
            for parity in range(2):
                first = (2 * cpair + parity) * (group // 2)
                yield cpair, parity, list(range(first, first + group // 2))

    def slot(set_index, qc, half):
        return (set_index * (Q_WIDTH // LANES) + qc) * 2 + half

    for si, (score_fns, _, _) in enumerate(kv_sets):
        fns = None
        for cpair, parity, qcs in kv_heads():
            if parity == 0:
                fns = score_fns(cpair)
            q_stack = jnp.concatenate(
                [q_ref[:, qc * LANES:(qc + 1) * LANES] for qc in qcs], axis=0)
            for half, score in enumerate(fns[parity]):
                s = score(q_stack)
                for j, qc in enumerate(qcs):
                    s_scr[slot(si, qc, half)] = s[j * n_rows:(j + 1) * n_rows]

    for si, (_, _, mask) in enumerate(kv_sets):
        mask_first, mask_rest = mask[:, :LANES], mask[:, LANES:]
        for qc in range(Q_WIDTH // LANES):
            for half in range(2):
                sink = sink_ref[2 * qc + half] * LOG2_E
                s = s_scr[slot(si, qc, half)]
                first = jnp.where(mask_first, s[:, :LANES],
                                  jnp.where(lane0_row, sink, NEG_INF))
                rest = jnp.where(mask_rest, s[:, LANES:], NEG_INF)
                m = jnp.max(jnp.maximum(first, rest), axis=-1, keepdims=True)
                p_scr[slot(si, qc, half)] = jnp.exp2(
                    jnp.concatenate([first, rest], axis=1) - m).astype(BF16)

    def store(qc, o):
        cols = slice(qc * LANES, (qc + 1) * LANES)
        og_ref[:, cols] = (o * _silu(gate_ref[:, cols].astype(F32))).astype(BF16)

    out_cols = [None] * (Q_WIDTH // LANES)
    for si, (_, value_fns, _) in enumerate(kv_sets):
        fns = None
        for cpair, parity, qcs in kv_heads():
            if parity == 0:
                fns = value_fns(cpair)
            stacked = [weigh(jnp.concatenate([p_scr[slot(si, qc, half)] for qc in qcs],
                                             axis=0))
                       for half, weigh in enumerate(fns[parity])]
            for j, qc in enumerate(qcs):
                oa, ob = (o[j * n_rows:(j + 1) * n_rows] for o in stacked)
                sums = pltpu.roll(jnp.where(low_out, ob, oa), HEAD_DIM, axis=1)
                o2 = jnp.where(low_out, oa, ob) * (1.0 / sums)
                if len(kv_sets) == 1:
                    store(qc, o2)
                else:
                    out_cols[qc] = o2 if out_cols[qc] is None else out_cols[qc] + o2

    if len(kv_sets) > 1:
        for qc, o in enumerate(out_cols):
            store(qc, o)


def _attention_scratch(n_sets, n_rows, n_keys):
    n = n_sets * N_Q_HEADS
    return [pltpu.VMEM((n, n_rows, n_keys), F32), pltpu.VMEM((n, n_rows, n_keys), BF16)]


def _swa_prompt_kernel(front, sink_ref, q_ref, gate_ref, kc_ref, kp_ref, vc_ref, vp_ref,
                       og_in_ref, og_ref, s_scr, p_scr):
    del og_in_ref
    blk = pl.program_id(1)
    row = lax.broadcasted_iota(jnp.int32, (CHUNK, 2 * CHUNK), 0)
    col = lax.broadcasted_iota(jnp.int32, (CHUNK, 2 * CHUNK), 1)
    mask = (col > row) & (col <= row + WINDOW) & ((blk - 1) * CHUNK + col >= front)
    lane_pairs = [slice(c * LANES, (c + 1) * LANES) for c in range(KV_WIDTH // LANES)]
    keys = [jnp.concatenate([kp_ref[:, c], kc_ref[:, c]], axis=0) for c in lane_pairs]
    vals = [jnp.concatenate([vp_ref[:, c], vc_ref[:, c]], axis=0) for c in lane_pairs]
    _attend_rows(q_ref, gate_ref, og_ref, sink_ref, [_natural_kv_set(keys, vals, mask)],
                 s_scr, p_scr)


def _swa_prompt(pq, kv, sinks, og, n_batch, chunks, front):
    cur = lambda col: (lambda b, i: (b * chunks + i, col))
    prev = lambda col: (lambda b, i: (b * chunks + jnp.maximum(i - 1, 0), col))
    return pl.pallas_call(
        functools.partial(_swa_prompt_kernel, front),
        grid=(n_batch, chunks),
        in_specs=[
            pl.BlockSpec(memory_space=pltpu.SMEM),
            pl.BlockSpec((CHUNK, Q_WIDTH), cur(0)),
            pl.BlockSpec((CHUNK, Q_WIDTH), cur(1)),
            pl.BlockSpec((CHUNK, KV_WIDTH), cur(0)),
            pl.BlockSpec((CHUNK, KV_WIDTH), prev(0)),
            pl.BlockSpec((CHUNK, KV_WIDTH), cur(1)),
            pl.BlockSpec((CHUNK, KV_WIDTH), prev(1)),
            pl.BlockSpec(memory_space=pl.ANY),
        ],
        out_specs=pl.BlockSpec((CHUNK, Q_WIDTH), cur(0)),
        out_shape=jax.ShapeDtypeStruct(og.shape, BF16),
        scratch_shapes=_attention_scratch(1, CHUNK, 2 * CHUNK),
        input_output_aliases={7: 0},
        compiler_params=_params(("parallel", "parallel")),
        name="swa_prompt",
    )(sinks, pq, pq, kv, kv, kv, kv, og)


def _swa_sample_kernel(t_len, sink_ref, q_ref, gate_ref, kn_ref, vn_ref, ckt_ref, cvt_ref,
                       og_in_ref, og_ref, s_scr, p_scr):
    del og_in_ref
    n = SEQ_GROUP * t_len
    win = ckt_ref.shape[2]
    row = lax.broadcasted_iota(jnp.int32, (n, 2 * win), 0)
    col = lax.broadcasted_iota(jnp.int32, (n, 2 * win), 1)
    t = row % t_len
    new = col - win
    in_window = ((col < win) & (col > t)) | (
        (new >= 0) & (new < n) & ((new // t_len) == (row // t_len)) & ((new % t_len) <= t))
    pad = jnp.zeros((win - n, LANES), F32)
    lane_pairs = [slice(c * LANES, (c + 1) * LANES) for c in range(KV_WIDTH // LANES)]
    k_new_t = [jnp.concatenate([kn_ref[:, c], pad], axis=0).T for c in lane_pairs]
    v_new_t = [jnp.concatenate([vn_ref[:, c], pad], axis=0).T for c in lane_pairs]
    kv_sets = [_window_kv_set(ckt_ref.at[u], cvt_ref.at[u], k_new_t, v_new_t,
                              in_window & ((row // t_len) == u))
               for u in range(SEQ_GROUP)]
    _attend_rows(q_ref, gate_ref, og_ref, sink_ref, kv_sets, s_scr, p_scr)


def _swa_sample(pq, kv, cache_kt, cache_vt, sinks, og, n_seq, t_len, row0):
    n = SEQ_GROUP * t_len
    rb0 = row0 // n
    win = cache_kt.shape[2]
    return pl.pallas_call(
        functools.partial(_swa_sample_kernel, t_len),
        grid=(n_seq // SEQ_GROUP,),
        in_specs=[
            pl.BlockSpec(memory_space=pltpu.SMEM),
            pl.BlockSpec((n, Q_WIDTH), lambda s: (rb0 + s, 0)),
            pl.BlockSpec((n, Q_WIDTH), lambda s: (rb0 + s, 1)),
            pl.BlockSpec((n, KV_WIDTH), lambda s: (rb0 + s, 0)),
            pl.BlockSpec((n, KV_WIDTH), lambda s: (rb0 + s, 1)),
            pl.BlockSpec((SEQ_GROUP, KV_WIDTH, win), lambda s: (s, 0, 0)),
            pl.BlockSpec((SEQ_GROUP, KV_WIDTH, win), lambda s: (s, 0, 0)),
            pl.BlockSpec(memory_space=pl.ANY),
        ],
        out_specs=pl.BlockSpec((n, Q_WIDTH), lambda s: (rb0 + s, 0)),
        out_shape=jax.ShapeDtypeStruct(og.shape, BF16),
        scratch_shapes=_attention_scratch(SEQ_GROUP, n, 2 * win),
        input_output_aliases={7: 0},
        compiler_params=_params(("parallel",)),
        name="swa_sample",
    )(sinks, pq, pq, kv, kv, cache_kt, cache_vt, og)


def kernel(x_prompt, x_sample, state_gla, cache_k_win, cache_v_win, meta_tokens, norm_a, w_in_a, w_gk2_a, b_gk2_a, onorm_a, w_out_a, norm_kv, w_kv, norm_b, w_in_b, sinks_b, w_out_b, norm_f):
    n_batch, seq, _ = x_prompt.shape
    n_seq, t_len, _ = x_sample.shape
    win = cache_k_win.shape[1]
    n_a = w_in_a.shape[0]
    n_b = w_in_b.shape[0]
    assert win == WINDOW == CHUNK and n_seq % SEQ_GROUP == 0 and n_seq % GLA_SEQ_GROUP == 0

    length = seq + N_META
    front = (-length) % CHUNK
    lp = length + front
    chunks = lp // CHUNK
    srow = n_batch * lp
    ns = n_seq * t_len
    rows = -(-(srow + ns) // ROW_TILE) * ROW_TILE

    assert front + N_META == CHUNK and ns == CHUNK and seq % CHUNK == 0
    assert rows % CHUNK == 0 and rows % (16 * IN_PROJ_ROW_STEPS) == 0

    w_in_a_t = jnp.swapaxes(w_in_a, 1, 2)

    def gate_weights(l):
        w_gate = jnp.pad(w_in_a_t[l, GLA_MAIN_WIDTH:], ((0, LANES - GATE_RANK), (0, 0)))
        w_gk = jnp.pad(w_gk2_a[l], ((0, LANES - GATE_RANK), (0, 0))).astype(BF16)
        return w_gate, w_gk, b_gk2_a[l][None]

    extras = jnp.concatenate([jnp.zeros((front, D_MODEL), F32), meta_tokens.astype(F32),
                              x_sample.reshape(ns, D_MODEL), jnp.zeros((CHUNK, D_MODEL), F32)])
    tokens = ((x_prompt.reshape(n_batch * seq, D_MODEL), extras), chunks, n_batch)
    hn, g = _embed(tokens[0], norm_a[0], gate_weights(0), rows, chunks, n_batch)
    x = None
    og = jnp.zeros((rows, D_MODEL), BF16)

    states_p = []
    states_s = None
    for l in range(n_a):
        proj, w_out = _in_proj(hn, w_in_a_t, l, GLA_MAIN_WIDTH, BF16, True, "gla_in_proj",
                               w_out_a, q_cols=GLA_KEY_DIM, q_scale=GLA_DK ** -0.5,
                               slab_major=True)
        onorm = onorm_a[l][None]
        og, s_p = _gla_prompt(proj, g, onorm, og, n_batch, chunks)
        og, states_s = _gla_sample(proj, g, onorm, state_gla, og, states_s, l,
                                   n_seq, t_len, srow)
        states_p.append(s_p)
        w_out = (w_out[None], 0)
        if l + 1 < n_a:
            x, hn, g = _out_proj(og, w_out, x, [norm_a[l + 1]], gate_weights(l + 1), True,
                                 BF16, "gla_out_proj", tokens=tokens)
        else:
            x, hn, kv = _out_proj(og, w_out, x, [norm_b[0]], None, True, BF16, "gla_out_proj",
                                  tokens=tokens, kv=(norm_kv, w_kv.astype(BF16)))

    cache_k = jnp.transpose(cache_k_win, (0, 2, 3, 1)).reshape(n_seq, KV_WIDTH, win)
    cache_v = jnp.transpose(cache_v_win, (0, 2, 3, 1)).reshape(n_seq, KV_WIDTH, win)
    for j in range(n_b):
        pq, w_out = _in_proj(hn, w_in_b, j, 2 * Q_WIDTH, BF16, False, "swa_in_proj",
                             w_out_b, q_cols=Q_WIDTH, q_scale=HEAD_DIM ** -0.5 * LOG2_E)
        og = _swa_prompt(pq, kv, sinks_b[j], og, n_batch, chunks, front)
        og = _swa_sample(pq, kv, cache_k, cache_v, sinks_b[j], og, n_seq, t_len, srow)
        w_out = (w_out[None], 0)
        if j + 1 < n_b:
            x, hn = _out_proj(og, w_out, x, [norm_b[j + 1]], None, True, BF16, "swa_out_proj")

    tiles_per_batch = seq // ROW_TILE
    lead_blocks = (lp - seq) // CHUNK

    def prompt_first_block(t):
        return ((t // tiles_per_batch) * chunks + lead_blocks
                + (t % tiles_per_batch) * (ROW_TILE // CHUNK))

    (y_prompt,) = _out_proj(og, w_out, x, [norm_f], None, False, F32, "final_prompt",
                            gather=(ROW_TILE, n_batch * tiles_per_batch, prompt_first_block))
    (y_sample,) = _out_proj(og, w_out, x, [norm_f], None, False, F32, "final_sample",
                            gather=(CHUNK, ns // CHUNK, lambda t: srow // CHUNK + t))
    y_prompt = y_prompt.reshape(n_batch, seq, D_MODEL)
    y_sample = y_sample.reshape(n_seq, t_len, D_MODEL)
    kv_p = jnp.stack([kv[(b + 1) * lp - win:(b + 1) * lp] for b in range(n_batch)])
    kv_p = kv_p.reshape(n_batch, win, 2, N_KV_HEADS, HEAD_DIM)
    kv_s = kv[srow:srow + ns].reshape(n_seq, t_len, 2, N_KV_HEADS, HEAD_DIM)
    k_win_s = jnp.concatenate([cache_k_win, kv_s[:, :, 0]], axis=1)[:, -win:]
    v_win_s = jnp.concatenate([cache_v_win, kv_s[:, :, 1]], axis=1)[:, -win:]
    return (y_prompt, y_sample, jnp.stack(states_p), states_s,
            kv_p[:, :, 0], kv_p[:, :, 1], k_win_s, v_win_s)
```

```python
import functools

import jax
import jax.numpy as jnp
from jax import lax
from jax.experimental import pallas as pl
from jax.experimental.pallas import tpu as pltpu

F32 = jnp.float32
BF16 = jnp.bfloat16

D_MODEL = 2048
N_META = 16
GLA_HEADS = 4
GLA_DK = 256
GLA_DV = 512
GLA_KEY_DIM = GLA_HEADS * GLA_DK
GLA_VAL_DIM = GLA_HEADS * GLA_DV
GLA_MAIN_WIDTH = 2 * GLA_KEY_DIM + 2 * GLA_VAL_DIM
GATE_RANK = 16
GATE_LOGIT_NORM = 16.0
HEAD_DIM = 64
N_Q_HEADS = 32
N_KV_HEADS = 8
Q_WIDTH = N_Q_HEADS * HEAD_DIM
KV_WIDTH = N_KV_HEADS * HEAD_DIM
WINDOW = 128
RMS_EPS = 1e-6
NEG_INF = -1e30
LOG2_E = 1.4426950408889634

LANES = 128
CHUNK = 128
GLA_LEAF = 8
LEAF_BATCH = 4
GLA_CHUNKS_PER_STEP = 3
GLA_BOUNDED_RANGE = 60.0
ROW_TILE = 512
IN_PROJ_COLS = 1024
IN_PROJ_ROW_STEPS = 8
SEQ_GROUP = 4
GLA_SEQ_GROUP = 8
GLA_SAMPLE_HEADS = 2
VMEM_LIMIT = 48 * 1024 * 1024

NT_DIMS = (((1,), (1,)), ((), ()))


def _silu(x):
    return x * (1.0 / (1.0 + jnp.exp2(x * -LOG2_E)))


def _params(sem):
    return pltpu.CompilerParams(dimension_semantics=sem, vmem_limit_bytes=VMEM_LIMIT)


def _emit_normed(y, nw_refs, gate_refs, hn_refs, g_ref):
    ms = jnp.mean(y * y, axis=-1, keepdims=True)
    yn = y * lax.rsqrt(ms + RMS_EPS)
    for nw_ref, hn_ref in zip(nw_refs, hn_refs):
        hn_ref[...] = (yn * nw_ref[...]).astype(hn_ref.dtype)
    if gate_refs:
        g_ref[...] = _decay_gate(hn_refs[0][...], *gate_refs)
    return yn


def _split_refs(refs, n_norms, with_gate):
    nw_refs = refs[:n_norms]
    gate_refs = refs[n_norms:n_norms + 3] if with_gate else None
    return nw_refs, gate_refs, refs[n_norms + 3 * with_gate:]


def _norm_specs(norms, gate, index):
    specs = [pl.BlockSpec((1, D_MODEL), index) for _ in norms]
    args = [nw[None] for nw in norms]
    if gate is not None:
        specs += [pl.BlockSpec((LANES, D_MODEL), index),
                  pl.BlockSpec((LANES, GLA_KEY_DIM), index),
                  pl.BlockSpec((1, GLA_KEY_DIM), index)]
        args += list(gate)
    return specs, args


TOKEN_SUB = ROW_TILE // CHUNK


def _token_rows(chunks, n_batch, src_refs):
    xp_refs, ex_refs = src_refs[:TOKEN_SUB], src_refs[TOKEN_SUB:]
    pieces = []
    for k in range(TOKEN_SUB):
        blk = pl.program_id(0) * TOKEN_SUB + k
        from_prompt = jnp.logical_and(blk % chunks != 0, blk < n_batch * chunks)
        pieces.append(jnp.where(from_prompt, xp_refs[k][...], ex_refs[k][...]))
    return jnp.concatenate(pieces, axis=0)


def _token_sources(x_prompt2d, extras, chunks, n_batch):
    prompt_blocks = x_prompt2d.shape[0] // CHUNK
    per_batch = prompt_blocks // n_batch

    def prompt_index(k):
        def index(i):
            blk = i * TOKEN_SUB + k
            src = (blk // chunks) * per_batch + (blk % chunks) - 1
            return (jnp.clip(src, 0, prompt_blocks - 1), 0)
        return index

    def extras_index(k):
        def index(i):
            blk = i * TOKEN_SUB + k
            return (jnp.clip(blk - n_batch * chunks + 1, 0, 2), 0)
        return index

    specs = ([pl.BlockSpec((CHUNK, D_MODEL), prompt_index(k)) for k in range(TOKEN_SUB)]
             + [pl.BlockSpec((CHUNK, D_MODEL), extras_index(k)) for k in range(TOKEN_SUB)])
    return specs, [x_prompt2d] * TOKEN_SUB + [extras] * TOKEN_SUB


def _embed_kernel(chunks, n_batch, with_gate, *refs):
    nw_refs, gate_refs, outs = _split_refs(refs[2 * TOKEN_SUB:], 1, with_gate)
    x = _token_rows(chunks, n_batch, refs[:2 * TOKEN_SUB])
    _emit_normed(x, nw_refs, gate_refs, outs[0:1], outs[1] if with_gate else None)


def _embed(tokens, norm, gate, rows, chunks, n_batch):
    src_specs, src_args = _token_sources(*tokens, chunks, n_batch)
    norm_specs, norm_args = _norm_specs([norm], gate, lambda i: (0, 0))
    out_specs = [pl.BlockSpec((ROW_TILE, D_MODEL), lambda i: (i, 0))]
    out_shape = [jax.ShapeDtypeStruct((rows, D_MODEL), BF16)]
    if gate is not None:
        out_specs.append(pl.BlockSpec((ROW_TILE, GLA_KEY_DIM), lambda i: (i, 0)))
        out_shape.append(jax.ShapeDtypeStruct((rows, GLA_KEY_DIM), F32))
    return pl.pallas_call(
        functools.partial(_embed_kernel, chunks, n_batch, gate is not None),
        grid=(rows // ROW_TILE,),
        in_specs=src_specs + norm_specs,
        out_specs=out_specs,
        out_shape=out_shape,
        compiler_params=_params(("parallel",)),
        name="embed",
    )(*src_args, *norm_args)


def _decay_gate(hn, wg_ref, wgk_ref, bgk_ref):
    glr = lax.dot_general(hn, wg_ref[...].astype(BF16), NT_DIMS, preferred_element_type=F32)
    z = jnp.dot(glr.astype(BF16), wgk_ref[...], preferred_element_type=F32) + bgk_ref[...]
    log_sig = jnp.minimum(z, 0.0) - jnp.log(1.0 + jnp.exp(-jnp.abs(z)))
    return log_sig * (LOG2_E / GATE_LOGIT_NORM)


def _in_proj_kernel(transposed, scale_ref, hn_ref, w_ref, w_out_ref, o_ref, w_out_bf16_ref,
                    w_bf16_ref):
    @pl.when(pl.program_id(1) == 0)
    def _():
        w_bf16_ref[...] = w_ref[...].astype(BF16)

    w_out_bf16_ref[...] = w_out_ref[...].astype(BF16)

    dims = NT_DIMS if transposed else (((1,), (0,)), ((), ()))
    o = lax.dot_general(hn_ref[...], w_bf16_ref[...], dims, preferred_element_type=F32)
    o_ref[...] = (o * scale_ref[pl.program_id(0)]).astype(o_ref.dtype)


def _in_proj(hn, w_stack, layer, n_cols, out_dtype, transposed, name, w_out_stack,
             q_cols=0, q_scale=1.0):
    rows = hn.shape[0]
    row_tile = rows // IN_PROJ_ROW_STEPS
    n_slabs = n_cols // IN_PROJ_COLS
    assert q_cols % IN_PROJ_COLS == 0
    scales = jnp.where(jnp.arange(n_slabs) < q_cols // IN_PROJ_COLS, q_scale, 1.0).astype(F32)
    if transposed:
        slab = (IN_PROJ_COLS, D_MODEL)
        w_spec = pl.BlockSpec((None,) + slab, lambda j, i: (layer, j, 0))
    else:
        slab = (D_MODEL, IN_PROJ_COLS)
        w_spec = pl.BlockSpec((None,) + slab, lambda j, i: (layer, 0, j))
    cast_blocks = D_MODEL // CHUNK
    assert n_slabs * IN_PROJ_ROW_STEPS >= cast_blocks
    cast_block = lambda j, i: jnp.minimum(j * IN_PROJ_ROW_STEPS + i, cast_blocks - 1)
    return pl.pallas_call(
        functools.partial(_in_proj_kernel, transposed),
        grid=(n_slabs, IN_PROJ_ROW_STEPS),
        in_specs=[pl.BlockSpec(memory_space=pltpu.SMEM),
                  pl.BlockSpec((row_tile, D_MODEL), lambda j, i: (i, 0)), w_spec,
                  pl.BlockSpec((None, CHUNK, D_MODEL),
                               lambda j, i: (layer, cast_block(j, i), 0))],
        out_specs=[pl.BlockSpec((row_tile, IN_PROJ_COLS), lambda j, i: (i, j)),
                   pl.BlockSpec((CHUNK, D_MODEL), lambda j, i: (cast_block(j, i), 0))],
        out_shape=[jax.ShapeDtypeStruct((rows, n_cols), out_dtype),
                   jax.ShapeDtypeStruct((D_MODEL, D_MODEL), BF16)],
        scratch_shapes=[pltpu.VMEM(slab, BF16)],
        compiler_params=_params(("arbitrary", "arbitrary")),
        name=name,
    )(scales, hn, w_stack, w_out_stack)


def _out_proj_kernel(sub, n_norms, with_gate, keep_x, token_layout, with_kv, *refs):
    og_refs, w_ref, refs = refs[:sub], refs[sub], refs[sub + 1:]
    n_x = sub if token_layout is None else 2 * TOKEN_SUB
    x_refs, refs = refs[:n_x], refs[n_x:]
    nw_refs, gate_refs, outs = _split_refs(refs, n_norms, with_gate)
    if with_kv:
        (nkv_ref, wkv_ref), outs, kv_ref = outs[:2], outs[2:-1], outs[-1]
    og = og_refs[0][...] if sub == 1 else jnp.concatenate([r[...] for r in og_refs], axis=0)
    if token_layout is not None:
        x = _token_rows(*token_layout, x_refs)
    elif sub == 1:
        x = x_refs[0][...]
    else:
        x = jnp.concatenate([r[...] for r in x_refs], axis=0)
    y = x + jnp.dot(og, w_ref[...], preferred_element_type=F32)
    if keep_x:
        outs[0][...] = y
        outs = outs[1:]
    yn = _emit_normed(y, nw_refs, gate_refs, outs[:n_norms],
                      outs[n_norms] if with_gate else None)
    if with_kv:
        kv_ref[...] = jnp.dot((yn * nkv_ref[...]).astype(BF16), wkv_ref[...],
                              preferred_element_type=F32)


def _out_proj(og, w, x, norms, gate, keep_x, normed_dtype, name, gather=None, tokens=None,
              kv=None):
    if gather is None:
        tile_rows, n_tiles, sub = ROW_TILE, og.shape[0] // ROW_TILE, 1
        in_row_specs = [pl.BlockSpec((ROW_TILE, D_MODEL), lambda i: (i, 0))]
    else:
        tile_rows, n_tiles, first_block = gather
        sub = tile_rows // CHUNK
        in_row_specs = [pl.BlockSpec((CHUNK, D_MODEL),
                                     functools.partial(lambda k, i: (first_block(i) + k, 0), k))
                        for k in range(sub)]
    rows = tile_rows * n_tiles
    row_spec = pl.BlockSpec((tile_rows, D_MODEL), lambda i: (i, 0))
    norm_specs, norm_args = _norm_specs(norms, gate, lambda i: (0, 0))
    out_specs, out_shape = [], []
    if keep_x:
        out_specs.append(row_spec)
        out_shape.append(jax.ShapeDtypeStruct((rows, D_MODEL), F32))
    for _ in norms:
        out_specs.append(row_spec)
        out_shape.append(jax.ShapeDtypeStruct((rows, D_MODEL), normed_dtype))
    if gate is not None:
        out_specs.append(pl.BlockSpec((tile_rows, GLA_KEY_DIM), lambda i: (i, 0)))
        out_shape.append(jax.ShapeDtypeStruct((rows, GLA_KEY_DIM), F32))
    w_stack, layer = w
    w_spec = pl.BlockSpec((None, D_MODEL, D_MODEL), lambda i: (layer, 0, 0),
                          pipeline_mode=pl.Buffered(1))
    if x is None:
        sources, chunks, n_batch = tokens
        x_specs, x_args = _token_sources(*sources, chunks, n_batch)
        token_layout = (chunks, n_batch)
    else:
        x_specs, x_args, token_layout = in_row_specs, [x] * sub, None
    if kv is not None:
        norm_kv, w_kv = kv
        norm_specs += [pl.BlockSpec((1, D_MODEL), lambda i: (0, 0)),
                       pl.BlockSpec(w_kv.shape, lambda i: (0, 0), pipeline_mode=pl.Buffered(1))]
        norm_args += [norm_kv[None], w_kv]
        out_specs.append(pl.BlockSpec((tile_rows, w_kv.shape[1]), lambda i: (i, 0)))
        out_shape.append(jax.ShapeDtypeStruct((rows, w_kv.shape[1]), F32))
    return pl.pallas_call(
        functools.partial(_out_proj_kernel, sub, len(norms), gate is not None, keep_x,
                          token_layout, kv is not None),
        grid=(n_tiles,),
        in_specs=in_row_specs + [w_spec] + x_specs + norm_specs,
        out_specs=out_specs,
        out_shape=out_shape,
        compiler_params=_params(("parallel",)),
        name=name,
    )(*([og] * sub), w_stack, *x_args, *norm_args)


def _exact_block(qs, k, b, n, width, col0):
    lane = lax.broadcasted_iota(jnp.int32, (n, width), 1)
    acc = jnp.zeros((n, width), F32)
    for j in range(n):
        t = qs * k[j:j + 1] * jnp.exp2(jnp.minimum(b - b[j:j + 1], 0.0))
        acc = jnp.where(lane == col0 + j, jnp.sum(t, axis=-1, keepdims=True), acc)
    return acc


def _masked_row_sums(mask_bf16, g):
    g_hi = g.astype(BF16)
    r1 = g - g_hi.astype(F32)
    g_mid = r1.astype(BF16)
    g_lo = (r1 - g_mid.astype(F32)).astype(BF16)
    return (jnp.dot(mask_bf16, g_hi, preferred_element_type=F32)
            + jnp.dot(mask_bf16, g_mid, preferred_element_type=F32)
            + jnp.dot(mask_bf16, g_lo, preferred_element_type=F32))


def _column_of(row_vec, n):
    return jnp.broadcast_to(row_vec, (LANES, n)).T[:, 0:1]


def _head_norm_gate(o, onorm, gate):
    ms = jnp.mean(o * o, axis=-1, keepdims=True)
    return (o * lax.rsqrt(ms + RMS_EPS) * onorm * _silu(gate)).astype(BF16)


def _intra_chunk_scores(qs, k, k_bf, b, row_xor_col):
    c_len = qs.shape[0]
    leaf = GLA_LEAF
    leaf_row = lax.broadcasted_iota(jnp.int32, (leaf, c_len), 0)
    leaf_col = lax.broadcasted_iota(jnp.int32, (leaf, c_len), 1)
    lane_in_leaf = [(leaf_col & (leaf - 1)) == j for j in range(leaf)]

    att_rows = []
    for blk0 in range(0, c_len // leaf, LEAF_BATCH):
        lhs = []
        for blk in range(blk0, blk0 + LEAF_BATCH):
            q_blk = qs[blk * leaf:(blk + 1) * leaf]
            b_blk = b[blk * leaf:(blk + 1) * leaf]
            lhs += [q_blk * jnp.exp2(jnp.minimum(b_blk - b_blk[j:j + 1], 0.0))
                    for j in range(leaf)]
        r = lax.dot_general(jnp.concatenate(lhs, axis=0).astype(BF16), k_bf, NT_DIMS,
                            preferred_element_type=F32)
        for i, blk in enumerate(range(blk0, blk0 + LEAF_BATCH)):
            acc = jnp.zeros((leaf, c_len), F32)
            for j in range(leaf):
                r0 = (i * leaf + j) * leaf
                acc = jnp.where(lane_in_leaf[j], r[r0:r0 + leaf], acc)
            rel = leaf_col - blk * leaf
            att_rows.append(jnp.where((rel >= 0) & (rel <= leaf_row), acc, 0.0))
    att = jnp.concatenate(att_rows, axis=0)

    m = c_len // 2
    while m >= leaf:
        q_parts, k_parts = [], []
        zeros = jnp.zeros((m, GLA_DK), F32)
        for base in range(0, c_len, 2 * m):
            ref = b[base + m - 1:base + m]
            lo = slice(base, base + m)
            hi = slice(base + m, base + 2 * m)
            k_parts += [k[lo] * jnp.exp2(ref - b[lo]), zeros]
            q_parts += [zeros, qs[hi] * jnp.exp2(b[hi] - ref)]
        qt = jnp.concatenate(q_parts, axis=0).astype(BF16)
        kt = jnp.concatenate(k_parts, axis=0).astype(BF16)
        a = lax.dot_general(qt, kt, NT_DIMS, preferred_element_type=F32)
        att = att + jnp.where(row_xor_col < 2 * m, a, 0.0)
        m //= 2
    return att


def _gla_prompt_kernel(q_ref, k_ref, v_ref, gate_ref, g_ref, onorm_ref, *rest):
    og_ref, s_ref, b_scr, qd_scr, att_scr = rest[-5:]
    c_len = CHUNK
    heads = [(slice(h * GLA_DK, (h + 1) * GLA_DK), slice(h * GLA_DV, (h + 1) * GLA_DV))
             for h in range(GLA_HEADS)]

    @pl.when(pl.program_id(1) == 0)
    def _():
        s_ref[...] = jnp.zeros_like(s_ref)

    row = lax.broadcasted_iota(jnp.int32, (c_len, c_len), 0)
    col = lax.broadcasted_iota(jnp.int32, (c_len, c_len), 1)
    lower = row >= col
    tri = jnp.where(lower, 1.0, 0.0).astype(BF16)
    onorm = onorm_ref[...]
    chunk_rows = [slice(c * c_len, (c + 1) * c_len) for c in range(q_ref.shape[0] // c_len)]

    total = None
    for rows in chunk_rows:
        b_c = _masked_row_sums(tri, g_ref[rows, :])
        b_scr[rows, :] = b_c
        qd_scr[rows, :] = (q_ref[rows, :].astype(F32) * jnp.exp2(b_c)).astype(BF16)
        last = b_c[c_len - 1:c_len]
        total = last if total is None else jnp.minimum(total, last)

    bounded = jnp.max(-total) <= GLA_BOUNDED_RANGE

    @pl.when(bounded)
    def _():
        for c, rows in enumerate(chunk_rows):
            for h, (ks, _) in enumerate(heads):
                k_inv = (k_ref[rows, ks].astype(F32) * jnp.exp2(-b_scr[rows, ks])).astype(BF16)
                a = lax.dot_general(qd_scr[rows, ks], k_inv, NT_DIMS,
                                    preferred_element_type=F32)
                att_scr[c * GLA_HEADS + h] = jnp.where(lower, a, 0.0).astype(BF16)

    @pl.when(jnp.logical_not(bounded))
    def _():
        for c, rows in enumerate(chunk_rows):
            for h, (ks, _) in enumerate(heads):
                k_bf = k_ref[rows, ks]
                att_scr[c * GLA_HEADS + h] = _intra_chunk_scores(
                    q_ref[rows, ks].astype(F32), k_bf.astype(F32), k_bf,
                    b_scr[rows, ks], row ^ col).astype(BF16)

    for c, rows in enumerate(chunk_rows):
        for h, (ks, vs) in enumerate(heads):
            k = k_ref[rows, ks].astype(F32)
            v = v_ref[rows, vs]
            b = b_scr[rows, ks]
            b_last = b[c_len - 1:c_len]
            state = s_ref[0, h]

            o = jnp.dot(qd_scr[rows, ks], state.astype(BF16), preferred_element_type=F32)
            o = o + jnp.dot(att_scr[c * GLA_HEADS + h], v, preferred_element_type=F32)

            kd_t = (k * jnp.exp2(b_last - b)).T.astype(BF16)
            s_ref[0, h] = (state * _column_of(jnp.exp2(b_last), GLA_DK)
                           + jnp.dot(kd_t, v, preferred_element_type=F32))

            og_ref[rows, vs] = _head_norm_gate(o, onorm, gate_ref[rows, vs].astype(F32))


def _gla_prompt(proj, g, onorm, og, n_batch, chunks, layer, n_layers, states):
    rows = proj.shape[0]
    per_step = next(n for n in (GLA_CHUNKS_PER_STEP, 1) if chunks % n == 0)
    steps = chunks // per_step
    tile = per_step * CHUNK
    blk = lambda col: (lambda b, c: (b * steps + c, col))
    in_specs = [
        pl.BlockSpec((tile, GLA_KEY_DIM), blk(0)),
        pl.BlockSpec((tile, GLA_KEY_DIM), blk(1)),
        pl.BlockSpec((tile, GLA_VAL_DIM), blk(1)),
        pl.BlockSpec((tile, GLA_VAL_DIM), blk(2)),
        pl.BlockSpec((tile, GLA_KEY_DIM), blk(0)),
        pl.BlockSpec((1, GLA_DV), lambda b, c: (0, 0)),
        pl.BlockSpec(memory_space=pl.ANY),
    ]
    args = [proj, proj, proj, proj, g, onorm, og]
    aliases = {6: 0}
    if states is not None:
        in_specs.append(pl.BlockSpec(memory_space=pl.ANY))
        args.append(states)
        aliases[7] = 1
    return pl.pallas_call(
        _gla_prompt_kernel,
        grid=(n_batch, steps),
        in_specs=in_specs,
        out_specs=[
            pl.BlockSpec((tile, GLA_VAL_DIM), blk(0)),
            pl.BlockSpec((None, 1, GLA_HEADS, GLA_DK, GLA_DV),
                         lambda b, c: (layer, b, 0, 0, 0)),
        ],
        out_shape=[
            jax.ShapeDtypeStruct((rows, GLA_VAL_DIM), BF16),
            jax.ShapeDtypeStruct((n_layers, n_batch, GLA_HEADS, GLA_DK, GLA_DV), F32),
        ],
        scratch_shapes=[pltpu.VMEM((tile, GLA_KEY_DIM), F32),
                        pltpu.VMEM((tile, GLA_KEY_DIM), BF16),
                        pltpu.VMEM((per_step * GLA_HEADS, CHUNK, CHUNK), BF16)],
        input_output_aliases=aliases,
        compiler_params=_params(("parallel", "arbitrary")),
        name="gla_prompt",
    )(*args)


def _gla_sample_kernel(t_len, q_ref, k_ref, v_ref, gate_ref, g_ref, onorm_ref, s0_ref,
                       og_in_ref, *rest):
    og_ref, s_ref = rest[-2], rest[-1]
    n = GLA_SEQ_GROUP * t_len
    row = lax.broadcasted_iota(jnp.int32, (n, LANES), 0)
    col = lax.broadcasted_iota(jnp.int32, (n, LANES), 1)
    causal = ((row // t_len) == (col // t_len)) & (row >= col)
    causal_bf16 = jnp.where(causal, 1.0, 0.0).astype(BF16)
    row_seq = lax.broadcasted_iota(jnp.int32, (n, 1), 0) // t_len
    pad_k = jnp.zeros((LANES - n, GLA_DK), F32)

    for h in range(GLA_SAMPLE_HEADS):
        ks = slice(h * GLA_DK, (h + 1) * GLA_DK)
        vs = slice(h * GLA_DV, (h + 1) * GLA_DV)
        qs = q_ref[:, ks].astype(F32)
        k = k_ref[:, ks].astype(F32)
        v_pad = jnp.concatenate([v_ref[:, vs], jnp.zeros((LANES - n, GLA_DV), BF16)], axis=0)
        b = _masked_row_sums(causal_bf16, jnp.concatenate([g_ref[:, ks], pad_k], axis=0))
        att = jnp.where(causal, _exact_block(qs, k, b, n, LANES, 0), 0.0)
        o = jnp.dot(att.astype(BF16), v_pad, preferred_element_type=F32)

        q_dec = (qs * jnp.exp2(b)).astype(BF16)
        for u in range(GLA_SEQ_GROUP):
            state = s0_ref[0, u, h]
            o_u = jnp.dot(q_dec, state.astype(BF16), preferred_element_type=F32)
            o = o + jnp.where(row_seq == u, o_u, 0.0)
            b_last = b[(u + 1) * t_len - 1:(u + 1) * t_len]
            kd = jnp.where(row_seq == u, k * jnp.exp2(jnp.minimum(b_last - b, 0.0)), 0.0)
            kd_t = jnp.concatenate([kd, pad_k], axis=0).T.astype(BF16)
            s_ref[0, u, h] = (state * _column_of(jnp.exp2(b_last), GLA_DK)
                              + jnp.dot(kd_t, v_pad, preferred_element_type=F32))

        og_ref[:, vs] = _head_norm_gate(o, onorm_ref[...], gate_ref[:, vs].astype(F32))


def _gla_sample(proj, g, onorm, state_in, og, s_out, layer, n_seq, t_len, row0):
    n = GLA_SEQ_GROUP * t_len
    rb0 = row0 // n
    n_layers = state_in.shape[0]
    hs = GLA_SAMPLE_HEADS
    kw, vw = hs * GLA_DK, hs * GLA_DV
    key_blocks = GLA_KEY_DIM // kw
    val_blocks = GLA_VAL_DIM // vw
    st_spec = pl.BlockSpec((1, GLA_SEQ_GROUP, hs, GLA_DK, GLA_DV),
                           lambda s, h: (layer, s, h, 0, 0))
    in_specs = [
        pl.BlockSpec((n, kw), lambda s, h: (rb0 + s, h)),
        pl.BlockSpec((n, kw), lambda s, h: (rb0 + s, key_blocks + h)),
        pl.BlockSpec((n, vw), lambda s, h: (rb0 + s, val_blocks + h)),
        pl.BlockSpec((n, vw), lambda s, h: (rb0 + s, 2 * val_blocks + h)),
        pl.BlockSpec((n, kw), lambda s, h: (rb0 + s, h)),
        pl.BlockSpec((1, GLA_DV), lambda s, h: (0, 0)),
        st_spec,
        pl.BlockSpec(memory_space=pl.ANY),
    ]
    args = [proj, proj, proj, proj, g, onorm, state_in, og]
    aliases = {7: 0}
    if s_out is not None:
        in_specs.append(pl.BlockSpec(memory_space=pl.ANY))
        args.append(s_out)
        aliases[8] = 1
    return pl.pallas_call(
        functools.partial(_gla_sample_kernel, t_len),
        grid=(n_seq // GLA_SEQ_GROUP, GLA_HEADS // hs),
        in_specs=in_specs,
        out_specs=[
            pl.BlockSpec((n, vw), lambda s, h: (rb0 + s, h)),
            st_spec,
        ],
        out_shape=[
            jax.ShapeDtypeStruct(og.shape, BF16),
            jax.ShapeDtypeStruct((n_layers, n_seq, GLA_HEADS, GLA_DK, GLA_DV), F32),
        ],
        input_output_aliases=aliases,
        compiler_params=_params(("parallel", "parallel")),
        name="gla_sample",
    )(*args)


def _nt_dot(a, b):
    return lax.dot_general(a, b, NT_DIMS, preferred_element_type=F32)


def _lane_padded_copies(pair, fill):
    low = lax.broadcasted_iota(jnp.int32, pair.shape, 1) < HEAD_DIM
    swapped = pltpu.roll(pair, HEAD_DIM, axis=1).astype(BF16)
    pair = pair.astype(BF16)
    fill = jnp.full((), fill, BF16)
    return ((jnp.where(low, pair, fill), jnp.where(low, fill, swapped)),
            (jnp.where(low, swapped, fill), jnp.where(low, fill, pair)))


def _natural_kv_set(keys, vals, mask):
    slot0 = lax.broadcasted_iota(jnp.int32, keys[0].shape, 0) == 0

    def score_fns(cpair):
        return tuple(tuple(functools.partial(_nt_dot, b=k) for k in pads)
                     for pads in _lane_padded_copies(keys[cpair], 0.0))

    def value_fns(cpair):
        pads_by_head = _lane_padded_copies(jnp.where(slot0, 0.0, vals[cpair]), 1.0)
        return tuple(tuple((lambda p, v=v: jnp.dot(p, v, preferred_element_type=F32))
                           for v in pads) for pads in pads_by_head)

    return score_fns, value_fns, mask


def _window_kv_set(kt_ref, vt_ref, k_new_t, v_new_t, mask):
    win = kt_ref.shape[1]
    lane0 = lax.broadcasted_iota(jnp.int32, (HEAD_DIM, win), 1) == 0

    def head_keys(old_ref, new_t, cpair, parity, zero_slot0):
        r0 = (2 * cpair + parity) * HEAD_DIM
        old = old_ref[r0:r0 + HEAD_DIM, :]
        if zero_slot0:
            old = jnp.where(lane0, 0.0, old)
        new = new_t[cpair][parity * HEAD_DIM:(parity + 1) * HEAD_DIM, :]
        return jnp.concatenate([old, new], axis=1).astype(BF16)

    def stacks(head, other):
        return (jnp.concatenate([head, other], axis=0), jnp.concatenate([other, head], axis=0))

    def score_fns(cpair):
        fns = []
        for parity in range(2):
            head = head_keys(kt_ref, k_new_t, cpair, parity, False)
            fns.append(tuple(
                (lambda q, k=k: jnp.dot(q, k, preferred_element_type=F32))
                for k in stacks(head, jnp.zeros_like(head))))
        return tuple(fns)

    def value_fns(cpair):
        fns = []
        for parity in range(2):
            head = head_keys(vt_ref, v_new_t, cpair, parity, True)
            fns.append(tuple(functools.partial(_nt_dot, b=v)
                             for v in stacks(head, jnp.ones_like(head))))
        return tuple(fns)

    return score_fns, value_fns, mask


def _attend_rows(q_ref, gate_ref, og_ref, sink_ref, kv_sets, s_scr, p_scr):
    n_rows = q_ref.shape[0]
    group = N_Q_HEADS // N_KV_HEADS
    low_out = lax.broadcasted_iota(jnp.int32, (n_rows, LANES), 1) < HEAD_DIM
    lane0_row = lax.broadcasted_iota(jnp.int32, (1, LANES), 1) == 0

    def kv_heads():
        for cpair in range(KV_WIDTH // LANES):
            for parity in range(2):
                first = (2 * cpair + parity) * (group // 2)
                yield cpair, parity, list(range(first, first + group // 2))

    def slot(set_index, qc, half):
        return (set_index * (Q_WIDTH // LANES) + qc) * 2 + half

    for si, (score_fns, _, _) in enumerate(kv_sets):
        fns = None
        for cpair, parity, qcs in kv_heads():
            if parity == 0:
                fns = score_fns(cpair)
            q_stack = jnp.concatenate(
                [q_ref[:, qc * LANES:(qc + 1) * LANES] for qc in qcs], axis=0)
            for half, score in enumerate(fns[parity]):
                s = score(q_stack)
                for j, qc in enumerate(qcs):
                    s_scr[slot(si, qc, half)] = s[j * n_rows:(j + 1) * n_rows]

    for si, (_, _, mask) in enumerate(kv_sets):
        mask_first, mask_rest = mask[:, :LANES], mask[:, LANES:]
        for qc in range(Q_WIDTH // LANES):
            for half in range(2):
                sink = sink_ref[2 * qc + half] * LOG2_E
                s = s_scr[slot(si, qc, half)]
                first = jnp.where(mask_first, s[:, :LANES],
                                  jnp.where(lane0_row, sink, NEG_INF))
                rest = jnp.where(mask_rest, s[:, LANES:], NEG_INF)
                m = jnp.max(jnp.maximum(first, rest), axis=-1, keepdims=True)
                p_scr[slot(si, qc, half)] = jnp.exp2(
                    jnp.concatenate([first, rest], axis=1) - m).astype(BF16)

    def store(qc, o):
        cols = slice(qc * LANES, (qc + 1) * LANES)
        og_ref[:, cols] = (o * _silu(gate_ref[:, cols].astype(F32))).astype(BF16)

    out_cols = [None] * (Q_WIDTH // LANES)
    for si, (_, value_fns, _) in enumerate(kv_sets):
        fns = None
        for cpair, parity, qcs in kv_heads():
            if parity == 0:
                fns = value_fns(cpair)
            stacked = [weigh(jnp.concatenate([p_scr[slot(si, qc, half)] for qc in qcs],
                                             axis=0))
                       for half, weigh in enumerate(fns[parity])]
            for j, qc in enumerate(qcs):
                oa, ob = (o[j * n_rows:(j + 1) * n_rows] for o in stacked)
                sums = pltpu.roll(jnp.where(low_out, ob, oa), HEAD_DIM, axis=1)
                o2 = jnp.where(low_out, oa, ob) * (1.0 / sums)
                if len(kv_sets) == 1:
                    store(qc, o2)
                else:
                    out_cols[qc] = o2 if out_cols[qc] is None else out_cols[qc] + o2

    if len(kv_sets) > 1:
        for qc, o in enumerate(out_cols):
            store(qc, o)


def _attention_scratch(n_sets, n_rows, n_keys):
    n = n_sets * N_Q_HEADS
    return [pltpu.VMEM((n, n_rows, n_keys), F32), pltpu.VMEM((n, n_rows, n_keys), BF16)]


def _swa_prompt_kernel(front, sink_ref, q_ref, gate_ref, kc_ref, kp_ref, vc_ref, vp_ref,
                       og_in_ref, og_ref, s_scr, p_scr):
    del og_in_ref
    blk = pl.program_id(1)
    row = lax.broadcasted_iota(jnp.int32, (CHUNK, 2 * CHUNK), 0)
    col = lax.broadcasted_iota(jnp.int32, (CHUNK, 2 * CHUNK), 1)
    mask = (col > row) & (col <= row + WINDOW) & ((blk - 1) * CHUNK + col >= front)
    lane_pairs = [slice(c * LANES, (c + 1) * LANES) for c in range(KV_WIDTH // LANES)]
    keys = [jnp.concatenate([kp_ref[:, c], kc_ref[:, c]], axis=0) for c in lane_pairs]
    vals = [jnp.concatenate([vp_ref[:, c], vc_ref[:, c]], axis=0) for c in lane_pairs]
    _attend_rows(q_ref, gate_ref, og_ref, sink_ref, [_natural_kv_set(keys, vals, mask)],
                 s_scr, p_scr)


def _swa_prompt(pq, kv, sinks, og, n_batch, chunks, front):
    cur = lambda col: (lambda b, i: (b * chunks + i, col))
    prev = lambda col: (lambda b, i: (b * chunks + jnp.maximum(i - 1, 0), col))
    return pl.pallas_call(
        functools.partial(_swa_prompt_kernel, front),
        grid=(n_batch, chunks),
        in_specs=[
            pl.BlockSpec(memory_space=pltpu.SMEM),
            pl.BlockSpec((CHUNK, Q_WIDTH), cur(0)),
            pl.BlockSpec((CHUNK, Q_WIDTH), cur(1)),
            pl.BlockSpec((CHUNK, KV_WIDTH), cur(0)),
            pl.BlockSpec((CHUNK, KV_WIDTH), prev(0)),
            pl.BlockSpec((CHUNK, KV_WIDTH), cur(1)),
            pl.BlockSpec((CHUNK, KV_WIDTH), prev(1)),
            pl.BlockSpec(memory_space=pl.ANY),
        ],
        out_specs=pl.BlockSpec((CHUNK, Q_WIDTH), cur(0)),
        out_shape=jax.ShapeDtypeStruct(og.shape, BF16),
        scratch_shapes=_attention_scratch(1, CHUNK, 2 * CHUNK),
        input_output_aliases={7: 0},
        compiler_params=_params(("parallel", "parallel")),
        name="swa_prompt",
    )(sinks, pq, pq, kv, kv, kv, kv, og)


def _swa_sample_kernel(t_len, sink_ref, q_ref, gate_ref, kn_ref, vn_ref, ckt_ref, cvt_ref,
                       og_in_ref, og_ref, s_scr, p_scr):
    del og_in_ref
    n = SEQ_GROUP * t_len
    win = ckt_ref.shape[2]
    row = lax.broadcasted_iota(jnp.int32, (n, 2 * win), 0)
    col = lax.broadcasted_iota(jnp.int32, (n, 2 * win), 1)
    t = row % t_len
    new = col - win
    in_window = ((col < win) & (col > t)) | (
        (new >= 0) & (new < n) & ((new // t_len) == (row // t_len)) & ((new % t_len) <= t))
    pad = jnp.zeros((win - n, LANES), F32)
    lane_pairs = [slice(c * LANES, (c + 1) * LANES) for c in range(KV_WIDTH // LANES)]
    k_new_t = [jnp.concatenate([kn_ref[:, c], pad], axis=0).T for c in lane_pairs]
    v_new_t = [jnp.concatenate([vn_ref[:, c], pad], axis=0).T for c in lane_pairs]
    kv_sets = [_window_kv_set(ckt_ref.at[u], cvt_ref.at[u], k_new_t, v_new_t,
                              in_window & ((row // t_len) == u))
               for u in range(SEQ_GROUP)]
    _attend_rows(q_ref, gate_ref, og_ref, sink_ref, kv_sets, s_scr, p_scr)


def _swa_sample(pq, kv, cache_kt, cache_vt, sinks, og, n_seq, t_len, row0):
    n = SEQ_GROUP * t_len
    rb0 = row0 // n
    win = cache_kt.shape[2]
    return pl.pallas_call(
        functools.partial(_swa_sample_kernel, t_len),
        grid=(n_seq // SEQ_GROUP,),
        in_specs=[
            pl.BlockSpec(memory_space=pltpu.SMEM),
            pl.BlockSpec((n, Q_WIDTH), lambda s: (rb0 + s, 0)),
            pl.BlockSpec((n, Q_WIDTH), lambda s: (rb0 + s, 1)),
            pl.BlockSpec((n, KV_WIDTH), lambda s: (rb0 + s, 0)),
            pl.BlockSpec((n, KV_WIDTH), lambda s: (rb0 + s, 1)),
            pl.BlockSpec((SEQ_GROUP, KV_WIDTH, win), lambda s: (s, 0, 0)),
            pl.BlockSpec((SEQ_GROUP, KV_WIDTH, win), lambda s: (s, 0, 0)),
            pl.BlockSpec(memory_space=pl.ANY),
        ],
        out_specs=pl.BlockSpec((n, Q_WIDTH), lambda s: (rb0 + s, 0)),
        out_shape=jax.ShapeDtypeStruct(og.shape, BF16),
        scratch_shapes=_attention_scratch(SEQ_GROUP, n, 2 * win),
        input_output_aliases={7: 0},
        compiler_params=_params(("parallel",)),
        name="swa_sample",
    )(sinks, pq, pq, kv, kv, cache_kt, cache_vt, og)


def kernel(x_prompt, x_sample, state_gla, cache_k_win, cache_v_win, meta_tokens, norm_a, w_in_a, w_gk2_a, b_gk2_a, onorm_a, w_out_a, norm_kv, w_kv, norm_b, w_in_b, sinks_b, w_out_b, norm_f):
    n_batch, seq, _ = x_prompt.shape
    n_seq, t_len, _ = x_sample.shape
    win = cache_k_win.shape[1]
    n_a = w_in_a.shape[0]
    n_b = w_in_b.shape[0]
    assert win == WINDOW == CHUNK and n_seq % SEQ_GROUP == 0 and n_seq % GLA_SEQ_GROUP == 0

    length = seq + N_META
    front = (-length) % CHUNK
    lp = length + front
    chunks = lp // CHUNK
    srow = n_batch * lp
    ns = n_seq * t_len
    rows = -(-(srow + ns) // ROW_TILE) * ROW_TILE

    assert front + N_META == CHUNK and ns == CHUNK and seq % CHUNK == 0
    assert rows % CHUNK == 0 and rows % (16 * IN_PROJ_ROW_STEPS) == 0

    w_in_a_t = jnp.swapaxes(w_in_a, 1, 2)

    def gate_weights(l):
        w_gate = jnp.pad(w_in_a_t[l, GLA_MAIN_WIDTH:], ((0, LANES - GATE_RANK), (0, 0)))
        w_gk = jnp.pad(w_gk2_a[l], ((0, LANES - GATE_RANK), (0, 0))).astype(BF16)
        return w_gate, w_gk, b_gk2_a[l][None]

    extras = jnp.concatenate([jnp.zeros((front, D_MODEL), F32), meta_tokens.astype(F32),
                              x_sample.reshape(ns, D_MODEL), jnp.zeros((CHUNK, D_MODEL), F32)])
    tokens = ((x_prompt.reshape(n_batch * seq, D_MODEL), extras), chunks, n_batch)
    hn, g = _embed(tokens[0], norm_a[0], gate_weights(0), rows, chunks, n_batch)
    x = None
    og = jnp.zeros((rows, D_MODEL), BF16)

    states_p = None
    states_s = None
    for l in range(n_a):
        proj, w_out = _in_proj(hn, w_in_a_t, l, GLA_MAIN_WIDTH, BF16, True, "gla_in_proj",
                               w_out_a, q_cols=GLA_KEY_DIM, q_scale=GLA_DK ** -0.5)
        onorm = onorm_a[l][None]
        og, states_p = _gla_prompt(proj, g, onorm, og, n_batch, chunks, l, n_a, states_p)
        og, states_s = _gla_sample(proj, g, onorm, state_gla, og, states_s, l,
                                   n_seq, t_len, srow)
        w_out = (w_out[None], 0)
        if l + 1 < n_a:
            x, hn, g = _out_proj(og, w_out, x, [norm_a[l + 1]], gate_weights(l + 1), True,
                                 BF16, "gla_out_proj", tokens=tokens)
        else:
            x, hn, kv = _out_proj(og, w_out, x, [norm_b[0]], None, True, BF16, "gla_out_proj",
                                  tokens=tokens, kv=(norm_kv, w_kv.astype(BF16)))

    cache_k = jnp.transpose(cache_k_win, (0, 2, 3, 1)).reshape(n_seq, KV_WIDTH, win)
    cache_v = jnp.transpose(cache_v_win, (0, 2, 3, 1)).reshape(n_seq, KV_WIDTH, win)
    for j in range(n_b):
        pq, w_out = _in_proj(hn, w_in_b, j, 2 * Q_WIDTH, BF16, False, "swa_in_proj",
                             w_out_b, q_cols=Q_WIDTH, q_scale=HEAD_DIM ** -0.5 * LOG2_E)
        og = _swa_prompt(pq, kv, sinks_b[j], og, n_batch, chunks, front)
        og = _swa_sample(pq, kv, cache_k, cache_v, sinks_b[j], og, n_seq, t_len, srow)
        w_out = (w_out[None], 0)
        if j + 1 < n_b:
            x, hn = _out_proj(og, w_out, x, [norm_b[j + 1]], None, True, BF16, "swa_out_proj")

    tiles_per_batch = seq // ROW_TILE
    lead_blocks = (lp - seq) // CHUNK

    def prompt_first_block(t):
        return ((t // tiles_per_batch) * chunks + lead_blocks
                + (t % tiles_per_batch) * (ROW_TILE // CHUNK))

    (y_prompt,) = _out_proj(og, w_out, x, [norm_f], None, False, F32, "final_prompt",
                            gather=(ROW_TILE, n_batch * tiles_per_batch, prompt_first_block))
    (y_sample,) = _out_proj(og, w_out, x, [norm_f], None, False, F32, "final_sample",
                            gather=(CHUNK, ns // CHUNK, lambda t: srow // CHUNK + t))
    y_prompt = y_prompt.reshape(n_batch, seq, D_MODEL)
    y_sample = y_sample.reshape(n_seq, t_len, D_MODEL)
    kv_p = jnp.stack([kv[(b + 1) * lp - win:(b + 1) * lp] for b in range(n_batch)])
    kv_p = kv_p.reshape(n_batch, win, 2, N_KV_HEADS, HEAD_DIM)
    kv_s = kv[srow:srow + ns].reshape(n_seq, t_len, 2, N_KV_HEADS, HEAD_DIM)
    k_win_s = jnp.concatenate([cache_k_win, kv_s[:, :, 0]], axis=1)[:, -win:]
    v_win_s = jnp.concatenate([cache_v_win, kv_s[:, :, 1]], axis=1)[:, -win:]
    return (y_prompt, y_sample, states_p, states_s,
            kv_p[:, :, 0], kv_p[:, :, 1], k_win_s, v_win_s)
```
